```python
import math
import jax
import jax.numpy as jnp
from jax import lax
import numpy as np

D_MODEL = 1024
BATCH = 16
SEQ = 2048
DEPTH = 1

CTX_LEN = 256
GRID_W = 64

N_HEADS = 8
N_KV_HEADS = 2
GQA_GROUP = N_HEADS // N_KV_HEADS
HEAD_DIM = 64
WINDOW = 128
ATTN_BLOCK = 128
ROPE_BASE = 10000.0

HY_WIDTH = 512
HY_ORDER = 2
HY_SHORT_CONV = 3
HY_EMB_DIM = 33
HY_FILTER_HIDDEN = 64
HY_FILTER_OUT_SCALE = 0.05
HY_DECAY_TARGET = 1e-2
HY_FAST_DECAY_PCT = 0.3
HY_SLOW_DECAY_PCT = 1.5

N_BRANCHES = 2

N_EXPERTS = 32
TOP_K = 4
D_FF = 1024
SWIGLU_LIMIT = 7.0
SWIGLU_ALPHA = 1.702
EXPERT_BLOCK = 128

LN_EPS = 1e-5

Q_W = N_HEADS * HEAD_DIM
KV_W = N_KV_HEADS * HEAD_DIM
K_OFF = Q_W
V_OFF = K_OFF + KV_W
HY_OFF = V_OFF + KV_W
GATE_OFF = HY_OFF + (HY_ORDER + 1) * HY_WIDTH
IN_W = GATE_OFF + N_BRANCHES * D_MODEL

DEEPNORM_ALPHA = (2 * DEPTH) ** 0.25
DEEPNORM_BETA = (8 * DEPTH) ** -0.25

kernel_name = 'hybrid_swa_hyena_moe_diffusion_block'


def _layer_norm(x, g=None, b=None):
    xf = x.astype(jnp.float32)
    mu = jnp.mean(xf, axis=-1, keepdims=True)
    var = jnp.mean(jnp.square(xf - mu), axis=-1, keepdims=True)
    y = (xf - mu) * lax.rsqrt(var + LN_EPS)
    if g is not None:
        y = y * g.astype(jnp.float32) + b.astype(jnp.float32)
    return y.astype(x.dtype)


def _modulation(cond, w, b):
    return jax.nn.silu(cond) @ w + b


def _modulate(x, shift, scale):
    return _layer_norm(x) * (1 + scale) + shift


def _axial_rope_tables(rows):
    row = jnp.repeat(jnp.arange(rows, dtype=jnp.float32), GRID_W)
    col = jnp.tile(jnp.arange(GRID_W, dtype=jnp.float32), rows)
    n_freq = HEAD_DIM // 4
    inv_freq = ROPE_BASE ** (-jnp.arange(n_freq, dtype=jnp.float32) / n_freq)
    ang = jnp.stack([row[:, None] * inv_freq, col[:, None] * inv_freq], axis=1)
    return jnp.cos(ang), jnp.sin(ang)


def _apply_axial_rope(x, cos, sin):
    B_, L, H, _ = x.shape
    xr = x.astype(jnp.float32).reshape(B_, L, H, 2, 2, HEAD_DIM // 4)
    x1, x2 = xr[..., 0, :], xr[..., 1, :]
    c = cos[None, :, None]
    s = sin[None, :, None]
    out = jnp.stack([x1 * c - x2 * s, x2 * c + x1 * s], axis=-2)
    return out.reshape(x.shape).astype(x.dtype)


def _split_proj(p):
    return (p[..., :K_OFF], p[..., K_OFF:V_OFF], p[..., V_OFF:HY_OFF],
            p[..., HY_OFF:GATE_OFF], p[..., GATE_OFF:])


def _latent_window_attention(q, k, v, k_ctx, v_ctx, sink):
    B_, L = q.shape[:2]
    n_ctx = k_ctx.shape[1]
    n_blocks = L // ATTN_BLOCK
    span = ATTN_BLOCK + 2 * WINDOW
    pad = ((0, 0), (WINDOW, WINDOW), (0, 0), (0, 0))
    k_pad, v_pad = jnp.pad(k, pad), jnp.pad(v, pad)
    q_blocks = jnp.moveaxis(
        q.reshape(B_, n_blocks, ATTN_BLOCK, N_KV_HEADS, GQA_GROUP, HEAD_DIM), 1, 0)
    qi = jnp.arange(ATTN_BLOCK)[:, None]
    kj = jnp.arange(span)[None, :]
    band = jnp.abs(kj - WINDOW - qi) <= WINDOW
    sink_logit = sink.astype(jnp.float32).reshape(1, N_KV_HEADS, GQA_GROUP, 1, 1)
    scale = HEAD_DIM ** -0.5

    def one_block(args):
        b_idx, q_blk = args
        start = b_idx * ATTN_BLOCK
        k_blk = lax.dynamic_slice_in_dim(k_pad, start, span, axis=1)
        v_blk = lax.dynamic_slice_in_dim(v_pad, start, span, axis=1)
        key_pos = start - WINDOW + kj
        valid = band & (key_pos >= 0) & (key_pos < L)
        s_loc = jnp.einsum('bqhgd,bshd->bhgqs', q_blk, k_blk).astype(jnp.float32) * scale
        s_loc = jnp.where(valid, s_loc, -jnp.inf)
        s_c = jnp.einsum('bqhgd,bchd->bhgqc', q_blk, k_ctx).astype(jnp.float32) * scale
        s_sink = jnp.broadcast_to(sink_logit, s_loc.shape[:-1] + (1,))
        p = jax.nn.softmax(jnp.concatenate([s_loc, s_c, s_sink], axis=-1), axis=-1).astype(v.dtype)
        o = (jnp.einsum('bhgqs,bshd->bqhgd', p[..., :span], v_blk)
             + jnp.einsum('bhgqc,bchd->bqhgd', p[..., span:span + n_ctx], v_ctx))
        return o

    out = lax.map(one_block, (jnp.arange(n_blocks), q_blocks))
    return jnp.moveaxis(out, 0, 1).reshape(B_, L, Q_W)


def _context_attention(q, k, v, sink):
    B_, C = q.shape[:2]
    s = jnp.einsum('bqhgd,bshd->bhgqs', q, k).astype(jnp.float32) * HEAD_DIM ** -0.5
    sink_logit = sink.astype(jnp.float32).reshape(1, N_KV_HEADS, GQA_GROUP, 1, 1)
    s_sink = jnp.broadcast_to(sink_logit, s.shape[:-1] + (1,))
    p = jax.nn.softmax(jnp.concatenate([s, s_sink], axis=-1), axis=-1)[..., :-1].astype(v.dtype)
    return jnp.einsum('bhgqs,bshd->bqhgd', p, v).reshape(B_, C, Q_W)


def _short_conv(u, w, b):
    L = u.shape[1]
    half = HY_SHORT_CONV // 2
    up = jnp.pad(u, ((0, 0), (half, half), (0, 0)))
    out = b
    for i in range(HY_SHORT_CONV):
        out = out + up[:, i:i + L] * w[i]
    return out


def _hyena_filters(L, w1, b1, w2, b2, w3):
    bands = (HY_EMB_DIM - 1) // 2
    t = jnp.linspace(0.0, 1.0, L, dtype=jnp.float32)[:, None]
    omega = 2.0 * math.pi * jnp.arange(L, dtype=jnp.float32)[:, None] / L
    f = jnp.linspace(1e-4, bands - 1, bands, dtype=jnp.float32)[None, :]
    z = jnp.concatenate([t, jnp.cos(f * omega), -jnp.sin(f * omega)], axis=-1)
    h = jnp.sin(z @ w1 + b1)
    h = jnp.sin(h @ w2 + b2)
    h = (h @ w3).astype(jnp.float32).reshape(L, HY_ORDER, 2, HY_WIDTH)
    min_decay = math.log(HY_DECAY_TARGET) / HY_FAST_DECAY_PCT
    max_decay = math.log(HY_DECAY_TARGET) / HY_SLOW_DECAY_PCT
    deltas = jnp.abs(jnp.linspace(min_decay, max_decay, HY_WIDTH, dtype=jnp.float32))
    h = h * jnp.exp(-t * deltas)[:, None, None, :]
    fwd, bwd = h[:, :, 0], h[:, :, 1]
    return jnp.concatenate([fwd, jnp.zeros_like(fwd[:1]), bwd[:0:-1]], axis=0)


def _hyena(u, conv_w, conv_b, fw1, fb1, fw2, fb2, fw3, skip):
    L = u.shape[1]
    n_fft = 2 * L
    streams = jnp.split(_short_conv(u, conv_w, conv_b).astype(jnp.float32), HY_ORDER + 1, axis=-1)
    filt_f = jnp.fft.rfft(_hyena_filters(L, fw1, fb1, fw2, fb2, fw3), axis=0)
    z = streams[-1]
    for o in range(HY_ORDER):
        conv = jnp.fft.irfft(jnp.fft.rfft(z, n=n_fft, axis=1) * filt_f[None, :, o],
                             n=n_fft, axis=1)[:, :L]
        z = streams[o] * (conv + skip[o].astype(jnp.float32) * z)
    return z.astype(u.dtype)


def _merge_branches(att, hy, gate_pre, w_ba, w_bh, w_o):
    g_att, g_hy = jnp.split(jax.nn.sigmoid(gate_pre), N_BRANCHES, axis=-1)
    return (g_att * (att @ w_ba) + g_hy * (hy @ w_bh)) @ w_o


def _moe(h, router_w, router_b, w1, b1, w2, b2):
    B_, L, D = h.shape
    T = B_ * L
    xt = h.reshape(T, D)
    logits = (xt @ router_w + router_b).astype(jnp.float32)
    top_vals, top_idx = lax.top_k(logits, TOP_K)
    gate = jax.nn.softmax(top_vals, axis=-1)
    A = T * TOP_K
    expert_ids = top_idx.reshape(A).astype(jnp.int32)
    token_ids = (jnp.arange(A, dtype=jnp.int32) // TOP_K)
    weights = gate.reshape(A)
    order = jnp.argsort(expert_ids)
    e_sorted = expert_ids[order]
    counts = jnp.bincount(expert_ids, length=N_EXPERTS).astype(jnp.int32)
    starts = jnp.cumsum(counts) - counts
    padded = (counts + EXPERT_BLOCK - 1) // EXPERT_BLOCK * EXPERT_BLOCK
    pad_end = jnp.cumsum(padded)
    pad_start = pad_end - padded
    dest = pad_start[e_sorted] + (jnp.arange(A, dtype=jnp.int32) - starts[e_sorted])
    n_blocks = A // EXPERT_BLOCK + N_EXPERTS
    n_slots = n_blocks * EXPERT_BLOCK
    slot_tok = jnp.full((n_slots,), T, jnp.int32).at[dest].set(token_ids[order])
    slot_w = jnp.zeros((n_slots,), jnp.float32).at[dest].set(weights[order])
    block_start = jnp.arange(n_blocks, dtype=jnp.int32) * EXPERT_BLOCK
    block_exp = jnp.minimum(jnp.searchsorted(pad_end, block_start, side='right'), N_EXPERTS - 1)
    x_pad = jnp.concatenate([xt, jnp.zeros((1, D), xt.dtype)], axis=0)

    def step(acc, blk):
        tok, wt, e = blk
        hb = x_pad[tok] @ w1[e] + b1[e]
        glu, lin = jnp.split(hb.astype(jnp.float32), 2, axis=-1)
        glu = jnp.minimum(glu, SWIGLU_LIMIT)
        lin = jnp.clip(lin, -SWIGLU_LIMIT, SWIGLU_LIMIT)
        act = (glu * jax.nn.sigmoid(SWIGLU_ALPHA * glu) * (lin + 1.0)).astype(h.dtype)
        yb = (act @ w2[e] + b2[e]).astype(jnp.float32)
        return acc.at[tok].add(yb * wt[:, None]), None

    acc0 = jnp.zeros((T + 1, D), jnp.float32)
    acc, _ = lax.scan(step, acc0, (slot_tok.reshape(n_blocks, EXPERT_BLOCK),
                                   slot_w.reshape(n_blocks, EXPERT_BLOCK), block_exp))
    return acc[:T].reshape(B_, L, D).astype(h.dtype)


def setup_inputs(seed: int = 0) -> dict:
    key = jax.random.key(seed)
    ks = jax.random.split(key, 32)
    D = D_MODEL
    hy_in = (HY_ORDER + 1) * HY_WIDTH
    filt_out = HY_ORDER * 2 * HY_WIDTH

    def nrm(k, shape, s):
        return jax.random.normal(k, shape, jnp.float32) * s

    return {
        'x': nrm(ks[0], (BATCH, SEQ, D), 1.0),
        'c': nrm(ks[1], (BATCH, D), 1.0),
        'ctx': nrm(ks[2], (BATCH, CTX_LEN, D), 1.0),
        'c_ctx': nrm(ks[3], (D,), 1.0),
        'w_mod': nrm(ks[4], (DEPTH, D, 6 * D), D ** -0.5),
        'b_mod': nrm(ks[5], (DEPTH, 6 * D), 0.02),
        'w_in': nrm(ks[6], (DEPTH, D, IN_W), D ** -0.5),
        'attn_sink': nrm(ks[7], (DEPTH, N_HEADS), 0.5),
        'hy_conv_w': nrm(ks[8], (DEPTH, HY_SHORT_CONV, hy_in), HY_SHORT_CONV ** -0.5),
        'hy_conv_b': nrm(ks[9], (DEPTH, hy_in), 0.02),
        'hy_filt_w1': nrm(ks[10], (DEPTH, HY_EMB_DIM, HY_FILTER_HIDDEN), HY_EMB_DIM ** -0.5),
        'hy_filt_b1': nrm(ks[11], (DEPTH, HY_FILTER_HIDDEN), 0.1),
        'hy_filt_w2': nrm(ks[12], (DEPTH, HY_FILTER_HIDDEN, HY_FILTER_HIDDEN), HY_FILTER_HIDDEN ** -0.5),
        'hy_filt_b2': nrm(ks[13], (DEPTH, HY_FILTER_HIDDEN), 0.1),
        'hy_filt_w3': nrm(ks[14], (DEPTH, HY_FILTER_HIDDEN, filt_out),
                          HY_FILTER_HIDDEN ** -0.5 * HY_FILTER_OUT_SCALE),
        'hy_skip': nrm(ks[15], (DEPTH, HY_ORDER, HY_WIDTH), 0.5),
        'w_branch_attn': nrm(ks[16], (DEPTH, Q_W, D), Q_W ** -0.5 * DEEPNORM_BETA),
        'w_branch_hyena': nrm(ks[17], (DEPTH, HY_WIDTH, D), HY_WIDTH ** -0.5 * DEEPNORM_BETA),
        'w_out': nrm(ks[18], (DEPTH, D, D), D ** -0.5 * DEEPNORM_BETA),
        'ln1_g': 1.0 + nrm(ks[19], (DEPTH, D), 0.02),
        'ln1_b': nrm(ks[20], (DEPTH, D), 0.02),
        'router_w': nrm(ks[21], (DEPTH, D, N_EXPERTS), D ** -0.5),
        'router_b': nrm(ks[22], (DEPTH, N_EXPERTS), 0.01),
        'exp_w1': nrm(ks[23], (DEPTH, N_EXPERTS, D, 2 * D_FF), D ** -0.5),
        'exp_b1': nrm(ks[24], (DEPTH, N_EXPERTS, 2 * D_FF), 0.02),
        'exp_w2': nrm(ks[25], (DEPTH, N_EXPERTS, D_FF, D), D_FF ** -0.5 * DEEPNORM_BETA),
        'exp_b2': nrm(ks[26], (DEPTH, N_EXPERTS, D), 0.02),
        'ln2_g': 1.0 + nrm(ks[27], (DEPTH, D), 0.02),
        'ln2_b': nrm(ks[28], (DEPTH, D), 0.02),
    }


def reference(x, c, ctx, c_ctx, w_mod, b_mod, w_in, attn_sink, hy_conv_w, hy_conv_b,
              hy_filt_w1, hy_filt_b1, hy_filt_w2, hy_filt_b2, hy_filt_w3, hy_skip,
              w_branch_attn, w_branch_hyena, w_out, ln1_g, ln1_b,
              router_w, router_b, exp_w1, exp_b1, exp_w2, exp_b2, ln2_g, ln2_b):
    B_, L, _ = x.shape
    C = ctx.shape[1]
    rows = L // GRID_W
    cos, sin = _axial_rope_tables(rows)
    s_ctx = ctx
    for l in range(DEPTH):
        w_in_l = w_in[l]
        filt = (hy_filt_w1[l], hy_filt_b1[l], hy_filt_w2[l], hy_filt_b2[l], hy_filt_w3[l])
        mod_x = _modulation(c, w_mod[l], b_mod[l])[:, None, :]
        mod_c = _modulation(c_ctx[None], w_mod[l], b_mod[l])[:, None, :]
        sh1, sc1, g1, sh2, sc2, g2 = jnp.split(mod_x, 6, axis=-1)
        csh1, csc1, cg1, csh2, csc2, cg2 = jnp.split(mod_c, 6, axis=-1)
        last = l + 1 == DEPTH

        hc = _modulate(s_ctx, csh1, csc1)
        if last:
            kv_c = hc @ w_in_l[:, K_OFF:HY_OFF]
            k_c, v_c = kv_c[..., :KV_W], kv_c[..., KV_W:]
        else:
            q_c, k_c, v_c, u_hy_c, gate_c = _split_proj(hc @ w_in_l)
        k_c = k_c.reshape(B_, C, N_KV_HEADS, HEAD_DIM)
        v_c = v_c.reshape(B_, C, N_KV_HEADS, HEAD_DIM)

        hx = _modulate(x, sh1, sc1)
        q, k, v, u_hy, gate_x = _split_proj(hx @ w_in_l)
        q = _apply_axial_rope(q.reshape(B_, L, N_HEADS, HEAD_DIM), cos, sin)
        q = q.reshape(B_, L, N_KV_HEADS, GQA_GROUP, HEAD_DIM)
        k = _apply_axial_rope(k.reshape(B_, L, N_KV_HEADS, HEAD_DIM), cos, sin)
        v = v.reshape(B_, L, N_KV_HEADS, HEAD_DIM)
        att = _latent_window_attention(q, k, v, k_c, v_c, attn_sink[l])
        hy = _hyena(u_hy, hy_conv_w[l], hy_conv_b[l], *filt, hy_skip[l])
        y = _merge_branches(att, hy, gate_x, w_branch_attn[l], w_branch_hyena[l], w_out[l])
        x_mid = _layer_norm(DEEPNORM_ALPHA * x + g1 * y, ln1_g[l], ln1_b[l])

        if not last:
            q_c = q_c.reshape(B_, C, N_KV_HEADS, GQA_GROUP, HEAD_DIM)
            att_c = _context_attention(q_c, k_c, v_c, attn_sink[l])
            hy_c = _hyena(u_hy_c, hy_conv_w[l], hy_conv_b[l], *filt, hy_skip[l])
            y_c = _merge_branches(att_c, hy_c, gate_c, w_branch_attn[l], w_branch_hyena[l], w_out[l])
            c_mid = _layer_norm(DEEPNORM_ALPHA * s_ctx + cg1 * y_c, ln1_g[l], ln1_b[l])
            f_c = _moe(_modulate(c_mid, csh2, csc2), router_w[l], router_b[l],
                       exp_w1[l], exp_b1[l], exp_w2[l], exp_b2[l])
            s_ctx = _layer_norm(DEEPNORM_ALPHA * c_mid + cg2 * f_c, ln2_g[l], ln2_b[l])

        f_x = _moe(_modulate(x_mid, sh2, sc2), router_w[l], router_b[l],
                   exp_w1[l], exp_b1[l], exp_w2[l], exp_b2[l])
        x = _layer_norm(DEEPNORM_ALPHA * x_mid + g2 * f_x, ln2_g[l], ln2_b[l])
    return x
```

```python
import functools
import math

import numpy as np
import jax
import jax.numpy as jnp
from jax import lax
from jax.experimental import pallas as pl
from jax.experimental.pallas import tpu as pltpu

F32 = jnp.float32
BF16 = jnp.bfloat16
HIGHEST = lax.Precision.HIGHEST

GRID_W = 64
N_HEADS = 8
N_KV_HEADS = 2
GQA_GROUP = N_HEADS // N_KV_HEADS
HEAD_DIM = 64
WINDOW = 128
ATTN_BLOCK = 128
ROPE_BASE = 10000.0

HY_WIDTH = 512
HY_ORDER = 2
HY_SHORT_CONV = 3
HY_EMB_DIM = 33
HY_DECAY_TARGET = 1e-2
HY_FAST_DECAY_PCT = 0.3
HY_SLOW_DECAY_PCT = 1.5

N_EXPERTS = 32
TOP_K = 4
SWIGLU_LIMIT = 7.0
SWIGLU_ALPHA = 1.702
LN_EPS = 1e-5

Q_W = N_HEADS * HEAD_DIM
KV_W = N_KV_HEADS * HEAD_DIM
K_OFF = Q_W
V_OFF = K_OFF + KV_W
HY_OFF = V_OFF + KV_W
GATE_OFF = HY_OFF + (HY_ORDER + 1) * HY_WIDTH

VMEM_LIMIT = 56 * 1024 * 1024
NEG_BIG = -1e30


def _cparams(sem):
    return pltpu.CompilerParams(dimension_semantics=sem, vmem_limit_bytes=VMEM_LIMIT)


def _ln(x):
    mu = jnp.mean(x, axis=-1, keepdims=True)
    xc = x - mu
    var = jnp.mean(xc * xc, axis=-1, keepdims=True)
    return xc * lax.rsqrt(var + LN_EPS)


def _bdot(a, b):
    return jnp.dot(a, b, preferred_element_type=F32)


def _fdot(a, b):
    return jnp.dot(a, b, preferred_element_type=F32, precision=HIGHEST)


def _mod_kernel(c_ref, w_ref, b_ref, o_ref):
    c = c_ref[...]
    s = c * jax.nn.sigmoid(c)
    o_ref[...] = _fdot(s, w_ref[...]) + b_ref[...]


def _modulation(cond, w, b, tn=512):
    r, d = cond.shape
    n = w.shape[1]
    return pl.pallas_call(
        _mod_kernel,
        grid=(n // tn,),
        in_specs=[pl.BlockSpec((r, d), lambda j: (0, 0)),
                  pl.BlockSpec((d, tn), lambda j: (0, j)),
                  pl.BlockSpec((1, tn), lambda j: (0, j))],
        out_specs=pl.BlockSpec((r, tn), lambda j: (0, j)),
        out_shape=jax.ShapeDtypeStruct((r, n), F32),
        compiler_params=_cparams(("arbitrary",)),
        name="modulation",
    )(cond, w, b.reshape(1, n))


def _rope(t, cos, sa, sb):
    n = t.shape[-1]
    return t * cos + pltpu.roll(t, n - 16, 1) * sa + pltpu.roll(t, 16, 1) * sb


def _inproj_kernel(x_ref, sh_ref, sc_ref, w_ref, cos_ref, sa_ref, sb_ref,
                   q_ref, k_ref, v_ref, u_ref, g_ref):
    h = _ln(x_ref[0]) * (1.0 + sc_ref[0]) + sh_ref[0]
    hb = h.astype(BF16)
    cos, sa, sb = cos_ref[...], sa_ref[...], sb_ref[...]
    scale = HEAD_DIM ** -0.5
    for j in range(Q_W // 128):
        t = _bdot(hb, w_ref[:, j * 128:(j + 1) * 128])
        q_ref[0, :, j * 128:(j + 1) * 128] = (_rope(t, cos, sa, sb) * scale).astype(BF16)
    t = _bdot(hb, w_ref[:, K_OFF:V_OFF])
    k_ref[0] = _rope(t, cos, sa, sb).astype(BF16)
    v_ref[0] = _bdot(hb, w_ref[:, V_OFF:HY_OFF]).astype(BF16)
    for j in range((GATE_OFF - HY_OFF) // 512):
        u_ref[0, :, j * 512:(j + 1) * 512] = _bdot(hb, w_ref[:, HY_OFF + j * 512:HY_OFF + (j + 1) * 512])
    n_gate = w_ref.shape[1] - GATE_OFF
    for j in range(n_gate // 512):
        g_ref[0, :, j * 512:(j + 1) * 512] = _bdot(
            hb, w_ref[:, GATE_OFF + j * 512:GATE_OFF + (j + 1) * 512])


def _in_projection(x, sh, sc, w_in_b, cos_t, sa_t, sb_t, tm=256):
    b, l, d = x.shape
    in_w = w_in_b.shape[1]
    hy_w = GATE_OFF - HY_OFF
    g_w = in_w - GATE_OFF
    row = lambda bi, i: (bi, i, 0)
    vec = lambda bi, i: (bi, 0, 0)
    tab = lambda bi, i: (i, 0)
    return pl.pallas_call(
        _inproj_kernel,
        grid=(b, l // tm),
        in_specs=[pl.BlockSpec((1, tm, d), row),
                  pl.BlockSpec((1, 1, d), vec),
                  pl.BlockSpec((1, 1, d), vec),
                  pl.BlockSpec((d, in_w), lambda bi, i: (0, 0)),
                  pl.BlockSpec((tm, 128), tab),
                  pl.BlockSpec((tm, 128), tab),
                  pl.BlockSpec((tm, 128), tab)],
        out_specs=[pl.BlockSpec((1, tm, Q_W), row),
                   pl.BlockSpec((1, tm, KV_W), row),
                   pl.BlockSpec((1, tm, KV_W), row),
                   pl.BlockSpec((1, tm, hy_w), row),
                   pl.BlockSpec((1, tm, g_w), row)],
        out_shape=[jax.ShapeDtypeStruct((b, l, Q_W), BF16),
                   jax.ShapeDtypeStruct((b, l, KV_W), BF16),
                   jax.ShapeDtypeStruct((b, l, KV_W), BF16),
                   jax.ShapeDtypeStruct((b, l, hy_w), F32),
                   jax.ShapeDtypeStruct((b, l, g_w), F32)],
        compiler_params=_cparams(("arbitrary", "arbitrary")),
        name="in_projection",
    )(x, sh, sc, w_in_b, cos_t, sa_t, sb_t)


def _ctx_kv_kernel(x_ref, sh_ref, sc_ref, w_ref, k_ref, v_ref):
    h = _ln(x_ref[0]) * (1.0 + sc_ref[0]) + sh_ref[0]
    kv = _bdot(h.astype(BF16), w_ref[...])
    k_ref[0] = kv[:, :KV_W].astype(BF16)
    v_ref[0] = kv[:, KV_W:].astype(BF16)


def _ctx_kv(ctx, sh, sc, w_kv_b):
    b, c, d = ctx.shape
    row = lambda bi: (bi, 0, 0)
    return pl.pallas_call(
        _ctx_kv_kernel,
        grid=(b,),
        in_specs=[pl.BlockSpec((1, c, d), row),
                  pl.BlockSpec((1, 1, d), lambda bi: (0, 0, 0)),
                  pl.BlockSpec((1, 1, d), lambda bi: (0, 0, 0)),
                  pl.BlockSpec((d, 2 * KV_W), lambda bi: (0, 0))],
        out_specs=[pl.BlockSpec((1, c, KV_W), row), pl.BlockSpec((1, c, KV_W), row)],
        out_shape=[jax.ShapeDtypeStruct((b, c, KV_W), BF16),
                   jax.ShapeDtypeStruct((b, c, KV_W), BF16)],
        compiler_params=_cparams(("arbitrary",)),
        name="ctx_kv",
    )(ctx, sh, sc, w_kv_b)


def _attn_kernel(sink_ref, q_ref, k_ref, v_ref, kc_ref, vc_ref, o_ref):
    l = k_ref.shape[1]
    span = ATTN_BLOCK + 2 * WINDOW
    j = pl.program_id(1)
    q0 = j * ATTN_BLOCK
    start = pl.multiple_of(jnp.clip(q0 - WINDOW, 0, l - span), ATTN_BLOCK)
    qpos = q0 + lax.broadcasted_iota(jnp.int32, (ATTN_BLOCK, span), 0)
    kpos = start + lax.broadcasted_iota(jnp.int32, (ATTN_BLOCK, span), 1)
    band = jnp.abs(kpos - qpos) <= WINDOW
    dn = (((1,), (1,)), ((), ()))
    outs = []
    for kv in range(N_KV_HEADS):
        ks = slice(kv * HEAD_DIM, (kv + 1) * HEAD_DIM)
        kl = k_ref[0, pl.ds(start, span), ks]
        vl = v_ref[0, pl.ds(start, span), ks]
        kc = kc_ref[0, :, ks]
        vc = vc_ref[0, :, ks]
        for g in range(GQA_GROUP):
            h = kv * GQA_GROUP + g
            qh = q_ref[0, :, h * HEAD_DIM:(h + 1) * HEAD_DIM]
            s_loc = lax.dot_general(qh, kl, dn, preferred_element_type=F32)
            s_loc = jnp.where(band, s_loc, NEG_BIG)
            s_ctx = lax.dot_general(qh, kc, dn, preferred_element_type=F32)
            sink = sink_ref[h]
            m = jnp.maximum(jnp.max(s_loc, axis=1, keepdims=True),
                            jnp.max(s_ctx, axis=1, keepdims=True))
            m = jnp.maximum(m, sink)
            p_loc = jnp.exp(s_loc - m)
            p_ctx = jnp.exp(s_ctx - m)
            den = (jnp.sum(p_loc, axis=1, keepdims=True) + jnp.sum(p_ctx, axis=1, keepdims=True)
                   + jnp.exp(sink - m))
            o = _bdot(p_loc.astype(BF16), vl) + _bdot(p_ctx.astype(BF16), vc)
            outs.append(o / den)
    o_ref[0] = jnp.concatenate(outs, axis=1).astype(BF16)


def _attention(q, k, v, kc, vc, sink):
    b, l, _ = q.shape
    c = kc.shape[1]
    full = lambda bi, j, s: (bi, 0, 0)
    grid_spec = pltpu.PrefetchScalarGridSpec(
        num_scalar_prefetch=1,
        grid=(b, l // ATTN_BLOCK),
        in_specs=[pl.BlockSpec((1, ATTN_BLOCK, Q_W), lambda bi, j, s: (bi, j, 0)),
                  pl.BlockSpec((1, l, KV_W), full),
                  pl.BlockSpec((1, l, KV_W), full),
                  pl.BlockSpec((1, c, KV_W), full),
                  pl.BlockSpec((1, c, KV_W), full)],
        out_specs=pl.BlockSpec((1, ATTN_BLOCK, Q_W), lambda bi, j, s: (bi, j, 0)),
    )
    return pl.pallas_call(
        _attn_kernel,
        grid_spec=grid_spec,
        out_shape=jax.ShapeDtypeStruct((b, l, Q_W), BF16),
        compiler_params=_cparams(("arbitrary", "arbitrary")),
        name="window_attention",
    )(sink, q, k, v, kc, vc)


def _dft_tables(l, wc):
    h = l // 2
    idx = np.arange(h, dtype=np.int64)
    ang = 2.0 * np.pi * ((idx[:, None] * idx[None, :]) % l).astype(np.float64) / l
    cm = np.cos(ang).astype(np.float32)
    sm = np.sin(ang).astype(np.float32)
    tw = 2.0 * np.pi * idx.astype(np.float64) / (2 * l)
    ones = np.ones((1, wc), np.float32)
    wr = np.cos(tw).astype(np.float32)[:, None] * ones
    wi = (-np.sin(tw)).astype(np.float32)[:, None] * ones
    sgn = np.where(idx % 2 == 0, 1.0, -1.0).astype(np.float32)[:, None] * ones
    wk = np.where(idx == 0, 1.0 / (2 * l), 2.0 / (2 * l)).astype(np.float32)[:, None] * ones
    return cm, sm, wr, wi, sgn, wk


def _half_spectrum(se, so, cm, sm, wr, wi, sgn):
    seb, sob = se.astype(BF16), so.astype(BF16)
    ce, ss_e = _bdot(cm, seb), _bdot(sm, seb)
    co, ss_o = _bdot(cm, sob), _bdot(sm, sob)
    vr = wr * co + wi * ss_o
    vi = wi * co - wr * ss_o
    zar, zai = ce + vr, vi - ss_e
    zbr, zbi = ce - vr, -ss_e - vi
    e_ny = jnp.sum(sgn * se, axis=0, keepdims=True)
    o_ny = jnp.sum(sgn * so, axis=0, keepdims=True)
    return zar, zai, zbr, zbi, e_ny, -o_ny


def _filter_kernel(ze_ref, zo_ref, w1_ref, b1_ref, w2_ref, b2_ref, w3f_ref, w3b_ref,
                   de_ref, do_ref, cm_ref, sm_ref, wr_ref, wi_ref, sgn_ref, wk_ref,
                   har_ref, hai_ref, hbr_ref, hbi_ref, hny_ref):
    cm, sm = cm_ref[...], sm_ref[...]
    wr, wi, sgn, wk = wr_ref[...], wi_ref[...], sgn_ref[...], wk_ref[...]
    h = cm.shape[0]

    def taps(z_ref, w3_ref, d_ref):
        a = jnp.sin(_fdot(z_ref[...], w1_ref[...]) + b1_ref[...])
        a = jnp.sin(_fdot(a, w2_ref[...]) + b2_ref[...])
        return _fdot(a, w3_ref[0]) * d_ref[...]

    fe, fo = taps(ze_ref, w3f_ref, de_ref), taps(zo_ref, w3f_ref, do_ref)
    be, bo = taps(ze_ref, w3b_ref, de_ref), taps(zo_ref, w3b_ref, do_ref)
    row = lax.broadcasted_iota(jnp.int32, be.shape, 0)
    be = jnp.where(row == 0, 0.0, be)
    far, fai, fbr, fbi, fnr, fni = _half_spectrum(fe, fo, cm, sm, wr, wi, sgn)
    bar, bai, bbr, bbi, bnr, bni = _half_spectrum(be, bo, cm, sm, wr, wi, sgn)
    har_ref[0] = wk * (far + bar)
    hai_ref[0] = wk * (fai - bai)
    hbr_ref[0] = wk * (fbr + bbr)
    hbi_ref[0] = wk * (fbi - bbi)
    ny_scale = 2.0 / (4 * h)
    nr = ny_scale * (fnr + bnr)
    ni = ny_scale * (fni - bni)
    rows = lax.broadcasted_iota(jnp.int32, (8, nr.shape[1]), 0)
    hny_ref[0] = jnp.where(rows == 0, nr, jnp.where(rows == 1, ni, 0.0))


def _hyena_filters(zfe, zfo, w1, b1, w2, b2, w3, dec_e, dec_o, tabs, wc):
    cm, sm, wr, wi, sgn, wk = tabs
    h = cm.shape[0]
    emb = zfe.shape[1]
    hid = w2.shape[0]
    nblk = HY_WIDTH // wc
    w3r = w3.reshape(hid, HY_ORDER * 2, HY_WIDTH).transpose(1, 0, 2)
    const = lambda o, cb: (0, 0)
    chan = lambda o, cb: (0, cb)
    out_spec = pl.BlockSpec((1, h, wc), lambda o, cb: (o, 0, cb))
    out_shape = jax.ShapeDtypeStruct((HY_ORDER, h, HY_WIDTH), F32)
    return pl.pallas_call(
        _filter_kernel,
        grid=(HY_ORDER, nblk),
        in_specs=[pl.BlockSpec((h, emb), const), pl.BlockSpec((h, emb), const),
                  pl.BlockSpec((emb, hid), const), pl.BlockSpec((1, hid), const),
                  pl.BlockSpec((hid, hid), const), pl.BlockSpec((1, hid), const),
                  pl.BlockSpec((1, hid, wc), lambda o, cb: (2 * o, 0, cb)),
                  pl.BlockSpec((1, hid, wc), lambda o, cb: (2 * o + 1, 0, cb)),
                  pl.BlockSpec((h, wc), chan), pl.BlockSpec((h, wc), chan),
                  pl.BlockSpec((h, h), const), pl.BlockSpec((h, h), const),
                  pl.BlockSpec((h, wc), const), pl.BlockSpec((h, wc), const),
                  pl.BlockSpec((h, wc), const), pl.BlockSpec((h, wc), const)],
        out_specs=[out_spec, out_spec, out_spec, out_spec,
                   pl.BlockSpec((1, 8, wc), lambda o, cb: (o, 0, cb))],
        out_shape=[out_shape, out_shape, out_shape, out_shape,
                   jax.ShapeDtypeStruct((HY_ORDER, 8, HY_WIDTH), F32)],
        compiler_params=_cparams(("arbitrary", "arbitrary")),
        name="hyena_filter_spectrum",
    )(zfe, zfo, w1, b1.reshape(1, hid), w2, b2.reshape(1, hid), w3r, w3r,
      dec_e, dec_o, cm, sm, wr, wi, sgn, wk)


def _long_conv(ze, zo, har, hai, hbr, hbi, hny, cm, sm, wr, wi, sgn):
    zar, zai, zbr, zbi, znr, zni = _half_spectrum(ze, zo, cm, sm, wr, wi, sgn)
    yar, yai = zar * har - zai * hai, zar * hai + zai * har
    ybr, ybi = zbr * hbr - zbi * hbi, zbr * hbi + zbi * hbr
    dr, di = yar - ybr, yai - ybi
    sar, sai = (yar + ybr).astype(BF16), (yai + ybi).astype(BF16)
    sbr, sbi = (dr * wr + di * wi).astype(BF16), (di * wr - dr * wi).astype(BF16)
    hnr, hni = hny[0:1], hny[1:2]
    ynr = znr * hnr - zni * hni
    yni = znr * hni + zni * hnr
    ye = _bdot(cm, sar) - _bdot(sm, sai) + sgn * ynr
    yo = _bdot(cm, sbr) - _bdot(sm, sbi) - sgn * yni
    return ye, yo


def _hyena_kernel(*refs):
    nsub = (len(refs) - 18) // 4
    u_refs = [refs[s * nsub:(s + 1) * nsub] for s in range(HY_ORDER + 1)]
    rest = refs[(HY_ORDER + 1) * nsub:]
    (cw0_ref, cw1_ref, cw2_ref, cb0_ref, cb1_ref, cb2_ref, skip_ref,
     har_ref, hai_ref, hbr_ref, hbi_ref, hny_ref,
     cm_ref, sm_ref, wr_ref, wi_ref, sgn_ref, o_ref) = rest[:18]
    il_refs = rest[18:]
    h = cm_ref.shape[0]
    cm, sm = cm_ref[...], sm_ref[...]
    wr, wi, sgn = wr_ref[...], wi_ref[...], sgn_ref[...]
    row = lax.broadcasted_iota(jnp.int32, (h, wr.shape[1]), 0)

    def short_conv(u_slabs, cw_ref, cb_ref):
        ue = jnp.concatenate([r[0, pl.ds(0, h, stride=2), :] for r in u_slabs], axis=1)
        uo = jnp.concatenate([r[0, pl.ds(1, h, stride=2), :] for r in u_slabs], axis=1)
        w0, w1, w2 = cw_ref[0:1], cw_ref[1:2], cw_ref[2:3]
        uo_prev = jnp.where(row == 0, 0.0, pltpu.roll(uo, 1, 0))
        ue_next = jnp.where(row == h - 1, 0.0, pltpu.roll(ue, h - 1, 0))
        cb = cb_ref[...]
        se = cb + uo_prev * w0 + ue * w1 + uo * w2
        so = cb + ue * w0 + uo * w1 + ue_next * w2
        return se, so

    x_streams = [short_conv(u_refs[0], cw0_ref, cb0_ref), short_conv(u_refs[1], cw1_ref, cb1_ref)]
    ze, zo = short_conv(u_refs[2], cw2_ref, cb2_ref)
    for o in range(HY_ORDER):
        ce, co = _long_conv(ze, zo, har_ref[o], hai_ref[o], hbr_ref[o], hbi_ref[o], hny_ref[o],
                            cm, sm, wr, wi, sgn)
        sk = skip_ref[o:o + 1]
        xe, xo = x_streams[o]
        ze = xe * (ce + sk * ze)
        zo = xo * (co + sk * zo)
    for j, il in enumerate(il_refs):
        il[pl.ds(0, h, stride=2), :] = ze[:, j * 128:(j + 1) * 128]
        il[pl.ds(1, h, stride=2), :] = zo[:, j * 128:(j + 1) * 128]
        o_ref[0, :, j * 128:(j + 1) * 128] = il[...]


def _hyena(u, conv_w, conv_b, skip, spectra, tabs, wc):
    b, l, _ = u.shape
    har, hai, hbr, hbi, hny = spectra
    cm, sm, wr, wi, sgn, _ = tabs
    h = l // 2
    nblk = HY_WIDTH // wc
    nsub = wc // 128
    conv_b = conv_b.reshape(1, -1)

    def ublk(s, j):
        return pl.BlockSpec((1, l, 128), lambda cb, bi: (bi, 0, (cb + s * nblk) * nsub + j))

    def wblk(r, s):
        return pl.BlockSpec((r, wc), lambda cb, bi: (0, cb + s * nblk))

    const = lambda cb, bi: (0, 0)
    spec = pl.BlockSpec((HY_ORDER, h, wc), lambda cb, bi: (0, 0, cb))
    return pl.pallas_call(
        _hyena_kernel,
        grid=(nblk, b),
        in_specs=[ublk(s, j) for s in range(HY_ORDER + 1) for j in range(nsub)] + [
                  wblk(HY_SHORT_CONV, 0), wblk(HY_SHORT_CONV, 1), wblk(HY_SHORT_CONV, 2),
                  wblk(1, 0), wblk(1, 1), wblk(1, 2),
                  pl.BlockSpec((HY_ORDER, wc), lambda cb, bi: (0, cb)),
                  spec, spec, spec, spec,
                  pl.BlockSpec((HY_ORDER, 8, wc), lambda cb, bi: (0, 0, cb)),
                  pl.BlockSpec((h, h), const), pl.BlockSpec((h, h), const),
                  pl.BlockSpec((h, wc), const), pl.BlockSpec((h, wc), const),
                  pl.BlockSpec((h, wc), const)],
        out_specs=pl.BlockSpec((1, l, wc), lambda cb, bi: (bi, 0, cb)),
        out_shape=jax.ShapeDtypeStruct((b, l, HY_WIDTH), F32),
        scratch_shapes=[pltpu.VMEM((l, 128), F32) for _ in range(nsub)],
        compiler_params=_cparams(("arbitrary", "arbitrary")),
        name="hyena_long_conv",
    )(*([u] * ((HY_ORDER + 1) * nsub)), conv_w, conv_w, conv_w, conv_b, conv_b, conv_b, skip,
      har, hai, hbr, hbi, hny, cm, sm, wr, wi, sgn)


def _merge_kernel(alpha, att_ref, hy_ref, gate_ref, x_ref, g1_ref, sh2_ref, sc2_ref,
                  wba_ref, wbh_ref, wo_ref, l1g_ref, l1b_ref, rw_ref, rb_ref,
                  xmid_ref, h2_ref, logit_ref):
    d = x_ref.shape[2]
    a = _bdot(att_ref[0], wba_ref[...])
    hh = _bdot(hy_ref[0].astype(BF16), wbh_ref[...])
    ga = jax.nn.sigmoid(gate_ref[0, :, :d])
    gh = jax.nn.sigmoid(gate_ref[0, :, d:])
    y = _bdot((ga * a + gh * hh).astype(BF16), wo_ref[...])
    xm = _ln(alpha * x_ref[0] + g1_ref[0] * y) * l1g_ref[...] + l1b_ref[...]
    xmid_ref[0] = xm
    h2 = _ln(xm) * (1.0 + sc2_ref[0]) + sh2_ref[0]
    h2_ref[0] = h2
    logit_ref[0] = _fdot(h2, rw_ref[...]) + rb_ref[...]


def _merge(att, hy, gate, x, g1, sh2, sc2, wba, wbh, wo, l1g, l1b, rw, rb, alpha, tm=256):
    b, l, d = x.shape
    ne = rw.shape[1]
    row = lambda bi, i: (bi, i, 0)
    vec = lambda bi, i: (bi, 0, 0)
    const = lambda bi, i: (0, 0)
    return pl.pallas_call(
        functools.partial(_merge_kernel, alpha),
        grid=(b, l // tm),
        in_specs=[pl.BlockSpec((1, tm, Q_W), row),
                  pl.BlockSpec((1, tm, HY_WIDTH), row),
                  pl.BlockSpec((1, tm, 2 * d), row),
                  pl.BlockSpec((1, tm, d), row),
                  pl.BlockSpec((1, 1, d), vec), pl.BlockSpec((1, 1, d), vec),
                  pl.BlockSpec((1, 1, d), vec),
                  pl.BlockSpec((Q_W, d), const), pl.BlockSpec((HY_WIDTH, d), const),
                  pl.BlockSpec((d, d), const),
                  pl.BlockSpec((1, d), const), pl.BlockSpec((1, d), const),
                  pl.BlockSpec((d, ne), const), pl.BlockSpec((1, ne), const)],
        out_specs=[pl.BlockSpec((1, tm, d), row), pl.BlockSpec((1, tm, d), row),
                   pl.BlockSpec((1, tm, ne), row)],
        out_shape=[jax.ShapeDtypeStruct((b, l, d), F32),
                   jax.ShapeDtypeStruct((b, l, d), F32),
                   jax.ShapeDtypeStruct((b, l, ne), F32)],
        compiler_params=_cparams(("arbitrary", "arbitrary")),
        name="merge_ln_router",
    )(att, hy, gate, x, g1, sh2, sc2, wba, wbh, wo, l1g.reshape(1, d), l1b.reshape(1, d),
      rw, rb.reshape(1, ne))


def _route_kernel(logit_ref, eid_ref, gate_ref, rank_ref, cnt_ref, run_ref):
    @pl.when(pl.program_id(0) == 0)
    def _():
        run_ref[...] = jnp.zeros_like(run_ref)

    lg = logit_ref[...]
    tr, ne = lg.shape
    lane = lax.broadcasted_iota(jnp.int32, (tr, ne), 1)
    work = lg
    vals, idxs, hots = [], [], []
    for _ in range(TOP_K):
        m = jnp.max(work, axis=1, keepdims=True)
        idx = jnp.min(jnp.where(work == m, lane, ne), axis=1, keepdims=True)
        hot = lane == idx
        vals.append(m)
        idxs.append(idx)
        hots.append(hot)
        work = jnp.where(hot, -jnp.inf, work)
    exps = [jnp.exp(v - vals[0]) for v in vals]
    den = exps[0] + exps[1] + exps[2] + exps[3]
    member = jnp.zeros((tr, ne), F32)
    for hot in hots:
        member = member + jnp.where(hot, 1.0, 0.0)
    r_i = lax.broadcasted_iota(jnp.int32, (tr, tr), 0)
    c_i = lax.broadcasted_iota(jnp.int32, (tr, tr), 1)
    tri = jnp.where(r_i > c_i, 1.0, 0.0).astype(BF16)
    base = run_ref[...] + _bdot(tri, member.astype(BF16))
    lane_k = lax.broadcasted_iota(jnp.int32, (tr, TOP_K), 1)
    eid = jnp.zeros((tr, TOP_K), jnp.int32)
    gate = jnp.zeros((tr, TOP_K), F32)
    rank = jnp.zeros((tr, TOP_K), F32)
    for k in range(TOP_K):
        rk = jnp.sum(jnp.where(hots[k], base, 0.0), axis=1, keepdims=True)
        eid = jnp.where(lane_k == k, idxs[k], eid)
        gate = jnp.where(lane_k == k, exps[k] / den, gate)
        rank = jnp.where(lane_k == k, rk, rank)
    eid_ref[...] = eid
    gate_ref[...] = gate
    rank_ref[...] = rank.astype(jnp.int32)
    run_ref[...] = run_ref[...] + jnp.sum(member, axis=0, keepdims=True)
    cnt_ref[...] = run_ref[...].astype(jnp.int32)


def _route(logits, tr=512):
    t, ne = logits.shape
    tok = lambda i: (i, 0)
    return pl.pallas_call(
        _route_kernel,
        grid=(t // tr,),
        in_specs=[pl.BlockSpec((tr, ne), tok)],
        out_specs=[pl.BlockSpec((tr, TOP_K), tok), pl.BlockSpec((tr, TOP_K), tok),
                   pl.BlockSpec((tr, TOP_K), tok), pl.BlockSpec((1, ne), lambda i: (0, 0))],
        out_shape=[jax.ShapeDtypeStruct((t, TOP_K), jnp.int32),
                   jax.ShapeDtypeStruct((t, TOP_K), F32),
                   jax.ShapeDtypeStruct((t, TOP_K), jnp.int32),
                   jax.ShapeDtypeStruct((1, ne), jnp.int32)],
        scratch_shapes=[pltpu.VMEM((1, ne), F32)],
        compiler_params=_cparams(("arbitrary",)),
        name="route_topk",
    )(logits)


def _dispatch_kernel(dest_ref, h_ref, xs_ref, sem):
    i = pl.program_id(0)
    n = dest_ref.shape[2]
    td = n // TOP_K

    def body(t, carry):
        src = h_ref.at[pl.ds(i * td + t, 1)]
        for k in range(TOP_K):
            pltpu.make_async_copy(src, xs_ref.at[pl.ds(dest_ref[0, 0, t * TOP_K + k], 1)], sem).start()
        return carry

    lax.fori_loop(0, td, body, 0)
    pltpu.make_async_copy(h_ref.at[pl.ds(0, n)], xs_ref.at[pl.ds(0, n)], sem).wait()


def _dispatch(h2, dest, td=256):
    t, d = h2.shape
    a = t * TOP_K
    n = td * TOP_K
    return pl.pallas_call(
        _dispatch_kernel,
        grid=(t // td,),
        in_specs=[pl.BlockSpec((1, 1, n), lambda i: (i, 0, 0), memory_space=pltpu.SMEM),
                  pl.BlockSpec(memory_space=pl.ANY)],
        out_specs=pl.BlockSpec(memory_space=pl.ANY),
        out_shape=jax.ShapeDtypeStruct((a, d), h2.dtype),
        scratch_shapes=[pltpu.SemaphoreType.DMA(())],
        compiler_params=_cparams(("arbitrary",)),
        name="moe_dispatch",
    )(dest.reshape(t // td, 1, n), h2)


def _expert_kernel(tile_ref, exp_ref, lo_ref, hi_ref, first_ref, nitem_ref,
                   xs_ref, w1_ref, b1_ref, w2_ref, b2_ref, ys_ref):
    w = pl.program_id(0)
    dff = w2_ref.shape[1]

    @pl.when(w < nitem_ref[0])
    def _():
        x = xs_ref[...].astype(BF16)
        hb = _bdot(x, w1_ref[0]) + b1_ref[0]
        glu = jnp.minimum(hb[:, :dff], SWIGLU_LIMIT)
        lin = jnp.clip(hb[:, dff:], -SWIGLU_LIMIT, SWIGLU_LIMIT)
        act = glu * jax.nn.sigmoid(SWIGLU_ALPHA * glu) * (lin + 1.0)
        y = _bdot(act.astype(BF16), w2_ref[0]) + b2_ref[0]
        row = lax.broadcasted_iota(jnp.int32, y.shape, 0)
        y = jnp.where((row >= lo_ref[w]) & (row < hi_ref[w]), y, 0.0)

        @pl.when(first_ref[w] == 1)
        def _():
            ys_ref[...] = y

        @pl.when(first_ref[w] == 0)
        def _():
            ys_ref[...] = ys_ref[...] + y


def _experts(xs, items, w1b, b1, w2b, b2, tm):
    a, d = xs.shape
    ne, _, dff2 = w1b.shape
    dff = dff2 // 2
    n_items = a // tm + ne - 1
    tile_w, exp_w, lo_w, hi_w, first_w, nitem = items
    grid_spec = pltpu.PrefetchScalarGridSpec(
        num_scalar_prefetch=6,
        grid=(n_items,),
        in_specs=[pl.BlockSpec((tm, d), lambda w, tl, ex, lo, hi, fi, ni: (tl[w], 0)),
                  pl.BlockSpec((1, d, dff2), lambda w, tl, ex, lo, hi, fi, ni: (ex[w], 0, 0)),
                  pl.BlockSpec((1, 1, dff2), lambda w, tl, ex, lo, hi, fi, ni: (ex[w], 0, 0)),
                  pl.BlockSpec((1, dff, d), lambda w, tl, ex, lo, hi, fi, ni: (ex[w], 0, 0)),
                  pl.BlockSpec((1, 1, d), lambda w, tl, ex, lo, hi, fi, ni: (ex[w], 0, 0))],
        out_specs=pl.BlockSpec((tm, d), lambda w, tl, ex, lo, hi, fi, ni: (tl[w], 0)),
    )
    return pl.pallas_call(
        _expert_kernel,
        grid_spec=grid_spec,
        out_shape=jax.ShapeDtypeStruct((a, d), F32),
        compiler_params=_cparams(("arbitrary",)),
        name="moe_experts",
    )(tile_w, exp_w, lo_w, hi_w, first_w, nitem,
      xs, w1b, b1.reshape(ne, 1, dff2), w2b, b2.reshape(ne, 1, d))


def _work_items(counts, a, tm):
    ne = counts.shape[0]
    n_items = a // tm + ne - 1
    ends = jnp.cumsum(counts)
    starts = ends - counts
    first_tile = starts // tm
    last_tile = (ends - 1) // tm
    nvis = jnp.where(counts > 0, last_tile - first_tile + 1, 0)
    vis_end = jnp.cumsum(nvis)
    vis_start = vis_end - nvis
    total = vis_end[-1]
    w = jnp.minimum(jnp.arange(n_items, dtype=jnp.int32), total - 1)
    e_w = jnp.searchsorted(vis_end, w, side='right').astype(jnp.int32)
    tile_w = (first_tile[e_w] + (w - vis_start[e_w])).astype(jnp.int32)
    lo_w = jnp.maximum(starts[e_w] - tile_w * tm, 0).astype(jnp.int32)
    hi_w = jnp.minimum(ends[e_w] - tile_w * tm, tm).astype(jnp.int32)
    prev = jnp.concatenate([jnp.full((1,), -1, jnp.int32), tile_w[:-1]])
    first_w = (tile_w != prev).astype(jnp.int32)
    return starts, (tile_w, e_w, lo_w, hi_w, first_w, total.reshape(1).astype(jnp.int32))


def _combine_kernel(alpha, dest_ref, ys_ref, gate_ref, xm_ref, g2_ref, lg_ref, lb_ref,
                    o_ref, buf_ref, sem):
    tc = xm_ref.shape[1]

    def body(t, carry):
        for k in range(TOP_K):
            pltpu.make_async_copy(ys_ref.at[pl.ds(dest_ref[0, 0, t * TOP_K + k], 1)],
                                  buf_ref.at[k, pl.ds(t, 1)], sem).start()
        return carry

    lax.fori_loop(0, tc, body, 0)
    for k in range(TOP_K):
        pltpu.make_async_copy(ys_ref.at[pl.ds(0, tc)], buf_ref.at[k], sem).wait()
    gate = gate_ref[0]
    f = buf_ref[0] * gate[:, 0:1]
    for k in range(1, TOP_K):
        f = f + buf_ref[k] * gate[:, k:k + 1]
    o_ref[0] = _ln(alpha * xm_ref[0] + g2_ref[0] * f) * lg_ref[...] + lb_ref[...]


def _combine(ys, dest, gates, x_mid, g2, lg, lb, alpha, tc=256):
    b, l, d = x_mid.shape
    n = tc * TOP_K
    nt = l // tc
    row = lambda bi, i: (bi, i, 0)
    const = lambda bi, i: (0, 0)
    return pl.pallas_call(
        functools.partial(_combine_kernel, alpha),
        grid=(b, nt),
        in_specs=[pl.BlockSpec((1, 1, n), lambda bi, i: (bi * nt + i, 0, 0), memory_space=pltpu.SMEM),
                  pl.BlockSpec(memory_space=pl.ANY),
                  pl.BlockSpec((1, tc, TOP_K), row),
                  pl.BlockSpec((1, tc, d), row),
                  pl.BlockSpec((1, 1, d), lambda bi, i: (bi, 0, 0)),
                  pl.BlockSpec((1, d), const), pl.BlockSpec((1, d), const)],
        out_specs=pl.BlockSpec((1, tc, d), row),
        out_shape=jax.ShapeDtypeStruct((b, l, d), F32),
        scratch_shapes=[pltpu.VMEM((TOP_K, tc, d), F32), pltpu.SemaphoreType.DMA(())],
        compiler_params=_cparams(("arbitrary", "arbitrary")),
        name="moe_combine_ln",
    )(dest.reshape(b * nt, 1, n), ys, gates.reshape(b, l, TOP_K), x_mid, g2,
      lg.reshape(1, d), lb.reshape(1, d))


def _rope_tables(l):
    rows = l // GRID_W
    row = jnp.repeat(jnp.arange(rows, dtype=F32), GRID_W)
    col = jnp.tile(jnp.arange(GRID_W, dtype=F32), rows)
    n_freq = HEAD_DIM // 4
    inv_freq = ROPE_BASE ** (-jnp.arange(n_freq, dtype=F32) / n_freq)
    ang_r = row[:, None] * inv_freq
    ang_c = col[:, None] * inv_freq
    zero = jnp.zeros_like(ang_r)
    cos_h = jnp.concatenate([jnp.cos(ang_r), jnp.cos(ang_r), jnp.cos(ang_c), jnp.cos(ang_c)], axis=1)
    sa_h = jnp.concatenate([-jnp.sin(ang_r), zero, -jnp.sin(ang_c), zero], axis=1)
    sb_h = jnp.concatenate([zero, jnp.sin(ang_r), zero, jnp.sin(ang_c)], axis=1)
    rep = 128 // HEAD_DIM
    return jnp.tile(cos_h, (1, rep)), jnp.tile(sa_h, (1, rep)), jnp.tile(sb_h, (1, rep))


def _filter_features(l):
    bands = (HY_EMB_DIM - 1) // 2
    t = jnp.linspace(0.0, 1.0, l, dtype=F32)[:, None]
    omega = 2.0 * math.pi * jnp.arange(l, dtype=F32)[:, None] / l
    f = jnp.linspace(1e-4, bands - 1, bands, dtype=F32)[None, :]
    z = jnp.concatenate([t, jnp.cos(f * omega), -jnp.sin(f * omega)], axis=-1)
    min_decay = math.log(HY_DECAY_TARGET) / HY_FAST_DECAY_PCT
    max_decay = math.log(HY_DECAY_TARGET) / HY_SLOW_DECAY_PCT
    deltas = jnp.abs(jnp.linspace(min_decay, max_decay, HY_WIDTH, dtype=F32))
    decay = jnp.exp(-t * deltas)
    return z, decay


def kernel(x, c, ctx, c_ctx, w_mod, b_mod, w_in, attn_sink, hy_conv_w, hy_conv_b, hy_filt_w1,
           hy_filt_b1, hy_filt_w2, hy_filt_b2, hy_filt_w3, hy_skip, w_branch_attn, w_branch_hyena,
           w_out, ln1_g, ln1_b, router_w, router_b, exp_w1, exp_b1, exp_w2, exp_b2, ln2_g, ln2_b):
    depth = w_mod.shape[0]
    assert depth == 1, "only the single-layer configuration is implemented"
    b, l, d = x.shape
    t = b * l
    alpha = (2 * depth) ** 0.25
    hy_wc = 256
    expert_tm = 256

    n_cond = b + 1
    pad = (-n_cond) % 8
    cond = jnp.concatenate([c, c_ctx[None], jnp.zeros((pad, d), F32)], axis=0)
    mod = _modulation(cond, w_mod[0], b_mod[0])
    mod_x = mod[:b].reshape(b, 1, 6, d)
    sh1, sc1, g1, sh2, sc2, g2 = (mod_x[:, :, i] for i in range(6))
    mod_c = mod[b:b + 1].reshape(1, 1, 6, d)
    csh1, csc1 = mod_c[:, :, 0], mod_c[:, :, 1]

    w_in_b = w_in[0].astype(BF16)
    cos_t, sa_t, sb_t = _rope_tables(l)
    q, k, v, u_hy, gate_x = _in_projection(x, sh1, sc1, w_in_b, cos_t, sa_t, sb_t)
    k_c, v_c = _ctx_kv(ctx, csh1, csc1, w_in_b[:, K_OFF:HY_OFF])
    att = _attention(q, k, v, k_c, v_c, attn_sink[0])

    tabs_np = _dft_tables(l, hy_wc)
    tabs = (jnp.asarray(tabs_np[0]).astype(BF16), jnp.asarray(tabs_np[1]).astype(BF16)) + tuple(
        jnp.asarray(a) for a in tabs_np[2:])
    zfeat, decay = _filter_features(l)
    emb_pad = (-HY_EMB_DIM) % 128
    zfeat = jnp.pad(zfeat, ((0, 0), (0, emb_pad)))
    fw1 = jnp.pad(hy_filt_w1[0], ((0, emb_pad), (0, 0)))
    spectra = _hyena_filters(zfeat[0::2], zfeat[1::2], fw1, hy_filt_b1[0], hy_filt_w2[0],
                             hy_filt_b2[0], hy_filt_w3[0], decay[0::2], decay[1::2], tabs, hy_wc)
    hy = _hyena(u_hy, hy_conv_w[0], hy_conv_b[0], hy_skip[0], spectra, tabs, hy_wc)

    x_mid, h2, logits = _merge(att, hy, gate_x, x, g1, sh2, sc2,
                               w_branch_attn[0].astype(BF16), w_branch_hyena[0].astype(BF16),
                               w_out[0].astype(BF16), ln1_g[0], ln1_b[0], router_w[0], router_b[0],
                               alpha)

    eid, gates, rank, counts = _route(logits.reshape(t, N_EXPERTS))
    starts, items = _work_items(counts[0], t * TOP_K, expert_tm)
    expert_iota = jnp.arange(N_EXPERTS, dtype=jnp.int32)
    dest = rank + jnp.sum(jnp.where(eid[..., None] == expert_iota, starts, 0), axis=-1)
    xs = _dispatch(h2.reshape(t, d), dest)
    ys = _experts(xs, items, exp_w1[0].astype(BF16), exp_b1[0], exp_w2[0].astype(BF16), exp_b2[0],
                  expert_tm)
    return _combine(ys, dest, gates, x_mid, g2, ln2_g[0], ln2_b[0], alpha)
```

```python
import functools
import math

import numpy as np
import jax
import jax.numpy as jnp
from jax import lax
from jax.experimental import pallas as pl
from jax.experimental.pallas import tpu as pltpu

F32 = jnp.float32
BF16 = jnp.bfloat16
HIGHEST = lax.Precision.HIGHEST

GRID_W = 64
N_HEADS = 8
N_KV_HEADS = 2
GQA_GROUP = N_HEADS // N_KV_HEADS
HEAD_DIM = 64
WINDOW = 128
ATTN_BLOCK = 128
ROPE_BASE = 10000.0

HY_WIDTH = 512
HY_ORDER = 2
HY_SHORT_CONV = 3
HY_EMB_DIM = 33
HY_DECAY_TARGET = 1e-2
HY_FAST_DECAY_PCT = 0.3
HY_SLOW_DECAY_PCT = 1.5

N_EXPERTS = 32
TOP_K = 4
SWIGLU_LIMIT = 7.0
SWIGLU_ALPHA = 1.702
LN_EPS = 1e-5

Q_W = N_HEADS * HEAD_DIM
KV_W = N_KV_HEADS * HEAD_DIM
K_OFF = Q_W
V_OFF = K_OFF + KV_W
HY_OFF = V_OFF + KV_W
GATE_OFF = HY_OFF + (HY_ORDER + 1) * HY_WIDTH

VMEM_LIMIT = 56 * 1024 * 1024
NEG_BIG = -1e30


def _cparams(sem):
    return pltpu.CompilerParams(dimension_semantics=sem, vmem_limit_bytes=VMEM_LIMIT)


def _ln(x):
    mu = jnp.mean(x, axis=-1, keepdims=True)
    xc = x - mu
    var = jnp.mean(xc * xc, axis=-1, keepdims=True)
    return xc * lax.rsqrt(var + LN_EPS)


def _bdot(a, b):
    return jnp.dot(a, b, preferred_element_type=F32)


def _fdot(a, b):
    return jnp.dot(a, b, preferred_element_type=F32, precision=HIGHEST)


def _mod_kernel(c_ref, w_ref, b_ref, o_ref):
    c = c_ref[...]
    s = c * jax.nn.sigmoid(c)
    o_ref[...] = _fdot(s, w_ref[...]) + b_ref[...]


def _modulation(cond, w, b, tn=512):
    r, d = cond.shape
    n = w.shape[1]
    return pl.pallas_call(
        _mod_kernel,
        grid=(n // tn,),
        in_specs=[pl.BlockSpec((r, d), lambda j: (0, 0)),
                  pl.BlockSpec((d, tn), lambda j: (0, j)),
                  pl.BlockSpec((1, tn), lambda j: (0, j))],
        out_specs=pl.BlockSpec((r, tn), lambda j: (0, j)),
        out_shape=jax.ShapeDtypeStruct((r, n), F32),
        compiler_params=_cparams(("arbitrary",)),
        name="modulation",
    )(cond, w, b.reshape(1, n))


def _rope(t, cos, sa, sb):
    n = t.shape[-1]
    return t * cos + pltpu.roll(t, n - 16, 1) * sa + pltpu.roll(t, 16, 1) * sb


def _inproj_kernel(x_ref, sh_ref, sc_ref, w_ref, cos_ref, sa_ref, sb_ref,
                   q_ref, k_ref, v_ref, u_ref, g_ref):
    h = _ln(x_ref[0]) * (1.0 + sc_ref[0]) + sh_ref[0]
    hb = h.astype(BF16)
    cos, sa, sb = cos_ref[...], sa_ref[...], sb_ref[...]
    scale = HEAD_DIM ** -0.5
    for j in range(Q_W // 128):
        t = _bdot(hb, w_ref[:, j * 128:(j + 1) * 128])
        q_ref[0, :, j * 128:(j + 1) * 128] = (_rope(t, cos, sa, sb) * scale).astype(BF16)
    t = _bdot(hb, w_ref[:, K_OFF:V_OFF])
    k_ref[0] = _rope(t, cos, sa, sb).astype(BF16)
    v_ref[0] = _bdot(hb, w_ref[:, V_OFF:HY_OFF]).astype(BF16)
    for j in range((GATE_OFF - HY_OFF) // 512):
        u_ref[0, :, j * 512:(j + 1) * 512] = _bdot(hb, w_ref[:, HY_OFF + j * 512:HY_OFF + (j + 1) * 512])
    n_gate = w_ref.shape[1] - GATE_OFF
    for j in range(n_gate // 512):
        g_ref[0, :, j * 512:(j + 1) * 512] = _bdot(
            hb, w_ref[:, GATE_OFF + j * 512:GATE_OFF + (j + 1) * 512])


def _in_projection(x, sh, sc, w_in_b, cos_t, sa_t, sb_t, tm=256):
    b, l, d = x.shape
    in_w = w_in_b.shape[1]
    hy_w = GATE_OFF - HY_OFF
    g_w = in_w - GATE_OFF
    row = lambda bi, i: (bi, i, 0)
    vec = lambda bi, i: (bi, 0, 0)
    tab = lambda bi, i: (i, 0)
    return pl.pallas_call(
        _inproj_kernel,
        grid=(b, l // tm),
        in_specs=[pl.BlockSpec((1, tm, d), row),
                  pl.BlockSpec((1, 1, d), vec),
                  pl.BlockSpec((1, 1, d), vec),
                  pl.BlockSpec((d, in_w), lambda bi, i: (0, 0)),
                  pl.BlockSpec((tm, 128), tab),
                  pl.BlockSpec((tm, 128), tab),
                  pl.BlockSpec((tm, 128), tab)],
        out_specs=[pl.BlockSpec((1, tm, Q_W), row),
                   pl.BlockSpec((1, tm, KV_W), row),
                   pl.BlockSpec((1, tm, KV_W), row),
                   pl.BlockSpec((1, tm, hy_w), row),
                   pl.BlockSpec((1, tm, g_w), row)],
        out_shape=[jax.ShapeDtypeStruct((b, l, Q_W), BF16),
                   jax.ShapeDtypeStruct((b, l, KV_W), BF16),
                   jax.ShapeDtypeStruct((b, l, KV_W), BF16),
                   jax.ShapeDtypeStruct((b, l, hy_w), F32),
                   jax.ShapeDtypeStruct((b, l, g_w), F32)],
        compiler_params=_cparams(("arbitrary", "arbitrary")),
        name="in_projection",
    )(x, sh, sc, w_in_b, cos_t, sa_t, sb_t)


def _ctx_kv_kernel(x_ref, sh_ref, sc_ref, w_ref, k_ref, v_ref):
    h = _ln(x_ref[0]) * (1.0 + sc_ref[0]) + sh_ref[0]
    kv = _bdot(h.astype(BF16), w_ref[...])
    k_ref[0] = kv[:, :KV_W].astype(BF16)
    v_ref[0] = kv[:, KV_W:].astype(BF16)


def _ctx_kv(ctx, sh, sc, w_kv_b):
    b, c, d = ctx.shape
    row = lambda bi: (bi, 0, 0)
    return pl.pallas_call(
        _ctx_kv_kernel,
        grid=(b,),
        in_specs=[pl.BlockSpec((1, c, d), row),
                  pl.BlockSpec((1, 1, d), lambda bi: (0, 0, 0)),
                  pl.BlockSpec((1, 1, d), lambda bi: (0, 0, 0)),
                  pl.BlockSpec((d, 2 * KV_W), lambda bi: (0, 0))],
        out_specs=[pl.BlockSpec((1, c, KV_W), row), pl.BlockSpec((1, c, KV_W), row)],
        out_shape=[jax.ShapeDtypeStruct((b, c, KV_W), BF16),
                   jax.ShapeDtypeStruct((b, c, KV_W), BF16)],
        compiler_params=_cparams(("arbitrary",)),
        name="ctx_kv",
    )(ctx, sh, sc, w_kv_b)


def _attn_kernel(sink_ref, q_ref, k_ref, v_ref, kc_ref, vc_ref, o_ref):
    l = k_ref.shape[1]
    span = ATTN_BLOCK + 2 * WINDOW
    j = pl.program_id(1)
    q0 = j * ATTN_BLOCK
    start = pl.multiple_of(jnp.clip(q0 - WINDOW, 0, l - span), ATTN_BLOCK)
    qpos = q0 + lax.broadcasted_iota(jnp.int32, (ATTN_BLOCK, span), 0)
    kpos = start + lax.broadcasted_iota(jnp.int32, (ATTN_BLOCK, span), 1)
    band = jnp.abs(kpos - qpos) <= WINDOW
    dn = (((1,), (1,)), ((), ()))
    outs = []
    for kv in range(N_KV_HEADS):
        ks = slice(kv * HEAD_DIM, (kv + 1) * HEAD_DIM)
        kl = k_ref[0, pl.ds(start, span), ks]
        vl = v_ref[0, pl.ds(start, span), ks]
        kc = kc_ref[0, :, ks]
        vc = vc_ref[0, :, ks]
        for g in range(GQA_GROUP):
            h = kv * GQA_GROUP + g
            qh = q_ref[0, :, h * HEAD_DIM:(h + 1) * HEAD_DIM]
            s_loc = lax.dot_general(qh, kl, dn, preferred_element_type=F32)
            s_loc = jnp.where(band, s_loc, NEG_BIG)
            s_ctx = lax.dot_general(qh, kc, dn, preferred_element_type=F32)
            sink = sink_ref[h]
            m = jnp.maximum(jnp.max(s_loc, axis=1, keepdims=True),
                            jnp.max(s_ctx, axis=1, keepdims=True))
            m = jnp.maximum(m, sink)
            p_loc = jnp.exp(s_loc - m)
            p_ctx = jnp.exp(s_ctx - m)
            den = (jnp.sum(p_loc, axis=1, keepdims=True) + jnp.sum(p_ctx, axis=1, keepdims=True)
                   + jnp.exp(sink - m))
            o = _bdot(p_loc.astype(BF16), vl) + _bdot(p_ctx.astype(BF16), vc)
            outs.append(o / den)
    o_ref[0] = jnp.concatenate(outs, axis=1).astype(BF16)


def _attention(q, k, v, kc, vc, sink):
    b, l, _ = q.shape
    c = kc.shape[1]
    full = lambda bi, j, s: (bi, 0, 0)
    grid_spec = pltpu.PrefetchScalarGridSpec(
        num_scalar_prefetch=1,
        grid=(b, l // ATTN_BLOCK),
        in_specs=[pl.BlockSpec((1, ATTN_BLOCK, Q_W), lambda bi, j, s: (bi, j, 0)),
                  pl.BlockSpec((1, l, KV_W), full),
                  pl.BlockSpec((1, l, KV_W), full),
                  pl.BlockSpec((1, c, KV_W), full),
                  pl.BlockSpec((1, c, KV_W), full)],
        out_specs=pl.BlockSpec((1, ATTN_BLOCK, Q_W), lambda bi, j, s: (bi, j, 0)),
    )
    return pl.pallas_call(
        _attn_kernel,
        grid_spec=grid_spec,
        out_shape=jax.ShapeDtypeStruct((b, l, Q_W), BF16),
        compiler_params=_cparams(("arbitrary", "arbitrary")),
        name="window_attention",
    )(sink, q, k, v, kc, vc)


def _dft_tables(l, wc):
    h = l // 2
    idx = np.arange(h, dtype=np.int64)
    ang = 2.0 * np.pi * ((idx[:, None] * idx[None, :]) % l).astype(np.float64) / l
    cm = np.cos(ang).astype(np.float32)
    sm = np.sin(ang).astype(np.float32)
    tw = 2.0 * np.pi * idx.astype(np.float64) / (2 * l)
    ones = np.ones((1, wc), np.float32)
    wr = np.cos(tw).astype(np.float32)[:, None] * ones
    wi = (-np.sin(tw)).astype(np.float32)[:, None] * ones
    sgn = np.where(idx % 2 == 0, 1.0, -1.0).astype(np.float32)[:, None] * ones
    wk = np.where(idx == 0, 1.0 / (2 * l), 2.0 / (2 * l)).astype(np.float32)[:, None] * ones
    return cm, sm, wr, wi, sgn, wk


def _half_spectrum(se, so, cm, sm, wr, wi, sgn):
    seb, sob = se.astype(BF16), so.astype(BF16)
    ce, ss_e = _bdot(cm, seb), _bdot(sm, seb)
    co, ss_o = _bdot(cm, sob), _bdot(sm, sob)
    vr = wr * co + wi * ss_o
    vi = wi * co - wr * ss_o
    zar, zai = ce + vr, vi - ss_e
    zbr, zbi = ce - vr, -ss_e - vi
    e_ny = jnp.sum(sgn * se, axis=0, keepdims=True)
    o_ny = jnp.sum(sgn * so, axis=0, keepdims=True)
    return zar, zai, zbr, zbi, e_ny, -o_ny


def _filter_kernel(ze_ref, zo_ref, w1_ref, b1_ref, w2_ref, b2_ref, w3f_ref, w3b_ref,
                   de_ref, do_ref, cm_ref, sm_ref, wr_ref, wi_ref, sgn_ref, wk_ref,
                   har_ref, hai_ref, hbr_ref, hbi_ref, hny_ref):
    cm, sm = cm_ref[...], sm_ref[...]
    wr, wi, sgn, wk = wr_ref[...], wi_ref[...], sgn_ref[...], wk_ref[...]
    h = cm.shape[0]

    def taps(z_ref, w3_ref, d_ref):
        a = jnp.sin(_fdot(z_ref[...], w1_ref[...]) + b1_ref[...])
        a = jnp.sin(_fdot(a, w2_ref[...]) + b2_ref[...])
        return _fdot(a, w3_ref[0]) * d_ref[...]

    fe, fo = taps(ze_ref, w3f_ref, de_ref), taps(zo_ref, w3f_ref, do_ref)
    be, bo = taps(ze_ref, w3b_ref, de_ref), taps(zo_ref, w3b_ref, do_ref)
    row = lax.broadcasted_iota(jnp.int32, be.shape, 0)
    be = jnp.where(row == 0, 0.0, be)
    far, fai, fbr, fbi, fnr, fni = _half_spectrum(fe, fo, cm, sm, wr, wi, sgn)
    bar, bai, bbr, bbi, bnr, bni = _half_spectrum(be, bo, cm, sm, wr, wi, sgn)
    har_ref[0] = wk * (far + bar)
    hai_ref[0] = wk * (fai - bai)
    hbr_ref[0] = wk * (fbr + bbr)
    hbi_ref[0] = wk * (fbi - bbi)
    ny_scale = 2.0 / (4 * h)
    nr = ny_scale * (fnr + bnr)
    ni = ny_scale * (fni - bni)
    rows = lax.broadcasted_iota(jnp.int32, (8, nr.shape[1]), 0)
    hny_ref[0] = jnp.where(rows == 0, nr, jnp.where(rows == 1, ni, 0.0))


def _hyena_filters(zfe, zfo, w1, b1, w2, b2, w3, dec_e, dec_o, tabs, wc):
    cm, sm, wr, wi, sgn, wk = tabs
    h = cm.shape[0]
    emb = zfe.shape[1]
    hid = w2.shape[0]
    nblk = HY_WIDTH // wc
    w3r = w3.reshape(hid, HY_ORDER * 2, HY_WIDTH).transpose(1, 0, 2)
    const = lambda o, cb: (0, 0)
    chan = lambda o, cb: (0, cb)
    out_spec = pl.BlockSpec((1, h, wc), lambda o, cb: (o, 0, cb))
    out_shape = jax.ShapeDtypeStruct((HY_ORDER, h, HY_WIDTH), F32)
    return pl.pallas_call(
        _filter_kernel,
        grid=(HY_ORDER, nblk),
        in_specs=[pl.BlockSpec((h, emb), const), pl.BlockSpec((h, emb), const),
                  pl.BlockSpec((emb, hid), const), pl.BlockSpec((1, hid), const),
                  pl.BlockSpec((hid, hid), const), pl.BlockSpec((1, hid), const),
                  pl.BlockSpec((1, hid, wc), lambda o, cb: (2 * o, 0, cb)),
                  pl.BlockSpec((1, hid, wc), lambda o, cb: (2 * o + 1, 0, cb)),
                  pl.BlockSpec((h, wc), chan), pl.BlockSpec((h, wc), chan),
                  pl.BlockSpec((h, h), const), pl.BlockSpec((h, h), const),
                  pl.BlockSpec((h, wc), const), pl.BlockSpec((h, wc), const),
                  pl.BlockSpec((h, wc), const), pl.BlockSpec((h, wc), const)],
        out_specs=[out_spec, out_spec, out_spec, out_spec,
                   pl.BlockSpec((1, 8, wc), lambda o, cb: (o, 0, cb))],
        out_shape=[out_shape, out_shape, out_shape, out_shape,
                   jax.ShapeDtypeStruct((HY_ORDER, 8, HY_WIDTH), F32)],
        compiler_params=_cparams(("arbitrary", "arbitrary")),
        name="hyena_filter_spectrum",
    )(zfe, zfo, w1, b1.reshape(1, hid), w2, b2.reshape(1, hid), w3r, w3r,
      dec_e, dec_o, cm, sm, wr, wi, sgn, wk)


def _long_conv(ze, zo, har, hai, hbr, hbi, hny, cm, sm, wr, wi, sgn):
    zar, zai, zbr, zbi, znr, zni = _half_spectrum(ze, zo, cm, sm, wr, wi, sgn)
    yar, yai = zar * har - zai * hai, zar * hai + zai * har
    ybr, ybi = zbr * hbr - zbi * hbi, zbr * hbi + zbi * hbr
    dr, di = yar - ybr, yai - ybi
    sar, sai = (yar + ybr).astype(BF16), (yai + ybi).astype(BF16)
    sbr, sbi = (dr * wr + di * wi).astype(BF16), (di * wr - dr * wi).astype(BF16)
    hnr, hni = hny[0:1], hny[1:2]
    ynr = znr * hnr - zni * hni
    yni = znr * hni + zni * hnr
    ye = _bdot(cm, sar) - _bdot(sm, sai) + sgn * ynr
    yo = _bdot(cm, sbr) - _bdot(sm, sbi) - sgn * yni
    return ye, yo


def _hyena_kernel(*refs):
    nsub = (len(refs) - 18) // 4
    u_refs = [refs[s * nsub:(s + 1) * nsub] for s in range(HY_ORDER + 1)]
    rest = refs[(HY_ORDER + 1) * nsub:]
    (cw0_ref, cw1_ref, cw2_ref, cb0_ref, cb1_ref, cb2_ref, skip_ref,
     har_ref, hai_ref, hbr_ref, hbi_ref, hny_ref,
     cm_ref, sm_ref, wr_ref, wi_ref, sgn_ref, o_ref) = rest[:18]
    il_refs = rest[18:]
    h = cm_ref.shape[0]
    cm, sm = cm_ref[...], sm_ref[...]
    wr, wi, sgn = wr_ref[...], wi_ref[...], sgn_ref[...]
    row = lax.broadcasted_iota(jnp.int32, (h, wr.shape[1]), 0)

    def short_conv(u_slabs, cw_ref, cb_ref):
        ue = jnp.concatenate([r[0, pl.ds(0, h, stride=2), :] for r in u_slabs], axis=1)
        uo = jnp.concatenate([r[0, pl.ds(1, h, stride=2), :] for r in u_slabs], axis=1)
        w0, w1, w2 = cw_ref[0:1], cw_ref[1:2], cw_ref[2:3]
        uo_prev = jnp.where(row == 0, 0.0, pltpu.roll(uo, 1, 0))
        ue_next = jnp.where(row == h - 1, 0.0, pltpu.roll(ue, h - 1, 0))
        cb = cb_ref[...]
        se = cb + uo_prev * w0 + ue * w1 + uo * w2
        so = cb + ue * w0 + uo * w1 + ue_next * w2
        return se, so

    x_streams = [short_conv(u_refs[0], cw0_ref, cb0_ref), short_conv(u_refs[1], cw1_ref, cb1_ref)]
    ze, zo = short_conv(u_refs[2], cw2_ref, cb2_ref)
    for o in range(HY_ORDER):
        ce, co = _long_conv(ze, zo, har_ref[o], hai_ref[o], hbr_ref[o], hbi_ref[o], hny_ref[o],
                            cm, sm, wr, wi, sgn)
        sk = skip_ref[o:o + 1]
        xe, xo = x_streams[o]
        ze = xe * (ce + sk * ze)
        zo = xo * (co + sk * zo)
    for j, il in enumerate(il_refs):
        il[pl.ds(0, h, stride=2), :] = ze[:, j * 128:(j + 1) * 128]
        il[pl.ds(1, h, stride=2), :] = zo[:, j * 128:(j + 1) * 128]
        o_ref[0, :, j * 128:(j + 1) * 128] = il[...]


def _hyena(u, conv_w, conv_b, skip, spectra, tabs, wc):
    b, l, _ = u.shape
    har, hai, hbr, hbi, hny = spectra
    cm, sm, wr, wi, sgn, _ = tabs
    h = l // 2
    nblk = HY_WIDTH // wc
    nsub = wc // 128
    conv_b = conv_b.reshape(1, -1)

    def ublk(s, j):
        return pl.BlockSpec((1, l, 128), lambda cb, bi: (bi, 0, (cb + s * nblk) * nsub + j))

    def wblk(r, s):
        return pl.BlockSpec((r, wc), lambda cb, bi: (0, cb + s * nblk))

    const = lambda cb, bi: (0, 0)
    spec = pl.BlockSpec((HY_ORDER, h, wc), lambda cb, bi: (0, 0, cb))
    return pl.pallas_call(
        _hyena_kernel,
        grid=(nblk, b),
        in_specs=[ublk(s, j) for s in range(HY_ORDER + 1) for j in range(nsub)] + [
                  wblk(HY_SHORT_CONV, 0), wblk(HY_SHORT_CONV, 1), wblk(HY_SHORT_CONV, 2),
                  wblk(1, 0), wblk(1, 1), wblk(1, 2),
                  pl.BlockSpec((HY_ORDER, wc), lambda cb, bi: (0, cb)),
                  spec, spec, spec, spec,
                  pl.BlockSpec((HY_ORDER, 8, wc), lambda cb, bi: (0, 0, cb)),
                  pl.BlockSpec((h, h), const), pl.BlockSpec((h, h), const),
                  pl.BlockSpec((h, wc), const), pl.BlockSpec((h, wc), const),
                  pl.BlockSpec((h, wc), const)],
        out_specs=pl.BlockSpec((1, l, wc), lambda cb, bi: (bi, 0, cb)),
        out_shape=jax.ShapeDtypeStruct((b, l, HY_WIDTH), F32),
        scratch_shapes=[pltpu.VMEM((l, 128), F32) for _ in range(nsub)],
        compiler_params=_cparams(("arbitrary", "arbitrary")),
        name="hyena_long_conv",
    )(*([u] * ((HY_ORDER + 1) * nsub)), conv_w, conv_w, conv_w, conv_b, conv_b, conv_b, skip,
      har, hai, hbr, hbi, hny, cm, sm, wr, wi, sgn)


def _merge_kernel(alpha, att_ref, hy_ref, gate_ref, x_ref, g1_ref, sh2_ref, sc2_ref,
                  wba_ref, wbh_ref, wo_ref, l1g_ref, l1b_ref, rw_ref, rb_ref,
                  xmid_ref, h2_ref, logit_ref):
    d = x_ref.shape[2]
    a = _bdot(att_ref[0], wba_ref[...])
    hh = _bdot(hy_ref[0].astype(BF16), wbh_ref[...])
    ga = jax.nn.sigmoid(gate_ref[0, :, :d])
    gh = jax.nn.sigmoid(gate_ref[0, :, d:])
    y = _bdot((ga * a + gh * hh).astype(BF16), wo_ref[...])
    xm = _ln(alpha * x_ref[0] + g1_ref[0] * y) * l1g_ref[...] + l1b_ref[...]
    xmid_ref[0] = xm
    h2 = _ln(xm) * (1.0 + sc2_ref[0]) + sh2_ref[0]
    h2_ref[0] = h2
    logit_ref[0] = _fdot(h2, rw_ref[...]) + rb_ref[...]


def _merge(att, hy, gate, x, g1, sh2, sc2, wba, wbh, wo, l1g, l1b, rw, rb, alpha, tm=256):
    b, l, d = x.shape
    ne = rw.shape[1]
    row = lambda bi, i: (bi, i, 0)
    vec = lambda bi, i: (bi, 0, 0)
    const = lambda bi, i: (0, 0)
    return pl.pallas_call(
        functools.partial(_merge_kernel, alpha),
        grid=(b, l // tm),
        in_specs=[pl.BlockSpec((1, tm, Q_W), row),
                  pl.BlockSpec((1, tm, HY_WIDTH), row),
                  pl.BlockSpec((1, tm, 2 * d), row),
                  pl.BlockSpec((1, tm, d), row),
                  pl.BlockSpec((1, 1, d), vec), pl.BlockSpec((1, 1, d), vec),
                  pl.BlockSpec((1, 1, d), vec),
                  pl.BlockSpec((Q_W, d), const), pl.BlockSpec((HY_WIDTH, d), const),
                  pl.BlockSpec((d, d), const),
                  pl.BlockSpec((1, d), const), pl.BlockSpec((1, d), const),
                  pl.BlockSpec((d, ne), const), pl.BlockSpec((1, ne), const)],
        out_specs=[pl.BlockSpec((1, tm, d), row), pl.BlockSpec((1, tm, d), row),
                   pl.BlockSpec((1, tm, ne), row)],
        out_shape=[jax.ShapeDtypeStruct((b, l, d), F32),
                   jax.ShapeDtypeStruct((b, l, d), F32),
                   jax.ShapeDtypeStruct((b, l, ne), F32)],
        compiler_params=_cparams(("arbitrary", "arbitrary")),
        name="merge_ln_router",
    )(att, hy, gate, x, g1, sh2, sc2, wba, wbh, wo, l1g.reshape(1, d), l1b.reshape(1, d),
      rw, rb.reshape(1, ne))


def _route_kernel(logit_ref, eid_ref, gate_ref, rank_ref, cnt_ref, run_ref):
    @pl.when(pl.program_id(0) == 0)
    def _():
        run_ref[...] = jnp.zeros_like(run_ref)

    lg = logit_ref[...]
    tr, ne = lg.shape
    lane = lax.broadcasted_iota(jnp.int32, (tr, ne), 1)
    work = lg
    vals, idxs, hots = [], [], []
    for _ in range(TOP_K):
        m = jnp.max(work, axis=1, keepdims=True)
        idx = jnp.min(jnp.where(work == m, lane, ne), axis=1, keepdims=True)
        hot = lane == idx
        vals.append(m)
        idxs.append(idx)
        hots.append(hot)
        work = jnp.where(hot, -jnp.inf, work)
    exps = [jnp.exp(v - vals[0]) for v in vals]
    den = exps[0] + exps[1] + exps[2] + exps[3]
    member = jnp.zeros((tr, ne), F32)
    for hot in hots:
        member = member + jnp.where(hot, 1.0, 0.0)
    r_i = lax.broadcasted_iota(jnp.int32, (tr, tr), 0)
    c_i = lax.broadcasted_iota(jnp.int32, (tr, tr), 1)
    tri = jnp.where(r_i > c_i, 1.0, 0.0).astype(BF16)
    base = run_ref[...] + _bdot(tri, member.astype(BF16))
    lane_k = lax.broadcasted_iota(jnp.int32, (tr, TOP_K), 1)
    eid = jnp.zeros((tr, TOP_K), jnp.int32)
    gate = jnp.zeros((tr, TOP_K), F32)
    rank = jnp.zeros((tr, TOP_K), F32)
    for k in range(TOP_K):
        rk = jnp.sum(jnp.where(hots[k], base, 0.0), axis=1, keepdims=True)
        eid = jnp.where(lane_k == k, idxs[k], eid)
        gate = jnp.where(lane_k == k, exps[k] / den, gate)
        rank = jnp.where(lane_k == k, rk, rank)
    eid_ref[...] = eid
    gate_ref[...] = gate
    rank_ref[...] = rank.astype(jnp.int32)
    run_ref[...] = run_ref[...] + jnp.sum(member, axis=0, keepdims=True)
    cnt_ref[...] = run_ref[...].astype(jnp.int32)


def _route(logits, tr=512):
    t, ne = logits.shape
    tok = lambda i: (i, 0)
    return pl.pallas_call(
        _route_kernel,
        grid=(t // tr,),
        in_specs=[pl.BlockSpec((tr, ne), tok)],
        out_specs=[pl.BlockSpec((tr, TOP_K), tok), pl.BlockSpec((tr, TOP_K), tok),
                   pl.BlockSpec((tr, TOP_K), tok), pl.BlockSpec((1, ne), lambda i: (0, 0))],
        out_shape=[jax.ShapeDtypeStruct((t, TOP_K), jnp.int32),
                   jax.ShapeDtypeStruct((t, TOP_K), F32),
                   jax.ShapeDtypeStruct((t, TOP_K), jnp.int32),
                   jax.ShapeDtypeStruct((1, ne), jnp.int32)],
        scratch_shapes=[pltpu.VMEM((1, ne), F32)],
        compiler_params=_cparams(("arbitrary",)),
        name="route_topk",
    )(logits)


def _dispatch_kernel(dest_ref, h_ref, xs_ref, sem):
    n = dest_ref.shape[2]
    td = n // TOP_K

    def body(t, carry):
        src = h_ref.at[pl.ds(t, 1)]
        for k in range(TOP_K):
            pltpu.make_async_copy(src, xs_ref.at[pl.ds(dest_ref[0, 0, t * TOP_K + k], 1)], sem).start()
        return carry

    lax.fori_loop(0, td, body, 0)
    for _ in range(TOP_K):
        pltpu.make_async_copy(h_ref, xs_ref.at[pl.ds(0, td)], sem).wait()


def _dispatch(h2, dest, td=256):
    t, d = h2.shape
    a = t * TOP_K
    n = td * TOP_K
    return pl.pallas_call(
        _dispatch_kernel,
        grid=(t // td,),
        in_specs=[pl.BlockSpec((1, 1, n), lambda i: (i, 0, 0), memory_space=pltpu.SMEM),
                  pl.BlockSpec((td, d), lambda i: (i, 0))],
        out_specs=pl.BlockSpec(memory_space=pl.ANY),
        out_shape=jax.ShapeDtypeStruct((a, d), h2.dtype),
        scratch_shapes=[pltpu.SemaphoreType.DMA(())],
        compiler_params=_cparams(("arbitrary",)),
        name="moe_dispatch",
    )(dest.reshape(t // td, 1, n), h2)


def _expert_kernel(tile_ref, exp_ref, lo_ref, hi_ref, first_ref, nitem_ref,
                   xs_ref, w1_ref, b1_ref, w2_ref, b2_ref, ys_ref):
    w = pl.program_id(0)
    dff = w2_ref.shape[1]

    @pl.when(w < nitem_ref[0])
    def _():
        x = xs_ref[...].astype(BF16)
        hb = _bdot(x, w1_ref[0]) + b1_ref[0]
        glu = jnp.minimum(hb[:, :dff], SWIGLU_LIMIT)
        lin = jnp.clip(hb[:, dff:], -SWIGLU_LIMIT, SWIGLU_LIMIT)
        act = glu * jax.nn.sigmoid(SWIGLU_ALPHA * glu) * (lin + 1.0)
        y = _bdot(act.astype(BF16), w2_ref[0]) + b2_ref[0]
        row = lax.broadcasted_iota(jnp.int32, y.shape, 0)
        y = jnp.where((row >= lo_ref[w]) & (row < hi_ref[w]), y, 0.0)

        @pl.when(first_ref[w] == 1)
        def _():
            ys_ref[...] = y

        @pl.when(first_ref[w] == 0)
        def _():
            ys_ref[...] = ys_ref[...] + y


def _experts(xs, items, w1b, b1, w2b, b2, tm):
    a, d = xs.shape
    ne, _, dff2 = w1b.shape
    dff = dff2 // 2
    n_items = a // tm + ne - 1
    tile_w, exp_w, lo_w, hi_w, first_w, nitem = items
    grid_spec = pltpu.PrefetchScalarGridSpec(
        num_scalar_prefetch=6,
        grid=(n_items,),
        in_specs=[pl.BlockSpec((tm, d), lambda w, tl, ex, lo, hi, fi, ni: (tl[w], 0)),
                  pl.BlockSpec((1, d, dff2), lambda w, tl, ex, lo, hi, fi, ni: (ex[w], 0, 0)),
                  pl.BlockSpec((1, 1, dff2), lambda w, tl, ex, lo, hi, fi, ni: (ex[w], 0, 0)),
                  pl.BlockSpec((1, dff, d), lambda w, tl, ex, lo, hi, fi, ni: (ex[w], 0, 0)),
                  pl.BlockSpec((1, 1, d), lambda w, tl, ex, lo, hi, fi, ni: (ex[w], 0, 0))],
        out_specs=pl.BlockSpec((tm, d), lambda w, tl, ex, lo, hi, fi, ni: (tl[w], 0)),
    )
    return pl.pallas_call(
        _expert_kernel,
        grid_spec=grid_spec,
        out_shape=jax.ShapeDtypeStruct((a, d), F32),
        compiler_params=_cparams(("arbitrary",)),
        name="moe_experts",
    )(tile_w, exp_w, lo_w, hi_w, first_w, nitem,
      xs, w1b, b1.reshape(ne, 1, dff2), w2b, b2.reshape(ne, 1, d))


def _work_items(counts, a, tm):
    ne = counts.shape[0]
    n_items = a // tm + ne - 1
    ends = jnp.cumsum(counts)
    starts = ends - counts
    first_tile = starts // tm
    last_tile = (ends - 1) // tm
    nvis = jnp.where(counts > 0, last_tile - first_tile + 1, 0)
    vis_end = jnp.cumsum(nvis)
    vis_start = vis_end - nvis
    total = vis_end[-1]
    w = jnp.minimum(jnp.arange(n_items, dtype=jnp.int32), total - 1)
    e_w = jnp.sum((vis_end[None, :] <= w[:, None]).astype(jnp.int32), axis=1)
    hot = e_w[:, None] == jnp.arange(ne, dtype=jnp.int32)[None, :]
    pick = lambda v: jnp.sum(jnp.where(hot, v[None, :], 0), axis=1)
    tile_w = (pick(first_tile) + (w - pick(vis_start))).astype(jnp.int32)
    lo_w = jnp.maximum(pick(starts) - tile_w * tm, 0).astype(jnp.int32)
    hi_w = jnp.minimum(pick(ends) - tile_w * tm, tm).astype(jnp.int32)
    prev = jnp.concatenate([jnp.full((1,), -1, jnp.int32), tile_w[:-1]])
    first_w = (tile_w != prev).astype(jnp.int32)
    return starts, (tile_w, e_w, lo_w, hi_w, first_w, total.reshape(1).astype(jnp.int32))


def _combine_kernel(alpha, dest_ref, ys_ref, gate_ref, xm_ref, g2_ref, lg_ref, lb_ref,
                    o_ref, buf_ref, sem):
    tc = xm_ref.shape[1]

    def body(t, carry):
        for k in range(TOP_K):
            pltpu.make_async_copy(ys_ref.at[pl.ds(dest_ref[0, 0, t * TOP_K + k], 1)],
                                  buf_ref.at[k, pl.ds(t, 1)], sem).start()
        return carry

    lax.fori_loop(0, tc, body, 0)
    for k in range(TOP_K):
        pltpu.make_async_copy(ys_ref.at[pl.ds(0, tc)], buf_ref.at[k], sem).wait()
    gate = gate_ref[0]
    f = buf_ref[0] * gate[:, 0:1]
    for k in range(1, TOP_K):
        f = f + buf_ref[k] * gate[:, k:k + 1]
    o_ref[0] = _ln(alpha * xm_ref[0] + g2_ref[0] * f) * lg_ref[...] + lb_ref[...]


def _combine(ys, dest, gates, x_mid, g2, lg, lb, alpha, tc=256):
    b, l, d = x_mid.shape
    n = tc * TOP_K
    nt = l // tc
    row = lambda bi, i: (bi, i, 0)
    const = lambda bi, i: (0, 0)
    return pl.pallas_call(
        functools.partial(_combine_kernel, alpha),
        grid=(b, nt),
        in_specs=[pl.BlockSpec((1, 1, n), lambda bi, i: (bi * nt + i, 0, 0), memory_space=pltpu.SMEM),
                  pl.BlockSpec(memory_space=pl.ANY),
                  pl.BlockSpec((1, tc, TOP_K), row),
                  pl.BlockSpec((1, tc, d), row),
                  pl.BlockSpec((1, 1, d), lambda bi, i: (bi, 0, 0)),
                  pl.BlockSpec((1, d), const), pl.BlockSpec((1, d), const)],
        out_specs=pl.BlockSpec((1, tc, d), row),
        out_shape=jax.ShapeDtypeStruct((b, l, d), F32),
        scratch_shapes=[pltpu.VMEM((TOP_K, tc, d), F32), pltpu.SemaphoreType.DMA(())],
        compiler_params=_cparams(("arbitrary", "arbitrary")),
        name="moe_combine_ln",
    )(dest.reshape(b * nt, 1, n), ys, gates.reshape(b, l, TOP_K), x_mid, g2,
      lg.reshape(1, d), lb.reshape(1, d))


def _rope_tables(l):
    f32 = np.float32
    rows = l // GRID_W
    row = np.repeat(np.arange(rows, dtype=f32), GRID_W)
    col = np.tile(np.arange(GRID_W, dtype=f32), rows)
    n_freq = HEAD_DIM // 4
    inv_freq = np.power(f32(ROPE_BASE), -np.arange(n_freq, dtype=f32) / f32(n_freq)).astype(f32)
    ang_r = (row[:, None] * inv_freq).astype(f32)
    ang_c = (col[:, None] * inv_freq).astype(f32)
    zero = np.zeros_like(ang_r)
    cos_r, sin_r, cos_c, sin_c = np.cos(ang_r), np.sin(ang_r), np.cos(ang_c), np.sin(ang_c)
    cos_h = np.concatenate([cos_r, cos_r, cos_c, cos_c], axis=1)
    sa_h = np.concatenate([-sin_r, zero, -sin_c, zero], axis=1)
    sb_h = np.concatenate([zero, sin_r, zero, sin_c], axis=1)
    rep = 128 // HEAD_DIM
    return tuple(np.tile(a, (1, rep)).astype(f32) for a in (cos_h, sa_h, sb_h))


def _filter_features(l):
    f32 = np.float32
    bands = (HY_EMB_DIM - 1) // 2
    t = np.linspace(0.0, 1.0, l, dtype=f32)[:, None]
    omega = (f32(2.0 * math.pi) * np.arange(l, dtype=f32)[:, None] / f32(l)).astype(f32)
    f = np.linspace(1e-4, bands - 1, bands, dtype=f32)[None, :]
    ang = (f * omega).astype(f32)
    z = np.concatenate([t, np.cos(ang), -np.sin(ang)], axis=-1).astype(f32)
    min_decay = math.log(HY_DECAY_TARGET) / HY_FAST_DECAY_PCT
    max_decay = math.log(HY_DECAY_TARGET) / HY_SLOW_DECAY_PCT
    deltas = np.abs(np.linspace(min_decay, max_decay, HY_WIDTH, dtype=f32))
    decay = np.exp(-t * deltas).astype(f32)
    return z, decay


def kernel(x, c, ctx, c_ctx, w_mod, b_mod, w_in, attn_sink, hy_conv_w, hy_conv_b, hy_filt_w1,
           hy_filt_b1, hy_filt_w2, hy_filt_b2, hy_filt_w3, hy_skip, w_branch_attn, w_branch_hyena,
           w_out, ln1_g, ln1_b, router_w, router_b, exp_w1, exp_b1, exp_w2, exp_b2, ln2_g, ln2_b):
    depth = w_mod.shape[0]
    assert depth == 1, "only the single-layer configuration is implemented"
    b, l, d = x.shape
    t = b * l
    alpha = (2 * depth) ** 0.25
    hy_wc = 256
    expert_tm = 256

    n_cond = b + 1
    pad = (-n_cond) % 8
    cond = jnp.concatenate([c, c_ctx[None], jnp.zeros((pad, d), F32)], axis=0)
    mod = _modulation(cond, w_mod[0], b_mod[0])
    mod_x = mod[:b].reshape(b, 1, 6, d)
    sh1, sc1, g1, sh2, sc2, g2 = (mod_x[:, :, i] for i in range(6))
    mod_c = mod[b:b + 1].reshape(1, 1, 6, d)
    csh1, csc1 = mod_c[:, :, 0], mod_c[:, :, 1]

    w_in_b = w_in[0].astype(BF16)
    cos_t, sa_t, sb_t = (jnp.asarray(a) for a in _rope_tables(l))
    q, k, v, u_hy, gate_x = _in_projection(x, sh1, sc1, w_in_b, cos_t, sa_t, sb_t)
    k_c, v_c = _ctx_kv(ctx, csh1, csc1, w_in_b[:, K_OFF:HY_OFF])
    att = _attention(q, k, v, k_c, v_c, attn_sink[0])

    tabs_np = _dft_tables(l, hy_wc)
    tabs = (jnp.asarray(tabs_np[0]).astype(BF16), jnp.asarray(tabs_np[1]).astype(BF16)) + tuple(
        jnp.asarray(a) for a in tabs_np[2:])
    zfeat, decay = _filter_features(l)
    emb_pad = (-HY_EMB_DIM) % 128
    zfeat = np.pad(zfeat, ((0, 0), (0, emb_pad)))
    fw1 = jnp.pad(hy_filt_w1[0], ((0, emb_pad), (0, 0)))
    spectra = _hyena_filters(jnp.asarray(zfeat[0::2]), jnp.asarray(zfeat[1::2]), fw1, hy_filt_b1[0],
                             hy_filt_w2[0], hy_filt_b2[0], hy_filt_w3[0], jnp.asarray(decay[0::2]),
                             jnp.asarray(decay[1::2]), tabs, hy_wc)
    hy = _hyena(u_hy, hy_conv_w[0], hy_conv_b[0], hy_skip[0], spectra, tabs, hy_wc)

    x_mid, h2, logits = _merge(att, hy, gate_x, x, g1, sh2, sc2,
                               w_branch_attn[0].astype(BF16), w_branch_hyena[0].astype(BF16),
                               w_out[0].astype(BF16), ln1_g[0], ln1_b[0], router_w[0], router_b[0],
                               alpha)

    eid, gates, rank, counts = _route(logits.reshape(t, N_EXPERTS))
    starts, items = _work_items(counts[0], t * TOP_K, expert_tm)
    expert_iota = jnp.arange(N_EXPERTS, dtype=jnp.int32)
    dest = rank + jnp.sum(jnp.where(eid[..., None] == expert_iota, starts, 0), axis=-1)
    xs = _dispatch(h2.reshape(t, d), dest)
    ys = _experts(xs, items, exp_w1[0].astype(BF16), exp_b1[0], exp_w2[0].astype(BF16), exp_b2[0],
                  expert_tm)
    return _combine(ys, dest, gates, x_mid, g2, ln2_g[0], ln2_b[0], alpha)
```

```python
import functools
import math

import numpy as np
import jax
import jax.numpy as jnp
from jax import lax
from jax.experimental import pallas as pl
from jax.experimental.pallas import tpu as pltpu

F32 = jnp.float32
BF16 = jnp.bfloat16
HIGHEST = lax.Precision.HIGHEST

GRID_W = 64
N_HEADS = 8
N_KV_HEADS = 2
GQA_GROUP = N_HEADS // N_KV_HEADS
HEAD_DIM = 64
WINDOW = 128
ATTN_BLOCK = 128
ROPE_BASE = 10000.0

HY_WIDTH = 512
HY_ORDER = 2
HY_SHORT_CONV = 3
HY_EMB_DIM = 33
HY_DECAY_TARGET = 1e-2
HY_FAST_DECAY_PCT = 0.3
HY_SLOW_DECAY_PCT = 1.5

N_EXPERTS = 32
TOP_K = 4
SWIGLU_LIMIT = 7.0
SWIGLU_ALPHA = 1.702
LN_EPS = 1e-5

Q_W = N_HEADS * HEAD_DIM
KV_W = N_KV_HEADS * HEAD_DIM
K_OFF = Q_W
V_OFF = K_OFF + KV_W
HY_OFF = V_OFF + KV_W
GATE_OFF = HY_OFF + (HY_ORDER + 1) * HY_WIDTH

VMEM_LIMIT = 56 * 1024 * 1024
NEG_BIG = -1e30


def _cparams(sem):
    return pltpu.CompilerParams(dimension_semantics=sem, vmem_limit_bytes=VMEM_LIMIT)


def _ln(x):
    mu = jnp.mean(x, axis=-1, keepdims=True)
    xc = x - mu
    var = jnp.mean(xc * xc, axis=-1, keepdims=True)
    return xc * lax.rsqrt(var + LN_EPS)


def _bdot(a, b):
    return jnp.dot(a, b, preferred_element_type=F32)


def _fdot(a, b):
    return jnp.dot(a, b, preferred_element_type=F32, precision=HIGHEST)


def _mod_kernel(c_ref, w_ref, b_ref, o_ref):
    c = c_ref[...]
    s = c * jax.nn.sigmoid(c)
    o_ref[...] = _fdot(s, w_ref[...]) + b_ref[...]


def _modulation(cond, w, b, tn=512):
    r, d = cond.shape
    n = w.shape[1]
    return pl.pallas_call(
        _mod_kernel,
        grid=(n // tn,),
        in_specs=[pl.BlockSpec((r, d), lambda j: (0, 0)),
                  pl.BlockSpec((d, tn), lambda j: (0, j)),
                  pl.BlockSpec((1, tn), lambda j: (0, j))],
        out_specs=pl.BlockSpec((r, tn), lambda j: (0, j)),
        out_shape=jax.ShapeDtypeStruct((r, n), F32),
        compiler_params=_cparams(("arbitrary",)),
        name="modulation",
    )(cond, w, b.reshape(1, n))


def _rope(t, cos, sa, sb):
    n = t.shape[-1]
    return t * cos + pltpu.roll(t, n - 16, 1) * sa + pltpu.roll(t, 16, 1) * sb


def _inproj_kernel(x_ref, sh_ref, sc_ref, w_ref, cos_ref, sa_ref, sb_ref,
                   q_ref, k_ref, v_ref, u_ref, g_ref):
    h = _ln(x_ref[0]) * (1.0 + sc_ref[0]) + sh_ref[0]
    hb = h.astype(BF16)
    cos, sa, sb = cos_ref[...], sa_ref[...], sb_ref[...]
    scale = HEAD_DIM ** -0.5
    for j in range(Q_W // 128):
        t = _bdot(hb, w_ref[:, j * 128:(j + 1) * 128])
        q_ref[0, :, j * 128:(j + 1) * 128] = (_rope(t, cos, sa, sb) * scale).astype(BF16)
    t = _bdot(hb, w_ref[:, K_OFF:V_OFF])
    k_ref[0] = _rope(t, cos, sa, sb).astype(BF16)
    v_ref[0] = _bdot(hb, w_ref[:, V_OFF:HY_OFF]).astype(BF16)
    for j in range((GATE_OFF - HY_OFF) // 512):
        u_ref[0, :, j * 512:(j + 1) * 512] = _bdot(hb, w_ref[:, HY_OFF + j * 512:HY_OFF + (j + 1) * 512])
    n_gate = w_ref.shape[1] - GATE_OFF
    for j in range(n_gate // 512):
        g_ref[0, :, j * 512:(j + 1) * 512] = _bdot(
            hb, w_ref[:, GATE_OFF + j * 512:GATE_OFF + (j + 1) * 512])


def _in_projection(x, sh, sc, w_in_b, cos_t, sa_t, sb_t, tm=256):
    b, l, d = x.shape
    in_w = w_in_b.shape[1]
    hy_w = GATE_OFF - HY_OFF
    g_w = in_w - GATE_OFF
    row = lambda bi, i: (bi, i, 0)
    vec = lambda bi, i: (bi, 0, 0)
    tab = lambda bi, i: (i, 0)
    return pl.pallas_call(
        _inproj_kernel,
        grid=(b, l // tm),
        in_specs=[pl.BlockSpec((1, tm, d), row),
                  pl.BlockSpec((1, 1, d), vec),
                  pl.BlockSpec((1, 1, d), vec),
                  pl.BlockSpec((d, in_w), lambda bi, i: (0, 0)),
                  pl.BlockSpec((tm, 128), tab),
                  pl.BlockSpec((tm, 128), tab),
                  pl.BlockSpec((tm, 128), tab)],
        out_specs=[pl.BlockSpec((1, tm, Q_W), row),
                   pl.BlockSpec((1, tm, KV_W), row),
                   pl.BlockSpec((1, tm, KV_W), row),
                   pl.BlockSpec((1, tm, hy_w), row),
                   pl.BlockSpec((1, tm, g_w), row)],
        out_shape=[jax.ShapeDtypeStruct((b, l, Q_W), BF16),
                   jax.ShapeDtypeStruct((b, l, KV_W), BF16),
                   jax.ShapeDtypeStruct((b, l, KV_W), BF16),
                   jax.ShapeDtypeStruct((b, l, hy_w), F32),
                   jax.ShapeDtypeStruct((b, l, g_w), F32)],
        compiler_params=_cparams(("arbitrary", "arbitrary")),
        name="in_projection",
    )(x, sh, sc, w_in_b, cos_t, sa_t, sb_t)


def _ctx_kv_kernel(x_ref, sh_ref, sc_ref, w_ref, k_ref, v_ref):
    h = _ln(x_ref[0]) * (1.0 + sc_ref[0]) + sh_ref[0]
    kv = _bdot(h.astype(BF16), w_ref[...])
    k_ref[0] = kv[:, :KV_W].astype(BF16)
    v_ref[0] = kv[:, KV_W:].astype(BF16)


def _ctx_kv(ctx, sh, sc, w_kv_b):
    b, c, d = ctx.shape
    row = lambda bi: (bi, 0, 0)
    return pl.pallas_call(
        _ctx_kv_kernel,
        grid=(b,),
        in_specs=[pl.BlockSpec((1, c, d), row),
                  pl.BlockSpec((1, 1, d), lambda bi: (0, 0, 0)),
                  pl.BlockSpec((1, 1, d), lambda bi: (0, 0, 0)),
                  pl.BlockSpec((d, 2 * KV_W), lambda bi: (0, 0))],
        out_specs=[pl.BlockSpec((1, c, KV_W), row), pl.BlockSpec((1, c, KV_W), row)],
        out_shape=[jax.ShapeDtypeStruct((b, c, KV_W), BF16),
                   jax.ShapeDtypeStruct((b, c, KV_W), BF16)],
        compiler_params=_cparams(("arbitrary",)),
        name="ctx_kv",
    )(ctx, sh, sc, w_kv_b)


def _attn_kernel(sink_ref, q_ref, k_ref, v_ref, kc_ref, vc_ref, o_ref):
    l = k_ref.shape[1]
    span = ATTN_BLOCK + 2 * WINDOW
    j = pl.program_id(1)
    q0 = j * ATTN_BLOCK
    start = pl.multiple_of(jnp.clip(q0 - WINDOW, 0, l - span), ATTN_BLOCK)
    qpos = q0 + lax.broadcasted_iota(jnp.int32, (ATTN_BLOCK, span), 0)
    kpos = start + lax.broadcasted_iota(jnp.int32, (ATTN_BLOCK, span), 1)
    band = jnp.abs(kpos - qpos) <= WINDOW
    dn = (((1,), (1,)), ((), ()))
    outs = []
    for kv in range(N_KV_HEADS):
        ks = slice(kv * HEAD_DIM, (kv + 1) * HEAD_DIM)
        kl = k_ref[0, pl.ds(start, span), ks]
        vl = v_ref[0, pl.ds(start, span), ks]
        kc = kc_ref[0, :, ks]
        vc = vc_ref[0, :, ks]
        for g in range(GQA_GROUP):
            h = kv * GQA_GROUP + g
            qh = q_ref[0, :, h * HEAD_DIM:(h + 1) * HEAD_DIM]
            s_loc = lax.dot_general(qh, kl, dn, preferred_element_type=F32)
            s_loc = jnp.where(band, s_loc, NEG_BIG)
            s_ctx = lax.dot_general(qh, kc, dn, preferred_element_type=F32)
            sink = sink_ref[h]
            m = jnp.maximum(jnp.max(s_loc, axis=1, keepdims=True),
                            jnp.max(s_ctx, axis=1, keepdims=True))
            m = jnp.maximum(m, sink)
            p_loc = jnp.exp(s_loc - m)
            p_ctx = jnp.exp(s_ctx - m)
            den = (jnp.sum(p_loc, axis=1, keepdims=True) + jnp.sum(p_ctx, axis=1, keepdims=True)
                   + jnp.exp(sink - m))
            o = _bdot(p_loc.astype(BF16), vl) + _bdot(p_ctx.astype(BF16), vc)
            outs.append(o / den)
    o_ref[0] = jnp.concatenate(outs, axis=1).astype(BF16)


def _attention(q, k, v, kc, vc, sink):
    b, l, _ = q.shape
    c = kc.shape[1]
    full = lambda bi, j, s: (bi, 0, 0)
    grid_spec = pltpu.PrefetchScalarGridSpec(
        num_scalar_prefetch=1,
        grid=(b, l // ATTN_BLOCK),
        in_specs=[pl.BlockSpec((1, ATTN_BLOCK, Q_W), lambda bi, j, s: (bi, j, 0)),
                  pl.BlockSpec((1, l, KV_W), full),
                  pl.BlockSpec((1, l, KV_W), full),
                  pl.BlockSpec((1, c, KV_W), full),
                  pl.BlockSpec((1, c, KV_W), full)],
        out_specs=pl.BlockSpec((1, ATTN_BLOCK, Q_W), lambda bi, j, s: (bi, j, 0)),
    )
    return pl.pallas_call(
        _attn_kernel,
        grid_spec=grid_spec,
        out_shape=jax.ShapeDtypeStruct((b, l, Q_W), BF16),
        compiler_params=_cparams(("arbitrary", "arbitrary")),
        name="window_attention",
    )(sink, q, k, v, kc, vc)


def _dft_tables(l, wc):
    h = l // 2
    idx = np.arange(h, dtype=np.int64)
    ang = 2.0 * np.pi * ((idx[:, None] * idx[None, :]) % l).astype(np.float64) / l
    cm = np.cos(ang).astype(np.float32)
    sm = np.sin(ang).astype(np.float32)
    tw = 2.0 * np.pi * idx.astype(np.float64) / (2 * l)
    ones = np.ones((1, wc), np.float32)
    wr = np.cos(tw).astype(np.float32)[:, None] * ones
    wi = (-np.sin(tw)).astype(np.float32)[:, None] * ones
    sgn = np.where(idx % 2 == 0, 1.0, -1.0).astype(np.float32)[:, None] * ones
    wk = np.where(idx == 0, 1.0 / (2 * l), 2.0 / (2 * l)).astype(np.float32)[:, None] * ones
    return cm, sm, wr, wi, sgn, wk


def _half_spectrum(se, so, cm, sm, wr, wi, sgn):
    seb, sob = se.astype(BF16), so.astype(BF16)
    ce, ss_e = _bdot(cm, seb), _bdot(sm, seb)
    co, ss_o = _bdot(cm, sob), _bdot(sm, sob)
    vr = wr * co + wi * ss_o
    vi = wi * co - wr * ss_o
    zar, zai = ce + vr, vi - ss_e
    zbr, zbi = ce - vr, -ss_e - vi
    e_ny = jnp.sum(sgn * se, axis=0, keepdims=True)
    o_ny = jnp.sum(sgn * so, axis=0, keepdims=True)
    return zar, zai, zbr, zbi, e_ny, -o_ny


def _filter_kernel(ze_ref, zo_ref, w1_ref, b1_ref, w2_ref, b2_ref, w3f_ref, w3b_ref,
                   de_ref, do_ref, cm_ref, sm_ref, wr_ref, wi_ref, sgn_ref, wk_ref,
                   har_ref, hai_ref, hbr_ref, hbi_ref, hny_ref):
    cm, sm = cm_ref[...], sm_ref[...]
    wr, wi, sgn, wk = wr_ref[...], wi_ref[...], sgn_ref[...], wk_ref[...]
    h = cm.shape[0]

    def taps(z_ref, w3_ref, d_ref):
        a = jnp.sin(_fdot(z_ref[...], w1_ref[...]) + b1_ref[...])
        a = jnp.sin(_fdot(a, w2_ref[...]) + b2_ref[...])
        return _fdot(a, w3_ref[0]) * d_ref[...]

    fe, fo = taps(ze_ref, w3f_ref, de_ref), taps(zo_ref, w3f_ref, do_ref)
    be, bo = taps(ze_ref, w3b_ref, de_ref), taps(zo_ref, w3b_ref, do_ref)
    row = lax.broadcasted_iota(jnp.int32, be.shape, 0)
    be = jnp.where(row == 0, 0.0, be)
    far, fai, fbr, fbi, fnr, fni = _half_spectrum(fe, fo, cm, sm, wr, wi, sgn)
    bar, bai, bbr, bbi, bnr, bni = _half_spectrum(be, bo, cm, sm, wr, wi, sgn)
    har_ref[0] = wk * (far + bar)
    hai_ref[0] = wk * (fai - bai)
    hbr_ref[0] = wk * (fbr + bbr)
    hbi_ref[0] = wk * (fbi - bbi)
    ny_scale = 2.0 / (4 * h)
    nr = ny_scale * (fnr + bnr)
    ni = ny_scale * (fni - bni)
    rows = lax.broadcasted_iota(jnp.int32, (8, nr.shape[1]), 0)
    hny_ref[0] = jnp.where(rows == 0, nr, jnp.where(rows == 1, ni, 0.0))


def _hyena_filters(zfe, zfo, w1, b1, w2, b2, w3, dec_e, dec_o, tabs, wc):
    cm, sm, wr, wi, sgn, wk = tabs
    h = cm.shape[0]
    emb = zfe.shape[1]
    hid = w2.shape[0]
    nblk = HY_WIDTH // wc
    w3r = w3.reshape(hid, HY_ORDER * 2, HY_WIDTH).transpose(1, 0, 2)
    const = lambda o, cb: (0, 0)
    chan = lambda o, cb: (0, cb)
    out_spec = pl.BlockSpec((1, h, wc), lambda o, cb: (o, 0, cb))
    out_shape = jax.ShapeDtypeStruct((HY_ORDER, h, HY_WIDTH), F32)
    return pl.pallas_call(
        _filter_kernel,
        grid=(HY_ORDER, nblk),
        in_specs=[pl.BlockSpec((h, emb), const), pl.BlockSpec((h, emb), const),
                  pl.BlockSpec((emb, hid), const), pl.BlockSpec((1, hid), const),
                  pl.BlockSpec((hid, hid), const), pl.BlockSpec((1, hid), const),
                  pl.BlockSpec((1, hid, wc), lambda o, cb: (2 * o, 0, cb)),
                  pl.BlockSpec((1, hid, wc), lambda o, cb: (2 * o + 1, 0, cb)),
                  pl.BlockSpec((h, wc), chan), pl.BlockSpec((h, wc), chan),
                  pl.BlockSpec((h, h), const), pl.BlockSpec((h, h), const),
                  pl.BlockSpec((h, wc), const), pl.BlockSpec((h, wc), const),
                  pl.BlockSpec((h, wc), const), pl.BlockSpec((h, wc), const)],
        out_specs=[out_spec, out_spec, out_spec, out_spec,
                   pl.BlockSpec((1, 8, wc), lambda o, cb: (o, 0, cb))],
        out_shape=[out_shape, out_shape, out_shape, out_shape,
                   jax.ShapeDtypeStruct((HY_ORDER, 8, HY_WIDTH), F32)],
        compiler_params=_cparams(("arbitrary", "arbitrary")),
        name="hyena_filter_spectrum",
    )(zfe, zfo, w1, b1.reshape(1, hid), w2, b2.reshape(1, hid), w3r, w3r,
      dec_e, dec_o, cm, sm, wr, wi, sgn, wk)


def _long_conv(ze, zo, har, hai, hbr, hbi, hny, cm, sm, wr, wi, sgn):
    zar, zai, zbr, zbi, znr, zni = _half_spectrum(ze, zo, cm, sm, wr, wi, sgn)
    yar, yai = zar * har - zai * hai, zar * hai + zai * har
    ybr, ybi = zbr * hbr - zbi * hbi, zbr * hbi + zbi * hbr
    dr, di = yar - ybr, yai - ybi
    sar, sai = (yar + ybr).astype(BF16), (yai + ybi).astype(BF16)
    sbr, sbi = (dr * wr + di * wi).astype(BF16), (di * wr - dr * wi).astype(BF16)
    hnr, hni = hny[0:1], hny[1:2]
    ynr = znr * hnr - zni * hni
    yni = znr * hni + zni * hnr
    ye = _bdot(cm, sar) - _bdot(sm, sai) + sgn * ynr
    yo = _bdot(cm, sbr) - _bdot(sm, sbi) - sgn * yni
    return ye, yo


def _hyena_kernel(*refs):
    nsub = (len(refs) - 18) // 4
    u_refs = [refs[s * nsub:(s + 1) * nsub] for s in range(HY_ORDER + 1)]
    rest = refs[(HY_ORDER + 1) * nsub:]
    (cw0_ref, cw1_ref, cw2_ref, cb0_ref, cb1_ref, cb2_ref, skip_ref,
     har_ref, hai_ref, hbr_ref, hbi_ref, hny_ref,
     cm_ref, sm_ref, wr_ref, wi_ref, sgn_ref, o_ref) = rest[:18]
    il_refs = rest[18:]
    h = cm_ref.shape[0]
    cm, sm = cm_ref[...], sm_ref[...]
    wr, wi, sgn = wr_ref[...], wi_ref[...], sgn_ref[...]
    row = lax.broadcasted_iota(jnp.int32, (h, wr.shape[1]), 0)

    def short_conv(u_slabs, cw_ref, cb_ref):
        ue = jnp.concatenate([r[0, pl.ds(0, h, stride=2), :] for r in u_slabs], axis=1)
        uo = jnp.concatenate([r[0, pl.ds(1, h, stride=2), :] for r in u_slabs], axis=1)
        w0, w1, w2 = cw_ref[0:1], cw_ref[1:2], cw_ref[2:3]
        uo_prev = jnp.where(row == 0, 0.0, pltpu.roll(uo, 1, 0))
        ue_next = jnp.where(row == h - 1, 0.0, pltpu.roll(ue, h - 1, 0))
        cb = cb_ref[...]
        se = cb + uo_prev * w0 + ue * w1 + uo * w2
        so = cb + ue * w0 + uo * w1 + ue_next * w2
        return se, so

    x_streams = [short_conv(u_refs[0], cw0_ref, cb0_ref), short_conv(u_refs[1], cw1_ref, cb1_ref)]
    ze, zo = short_conv(u_refs[2], cw2_ref, cb2_ref)
    for o in range(HY_ORDER):
        ce, co = _long_conv(ze, zo, har_ref[o], hai_ref[o], hbr_ref[o], hbi_ref[o], hny_ref[o],
                            cm, sm, wr, wi, sgn)
        sk = skip_ref[o:o + 1]
        xe, xo = x_streams[o]
        ze = xe * (ce + sk * ze)
        zo = xo * (co + sk * zo)
    for j, il in enumerate(il_refs):
        il[pl.ds(0, h, stride=2), :] = ze[:, j * 128:(j + 1) * 128]
        il[pl.ds(1, h, stride=2), :] = zo[:, j * 128:(j + 1) * 128]
        o_ref[0, :, j * 128:(j + 1) * 128] = il[...]


def _hyena(u, conv_w, conv_b, skip, spectra, tabs, wc):
    b, l, _ = u.shape
    har, hai, hbr, hbi, hny = spectra
    cm, sm, wr, wi, sgn, _ = tabs
    h = l // 2
    nblk = HY_WIDTH // wc
    nsub = wc // 128
    conv_b = conv_b.reshape(1, -1)

    def ublk(s, j):
        return pl.BlockSpec((1, l, 128), lambda cb, bi: (bi, 0, (cb + s * nblk) * nsub + j))

    def wblk(r, s):
        return pl.BlockSpec((r, wc), lambda cb, bi: (0, cb + s * nblk))

    const = lambda cb, bi: (0, 0)
    spec = pl.BlockSpec((HY_ORDER, h, wc), lambda cb, bi: (0, 0, cb))
    return pl.pallas_call(
        _hyena_kernel,
        grid=(nblk, b),
        in_specs=[ublk(s, j) for s in range(HY_ORDER + 1) for j in range(nsub)] + [
                  wblk(HY_SHORT_CONV, 0), wblk(HY_SHORT_CONV, 1), wblk(HY_SHORT_CONV, 2),
                  wblk(1, 0), wblk(1, 1), wblk(1, 2),
                  pl.BlockSpec((HY_ORDER, wc), lambda cb, bi: (0, cb)),
                  spec, spec, spec, spec,
                  pl.BlockSpec((HY_ORDER, 8, wc), lambda cb, bi: (0, 0, cb)),
                  pl.BlockSpec((h, h), const), pl.BlockSpec((h, h), const),
                  pl.BlockSpec((h, wc), const), pl.BlockSpec((h, wc), const),
                  pl.BlockSpec((h, wc), const)],
        out_specs=pl.BlockSpec((1, l, wc), lambda cb, bi: (bi, 0, cb)),
        out_shape=jax.ShapeDtypeStruct((b, l, HY_WIDTH), F32),
        scratch_shapes=[pltpu.VMEM((l, 128), F32) for _ in range(nsub)],
        compiler_params=_cparams(("arbitrary", "arbitrary")),
        name="hyena_long_conv",
    )(*([u] * ((HY_ORDER + 1) * nsub)), conv_w, conv_w, conv_w, conv_b, conv_b, conv_b, skip,
      har, hai, hbr, hbi, hny, cm, sm, wr, wi, sgn)


def _merge_kernel(alpha, att_ref, hy_ref, gate_ref, x_ref, g1_ref, sh2_ref, sc2_ref,
                  wba_ref, wbh_ref, wo_ref, l1g_ref, l1b_ref, rw_ref, rb_ref,
                  xmid_ref, h2_ref, logit_ref):
    d = x_ref.shape[2]
    a = _bdot(att_ref[0], wba_ref[...])
    hh = _bdot(hy_ref[0].astype(BF16), wbh_ref[...])
    ga = jax.nn.sigmoid(gate_ref[0, :, :d])
    gh = jax.nn.sigmoid(gate_ref[0, :, d:])
    y = _bdot((ga * a + gh * hh).astype(BF16), wo_ref[...])
    xm = _ln(alpha * x_ref[0] + g1_ref[0] * y) * l1g_ref[...] + l1b_ref[...]
    xmid_ref[0] = xm
    h2 = _ln(xm) * (1.0 + sc2_ref[0]) + sh2_ref[0]
    h2_ref[0] = h2
    logit_ref[0] = _fdot(h2, rw_ref[...]) + rb_ref[...]


def _merge(att, hy, gate, x, g1, sh2, sc2, wba, wbh, wo, l1g, l1b, rw, rb, alpha, tm=256):
    b, l, d = x.shape
    ne = rw.shape[1]
    row = lambda bi, i: (bi, i, 0)
    vec = lambda bi, i: (bi, 0, 0)
    const = lambda bi, i: (0, 0)
    return pl.pallas_call(
        functools.partial(_merge_kernel, alpha),
        grid=(b, l // tm),
        in_specs=[pl.BlockSpec((1, tm, Q_W), row),
                  pl.BlockSpec((1, tm, HY_WIDTH), row),
                  pl.BlockSpec((1, tm, 2 * d), row),
                  pl.BlockSpec((1, tm, d), row),
                  pl.BlockSpec((1, 1, d), vec), pl.BlockSpec((1, 1, d), vec),
                  pl.BlockSpec((1, 1, d), vec),
                  pl.BlockSpec((Q_W, d), const), pl.BlockSpec((HY_WIDTH, d), const),
                  pl.BlockSpec((d, d), const),
                  pl.BlockSpec((1, d), const), pl.BlockSpec((1, d), const),
                  pl.BlockSpec((d, ne), const), pl.BlockSpec((1, ne), const)],
        out_specs=[pl.BlockSpec((1, tm, d), row), pl.BlockSpec((1, tm, d), row),
                   pl.BlockSpec((1, tm, ne), row)],
        out_shape=[jax.ShapeDtypeStruct((b, l, d), F32),
                   jax.ShapeDtypeStruct((b, l, d), F32),
                   jax.ShapeDtypeStruct((b, l, ne), F32)],
        compiler_params=_cparams(("arbitrary", "arbitrary")),
        name="merge_ln_router",
    )(att, hy, gate, x, g1, sh2, sc2, wba, wbh, wo, l1g.reshape(1, d), l1b.reshape(1, d),
      rw, rb.reshape(1, ne))


def _route_kernel(lg_ref, gate_ref, pos_ref, tcnt_ref, tbase_ref, cnt_ref, run_ref):
    @pl.when(pl.program_id(0) == 0)
    def _():
        run_ref[...] = jnp.zeros_like(run_ref)

    lg = lg_ref[...]
    ne, tr = lg.shape
    sub = lax.broadcasted_iota(jnp.int32, (ne, tr), 0)
    work = lg
    vals, hots = [], []
    for _ in range(TOP_K):
        m = jnp.max(work, axis=0, keepdims=True)
        idx = jnp.min(jnp.where(work == m, sub, ne), axis=0, keepdims=True)
        hot = sub == idx
        vals.append(m)
        hots.append(hot)
        work = jnp.where(hot, -jnp.inf, work)
    exps = [jnp.exp(v - vals[0]) for v in vals]
    den = exps[0] + exps[1] + exps[2] + exps[3]
    member = jnp.zeros((ne, tr), F32)
    for hot in hots:
        member = member + jnp.where(hot, 1.0, 0.0)
    r_i = lax.broadcasted_iota(jnp.int32, (tr, tr), 0)
    c_i = lax.broadcasted_iota(jnp.int32, (tr, tr), 1)
    earlier = jnp.where(r_i < c_i, 1.0, 0.0).astype(BF16)
    prefix = _bdot(member.astype(BF16), earlier)
    cnt = jnp.broadcast_to(jnp.sum(member, axis=1, keepdims=True), (ne, 128))
    e_r = lax.broadcasted_iota(jnp.int32, (ne, ne), 0)
    e_c = lax.broadcasted_iota(jnp.int32, (ne, ne), 1)
    lower = jnp.where(e_c < e_r, 1.0, 0.0).astype(BF16)
    off = _bdot(lower, cnt.astype(BF16))
    base = off[:, 0:1] + prefix
    sub_k = lax.broadcasted_iota(jnp.int32, (TOP_K, tr), 0)
    gate = jnp.zeros((TOP_K, tr), F32)
    pos = jnp.zeros((TOP_K, tr), F32)
    for k in range(TOP_K):
        pk = jnp.sum(jnp.where(hots[k], base, 0.0), axis=0, keepdims=True)
        gate = jnp.where(sub_k == k, exps[k] / den, gate)
        pos = jnp.where(sub_k == k, pk, pos)
    gate_ref[...] = gate
    pos_ref[...] = pos.astype(jnp.int32)
    tcnt_ref[0] = cnt.astype(jnp.int32)
    tbase_ref[0] = run_ref[...].astype(jnp.int32)
    run_ref[...] = run_ref[...] + cnt
    cnt_ref[...] = run_ref[...].astype(jnp.int32)


def _route(logits_t, tr):
    ne, t = logits_t.shape
    nt = t // tr
    tok = lambda i: (0, i)
    tile = lambda i: (i, 0, 0)
    return pl.pallas_call(
        _route_kernel,
        grid=(nt,),
        in_specs=[pl.BlockSpec((ne, tr), tok)],
        out_specs=[pl.BlockSpec((TOP_K, tr), tok), pl.BlockSpec((TOP_K, tr), tok),
                   pl.BlockSpec((1, ne, 128), tile), pl.BlockSpec((1, ne, 128), tile),
                   pl.BlockSpec((ne, 128), lambda i: (0, 0))],
        out_shape=[jax.ShapeDtypeStruct((TOP_K, t), F32),
                   jax.ShapeDtypeStruct((TOP_K, t), jnp.int32),
                   jax.ShapeDtypeStruct((nt, ne, 128), jnp.int32),
                   jax.ShapeDtypeStruct((nt, ne, 128), jnp.int32),
                   jax.ShapeDtypeStruct((ne, 128), jnp.int32)],
        scratch_shapes=[pltpu.VMEM((ne, 128), F32)],
        compiler_params=_cparams(("arbitrary",)),
        name="route_topk",
    )(logits_t)


def _row_tile(d):
    return (d // 128, 128)


def _strip_copies(n_ref, dst_ref, make_copy, max_rows):
    ne = n_ref.shape[2]
    bits = [1 << s for s in range(max_rows.bit_length() - 1, -1, -1)]

    def body(e, off):
        n = n_ref[0, 0, e]
        dst = dst_ref[0, 0, e]
        done = jnp.int32(0)
        for bit in bits:
            part = n & bit

            @pl.when(part != 0)
            def _():
                make_copy(off + done, dst + done, bit).start()

            done = done + part
        return off + n

    lax.fori_loop(0, ne, body, jnp.int32(0))


def _dispatch_kernel(n_ref, dst_ref, pos_ref, h_ref, xs_ref, srt_ref, sem):
    i = pl.program_id(0)
    slot = i % 2
    td = h_ref.shape[0]
    n = td * TOP_K
    pos = pos_ref[...]
    rows = lax.broadcasted_iota(jnp.int32, (n, td), 0)
    hit = rows == pos[0:1]
    for k in range(1, TOP_K):
        hit = hit | (rows == pos[k:k + 1])
    perm = jnp.where(hit, 1.0, 0.0).astype(BF16)
    srt_ref[slot] = _bdot(perm, h_ref[...].astype(BF16)).reshape(srt_ref.shape[1:])

    def all_rows(s):
        return pltpu.make_async_copy(srt_ref.at[s], xs_ref.at[pl.ds(0, n)], sem.at[s])

    _strip_copies(n_ref, dst_ref,
                  lambda src, dst, size: pltpu.make_async_copy(
                      srt_ref.at[slot, pl.ds(src, size)], xs_ref.at[pl.ds(dst, size)], sem.at[slot]),
                  td)

    @pl.when(i > 0)
    def _():
        all_rows(1 - slot).wait()

    @pl.when(i == pl.num_programs(0) - 1)
    def _():
        all_rows(slot).wait()


def _dispatch(h2, pos_t, tile_n, tile_dst, td):
    t, d = h2.shape
    a = t * TOP_K
    n = td * TOP_K
    nt = t // td
    ne = tile_n.shape[-1]
    smem = lambda: pl.BlockSpec((1, 1, ne), lambda i: (i, 0, 0), memory_space=pltpu.SMEM)
    return pl.pallas_call(
        _dispatch_kernel,
        grid=(nt,),
        in_specs=[smem(), smem(),
                  pl.BlockSpec((TOP_K, td), lambda i: (0, i)),
                  pl.BlockSpec((td, d), lambda i: (i, 0))],
        out_specs=pl.BlockSpec(memory_space=pl.ANY),
        out_shape=jax.ShapeDtypeStruct((a,) + _row_tile(d), F32),
        scratch_shapes=[pltpu.VMEM((2, n) + _row_tile(d), F32), pltpu.SemaphoreType.DMA((2,))],
        compiler_params=_cparams(("arbitrary",)),
        name="moe_dispatch",
    )(tile_n.reshape(nt, 1, ne), tile_dst.reshape(nt, 1, ne), pos_t, h2)


def _expert_kernel(tile_ref, exp_ref, lo_ref, hi_ref, first_ref, nitem_ref,
                   xs_ref, w1_ref, b1_ref, w2_ref, b2_ref, ys_ref, w1b_ref, w2b_ref):
    w = pl.program_id(0)
    tm = xs_ref.shape[0]
    d, dff = w2_ref.shape[2], w2_ref.shape[1]

    @pl.when((w == 0) | (exp_ref[w] != exp_ref[jnp.maximum(w - 1, 0)]))
    def _():
        w1b_ref[...] = w1_ref[0].astype(BF16)
        w2b_ref[...] = w2_ref[0].astype(BF16)

    @pl.when(w < nitem_ref[0])
    def _():
        x = xs_ref[...].reshape(tm, d).astype(BF16)
        hb = _bdot(x, w1b_ref[...]) + b1_ref[0]
        glu = jnp.minimum(hb[:, :dff], SWIGLU_LIMIT)
        lin = jnp.clip(hb[:, dff:], -SWIGLU_LIMIT, SWIGLU_LIMIT)
        act = glu * jax.nn.sigmoid(SWIGLU_ALPHA * glu) * (lin + 1.0)
        y = _bdot(act.astype(BF16), w2b_ref[...]) + b2_ref[0]
        row = lax.broadcasted_iota(jnp.int32, y.shape, 0)
        y = jnp.where((row >= lo_ref[w]) & (row < hi_ref[w]), y, 0.0).reshape(ys_ref.shape)

        @pl.when(first_ref[w] == 1)
        def _():
            ys_ref[...] = y

        @pl.when(first_ref[w] == 0)
        def _():
            ys_ref[...] = ys_ref[...] + y


def _experts(xs, items, w1, b1, w2, b2, tm):
    a = xs.shape[0]
    ne, d, dff2 = w1.shape
    dff = dff2 // 2
    n_items = a // tm + ne - 1
    tile_w, exp_w, lo_w, hi_w, first_w, nitem = items
    rows = pl.BlockSpec((tm,) + _row_tile(d), lambda w, tl, ex, lo, hi, fi, ni: (tl[w], 0, 0))
    per_expert = lambda w, tl, ex, lo, hi, fi, ni: (ex[w], 0, 0)
    grid_spec = pltpu.PrefetchScalarGridSpec(
        num_scalar_prefetch=6,
        grid=(n_items,),
        in_specs=[rows,
                  pl.BlockSpec((1, d, dff2), per_expert),
                  pl.BlockSpec((1, 1, dff2), per_expert),
                  pl.BlockSpec((1, dff, d), per_expert),
                  pl.BlockSpec((1, 1, d), per_expert)],
        out_specs=rows,
        scratch_shapes=[pltpu.VMEM((d, dff2), BF16), pltpu.VMEM((dff, d), BF16)],
    )
    return pl.pallas_call(
        _expert_kernel,
        grid_spec=grid_spec,
        out_shape=jax.ShapeDtypeStruct(xs.shape, F32),
        compiler_params=_cparams(("arbitrary",)),
        name="moe_experts",
    )(tile_w, exp_w, lo_w, hi_w, first_w, nitem,
      xs, w1, b1.reshape(ne, 1, dff2), w2, b2.reshape(ne, 1, d))


def _work_items(counts, a, tm):
    ne = counts.shape[0]
    n_items = a // tm + ne - 1
    ends = jnp.cumsum(counts)
    starts = ends - counts
    first_tile = starts // tm
    last_tile = (ends - 1) // tm
    nvis = jnp.where(counts > 0, last_tile - first_tile + 1, 0)
    vis_end = jnp.cumsum(nvis)
    vis_start = vis_end - nvis
    total = vis_end[-1]
    w = jnp.minimum(jnp.arange(n_items, dtype=jnp.int32), total - 1)
    e_w = jnp.sum((vis_end[None, :] <= w[:, None]).astype(jnp.int32), axis=1)
    hot = e_w[:, None] == jnp.arange(ne, dtype=jnp.int32)[None, :]
    pick = lambda v: jnp.sum(jnp.where(hot, v[None, :], 0), axis=1)
    tile_w = (pick(first_tile) + (w - pick(vis_start))).astype(jnp.int32)
    lo_w = jnp.maximum(pick(starts) - tile_w * tm, 0).astype(jnp.int32)
    hi_w = jnp.minimum(pick(ends) - tile_w * tm, tm).astype(jnp.int32)
    prev = jnp.concatenate([jnp.full((1,), -1, jnp.int32), tile_w[:-1]])
    first_w = (tile_w != prev).astype(jnp.int32)
    return starts, (tile_w, e_w, lo_w, hi_w, first_w, total.reshape(1).astype(jnp.int32))


def _combine_kernel(alpha, n_ref, dst_ref, nn_ref, ndst_ref, post_ref, gatet_ref, pos_ref,
                    ys_ref, xm_ref, g2_ref, lg_ref, lb_ref, o_ref, srt_ref, sem):
    i = pl.program_id(0)
    last = pl.num_programs(0) - 1
    slot = i % 2
    td = xm_ref.shape[0]
    n = td * TOP_K

    def fetch(cnt_ref, from_ref, s):
        _strip_copies(cnt_ref, from_ref,
                      lambda row, src, size: pltpu.make_async_copy(
                          ys_ref.at[pl.ds(src, size)], srt_ref.at[s, pl.ds(row, size)], sem.at[s]),
                      td)

    @pl.when(i == 0)
    def _():
        fetch(n_ref, dst_ref, slot)

    @pl.when(i < last)
    def _():
        fetch(nn_ref, ndst_ref, 1 - slot)

    pltpu.make_async_copy(ys_ref.at[pl.ds(0, n)], srt_ref.at[slot], sem.at[slot]).wait()

    post, gatet, pos = post_ref[...], gatet_ref[...], pos_ref[...]
    rows = lax.broadcasted_iota(jnp.int32, (n, td), 0)
    wsel = jnp.where(rows == post[0:1], gatet[0:1], 0.0)
    for k in range(1, TOP_K):
        wsel = wsel + jnp.where(rows == post[k:k + 1], gatet[k:k + 1], 0.0)
    gs = jnp.sum(wsel, axis=1, keepdims=True)
    yw = srt_ref[slot].reshape(n, xm_ref.shape[1]) * gs
    hi = yw.astype(BF16)
    lo = (yw - hi.astype(F32)).astype(BF16)
    lanes = lax.broadcasted_iota(jnp.int32, (td, n), 1)
    hit = lanes == pos[:, 0:1]
    for k in range(1, TOP_K):
        hit = hit | (lanes == pos[:, k:k + 1])
    unperm = jnp.where(hit, 1.0, 0.0).astype(BF16)
    f = _bdot(unperm, hi) + _bdot(unperm, lo)
    o_ref[...] = _ln(alpha * xm_ref[...] + g2_ref[0] * f) * lg_ref[...] + lb_ref[...]


def _combine(ys, pos_t, gate_t, pos, tile_n, tile_dst, x_mid, g2, lg, lb, alpha, td):
    t, d = x_mid.shape
    n = td * TOP_K
    nt = t // td
    per_b = nt // g2.shape[0]
    ne = tile_n.shape[-1]
    cur = lambda: pl.BlockSpec((1, 1, ne), lambda i: (i, 0, 0), memory_space=pltpu.SMEM)
    nxt = lambda: pl.BlockSpec((1, 1, ne), lambda i: (jnp.minimum(i + 1, nt - 1), 0, 0),
                               memory_space=pltpu.SMEM)
    const = lambda i: (0, 0)
    tile_n = tile_n.reshape(nt, 1, ne)
    tile_dst = tile_dst.reshape(nt, 1, ne)
    return pl.pallas_call(
        functools.partial(_combine_kernel, alpha),
        grid=(nt,),
        in_specs=[cur(), cur(), nxt(), nxt(),
                  pl.BlockSpec((TOP_K, td), lambda i: (0, i)),
                  pl.BlockSpec((TOP_K, td), lambda i: (0, i)),
                  pl.BlockSpec((td, TOP_K), lambda i: (i, 0)),
                  pl.BlockSpec(memory_space=pl.ANY),
                  pl.BlockSpec((td, d), lambda i: (i, 0)),
                  pl.BlockSpec((1, 1, d), lambda i: (i // per_b, 0, 0)),
                  pl.BlockSpec((1, d), const), pl.BlockSpec((1, d), const)],
        out_specs=pl.BlockSpec((td, d), lambda i: (i, 0)),
        out_shape=jax.ShapeDtypeStruct((t, d), F32),
        scratch_shapes=[pltpu.VMEM((2, n) + _row_tile(d), F32), pltpu.SemaphoreType.DMA((2,))],
        compiler_params=_cparams(("arbitrary",)),
        name="moe_combine_ln",
    )(tile_n, tile_dst, tile_n, tile_dst, pos_t, gate_t, pos, ys, x_mid, g2,
      lg.reshape(1, d), lb.reshape(1, d))


def _rope_tables(l):
    f32 = np.float32
    rows = l // GRID_W
    row = np.repeat(np.arange(rows, dtype=f32), GRID_W)
    col = np.tile(np.arange(GRID_W, dtype=f32), rows)
    n_freq = HEAD_DIM // 4
    inv_freq = np.power(f32(ROPE_BASE), -np.arange(n_freq, dtype=f32) / f32(n_freq)).astype(f32)
    ang_r = (row[:, None] * inv_freq).astype(f32)
    ang_c = (col[:, None] * inv_freq).astype(f32)
    zero = np.zeros_like(ang_r)
    cos_r, sin_r, cos_c, sin_c = np.cos(ang_r), np.sin(ang_r), np.cos(ang_c), np.sin(ang_c)
    cos_h = np.concatenate([cos_r, cos_r, cos_c, cos_c], axis=1)
    sa_h = np.concatenate([-sin_r, zero, -sin_c, zero], axis=1)
    sb_h = np.concatenate([zero, sin_r, zero, sin_c], axis=1)
    rep = 128 // HEAD_DIM
    return tuple(np.tile(a, (1, rep)).astype(f32) for a in (cos_h, sa_h, sb_h))


def _filter_features(l):
    f32 = np.float32
    bands = (HY_EMB_DIM - 1) // 2
    t = np.linspace(0.0, 1.0, l, dtype=f32)[:, None]
    omega = (f32(2.0 * math.pi) * np.arange(l, dtype=f32)[:, None] / f32(l)).astype(f32)
    f = np.linspace(1e-4, bands - 1, bands, dtype=f32)[None, :]
    ang = (f * omega).astype(f32)
    z = np.concatenate([t, np.cos(ang), -np.sin(ang)], axis=-1).astype(f32)
    min_decay = math.log(HY_DECAY_TARGET) / HY_FAST_DECAY_PCT
    max_decay = math.log(HY_DECAY_TARGET) / HY_SLOW_DECAY_PCT
    deltas = np.abs(np.linspace(min_decay, max_decay, HY_WIDTH, dtype=f32))
    decay = np.exp(-t * deltas).astype(f32)
    return z, decay


def kernel(x, c, ctx, c_ctx, w_mod, b_mod, w_in, attn_sink, hy_conv_w, hy_conv_b, hy_filt_w1,
           hy_filt_b1, hy_filt_w2, hy_filt_b2, hy_filt_w3, hy_skip, w_branch_attn, w_branch_hyena,
           w_out, ln1_g, ln1_b, router_w, router_b, exp_w1, exp_b1, exp_w2, exp_b2, ln2_g, ln2_b):
    depth = w_mod.shape[0]
    assert depth == 1, "only the single-layer configuration is implemented"
    b, l, d = x.shape
    t = b * l
    alpha = (2 * depth) ** 0.25
    hy_wc = 256
    expert_tm = 256

    n_cond = b + 1
    pad = (-n_cond) % 8
    cond = jnp.concatenate([c, c_ctx[None], jnp.zeros((pad, d), F32)], axis=0)
    mod = _modulation(cond, w_mod[0], b_mod[0])
    mod_x = mod[:b].reshape(b, 1, 6, d)
    sh1, sc1, g1, sh2, sc2, g2 = (mod_x[:, :, i] for i in range(6))
    mod_c = mod[b:b + 1].reshape(1, 1, 6, d)
    csh1, csc1 = mod_c[:, :, 0], mod_c[:, :, 1]

    w_in_b = w_in[0].astype(BF16)
    cos_t, sa_t, sb_t = (jnp.asarray(a) for a in _rope_tables(l))
    q, k, v, u_hy, gate_x = _in_projection(x, sh1, sc1, w_in_b, cos_t, sa_t, sb_t)
    k_c, v_c = _ctx_kv(ctx, csh1, csc1, w_in_b[:, K_OFF:HY_OFF])
    att = _attention(q, k, v, k_c, v_c, attn_sink[0])

    tabs_np = _dft_tables(l, hy_wc)
    tabs = (jnp.asarray(tabs_np[0]).astype(BF16), jnp.asarray(tabs_np[1]).astype(BF16)) + tuple(
        jnp.asarray(a) for a in tabs_np[2:])
    zfeat, decay = _filter_features(l)
    emb_pad = (-HY_EMB_DIM) % 128
    zfeat = np.pad(zfeat, ((0, 0), (0, emb_pad)))
    fw1 = jnp.pad(hy_filt_w1[0], ((0, emb_pad), (0, 0)))
    spectra = _hyena_filters(jnp.asarray(zfeat[0::2]), jnp.asarray(zfeat[1::2]), fw1, hy_filt_b1[0],
                             hy_filt_w2[0], hy_filt_b2[0], hy_filt_w3[0], jnp.asarray(decay[0::2]),
                             jnp.asarray(decay[1::2]), tabs, hy_wc)
    hy = _hyena(u_hy, hy_conv_w[0], hy_conv_b[0], hy_skip[0], spectra, tabs, hy_wc)

    x_mid, h2, logits = _merge(att, hy, gate_x, x, g1, sh2, sc2,
                               w_branch_attn[0].astype(BF16), w_branch_hyena[0].astype(BF16),
                               w_out[0].astype(BF16), ln1_g[0], ln1_b[0], router_w[0], router_b[0],
                               alpha)

    moe_td = 256
    gate_t, pos_t, tile_cnt, tile_base, counts = _route(
        jnp.transpose(logits.reshape(t, N_EXPERTS)), moe_td)
    starts, items = _work_items(counts[:, 0], t * TOP_K, expert_tm)
    tile_n = tile_cnt[:, :, 0]
    tile_dst = starts[None, :] + tile_base[:, :, 0]
    xs = _dispatch(h2.reshape(t, d), pos_t, tile_n, tile_dst, moe_td)
    ys = _experts(xs, items, exp_w1[0], exp_b1[0], exp_w2[0], exp_b2[0], expert_tm)
    out = _combine(ys, pos_t, gate_t, jnp.transpose(pos_t), tile_n, tile_dst, x_mid.reshape(t, d),
                   g2, ln2_g[0], ln2_b[0], alpha, moe_td)
    return out.reshape(b, l, d)
```

```python
import functools
import math

import numpy as np
import jax
import jax.numpy as jnp
from jax import lax
from jax.experimental import pallas as pl
from jax.experimental.pallas import tpu as pltpu

F32 = jnp.float32
BF16 = jnp.bfloat16
HIGHEST = lax.Precision.HIGHEST

GRID_W = 64
N_HEADS = 8
N_KV_HEADS = 2
GQA_GROUP = N_HEADS // N_KV_HEADS
HEAD_DIM = 64
WINDOW = 128
ATTN_BLOCK = 128
ROPE_BASE = 10000.0

HY_WIDTH = 512
HY_ORDER = 2
HY_SHORT_CONV = 3
HY_EMB_DIM = 33
HY_DECAY_TARGET = 1e-2
HY_FAST_DECAY_PCT = 0.3
HY_SLOW_DECAY_PCT = 1.5

N_EXPERTS = 32
TOP_K = 4
SWIGLU_LIMIT = 7.0
SWIGLU_ALPHA = 1.702
LN_EPS = 1e-5

Q_W = N_HEADS * HEAD_DIM
KV_W = N_KV_HEADS * HEAD_DIM
K_OFF = Q_W
V_OFF = K_OFF + KV_W
HY_OFF = V_OFF + KV_W
GATE_OFF = HY_OFF + (HY_ORDER + 1) * HY_WIDTH

VMEM_LIMIT = 56 * 1024 * 1024
NEG_BIG = -1e30


def _cparams(sem):
    return pltpu.CompilerParams(dimension_semantics=sem, vmem_limit_bytes=VMEM_LIMIT)


def _ln(x):
    mu = jnp.mean(x, axis=-1, keepdims=True)
    xc = x - mu
    var = jnp.mean(xc * xc, axis=-1, keepdims=True)
    return xc * lax.rsqrt(var + LN_EPS)


def _bdot(a, b):
    return jnp.dot(a, b, preferred_element_type=F32)


def _fdot(a, b):
    return jnp.dot(a, b, preferred_element_type=F32, precision=HIGHEST)


def _mod_kernel(c_ref, w_ref, b_ref, o_ref):
    c = c_ref[...]
    s = c * jax.nn.sigmoid(c)
    o_ref[...] = _fdot(s, w_ref[...]) + b_ref[...]


def _modulation(cond, w, b, tn=512):
    r, d = cond.shape
    n = w.shape[1]
    return pl.pallas_call(
        _mod_kernel,
        grid=(n // tn,),
        in_specs=[pl.BlockSpec((r, d), lambda j: (0, 0)),
                  pl.BlockSpec((d, tn), lambda j: (0, j)),
                  pl.BlockSpec((1, tn), lambda j: (0, j))],
        out_specs=pl.BlockSpec((r, tn), lambda j: (0, j)),
        out_shape=jax.ShapeDtypeStruct((r, n), F32),
        compiler_params=_cparams(("arbitrary",)),
        name="modulation",
    )(cond, w, b.reshape(1, n))


def _rope(t, cos, sa, sb):
    n = t.shape[-1]
    return t * cos + pltpu.roll(t, n - 16, 1) * sa + pltpu.roll(t, 16, 1) * sb


def _inproj_kernel(x_ref, sh_ref, sc_ref, w_ref, cos_ref, sa_ref, sb_ref,
                   q_ref, k_ref, v_ref, u_ref, g_ref):
    h = _ln(x_ref[0]) * (1.0 + sc_ref[0]) + sh_ref[0]
    hb = h.astype(BF16)
    cos, sa, sb = cos_ref[...], sa_ref[...], sb_ref[...]
    scale = HEAD_DIM ** -0.5
    for j in range(Q_W // 128):
        t = _bdot(hb, w_ref[:, j * 128:(j + 1) * 128])
        q_ref[0, :, j * 128:(j + 1) * 128] = (_rope(t, cos, sa, sb) * scale).astype(BF16)
    t = _bdot(hb, w_ref[:, K_OFF:V_OFF])
    k_ref[0] = _rope(t, cos, sa, sb).astype(BF16)
    v_ref[0] = _bdot(hb, w_ref[:, V_OFF:HY_OFF]).astype(BF16)
    for j in range((GATE_OFF - HY_OFF) // 512):
        u_ref[0, :, j * 512:(j + 1) * 512] = _bdot(hb, w_ref[:, HY_OFF + j * 512:HY_OFF + (j + 1) * 512])
    n_gate = w_ref.shape[1] - GATE_OFF
    for j in range(n_gate // 512):
        g_ref[0, :, j * 512:(j + 1) * 512] = _bdot(
            hb, w_ref[:, GATE_OFF + j * 512:GATE_OFF + (j + 1) * 512])


def _in_projection(x, sh, sc, w_in_b, cos_t, sa_t, sb_t, tm=256):
    b, l, d = x.shape
    in_w = w_in_b.shape[1]
    hy_w = GATE_OFF - HY_OFF
    g_w = in_w - GATE_OFF
    row = lambda bi, i: (bi, i, 0)
    vec = lambda bi, i: (bi, 0, 0)
    tab = lambda bi, i: (i, 0)
    return pl.pallas_call(
        _inproj_kernel,
        grid=(b, l // tm),
        in_specs=[pl.BlockSpec((1, tm, d), row),
                  pl.BlockSpec((1, 1, d), vec),
                  pl.BlockSpec((1, 1, d), vec),
                  pl.BlockSpec((d, in_w), lambda bi, i: (0, 0)),
                  pl.BlockSpec((tm, 128), tab),
                  pl.BlockSpec((tm, 128), tab),
                  pl.BlockSpec((tm, 128), tab)],
        out_specs=[pl.BlockSpec((1, tm, Q_W), row),
                   pl.BlockSpec((1, tm, KV_W), row),
                   pl.BlockSpec((1, tm, KV_W), row),
                   pl.BlockSpec((1, tm, hy_w), row),
                   pl.BlockSpec((1, tm, g_w), row)],
        out_shape=[jax.ShapeDtypeStruct((b, l, Q_W), BF16),
                   jax.ShapeDtypeStruct((b, l, KV_W), BF16),
                   jax.ShapeDtypeStruct((b, l, KV_W), BF16),
                   jax.ShapeDtypeStruct((b, l, hy_w), F32),
                   jax.ShapeDtypeStruct((b, l, g_w), F32)],
        compiler_params=_cparams(("arbitrary", "arbitrary")),
        name="in_projection",
    )(x, sh, sc, w_in_b, cos_t, sa_t, sb_t)


def _ctx_kv_kernel(x_ref, sh_ref, sc_ref, w_ref, k_ref, v_ref):
    h = _ln(x_ref[0]) * (1.0 + sc_ref[0]) + sh_ref[0]
    kv = _bdot(h.astype(BF16), w_ref[...])
    k_ref[0] = kv[:, :KV_W].astype(BF16)
    v_ref[0] = kv[:, KV_W:].astype(BF16)


def _ctx_kv(ctx, sh, sc, w_kv_b):
    b, c, d = ctx.shape
    row = lambda bi: (bi, 0, 0)
    return pl.pallas_call(
        _ctx_kv_kernel,
        grid=(b,),
        in_specs=[pl.BlockSpec((1, c, d), row),
                  pl.BlockSpec((1, 1, d), lambda bi: (0, 0, 0)),
                  pl.BlockSpec((1, 1, d), lambda bi: (0, 0, 0)),
                  pl.BlockSpec((d, 2 * KV_W), lambda bi: (0, 0))],
        out_specs=[pl.BlockSpec((1, c, KV_W), row), pl.BlockSpec((1, c, KV_W), row)],
        out_shape=[jax.ShapeDtypeStruct((b, c, KV_W), BF16),
                   jax.ShapeDtypeStruct((b, c, KV_W), BF16)],
        compiler_params=_cparams(("arbitrary",)),
        name="ctx_kv",
    )(ctx, sh, sc, w_kv_b)


def _attn_kernel(sink_ref, q_ref, k_ref, v_ref, kc_ref, vc_ref, o_ref):
    l = k_ref.shape[1]
    span = ATTN_BLOCK + 2 * WINDOW
    j = pl.program_id(1)
    q0 = j * ATTN_BLOCK
    start = pl.multiple_of(jnp.clip(q0 - WINDOW, 0, l - span), ATTN_BLOCK)
    rows = GQA_GROUP * ATTN_BLOCK
    r_i = lax.broadcasted_iota(jnp.int32, (rows, span), 0)
    qpos = q0 + r_i % ATTN_BLOCK
    kpos = start + lax.broadcasted_iota(jnp.int32, (rows, span), 1)
    band = jnp.abs(kpos - qpos) <= WINDOW
    head_of_row = lax.broadcasted_iota(jnp.int32, (rows, 1), 0) // ATTN_BLOCK
    dn = (((1,), (1,)), ((), ()))
    outs = []
    for kv in range(N_KV_HEADS):
        ks = slice(kv * HEAD_DIM, (kv + 1) * HEAD_DIM)
        kl = k_ref[0, pl.ds(start, span), ks]
        vl = v_ref[0, pl.ds(start, span), ks]
        kc = kc_ref[0, :, ks]
        vc = vc_ref[0, :, ks]
        heads = [kv * GQA_GROUP + g for g in range(GQA_GROUP)]
        qg = jnp.concatenate([q_ref[0, :, h * HEAD_DIM:(h + 1) * HEAD_DIM] for h in heads], axis=0)
        sink = jnp.zeros((rows, 1), F32)
        for g, h in enumerate(heads):
            sink = jnp.where(head_of_row == g, sink_ref[h], sink)
        s_loc = lax.dot_general(qg, kl, dn, preferred_element_type=F32)
        s_loc = jnp.where(band, s_loc, NEG_BIG)
        s_ctx = lax.dot_general(qg, kc, dn, preferred_element_type=F32)
        m = jnp.maximum(jnp.max(s_loc, axis=1, keepdims=True),
                        jnp.max(s_ctx, axis=1, keepdims=True))
        m = jnp.maximum(m, sink)
        p_loc = jnp.exp(s_loc - m)
        p_ctx = jnp.exp(s_ctx - m)
        den = (jnp.sum(p_loc, axis=1, keepdims=True) + jnp.sum(p_ctx, axis=1, keepdims=True)
               + jnp.exp(sink - m))
        o = (_bdot(p_loc.astype(BF16), vl) + _bdot(p_ctx.astype(BF16), vc)) / den
        outs.extend(o[g * ATTN_BLOCK:(g + 1) * ATTN_BLOCK] for g in range(GQA_GROUP))
    o_ref[0] = jnp.concatenate(outs, axis=1).astype(BF16)


def _attention(q, k, v, kc, vc, sink):
    b, l, _ = q.shape
    c = kc.shape[1]
    full = lambda bi, j, s: (bi, 0, 0)
    grid_spec = pltpu.PrefetchScalarGridSpec(
        num_scalar_prefetch=1,
        grid=(b, l // ATTN_BLOCK),
        in_specs=[pl.BlockSpec((1, ATTN_BLOCK, Q_W), lambda bi, j, s: (bi, j, 0)),
                  pl.BlockSpec((1, l, KV_W), full),
                  pl.BlockSpec((1, l, KV_W), full),
                  pl.BlockSpec((1, c, KV_W), full),
                  pl.BlockSpec((1, c, KV_W), full)],
        out_specs=pl.BlockSpec((1, ATTN_BLOCK, Q_W), lambda bi, j, s: (bi, j, 0)),
    )
    return pl.pallas_call(
        _attn_kernel,
        grid_spec=grid_spec,
        out_shape=jax.ShapeDtypeStruct((b, l, Q_W), BF16),
        compiler_params=_cparams(("arbitrary", "arbitrary")),
        name="window_attention",
    )(sink, q, k, v, kc, vc)


def _dft_tables(l, wc):
    h = l // 2
    idx = np.arange(h, dtype=np.int64)
    ang = 2.0 * np.pi * ((idx[:, None] * idx[None, :]) % l).astype(np.float64) / l
    cm = np.cos(ang).astype(np.float32)
    sm = np.sin(ang).astype(np.float32)
    tw = 2.0 * np.pi * idx.astype(np.float64) / (2 * l)
    ones = np.ones((1, wc), np.float32)
    wr = np.cos(tw).astype(np.float32)[:, None] * ones
    wi = (-np.sin(tw)).astype(np.float32)[:, None] * ones
    sgn = np.where(idx % 2 == 0, 1.0, -1.0).astype(np.float32)[:, None] * ones
    wk = np.where(idx == 0, 1.0 / (2 * l), 2.0 / (2 * l)).astype(np.float32)[:, None] * ones
    return cm, sm, wr, wi, sgn, wk


def _half_spectrum(se, so, cm, sm, wr, wi, sgn):
    seb, sob = se.astype(BF16), so.astype(BF16)
    ce, ss_e = _bdot(cm, seb), _bdot(sm, seb)
    co, ss_o = _bdot(cm, sob), _bdot(sm, sob)
    vr = wr * co + wi * ss_o
    vi = wi * co - wr * ss_o
    zar, zai = ce + vr, vi - ss_e
    zbr, zbi = ce - vr, -ss_e - vi
    e_ny = jnp.sum(sgn * se, axis=0, keepdims=True)
    o_ny = jnp.sum(sgn * so, axis=0, keepdims=True)
    return zar, zai, zbr, zbi, e_ny, -o_ny


def _filter_kernel(ze_ref, zo_ref, w1_ref, b1_ref, w2_ref, b2_ref, w3f_ref, w3b_ref,
                   de_ref, do_ref, cm_ref, sm_ref, wr_ref, wi_ref, sgn_ref, wk_ref,
                   har_ref, hai_ref, hbr_ref, hbi_ref, hny_ref):
    cm, sm = cm_ref[...], sm_ref[...]
    wr, wi, sgn, wk = wr_ref[...], wi_ref[...], sgn_ref[...], wk_ref[...]
    h = cm.shape[0]

    def taps(z_ref, w3_ref, d_ref):
        a = jnp.sin(_fdot(z_ref[...], w1_ref[...]) + b1_ref[...])
        a = jnp.sin(_fdot(a, w2_ref[...]) + b2_ref[...])
        return _fdot(a, w3_ref[0]) * d_ref[...]

    fe, fo = taps(ze_ref, w3f_ref, de_ref), taps(zo_ref, w3f_ref, do_ref)
    be, bo = taps(ze_ref, w3b_ref, de_ref), taps(zo_ref, w3b_ref, do_ref)
    row = lax.broadcasted_iota(jnp.int32, be.shape, 0)
    be = jnp.where(row == 0, 0.0, be)
    far, fai, fbr, fbi, fnr, fni = _half_spectrum(fe, fo, cm, sm, wr, wi, sgn)
    bar, bai, bbr, bbi, bnr, bni = _half_spectrum(be, bo, cm, sm, wr, wi, sgn)
    har_ref[0] = wk * (far + bar)
    hai_ref[0] = wk * (fai - bai)
    hbr_ref[0] = wk * (fbr + bbr)
    hbi_ref[0] = wk * (fbi - bbi)
    ny_scale = 2.0 / (4 * h)
    nr = ny_scale * (fnr + bnr)
    ni = ny_scale * (fni - bni)
    rows = lax.broadcasted_iota(jnp.int32, (8, nr.shape[1]), 0)
    hny_ref[0] = jnp.where(rows == 0, nr, jnp.where(rows == 1, ni, 0.0))


def _hyena_filters(zfe, zfo, w1, b1, w2, b2, w3, dec_e, dec_o, tabs, wc):
    cm, sm, wr, wi, sgn, wk = tabs
    h = cm.shape[0]
    emb = zfe.shape[1]
    hid = w2.shape[0]
    nblk = HY_WIDTH // wc
    w3r = w3.reshape(hid, HY_ORDER * 2, HY_WIDTH).transpose(1, 0, 2)
    const = lambda o, cb: (0, 0)
    chan = lambda o, cb: (0, cb)
    out_spec = pl.BlockSpec((1, h, wc), lambda o, cb: (o, 0, cb))
    out_shape = jax.ShapeDtypeStruct((HY_ORDER, h, HY_WIDTH), F32)
    return pl.pallas_call(
        _filter_kernel,
        grid=(HY_ORDER, nblk),
        in_specs=[pl.BlockSpec((h, emb), const), pl.BlockSpec((h, emb), const),
                  pl.BlockSpec((emb, hid), const), pl.BlockSpec((1, hid), const),
                  pl.BlockSpec((hid, hid), const), pl.BlockSpec((1, hid), const),
                  pl.BlockSpec((1, hid, wc), lambda o, cb: (2 * o, 0, cb)),
                  pl.BlockSpec((1, hid, wc), lambda o, cb: (2 * o + 1, 0, cb)),
                  pl.BlockSpec((h, wc), chan), pl.BlockSpec((h, wc), chan),
                  pl.BlockSpec((h, h), const), pl.BlockSpec((h, h), const),
                  pl.BlockSpec((h, wc), const), pl.BlockSpec((h, wc), const),
                  pl.BlockSpec((h, wc), const), pl.BlockSpec((h, wc), const)],
        out_specs=[out_spec, out_spec, out_spec, out_spec,
                   pl.BlockSpec((1, 8, wc), lambda o, cb: (o, 0, cb))],
        out_shape=[out_shape, out_shape, out_shape, out_shape,
                   jax.ShapeDtypeStruct((HY_ORDER, 8, HY_WIDTH), F32)],
        compiler_params=_cparams(("arbitrary", "arbitrary")),
        name="hyena_filter_spectrum",
    )(zfe, zfo, w1, b1.reshape(1, hid), w2, b2.reshape(1, hid), w3r, w3r,
      dec_e, dec_o, cm, sm, wr, wi, sgn, wk)


def _long_conv(ze, zo, har, hai, hbr, hbi, hny, cm, sm, wr, wi, sgn):
    zar, zai, zbr, zbi, znr, zni = _half_spectrum(ze, zo, cm, sm, wr, wi, sgn)
    yar, yai = zar * har - zai * hai, zar * hai + zai * har
    ybr, ybi = zbr * hbr - zbi * hbi, zbr * hbi + zbi * hbr
    dr, di = yar - ybr, yai - ybi
    sar, sai = (yar + ybr).astype(BF16), (yai + ybi).astype(BF16)
    sbr, sbi = (dr * wr + di * wi).astype(BF16), (di * wr - dr * wi).astype(BF16)
    hnr, hni = hny[0:1], hny[1:2]
    ynr = znr * hnr - zni * hni
    yni = znr * hni + zni * hnr
    ye = _bdot(cm, sar) - _bdot(sm, sai) + sgn * ynr
    yo = _bdot(cm, sbr) - _bdot(sm, sbi) - sgn * yni
    return ye, yo


def _hyena_kernel(*refs):
    nsub = (len(refs) - 18) // 4
    u_refs = [refs[s * nsub:(s + 1) * nsub] for s in range(HY_ORDER + 1)]
    rest = refs[(HY_ORDER + 1) * nsub:]
    (cw0_ref, cw1_ref, cw2_ref, cb0_ref, cb1_ref, cb2_ref, skip_ref,
     har_ref, hai_ref, hbr_ref, hbi_ref, hny_ref,
     cm_ref, sm_ref, wr_ref, wi_ref, sgn_ref, o_ref) = rest[:18]
    il_refs = rest[18:]
    h = cm_ref.shape[0]
    cm, sm = cm_ref[...], sm_ref[...]
    wr, wi, sgn = wr_ref[...], wi_ref[...], sgn_ref[...]
    row = lax.broadcasted_iota(jnp.int32, (h, wr.shape[1]), 0)

    def short_conv(u_slabs, cw_ref, cb_ref):
        ue = jnp.concatenate([r[0, pl.ds(0, h, stride=2), :] for r in u_slabs], axis=1)
        uo = jnp.concatenate([r[0, pl.ds(1, h, stride=2), :] for r in u_slabs], axis=1)
        w0, w1, w2 = cw_ref[0:1], cw_ref[1:2], cw_ref[2:3]
        uo_prev = jnp.where(row == 0, 0.0, pltpu.roll(uo, 1, 0))
        ue_next = jnp.where(row == h - 1, 0.0, pltpu.roll(ue, h - 1, 0))
        cb = cb_ref[...]
        se = cb + uo_prev * w0 + ue * w1 + uo * w2
        so = cb + ue * w0 + uo * w1 + ue_next * w2
        return se, so

    x_streams = [short_conv(u_refs[0], cw0_ref, cb0_ref), short_conv(u_refs[1], cw1_ref, cb1_ref)]
    ze, zo = short_conv(u_refs[2], cw2_ref, cb2_ref)
    for o in range(HY_ORDER):
        ce, co = _long_conv(ze, zo, har_ref[o], hai_ref[o], hbr_ref[o], hbi_ref[o], hny_ref[o],
                            cm, sm, wr, wi, sgn)
        sk = skip_ref[o:o + 1]
        xe, xo = x_streams[o]
        ze = xe * (ce + sk * ze)
        zo = xo * (co + sk * zo)
    for j, il in enumerate(il_refs):
        il[pl.ds(0, h, stride=2), :] = ze[:, j * 128:(j + 1) * 128]
        il[pl.ds(1, h, stride=2), :] = zo[:, j * 128:(j + 1) * 128]
        o_ref[0, :, j * 128:(j + 1) * 128] = il[...]


def _hyena(u, conv_w, conv_b, skip, spectra, tabs, wc):
    b, l, _ = u.shape
    har, hai, hbr, hbi, hny = spectra
    cm, sm, wr, wi, sgn, _ = tabs
    h = l // 2
    nblk = HY_WIDTH // wc
    nsub = wc // 128
    conv_b = conv_b.reshape(1, -1)

    def ublk(s, j):
        return pl.BlockSpec((1, l, 128), lambda cb, bi: (bi, 0, (cb + s * nblk) * nsub + j))

    def wblk(r, s):
        return pl.BlockSpec((r, wc), lambda cb, bi: (0, cb + s * nblk))

    const = lambda cb, bi: (0, 0)
    spec = pl.BlockSpec((HY_ORDER, h, wc), lambda cb, bi: (0, 0, cb))
    return pl.pallas_call(
        _hyena_kernel,
        grid=(nblk, b),
        in_specs=[ublk(s, j) for s in range(HY_ORDER + 1) for j in range(nsub)] + [
                  wblk(HY_SHORT_CONV, 0), wblk(HY_SHORT_CONV, 1), wblk(HY_SHORT_CONV, 2),
                  wblk(1, 0), wblk(1, 1), wblk(1, 2),
                  pl.BlockSpec((HY_ORDER, wc), lambda cb, bi: (0, cb)),
                  spec, spec, spec, spec,
                  pl.BlockSpec((HY_ORDER, 8, wc), lambda cb, bi: (0, 0, cb)),
                  pl.BlockSpec((h, h), const), pl.BlockSpec((h, h), const),
                  pl.BlockSpec((h, wc), const), pl.BlockSpec((h, wc), const),
                  pl.BlockSpec((h, wc), const)],
        out_specs=pl.BlockSpec((1, l, wc), lambda cb, bi: (bi, 0, cb)),
        out_shape=jax.ShapeDtypeStruct((b, l, HY_WIDTH), F32),
        scratch_shapes=[pltpu.VMEM((l, 128), F32) for _ in range(nsub)],
        compiler_params=_cparams(("arbitrary", "arbitrary")),
        name="hyena_long_conv",
    )(*([u] * ((HY_ORDER + 1) * nsub)), conv_w, conv_w, conv_w, conv_b, conv_b, conv_b, skip,
      har, hai, hbr, hbi, hny, cm, sm, wr, wi, sgn)


def _merge_kernel(alpha, att_ref, hy_ref, gate_ref, x_ref, g1_ref, sh2_ref, sc2_ref,
                  wba_ref, wbh_ref, wo_ref, l1g_ref, l1b_ref, rw_ref, rb_ref,
                  xmid_ref, h2_ref, logit_ref):
    tm, d = x_ref.shape[1], x_ref.shape[2]
    rw = rw_ref[...]
    r_hi = rw.astype(BF16)
    r_lo = (rw - r_hi.astype(F32)).astype(BF16)
    dn = (((1,), (1,)), ((), ()))
    nt_dot = lambda p, q: lax.dot_general(p, q, dn, preferred_element_type=F32)
    half = tm // 2
    for r in (slice(0, half), slice(half, tm)):
        a = _bdot(att_ref[0, r], wba_ref[...])
        hh = _bdot(hy_ref[0, r].astype(BF16), wbh_ref[...])
        ga = jax.nn.sigmoid(gate_ref[0, r, :d])
        gh = jax.nn.sigmoid(gate_ref[0, r, d:])
        y = _bdot((ga * a + gh * hh).astype(BF16), wo_ref[...])
        xm = _ln(alpha * x_ref[0, r] + g1_ref[0] * y) * l1g_ref[...] + l1b_ref[...]
        xmid_ref[0, r] = xm
        h2 = _ln(xm) * (1.0 + sc2_ref[0]) + sh2_ref[0]
        h_hi = h2.astype(BF16)
        h2_ref[0, r] = h_hi
        h_lo = (h2 - h_hi.astype(F32)).astype(BF16)
        logit_ref[:, r] = nt_dot(r_hi, h_hi) + nt_dot(r_hi, h_lo) + nt_dot(r_lo, h_hi) + rb_ref[...]


def _merge(att, hy, gate, x, g1, sh2, sc2, wba, wbh, wo, l1g, l1b, rw_t, rb, alpha, tm=512):
    b, l, d = x.shape
    ne = rw_t.shape[0]
    nt = l // tm
    row = lambda bi, i: (bi, i, 0)
    vec = lambda bi, i: (bi, 0, 0)
    const = lambda bi, i: (0, 0)
    return pl.pallas_call(
        functools.partial(_merge_kernel, alpha),
        grid=(b, l // tm),
        in_specs=[pl.BlockSpec((1, tm, Q_W), row),
                  pl.BlockSpec((1, tm, HY_WIDTH), row),
                  pl.BlockSpec((1, tm, 2 * d), row),
                  pl.BlockSpec((1, tm, d), row),
                  pl.BlockSpec((1, 1, d), vec), pl.BlockSpec((1, 1, d), vec),
                  pl.BlockSpec((1, 1, d), vec),
                  pl.BlockSpec((Q_W, d), const), pl.BlockSpec((HY_WIDTH, d), const),
                  pl.BlockSpec((d, d), const),
                  pl.BlockSpec((1, d), const), pl.BlockSpec((1, d), const),
                  pl.BlockSpec((ne, d), const), pl.BlockSpec((ne, 1), const)],
        out_specs=[pl.BlockSpec((1, tm, d), row), pl.BlockSpec((1, tm, d), row),
                   pl.BlockSpec((ne, tm), lambda bi, i: (0, bi * nt + i))],
        out_shape=[jax.ShapeDtypeStruct((b, l, d), F32),
                   jax.ShapeDtypeStruct((b, l, d), BF16),
                   jax.ShapeDtypeStruct((ne, b * l), F32)],
        compiler_params=_cparams(("arbitrary", "arbitrary")),
        name="merge_ln_router",
    )(att, hy, gate, x, g1, sh2, sc2, wba, wbh, wo, l1g.reshape(1, d), l1b.reshape(1, d),
      rw_t, rb.reshape(ne, 1))


def _route_kernel(lg_ref, gate_ref, pos_ref, tcnt_ref, tbase_ref, cnt_ref, run_ref):
    @pl.when(pl.program_id(0) == 0)
    def _():
        run_ref[...] = jnp.zeros_like(run_ref)

    lg = lg_ref[...]
    ne, tr = lg.shape
    sub = lax.broadcasted_iota(jnp.int32, (ne, tr), 0)
    work = lg
    vals, hots = [], []
    for _ in range(TOP_K):
        m = jnp.max(work, axis=0, keepdims=True)
        idx = jnp.min(jnp.where(work == m, sub, ne), axis=0, keepdims=True)
        hot = sub == idx
        vals.append(m)
        hots.append(hot)
        work = jnp.where(hot, -jnp.inf, work)
    exps = [jnp.exp(v - vals[0]) for v in vals]
    den = exps[0] + exps[1] + exps[2] + exps[3]
    member = jnp.zeros((ne, tr), F32)
    for hot in hots:
        member = member + jnp.where(hot, 1.0, 0.0)
    r_i = lax.broadcasted_iota(jnp.int32, (tr, tr), 0)
    c_i = lax.broadcasted_iota(jnp.int32, (tr, tr), 1)
    earlier = jnp.where(r_i < c_i, 1.0, 0.0).astype(BF16)
    prefix = _bdot(member.astype(BF16), earlier)
    cnt = jnp.broadcast_to(jnp.sum(member, axis=1, keepdims=True), (ne, 128))
    e_r = lax.broadcasted_iota(jnp.int32, (ne, ne), 0)
    e_c = lax.broadcasted_iota(jnp.int32, (ne, ne), 1)
    lower = jnp.where(e_c < e_r, 1.0, 0.0).astype(BF16)
    off = _bdot(lower, cnt.astype(BF16))
    base = off[:, 0:1] + prefix
    sub_k = lax.broadcasted_iota(jnp.int32, (TOP_K, tr), 0)
    gate = jnp.zeros((TOP_K, tr), F32)
    pos = jnp.zeros((TOP_K, tr), F32)
    for k in range(TOP_K):
        pk = jnp.sum(jnp.where(hots[k], base, 0.0), axis=0, keepdims=True)
        gate = jnp.where(sub_k == k, exps[k] / den, gate)
        pos = jnp.where(sub_k == k, pk, pos)
    gate_ref[...] = gate
    pos_ref[...] = pos.astype(jnp.int32)
    tcnt_ref[0] = cnt.astype(jnp.int32)
    tbase_ref[0] = run_ref[...].astype(jnp.int32)
    run_ref[...] = run_ref[...] + cnt
    cnt_ref[...] = run_ref[...].astype(jnp.int32)


def _route(logits_t, tr):
    ne, t = logits_t.shape
    nt = t // tr
    tok = lambda i: (0, i)
    tile = lambda i: (i, 0, 0)
    return pl.pallas_call(
        _route_kernel,
        grid=(nt,),
        in_specs=[pl.BlockSpec((ne, tr), tok)],
        out_specs=[pl.BlockSpec((TOP_K, tr), tok), pl.BlockSpec((TOP_K, tr), tok),
                   pl.BlockSpec((1, ne, 128), tile), pl.BlockSpec((1, ne, 128), tile),
                   pl.BlockSpec((ne, 128), lambda i: (0, 0))],
        out_shape=[jax.ShapeDtypeStruct((TOP_K, t), F32),
                   jax.ShapeDtypeStruct((TOP_K, t), jnp.int32),
                   jax.ShapeDtypeStruct((nt, ne, 128), jnp.int32),
                   jax.ShapeDtypeStruct((nt, ne, 128), jnp.int32),
                   jax.ShapeDtypeStruct((ne, 128), jnp.int32)],
        scratch_shapes=[pltpu.VMEM((ne, 128), F32)],
        compiler_params=_cparams(("arbitrary",)),
        name="route_topk",
    )(logits_t)


def _row_tile(d):
    return (d // 128, 128)


def _strip_copies(n_ref, dst_ref, make_copy, max_rows):
    ne = n_ref.shape[2]
    bits = [1 << s for s in range(max_rows.bit_length() - 1, -1, -1)]

    def body(e, off):
        n = n_ref[0, 0, e]
        dst = dst_ref[0, 0, e]
        done = jnp.int32(0)
        for bit in bits:
            part = n & bit

            @pl.when(part != 0)
            def _():
                make_copy(off + done, dst + done, bit).start()

            done = done + part
        return off + n

    lax.fori_loop(0, ne, body, jnp.int32(0))


def _dispatch_kernel(n_ref, dst_ref, pos_ref, h_ref, xs_ref, srt_ref, sem):
    i = pl.program_id(0)
    slot = i % 2
    td = h_ref.shape[0]
    n = td * TOP_K
    pos = pos_ref[...]
    rows = lax.broadcasted_iota(jnp.int32, (n, td), 0)
    hit = rows == pos[0:1]
    for k in range(1, TOP_K):
        hit = hit | (rows == pos[k:k + 1])
    perm = jnp.where(hit, 1.0, 0.0).astype(BF16)
    srt_ref[slot] = _bdot(perm, h_ref[...]).reshape(srt_ref.shape[1:])

    def all_rows(s):
        return pltpu.make_async_copy(srt_ref.at[s], xs_ref.at[pl.ds(0, n)], sem.at[s])

    _strip_copies(n_ref, dst_ref,
                  lambda src, dst, size: pltpu.make_async_copy(
                      srt_ref.at[slot, pl.ds(src, size)], xs_ref.at[pl.ds(dst, size)], sem.at[slot]),
                  td)

    @pl.when(i > 0)
    def _():
        all_rows(1 - slot).wait()

    @pl.when(i == pl.num_programs(0) - 1)
    def _():
        all_rows(slot).wait()


def _dispatch(h2, pos_t, tile_n, tile_dst, td):
    t, d = h2.shape
    a = t * TOP_K
    n = td * TOP_K
    nt = t // td
    ne = tile_n.shape[-1]
    smem = lambda: pl.BlockSpec((1, 1, ne), lambda i: (i, 0, 0), memory_space=pltpu.SMEM)
    return pl.pallas_call(
        _dispatch_kernel,
        grid=(nt,),
        in_specs=[smem(), smem(),
                  pl.BlockSpec((TOP_K, td), lambda i: (0, i)),
                  pl.BlockSpec((td, d), lambda i: (i, 0))],
        out_specs=pl.BlockSpec(memory_space=pl.ANY),
        out_shape=jax.ShapeDtypeStruct((a,) + _row_tile(d), F32),
        scratch_shapes=[pltpu.VMEM((2, n) + _row_tile(d), F32), pltpu.SemaphoreType.DMA((2,))],
        compiler_params=_cparams(("arbitrary",)),
        name="moe_dispatch",
    )(tile_n.reshape(nt, 1, ne), tile_dst.reshape(nt, 1, ne), pos_t, h2)


def _expert_kernel(tile_ref, exp_ref, lo_ref, hi_ref, first_ref, nitem_ref,
                   xs_ref, w1_ref, b1_ref, w2_ref, b2_ref, ys_ref, w1b_ref, w2b_ref):
    w = pl.program_id(0)
    tm = xs_ref.shape[0]
    d, dff = w2_ref.shape[2], w2_ref.shape[1]

    @pl.when((w == 0) | (exp_ref[w] != exp_ref[jnp.maximum(w - 1, 0)]))
    def _():
        w1b_ref[...] = w1_ref[0].astype(BF16)
        w2b_ref[...] = w2_ref[0].astype(BF16)

    @pl.when(w < nitem_ref[0])
    def _():
        x = xs_ref[...].reshape(tm, d).astype(BF16)
        hb = _bdot(x, w1b_ref[...]) + b1_ref[0]
        glu = jnp.minimum(hb[:, :dff], SWIGLU_LIMIT)
        lin = jnp.clip(hb[:, dff:], -SWIGLU_LIMIT, SWIGLU_LIMIT)
        act = glu * jax.nn.sigmoid(SWIGLU_ALPHA * glu) * (lin + 1.0)
        y = _bdot(act.astype(BF16), w2b_ref[...]) + b2_ref[0]
        row = lax.broadcasted_iota(jnp.int32, y.shape, 0)
        y = jnp.where((row >= lo_ref[w]) & (row < hi_ref[w]), y, 0.0).reshape(ys_ref.shape)

        @pl.when(first_ref[w] == 1)
        def _():
            ys_ref[...] = y

        @pl.when(first_ref[w] == 0)
        def _():
            ys_ref[...] = ys_ref[...] + y


def _experts(xs, items, w1, b1, w2, b2, tm):
    a = xs.shape[0]
    ne, d, dff2 = w1.shape
    dff = dff2 // 2
    n_items = a // tm + ne - 1
    tile_w, exp_w, lo_w, hi_w, first_w, nitem = items
    rows = pl.BlockSpec((tm,) + _row_tile(d), lambda w, tl, ex, lo, hi, fi, ni: (tl[w], 0, 0))
    per_expert = lambda w, tl, ex, lo, hi, fi, ni: (ex[w], 0, 0)
    grid_spec = pltpu.PrefetchScalarGridSpec(
        num_scalar_prefetch=6,
        grid=(n_items,),
        in_specs=[rows,
                  pl.BlockSpec((1, d, dff2), per_expert),
                  pl.BlockSpec((1, 1, dff2), per_expert),
                  pl.BlockSpec((1, dff, d), per_expert),
                  pl.BlockSpec((1, 1, d), per_expert)],
        out_specs=rows,
        scratch_shapes=[pltpu.VMEM((d, dff2), BF16), pltpu.VMEM((dff, d), BF16)],
    )
    return pl.pallas_call(
        _expert_kernel,
        grid_spec=grid_spec,
        out_shape=jax.ShapeDtypeStruct(xs.shape, F32),
        compiler_params=_cparams(("arbitrary",)),
        name="moe_experts",
    )(tile_w, exp_w, lo_w, hi_w, first_w, nitem,
      xs, w1, b1.reshape(ne, 1, dff2), w2, b2.reshape(ne, 1, d))


def _work_items(counts, a, tm):
    ne = counts.shape[0]
    n_items = a // tm + ne - 1
    ends = jnp.cumsum(counts)
    starts = ends - counts
    first_tile = starts // tm
    last_tile = (ends - 1) // tm
    nvis = jnp.where(counts > 0, last_tile - first_tile + 1, 0)
    vis_end = jnp.cumsum(nvis)
    vis_start = vis_end - nvis
    total = vis_end[-1]
    w = jnp.minimum(jnp.arange(n_items, dtype=jnp.int32), total - 1)
    e_w = jnp.sum((vis_end[None, :] <= w[:, None]).astype(jnp.int32), axis=1)
    hot = e_w[:, None] == jnp.arange(ne, dtype=jnp.int32)[None, :]
    pick = lambda v: jnp.sum(jnp.where(hot, v[None, :], 0), axis=1)
    tile_w = (pick(first_tile) + (w - pick(vis_start))).astype(jnp.int32)
    lo_w = jnp.maximum(pick(starts) - tile_w * tm, 0).astype(jnp.int32)
    hi_w = jnp.minimum(pick(ends) - tile_w * tm, tm).astype(jnp.int32)
    prev = jnp.concatenate([jnp.full((1,), -1, jnp.int32), tile_w[:-1]])
    first_w = (tile_w != prev).astype(jnp.int32)
    return starts, (tile_w, e_w, lo_w, hi_w, first_w, total.reshape(1).astype(jnp.int32))


def _combine_kernel(alpha, n_ref, dst_ref, nn_ref, ndst_ref, post_ref, gatet_ref, pos_ref,
                    ys_ref, xm_ref, g2_ref, lg_ref, lb_ref, o_ref, srt_ref, sem):
    i = pl.program_id(0)
    last = pl.num_programs(0) - 1
    slot = i % 2
    td = xm_ref.shape[0]
    n = td * TOP_K

    def fetch(cnt_ref, from_ref, s):
        _strip_copies(cnt_ref, from_ref,
                      lambda row, src, size: pltpu.make_async_copy(
                          ys_ref.at[pl.ds(src, size)], srt_ref.at[s, pl.ds(row, size)], sem.at[s]),
                      td)

    @pl.when(i == 0)
    def _():
        fetch(n_ref, dst_ref, slot)

    @pl.when(i < last)
    def _():
        fetch(nn_ref, ndst_ref, 1 - slot)

    pltpu.make_async_copy(ys_ref.at[pl.ds(0, n)], srt_ref.at[slot], sem.at[slot]).wait()

    post, gatet, pos = post_ref[...], gatet_ref[...], pos_ref[...]
    rows = lax.broadcasted_iota(jnp.int32, (n, td), 0)
    wsel = jnp.where(rows == post[0:1], gatet[0:1], 0.0)
    for k in range(1, TOP_K):
        wsel = wsel + jnp.where(rows == post[k:k + 1], gatet[k:k + 1], 0.0)
    gs = jnp.sum(wsel, axis=1, keepdims=True)
    yw = srt_ref[slot].reshape(n, xm_ref.shape[1]) * gs
    hi = yw.astype(BF16)
    lo = (yw - hi.astype(F32)).astype(BF16)
    lanes = lax.broadcasted_iota(jnp.int32, (td, n), 1)
    hit = lanes == pos[:, 0:1]
    for k in range(1, TOP_K):
        hit = hit | (lanes == pos[:, k:k + 1])
    unperm = jnp.where(hit, 1.0, 0.0).astype(BF16)
    f = _bdot(unperm, hi) + _bdot(unperm, lo)
    o_ref[...] = _ln(alpha * xm_ref[...] + g2_ref[0] * f) * lg_ref[...] + lb_ref[...]


def _combine(ys, pos_t, gate_t, pos, tile_n, tile_dst, x_mid, g2, lg, lb, alpha, td):
    t, d = x_mid.shape
    n = td * TOP_K
    nt = t // td
    per_b = nt // g2.shape[0]
    ne = tile_n.shape[-1]
    cur = lambda: pl.BlockSpec((1, 1, ne), lambda i: (i, 0, 0), memory_space=pltpu.SMEM)
    nxt = lambda: pl.BlockSpec((1, 1, ne), lambda i: (jnp.minimum(i + 1, nt - 1), 0, 0),
                               memory_space=pltpu.SMEM)
    const = lambda i: (0, 0)
    tile_n = tile_n.reshape(nt, 1, ne)
    tile_dst = tile_dst.reshape(nt, 1, ne)
    return pl.pallas_call(
        functools.partial(_combine_kernel, alpha),
        grid=(nt,),
        in_specs=[cur(), cur(), nxt(), nxt(),
                  pl.BlockSpec((TOP_K, td), lambda i: (0, i)),
                  pl.BlockSpec((TOP_K, td), lambda i: (0, i)),
                  pl.BlockSpec((td, TOP_K), lambda i: (i, 0)),
                  pl.BlockSpec(memory_space=pl.ANY),
                  pl.BlockSpec((td, d), lambda i: (i, 0)),
                  pl.BlockSpec((1, 1, d), lambda i: (i // per_b, 0, 0)),
                  pl.BlockSpec((1, d), const), pl.BlockSpec((1, d), const)],
        out_specs=pl.BlockSpec((td, d), lambda i: (i, 0)),
        out_shape=jax.ShapeDtypeStruct((t, d), F32),
        scratch_shapes=[pltpu.VMEM((2, n) + _row_tile(d), F32), pltpu.SemaphoreType.DMA((2,))],
        compiler_params=_cparams(("arbitrary",)),
        name="moe_combine_ln",
    )(tile_n, tile_dst, tile_n, tile_dst, pos_t, gate_t, pos, ys, x_mid, g2,
      lg.reshape(1, d), lb.reshape(1, d))


def _rope_tables(l):
    f32 = np.float32
    rows = l // GRID_W
    row = np.repeat(np.arange(rows, dtype=f32), GRID_W)
    col = np.tile(np.arange(GRID_W, dtype=f32), rows)
    n_freq = HEAD_DIM // 4
    inv_freq = np.power(f32(ROPE_BASE), -np.arange(n_freq, dtype=f32) / f32(n_freq)).astype(f32)
    ang_r = (row[:, None] * inv_freq).astype(f32)
    ang_c = (col[:, None] * inv_freq).astype(f32)
    zero = np.zeros_like(ang_r)
    cos_r, sin_r, cos_c, sin_c = np.cos(ang_r), np.sin(ang_r), np.cos(ang_c), np.sin(ang_c)
    cos_h = np.concatenate([cos_r, cos_r, cos_c, cos_c], axis=1)
    sa_h = np.concatenate([-sin_r, zero, -sin_c, zero], axis=1)
    sb_h = np.concatenate([zero, sin_r, zero, sin_c], axis=1)
    rep = 128 // HEAD_DIM
    return tuple(np.tile(a, (1, rep)).astype(f32) for a in (cos_h, sa_h, sb_h))


def _filter_features(l):
    f32 = np.float32
    bands = (HY_EMB_DIM - 1) // 2
    t = np.linspace(0.0, 1.0, l, dtype=f32)[:, None]
    omega = (f32(2.0 * math.pi) * np.arange(l, dtype=f32)[:, None] / f32(l)).astype(f32)
    f = np.linspace(1e-4, bands - 1, bands, dtype=f32)[None, :]
    ang = (f * omega).astype(f32)
    z = np.concatenate([t, np.cos(ang), -np.sin(ang)], axis=-1).astype(f32)
    min_decay = math.log(HY_DECAY_TARGET) / HY_FAST_DECAY_PCT
    max_decay = math.log(HY_DECAY_TARGET) / HY_SLOW_DECAY_PCT
    deltas = np.abs(np.linspace(min_decay, max_decay, HY_WIDTH, dtype=f32))
    decay = np.exp(-t * deltas).astype(f32)
    return z, decay


def kernel(x, c, ctx, c_ctx, w_mod, b_mod, w_in, attn_sink, hy_conv_w, hy_conv_b, hy_filt_w1,
           hy_filt_b1, hy_filt_w2, hy_filt_b2, hy_filt_w3, hy_skip, w_branch_attn, w_branch_hyena,
           w_out, ln1_g, ln1_b, router_w, router_b, exp_w1, exp_b1, exp_w2, exp_b2, ln2_g, ln2_b):
    depth = w_mod.shape[0]
    assert depth == 1, "only the single-layer configuration is implemented"
    b, l, d = x.shape
    t = b * l
    alpha = (2 * depth) ** 0.25
    hy_wc = 256
    expert_tm = 256

    n_cond = b + 1
    pad = (-n_cond) % 8
    cond = jnp.concatenate([c, c_ctx[None], jnp.zeros((pad, d), F32)], axis=0)
    mod = _modulation(cond, w_mod[0], b_mod[0])
    mod_x = mod[:b].reshape(b, 1, 6, d)
    sh1, sc1, g1, sh2, sc2, g2 = (mod_x[:, :, i] for i in range(6))
    mod_c = mod[b:b + 1].reshape(1, 1, 6, d)
    csh1, csc1 = mod_c[:, :, 0], mod_c[:, :, 1]

    w_in_b = w_in[0].astype(BF16)
    cos_t, sa_t, sb_t = (jnp.asarray(a) for a in _rope_tables(l))
    q, k, v, u_hy, gate_x = _in_projection(x, sh1, sc1, w_in_b, cos_t, sa_t, sb_t)
    k_c, v_c = _ctx_kv(ctx, csh1, csc1, w_in_b[:, K_OFF:HY_OFF])
    att = _attention(q, k, v, k_c, v_c, attn_sink[0])

    tabs_np = _dft_tables(l, hy_wc)
    tabs = (jnp.asarray(tabs_np[0]).astype(BF16), jnp.asarray(tabs_np[1]).astype(BF16)) + tuple(
        jnp.asarray(a) for a in tabs_np[2:])
    zfeat, decay = _filter_features(l)
    emb_pad = (-HY_EMB_DIM) % 128
    zfeat = np.pad(zfeat, ((0, 0), (0, emb_pad)))
    fw1 = jnp.pad(hy_filt_w1[0], ((0, emb_pad), (0, 0)))
    spectra = _hyena_filters(jnp.asarray(zfeat[0::2]), jnp.asarray(zfeat[1::2]), fw1, hy_filt_b1[0],
                             hy_filt_w2[0], hy_filt_b2[0], hy_filt_w3[0], jnp.asarray(decay[0::2]),
                             jnp.asarray(decay[1::2]), tabs, hy_wc)
    hy = _hyena(u_hy, hy_conv_w[0], hy_conv_b[0], hy_skip[0], spectra, tabs, hy_wc)

    x_mid, h2, logits_t = _merge(att, hy, gate_x, x, g1, sh2, sc2,
                                 w_branch_attn[0].astype(BF16), w_branch_hyena[0].astype(BF16),
                                 w_out[0].astype(BF16), ln1_g[0], ln1_b[0],
                                 jnp.transpose(router_w[0]), router_b[0], alpha)

    moe_td = 256
    gate_t, pos_t, tile_cnt, tile_base, counts = _route(logits_t, moe_td)
    starts, items = _work_items(counts[:, 0], t * TOP_K, expert_tm)
    tile_n = tile_cnt[:, :, 0]
    tile_dst = starts[None, :] + tile_base[:, :, 0]
    xs = _dispatch(h2.reshape(t, d), pos_t, tile_n, tile_dst, moe_td)
    ys = _experts(xs, items, exp_w1[0], exp_b1[0], exp_w2[0], exp_b2[0], expert_tm)
    out = _combine(ys, pos_t, gate_t, jnp.transpose(pos_t), tile_n, tile_dst, x_mid.reshape(t, d),
                   g2, ln2_g[0], ln2_b[0], alpha, moe_td)
    return out.reshape(b, l, d)
```

```python
import functools
import math

import numpy as np
import jax
import jax.numpy as jnp
from jax import lax
from jax.experimental import pallas as pl
from jax.experimental.pallas import tpu as pltpu

F32 = jnp.float32
BF16 = jnp.bfloat16
HIGHEST = lax.Precision.HIGHEST

GRID_W = 64
N_HEADS = 8
N_KV_HEADS = 2
GQA_GROUP = N_HEADS // N_KV_HEADS
HEAD_DIM = 64
WINDOW = 128
ATTN_BLOCK = 128
ROPE_BASE = 10000.0

HY_WIDTH = 512
HY_ORDER = 2
HY_SHORT_CONV = 3
HY_EMB_DIM = 33
HY_DECAY_TARGET = 1e-2
HY_FAST_DECAY_PCT = 0.3
HY_SLOW_DECAY_PCT = 1.5

N_EXPERTS = 32
TOP_K = 4
SWIGLU_LIMIT = 7.0
SWIGLU_ALPHA = 1.702
LN_EPS = 1e-5

Q_W = N_HEADS * HEAD_DIM
KV_W = N_KV_HEADS * HEAD_DIM
K_OFF = Q_W
V_OFF = K_OFF + KV_W
HY_OFF = V_OFF + KV_W
GATE_OFF = HY_OFF + (HY_ORDER + 1) * HY_WIDTH

VMEM_LIMIT = 56 * 1024 * 1024
NEG_BIG = -1e30


def _cparams(sem):
    return pltpu.CompilerParams(dimension_semantics=sem, vmem_limit_bytes=VMEM_LIMIT)


def _ln(x):
    mu = jnp.mean(x, axis=-1, keepdims=True)
    xc = x - mu
    var = jnp.mean(xc * xc, axis=-1, keepdims=True)
    return xc * lax.rsqrt(var + LN_EPS)


def _bdot(a, b):
    return jnp.dot(a, b, preferred_element_type=F32)


def _fdot(a, b):
    return jnp.dot(a, b, preferred_element_type=F32, precision=HIGHEST)


def _mod_kernel(c_ref, w_ref, b_ref, o_ref):
    c = c_ref[...]
    s = c * jax.nn.sigmoid(c)
    o_ref[...] = _fdot(s, w_ref[...]) + b_ref[...]


def _modulation(cond, w, b, tn=512):
    r, d = cond.shape
    n = w.shape[1]
    return pl.pallas_call(
        _mod_kernel,
        grid=(n // tn,),
        in_specs=[pl.BlockSpec((r, d), lambda j: (0, 0)),
                  pl.BlockSpec((d, tn), lambda j: (0, j)),
                  pl.BlockSpec((1, tn), lambda j: (0, j))],
        out_specs=pl.BlockSpec((r, tn), lambda j: (0, j)),
        out_shape=jax.ShapeDtypeStruct((r, n), F32),
        compiler_params=_cparams(("arbitrary",)),
        name="modulation",
    )(cond, w, b.reshape(1, n))


def _rope(t, cos, sa, sb):
    n = t.shape[-1]
    return t * cos + pltpu.roll(t, n - 16, 1) * sa + pltpu.roll(t, 16, 1) * sb


def _inproj_kernel(x_ref, sh_ref, sc_ref, w_ref, cos_ref, sa_ref, sb_ref,
                   q_ref, k_ref, v_ref, u_ref, g_ref):
    h = _ln(x_ref[0]) * (1.0 + sc_ref[0]) + sh_ref[0]
    hb = h.astype(BF16)
    cos, sa, sb = cos_ref[...], sa_ref[...], sb_ref[...]
    scale = HEAD_DIM ** -0.5
    for j in range(Q_W // 128):
        t = _bdot(hb, w_ref[:, j * 128:(j + 1) * 128])
        q_ref[0, :, j * 128:(j + 1) * 128] = (_rope(t, cos, sa, sb) * scale).astype(BF16)
    t = _bdot(hb, w_ref[:, K_OFF:V_OFF])
    k_ref[0] = _rope(t, cos, sa, sb).astype(BF16)
    v_ref[0] = _bdot(hb, w_ref[:, V_OFF:HY_OFF]).astype(BF16)
    for j in range((GATE_OFF - HY_OFF) // 512):
        u_ref[0, :, j * 512:(j + 1) * 512] = _bdot(hb, w_ref[:, HY_OFF + j * 512:HY_OFF + (j + 1) * 512])
    n_gate = w_ref.shape[1] - GATE_OFF
    for j in range(n_gate // 512):
        g_ref[0, :, j * 512:(j + 1) * 512] = _bdot(
            hb, w_ref[:, GATE_OFF + j * 512:GATE_OFF + (j + 1) * 512])


def _in_projection(x, sh, sc, w_in_b, cos_t, sa_t, sb_t, tm=256):
    b, l, d = x.shape
    in_w = w_in_b.shape[1]
    hy_w = GATE_OFF - HY_OFF
    g_w = in_w - GATE_OFF
    row = lambda bi, i: (bi, i, 0)
    vec = lambda bi, i: (bi, 0, 0)
    tab = lambda bi, i: (i, 0)
    return pl.pallas_call(
        _inproj_kernel,
        grid=(b, l // tm),
        in_specs=[pl.BlockSpec((1, tm, d), row),
                  pl.BlockSpec((1, 1, d), vec),
                  pl.BlockSpec((1, 1, d), vec),
                  pl.BlockSpec((d, in_w), lambda bi, i: (0, 0)),
                  pl.BlockSpec((tm, 128), tab),
                  pl.BlockSpec((tm, 128), tab),
                  pl.BlockSpec((tm, 128), tab)],
        out_specs=[pl.BlockSpec((1, tm, Q_W), row),
                   pl.BlockSpec((1, tm, KV_W), row),
                   pl.BlockSpec((1, tm, KV_W), row),
                   pl.BlockSpec((1, tm, hy_w), row),
                   pl.BlockSpec((1, tm, g_w), row)],
        out_shape=[jax.ShapeDtypeStruct((b, l, Q_W), BF16),
                   jax.ShapeDtypeStruct((b, l, KV_W), BF16),
                   jax.ShapeDtypeStruct((b, l, KV_W), BF16),
                   jax.ShapeDtypeStruct((b, l, hy_w), F32),
                   jax.ShapeDtypeStruct((b, l, g_w), F32)],
        compiler_params=_cparams(("arbitrary", "arbitrary")),
        name="in_projection",
    )(x, sh, sc, w_in_b, cos_t, sa_t, sb_t)


def _ctx_kv_kernel(x_ref, sh_ref, sc_ref, w_ref, k_ref, v_ref):
    h = _ln(x_ref[0]) * (1.0 + sc_ref[0]) + sh_ref[0]
    kv = _bdot(h.astype(BF16), w_ref[...])
    k_ref[0] = kv[:, :KV_W].astype(BF16)
    v_ref[0] = kv[:, KV_W:].astype(BF16)


def _ctx_kv(ctx, sh, sc, w_kv_b):
    b, c, d = ctx.shape
    row = lambda bi: (bi, 0, 0)
    return pl.pallas_call(
        _ctx_kv_kernel,
        grid=(b,),
        in_specs=[pl.BlockSpec((1, c, d), row),
                  pl.BlockSpec((1, 1, d), lambda bi: (0, 0, 0)),
                  pl.BlockSpec((1, 1, d), lambda bi: (0, 0, 0)),
                  pl.BlockSpec((d, 2 * KV_W), lambda bi: (0, 0))],
        out_specs=[pl.BlockSpec((1, c, KV_W), row), pl.BlockSpec((1, c, KV_W), row)],
        out_shape=[jax.ShapeDtypeStruct((b, c, KV_W), BF16),
                   jax.ShapeDtypeStruct((b, c, KV_W), BF16)],
        compiler_params=_cparams(("arbitrary",)),
        name="ctx_kv",
    )(ctx, sh, sc, w_kv_b)


def _attn_kernel(sink_ref, q_ref, k_ref, v_ref, kc_ref, vc_ref, o_ref):
    l = k_ref.shape[1]
    span = ATTN_BLOCK + 2 * WINDOW
    j = pl.program_id(1)
    q0 = j * ATTN_BLOCK
    start = pl.multiple_of(jnp.clip(q0 - WINDOW, 0, l - span), ATTN_BLOCK)
    rows = GQA_GROUP * ATTN_BLOCK
    r_i = lax.broadcasted_iota(jnp.int32, (rows, span), 0)
    qpos = q0 + r_i % ATTN_BLOCK
    kpos = start + lax.broadcasted_iota(jnp.int32, (rows, span), 1)
    band = jnp.abs(kpos - qpos) <= WINDOW
    head_of_row = lax.broadcasted_iota(jnp.int32, (rows, 1), 0) // ATTN_BLOCK
    dn = (((1,), (1,)), ((), ()))
    outs = []
    for kv in range(N_KV_HEADS):
        ks = slice(kv * HEAD_DIM, (kv + 1) * HEAD_DIM)
        kl = k_ref[0, pl.ds(start, span), ks]
        vl = v_ref[0, pl.ds(start, span), ks]
        kc = kc_ref[0, :, ks]
        vc = vc_ref[0, :, ks]
        heads = [kv * GQA_GROUP + g for g in range(GQA_GROUP)]
        qg = jnp.concatenate([q_ref[0, :, h * HEAD_DIM:(h + 1) * HEAD_DIM] for h in heads], axis=0)
        sink = jnp.zeros((rows, 1), F32)
        for g, h in enumerate(heads):
            sink = jnp.where(head_of_row == g, sink_ref[h], sink)
        s_loc = lax.dot_general(qg, kl, dn, preferred_element_type=F32)
        s_loc = jnp.where(band, s_loc, NEG_BIG)
        s_ctx = lax.dot_general(qg, kc, dn, preferred_element_type=F32)
        m = jnp.maximum(jnp.max(s_loc, axis=1, keepdims=True),
                        jnp.max(s_ctx, axis=1, keepdims=True))
        m = jnp.maximum(m, sink)
        p_loc = jnp.exp(s_loc - m)
        p_ctx = jnp.exp(s_ctx - m)
        den = (jnp.sum(p_loc, axis=1, keepdims=True) + jnp.sum(p_ctx, axis=1, keepdims=True)
               + jnp.exp(sink - m))
        o = (_bdot(p_loc.astype(BF16), vl) + _bdot(p_ctx.astype(BF16), vc)) / den
        outs.extend(o[g * ATTN_BLOCK:(g + 1) * ATTN_BLOCK] for g in range(GQA_GROUP))
    o_ref[0] = jnp.concatenate(outs, axis=1).astype(BF16)


def _attention(q, k, v, kc, vc, sink):
    b, l, _ = q.shape
    c = kc.shape[1]
    full = lambda bi, j, s: (bi, 0, 0)
    grid_spec = pltpu.PrefetchScalarGridSpec(
        num_scalar_prefetch=1,
        grid=(b, l // ATTN_BLOCK),
        in_specs=[pl.BlockSpec((1, ATTN_BLOCK, Q_W), lambda bi, j, s: (bi, j, 0)),
                  pl.BlockSpec((1, l, KV_W), full),
                  pl.BlockSpec((1, l, KV_W), full),
                  pl.BlockSpec((1, c, KV_W), full),
                  pl.BlockSpec((1, c, KV_W), full)],
        out_specs=pl.BlockSpec((1, ATTN_BLOCK, Q_W), lambda bi, j, s: (bi, j, 0)),
    )
    return pl.pallas_call(
        _attn_kernel,
        grid_spec=grid_spec,
        out_shape=jax.ShapeDtypeStruct((b, l, Q_W), BF16),
        compiler_params=_cparams(("arbitrary", "arbitrary")),
        name="window_attention",
    )(sink, q, k, v, kc, vc)


def _dft_tables(l, wc):
    h = l // 2
    idx = np.arange(h, dtype=np.int64)
    ang = 2.0 * np.pi * ((idx[:, None] * idx[None, :]) % l).astype(np.float64) / l
    cm = np.cos(ang).astype(np.float32)
    sm = np.sin(ang).astype(np.float32)
    tw = 2.0 * np.pi * idx.astype(np.float64) / (2 * l)
    ones = np.ones((1, wc), np.float32)
    wr = np.cos(tw).astype(np.float32)[:, None] * ones
    wi = (-np.sin(tw)).astype(np.float32)[:, None] * ones
    sgn = np.where(idx % 2 == 0, 1.0, -1.0).astype(np.float32)[:, None] * ones
    wk = np.where(idx == 0, 1.0 / (2 * l), 2.0 / (2 * l)).astype(np.float32)[:, None] * ones
    return cm, sm, wr, wi, sgn, wk


def _half_spectrum(se, so, cm, sm, wr, wi, sgn):
    seb, sob = se.astype(BF16), so.astype(BF16)
    ce, ss_e = _bdot(cm, seb), _bdot(sm, seb)
    co, ss_o = _bdot(cm, sob), _bdot(sm, sob)
    vr = wr * co + wi * ss_o
    vi = wi * co - wr * ss_o
    zar, zai = ce + vr, vi - ss_e
    zbr, zbi = ce - vr, -ss_e - vi
    e_ny = jnp.sum(sgn * se, axis=0, keepdims=True)
    o_ny = jnp.sum(sgn * so, axis=0, keepdims=True)
    return zar, zai, zbr, zbi, e_ny, -o_ny


def _filter_kernel(ze_ref, zo_ref, w1_ref, b1_ref, w2_ref, b2_ref, w3f_ref, w3b_ref,
                   de_ref, do_ref, cm_ref, sm_ref, wr_ref, wi_ref, sgn_ref, wk_ref,
                   har_ref, hai_ref, hbr_ref, hbi_ref, hny_ref, ae_ref, ao_ref):
    cm, sm = cm_ref[...], sm_ref[...]
    wr, wi, sgn, wk = wr_ref[...], wi_ref[...], sgn_ref[...], wk_ref[...]
    h = cm.shape[0]

    @pl.when((pl.program_id(0) == 0) & (pl.program_id(1) == 0))
    def _():
        for z_ref, a_ref in ((ze_ref, ae_ref), (zo_ref, ao_ref)):
            a = jnp.sin(_fdot(z_ref[...], w1_ref[...]) + b1_ref[...])
            a_ref[...] = jnp.sin(_fdot(a, w2_ref[...]) + b2_ref[...])

    def taps(a_ref, w3_ref, d_ref):
        return _fdot(a_ref[...], w3_ref[0]) * d_ref[...]

    fe, fo = taps(ae_ref, w3f_ref, de_ref), taps(ao_ref, w3f_ref, do_ref)
    be, bo = taps(ae_ref, w3b_ref, de_ref), taps(ao_ref, w3b_ref, do_ref)
    row = lax.broadcasted_iota(jnp.int32, be.shape, 0)
    be = jnp.where(row == 0, 0.0, be)
    far, fai, fbr, fbi, fnr, fni = _half_spectrum(fe, fo, cm, sm, wr, wi, sgn)
    bar, bai, bbr, bbi, bnr, bni = _half_spectrum(be, bo, cm, sm, wr, wi, sgn)
    har_ref[0] = wk * (far + bar)
    hai_ref[0] = wk * (fai - bai)
    hbr_ref[0] = wk * (fbr + bbr)
    hbi_ref[0] = wk * (fbi - bbi)
    ny_scale = 2.0 / (4 * h)
    nr = ny_scale * (fnr + bnr)
    ni = ny_scale * (fni - bni)
    rows = lax.broadcasted_iota(jnp.int32, (8, nr.shape[1]), 0)
    hny_ref[0] = jnp.where(rows == 0, nr, jnp.where(rows == 1, ni, 0.0))


def _hyena_filters(zfe, zfo, w1, b1, w2, b2, w3, dec_e, dec_o, tabs, wc):
    cm, sm, wr, wi, sgn, wk = tabs
    h = cm.shape[0]
    emb = zfe.shape[1]
    hid = w2.shape[0]
    nblk = HY_WIDTH // wc
    w3r = w3.reshape(hid, HY_ORDER * 2, HY_WIDTH).transpose(1, 0, 2)
    const = lambda o, cb: (0, 0)
    chan = lambda o, cb: (0, cb)
    out_spec = pl.BlockSpec((1, h, wc), lambda o, cb: (o, 0, cb))
    out_shape = jax.ShapeDtypeStruct((HY_ORDER, h, HY_WIDTH), F32)
    return pl.pallas_call(
        _filter_kernel,
        grid=(HY_ORDER, nblk),
        in_specs=[pl.BlockSpec((h, emb), const), pl.BlockSpec((h, emb), const),
                  pl.BlockSpec((emb, hid), const), pl.BlockSpec((1, hid), const),
                  pl.BlockSpec((hid, hid), const), pl.BlockSpec((1, hid), const),
                  pl.BlockSpec((1, hid, wc), lambda o, cb: (2 * o, 0, cb)),
                  pl.BlockSpec((1, hid, wc), lambda o, cb: (2 * o + 1, 0, cb)),
                  pl.BlockSpec((h, wc), chan), pl.BlockSpec((h, wc), chan),
                  pl.BlockSpec((h, h), const), pl.BlockSpec((h, h), const),
                  pl.BlockSpec((h, wc), const), pl.BlockSpec((h, wc), const),
                  pl.BlockSpec((h, wc), const), pl.BlockSpec((h, wc), const)],
        out_specs=[out_spec, out_spec, out_spec, out_spec,
                   pl.BlockSpec((1, 8, wc), lambda o, cb: (o, 0, cb))],
        out_shape=[out_shape, out_shape, out_shape, out_shape,
                   jax.ShapeDtypeStruct((HY_ORDER, 8, HY_WIDTH), F32)],
        scratch_shapes=[pltpu.VMEM((h, hid), F32), pltpu.VMEM((h, hid), F32)],
        compiler_params=_cparams(("arbitrary", "arbitrary")),
        name="hyena_filter_spectrum",
    )(zfe, zfo, w1, b1.reshape(1, hid), w2, b2.reshape(1, hid), w3r, w3r,
      dec_e, dec_o, cm, sm, wr, wi, sgn, wk)


def _long_conv(ze, zo, har, hai, hbr, hbi, hny, cm, sm, wr, wi, sgn):
    zar, zai, zbr, zbi, znr, zni = _half_spectrum(ze, zo, cm, sm, wr, wi, sgn)
    yar, yai = zar * har - zai * hai, zar * hai + zai * har
    ybr, ybi = zbr * hbr - zbi * hbi, zbr * hbi + zbi * hbr
    dr, di = yar - ybr, yai - ybi
    sar, sai = (yar + ybr).astype(BF16), (yai + ybi).astype(BF16)
    sbr, sbi = (dr * wr + di * wi).astype(BF16), (di * wr - dr * wi).astype(BF16)
    hnr, hni = hny[0:1], hny[1:2]
    ynr = znr * hnr - zni * hni
    yni = znr * hni + zni * hnr
    ye = _bdot(cm, sar) - _bdot(sm, sai) + sgn * ynr
    yo = _bdot(cm, sbr) - _bdot(sm, sbi) - sgn * yni
    return ye, yo


def _hyena_kernel(*refs):
    nsub = (len(refs) - 18) // 4
    u_refs = [refs[s * nsub:(s + 1) * nsub] for s in range(HY_ORDER + 1)]
    rest = refs[(HY_ORDER + 1) * nsub:]
    (cw0_ref, cw1_ref, cw2_ref, cb0_ref, cb1_ref, cb2_ref, skip_ref,
     har_ref, hai_ref, hbr_ref, hbi_ref, hny_ref,
     cm_ref, sm_ref, wr_ref, wi_ref, sgn_ref, o_ref) = rest[:18]
    il_refs = rest[18:]
    h = cm_ref.shape[0]
    cm, sm = cm_ref[...], sm_ref[...]
    wr, wi, sgn = wr_ref[...], wi_ref[...], sgn_ref[...]
    row = lax.broadcasted_iota(jnp.int32, (h, wr.shape[1]), 0)

    def short_conv(u_slabs, cw_ref, cb_ref):
        ue = jnp.concatenate([r[0, pl.ds(0, h, stride=2), :] for r in u_slabs], axis=1)
        uo = jnp.concatenate([r[0, pl.ds(1, h, stride=2), :] for r in u_slabs], axis=1)
        w0, w1, w2 = cw_ref[0:1], cw_ref[1:2], cw_ref[2:3]
        uo_prev = jnp.where(row == 0, 0.0, pltpu.roll(uo, 1, 0))
        ue_next = jnp.where(row == h - 1, 0.0, pltpu.roll(ue, h - 1, 0))
        cb = cb_ref[...]
        se = cb + uo_prev * w0 + ue * w1 + uo * w2
        so = cb + ue * w0 + uo * w1 + ue_next * w2
        return se, so

    x_streams = [short_conv(u_refs[0], cw0_ref, cb0_ref), short_conv(u_refs[1], cw1_ref, cb1_ref)]
    ze, zo = short_conv(u_refs[2], cw2_ref, cb2_ref)
    for o in range(HY_ORDER):
        ce, co = _long_conv(ze, zo, har_ref[o], hai_ref[o], hbr_ref[o], hbi_ref[o], hny_ref[o],
                            cm, sm, wr, wi, sgn)
        sk = skip_ref[o:o + 1]
        xe, xo = x_streams[o]
        ze = xe * (ce + sk * ze)
        zo = xo * (co + sk * zo)
    for j, il in enumerate(il_refs):
        il[pl.ds(0, h, stride=2), :] = ze[:, j * 128:(j + 1) * 128]
        il[pl.ds(1, h, stride=2), :] = zo[:, j * 128:(j + 1) * 128]
        o_ref[0, :, j * 128:(j + 1) * 128] = il[...]


def _hyena(u, conv_w, conv_b, skip, spectra, tabs, wc):
    b, l, _ = u.shape
    har, hai, hbr, hbi, hny = spectra
    cm, sm, wr, wi, sgn, _ = tabs
    h = l // 2
    nblk = HY_WIDTH // wc
    nsub = wc // 128
    conv_b = conv_b.reshape(1, -1)

    def ublk(s, j):
        return pl.BlockSpec((1, l, 128), lambda cb, bi: (bi, 0, (cb + s * nblk) * nsub + j))

    def wblk(r, s):
        return pl.BlockSpec((r, wc), lambda cb, bi: (0, cb + s * nblk))

    const = lambda cb, bi: (0, 0)
    spec = pl.BlockSpec((HY_ORDER, h, wc), lambda cb, bi: (0, 0, cb))
    return pl.pallas_call(
        _hyena_kernel,
        grid=(nblk, b),
        in_specs=[ublk(s, j) for s in range(HY_ORDER + 1) for j in range(nsub)] + [
                  wblk(HY_SHORT_CONV, 0), wblk(HY_SHORT_CONV, 1), wblk(HY_SHORT_CONV, 2),
                  wblk(1, 0), wblk(1, 1), wblk(1, 2),
                  pl.BlockSpec((HY_ORDER, wc), lambda cb, bi: (0, cb)),
                  spec, spec, spec, spec,
                  pl.BlockSpec((HY_ORDER, 8, wc), lambda cb, bi: (0, 0, cb)),
                  pl.BlockSpec((h, h), const), pl.BlockSpec((h, h), const),
                  pl.BlockSpec((h, wc), const), pl.BlockSpec((h, wc), const),
                  pl.BlockSpec((h, wc), const)],
        out_specs=pl.BlockSpec((1, l, wc), lambda cb, bi: (bi, 0, cb)),
        out_shape=jax.ShapeDtypeStruct((b, l, HY_WIDTH), F32),
        scratch_shapes=[pltpu.VMEM((l, 128), F32) for _ in range(nsub)],
        compiler_params=_cparams(("arbitrary", "arbitrary")),
        name="hyena_long_conv",
    )(*([u] * ((HY_ORDER + 1) * nsub)), conv_w, conv_w, conv_w, conv_b, conv_b, conv_b, skip,
      har, hai, hbr, hbi, hny, cm, sm, wr, wi, sgn)


def _merge_kernel(alpha, att_ref, hy_ref, gate_ref, x_ref, g1_ref, sh2_ref, sc2_ref,
                  wba_ref, wbh_ref, wo_ref, l1g_ref, l1b_ref, rw_ref, rb_ref,
                  xmid_ref, h2_ref, logit_ref):
    tm, d = x_ref.shape[1], x_ref.shape[2]
    rw = rw_ref[...]
    r_hi = rw.astype(BF16)
    r_lo = (rw - r_hi.astype(F32)).astype(BF16)
    dn = (((1,), (1,)), ((), ()))
    nt_dot = lambda p, q: lax.dot_general(p, q, dn, preferred_element_type=F32)
    half = tm // 2
    for r in (slice(0, half), slice(half, tm)):
        a = _bdot(att_ref[0, r], wba_ref[...])
        hh = _bdot(hy_ref[0, r].astype(BF16), wbh_ref[...])
        ga = jax.nn.sigmoid(gate_ref[0, r, :d])
        gh = jax.nn.sigmoid(gate_ref[0, r, d:])
        y = _bdot((ga * a + gh * hh).astype(BF16), wo_ref[...])
        xm = _ln(alpha * x_ref[0, r] + g1_ref[0] * y) * l1g_ref[...] + l1b_ref[...]
        xmid_ref[0, r] = xm
        h2 = _ln(xm) * (1.0 + sc2_ref[0]) + sh2_ref[0]
        h_hi = h2.astype(BF16)
        h2_ref[0, r] = h_hi
        h_lo = (h2 - h_hi.astype(F32)).astype(BF16)
        logit_ref[:, r] = nt_dot(r_hi, h_hi) + nt_dot(r_hi, h_lo) + nt_dot(r_lo, h_hi) + rb_ref[...]


def _merge(att, hy, gate, x, g1, sh2, sc2, wba, wbh, wo, l1g, l1b, rw_t, rb, alpha, tm=512):
    b, l, d = x.shape
    ne = rw_t.shape[0]
    nt = l // tm
    row = lambda bi, i: (bi, i, 0)
    vec = lambda bi, i: (bi, 0, 0)
    const = lambda bi, i: (0, 0)
    return pl.pallas_call(
        functools.partial(_merge_kernel, alpha),
        grid=(b, l // tm),
        in_specs=[pl.BlockSpec((1, tm, Q_W), row),
                  pl.BlockSpec((1, tm, HY_WIDTH), row),
                  pl.BlockSpec((1, tm, 2 * d), row),
                  pl.BlockSpec((1, tm, d), row),
                  pl.BlockSpec((1, 1, d), vec), pl.BlockSpec((1, 1, d), vec),
                  pl.BlockSpec((1, 1, d), vec),
                  pl.BlockSpec((Q_W, d), const), pl.BlockSpec((HY_WIDTH, d), const),
                  pl.BlockSpec((d, d), const),
                  pl.BlockSpec((1, d), const), pl.BlockSpec((1, d), const),
                  pl.BlockSpec((ne, d), const), pl.BlockSpec((ne, 1), const)],
        out_specs=[pl.BlockSpec((1, tm, d), row), pl.BlockSpec((1, tm, d), row),
                   pl.BlockSpec((ne, tm), lambda bi, i: (0, bi * nt + i))],
        out_shape=[jax.ShapeDtypeStruct((b, l, d), F32),
                   jax.ShapeDtypeStruct((b, l, d), BF16),
                   jax.ShapeDtypeStruct((ne, b * l), F32)],
        compiler_params=_cparams(("arbitrary", "arbitrary")),
        name="merge_ln_router",
    )(att, hy, gate, x, g1, sh2, sc2, wba, wbh, wo, l1g.reshape(1, d), l1b.reshape(1, d),
      rw_t, rb.reshape(ne, 1))


def _route_kernel(lg_ref, gate_ref, pos_ref, tcnt_ref, tbase_ref, cnt_ref, run_ref):
    @pl.when(pl.program_id(0) == 0)
    def _():
        run_ref[...] = jnp.zeros_like(run_ref)

    lg = lg_ref[...]
    ne, tr = lg.shape
    sub = lax.broadcasted_iota(jnp.int32, (ne, tr), 0)
    work = lg
    vals, hots = [], []
    for _ in range(TOP_K):
        m = jnp.max(work, axis=0, keepdims=True)
        idx = jnp.min(jnp.where(work == m, sub, ne), axis=0, keepdims=True)
        hot = sub == idx
        vals.append(m)
        hots.append(hot)
        work = jnp.where(hot, -jnp.inf, work)
    exps = [jnp.exp(v - vals[0]) for v in vals]
    den = exps[0] + exps[1] + exps[2] + exps[3]
    member = jnp.zeros((ne, tr), F32)
    for hot in hots:
        member = member + jnp.where(hot, 1.0, 0.0)
    r_i = lax.broadcasted_iota(jnp.int32, (tr, tr), 0)
    c_i = lax.broadcasted_iota(jnp.int32, (tr, tr), 1)
    earlier = jnp.where(r_i < c_i, 1.0, 0.0).astype(BF16)
    prefix = _bdot(member.astype(BF16), earlier)
    cnt = jnp.broadcast_to(jnp.sum(member, axis=1, keepdims=True), (ne, 128))
    e_r = lax.broadcasted_iota(jnp.int32, (ne, ne), 0)
    e_c = lax.broadcasted_iota(jnp.int32, (ne, ne), 1)
    lower = jnp.where(e_c < e_r, 1.0, 0.0).astype(BF16)
    off = _bdot(lower, cnt.astype(BF16))
    base = off[:, 0:1] + prefix
    sub_k = lax.broadcasted_iota(jnp.int32, (TOP_K, tr), 0)
    gate = jnp.zeros((TOP_K, tr), F32)
    pos = jnp.zeros((TOP_K, tr), F32)
    for k in range(TOP_K):
        pk = jnp.sum(jnp.where(hots[k], base, 0.0), axis=0, keepdims=True)
        gate = jnp.where(sub_k == k, exps[k] / den, gate)
        pos = jnp.where(sub_k == k, pk, pos)
    gate_ref[...] = gate
    pos_ref[...] = pos.astype(jnp.int32)
    tcnt_ref[0] = cnt.astype(jnp.int32)
    tbase_ref[0] = run_ref[...].astype(jnp.int32)
    run_ref[...] = run_ref[...] + cnt
    cnt_ref[...] = run_ref[...].astype(jnp.int32)


def _route(logits_t, tr):
    ne, t = logits_t.shape
    nt = t // tr
    tok = lambda i: (0, i)
    tile = lambda i: (i, 0, 0)
    return pl.pallas_call(
        _route_kernel,
        grid=(nt,),
        in_specs=[pl.BlockSpec((ne, tr), tok)],
        out_specs=[pl.BlockSpec((TOP_K, tr), tok), pl.BlockSpec((TOP_K, tr), tok),
                   pl.BlockSpec((1, ne, 128), tile), pl.BlockSpec((1, ne, 128), tile),
                   pl.BlockSpec((ne, 128), lambda i: (0, 0))],
        out_shape=[jax.ShapeDtypeStruct((TOP_K, t), F32),
                   jax.ShapeDtypeStruct((TOP_K, t), jnp.int32),
                   jax.ShapeDtypeStruct((nt, ne, 128), jnp.int32),
                   jax.ShapeDtypeStruct((nt, ne, 128), jnp.int32),
                   jax.ShapeDtypeStruct((ne, 128), jnp.int32)],
        scratch_shapes=[pltpu.VMEM((ne, 128), F32)],
        compiler_params=_cparams(("arbitrary",)),
        name="route_topk",
    )(logits_t)


def _row_tile(d):
    return (d // 256, 128)


def _pack_rows(v):
    n, d = v.shape
    bits = lax.bitcast_convert_type(v.astype(BF16).astype(F32), jnp.uint32)
    word = (bits[:, d // 2:] & jnp.uint32(0xFFFF0000)) | (bits[:, :d // 2] >> 16)
    return word.reshape((n,) + _row_tile(d))


def _unpack_rows(w):
    n = w.shape[0]
    word = w.reshape(n, w.shape[1] * w.shape[2])
    lo = lax.bitcast_convert_type(word << 16, F32)
    hi = lax.bitcast_convert_type(word & jnp.uint32(0xFFFF0000), F32)
    return jnp.concatenate([lo, hi], axis=1)


def _strip_copies(n_ref, dst_ref, make_copy, max_rows):
    ne = n_ref.shape[2]
    bits = [1 << s for s in range(max_rows.bit_length() - 1, -1, -1)]

    def body(e, off):
        n = n_ref[0, 0, e]
        dst = dst_ref[0, 0, e]
        done = jnp.int32(0)
        for bit in bits:
            part = n & bit

            @pl.when(part != 0)
            def _():
                make_copy(off + done, dst + done, bit).start()

            done = done + part
        return off + n

    lax.fori_loop(0, ne, body, jnp.int32(0))


def _dispatch_kernel(n_ref, dst_ref, pos_ref, h_ref, xs_ref, srt_ref, sem):
    i = pl.program_id(0)
    slot = i % 2
    td = h_ref.shape[0]
    n = td * TOP_K
    pos = pos_ref[...]
    rows = lax.broadcasted_iota(jnp.int32, (n, td), 0)
    hit = rows == pos[0:1]
    for k in range(1, TOP_K):
        hit = hit | (rows == pos[k:k + 1])
    perm = jnp.where(hit, 1.0, 0.0).astype(BF16)
    srt_ref[slot] = _pack_rows(_bdot(perm, h_ref[...]))

    def all_rows(s):
        return pltpu.make_async_copy(srt_ref.at[s], xs_ref.at[pl.ds(0, n)], sem.at[s])

    _strip_copies(n_ref, dst_ref,
                  lambda src, dst, size: pltpu.make_async_copy(
                      srt_ref.at[slot, pl.ds(src, size)], xs_ref.at[pl.ds(dst, size)], sem.at[slot]),
                  td)

    @pl.when(i > 0)
    def _():
        all_rows(1 - slot).wait()

    @pl.when(i == pl.num_programs(0) - 1)
    def _():
        all_rows(slot).wait()


def _dispatch(h2, pos_t, tile_n, tile_dst, td):
    t, d = h2.shape
    a = t * TOP_K
    n = td * TOP_K
    nt = t // td
    ne = tile_n.shape[-1]
    smem = lambda: pl.BlockSpec((1, 1, ne), lambda i: (i, 0, 0), memory_space=pltpu.SMEM)
    return pl.pallas_call(
        _dispatch_kernel,
        grid=(nt,),
        in_specs=[smem(), smem(),
                  pl.BlockSpec((TOP_K, td), lambda i: (0, i)),
                  pl.BlockSpec((td, d), lambda i: (i, 0))],
        out_specs=pl.BlockSpec(memory_space=pl.ANY),
        out_shape=jax.ShapeDtypeStruct((a,) + _row_tile(d), jnp.uint32),
        scratch_shapes=[pltpu.VMEM((2, n) + _row_tile(d), jnp.uint32),
                        pltpu.SemaphoreType.DMA((2,))],
        compiler_params=_cparams(("arbitrary",)),
        name="moe_dispatch",
    )(tile_n.reshape(nt, 1, ne), tile_dst.reshape(nt, 1, ne), pos_t, h2)


def _expert_kernel(tile_ref, exp_ref, lo_ref, hi_ref, first_ref, nitem_ref,
                   xs_ref, w1_ref, b1_ref, w2_ref, b2_ref, ys_ref, w1b_ref, w2b_ref):
    w = pl.program_id(0)
    tm = xs_ref.shape[0]
    d, dff = w2_ref.shape[2], w2_ref.shape[1]

    @pl.when((w == 0) | (exp_ref[w] != exp_ref[jnp.maximum(w - 1, 0)]))
    def _():
        w1b_ref[...] = w1_ref[0].astype(BF16)
        w2b_ref[...] = w2_ref[0].astype(BF16)

    @pl.when(w < nitem_ref[0])
    def _():
        x = _unpack_rows(xs_ref[...]).astype(BF16)
        hb = _bdot(x, w1b_ref[...]) + b1_ref[0]
        glu = jnp.minimum(hb[:, :dff], SWIGLU_LIMIT)
        lin = jnp.clip(hb[:, dff:], -SWIGLU_LIMIT, SWIGLU_LIMIT)
        act = glu * jax.nn.sigmoid(SWIGLU_ALPHA * glu) * (lin + 1.0)
        y = _bdot(act.astype(BF16), w2b_ref[...]) + b2_ref[0]
        row = lax.broadcasted_iota(jnp.int32, y.shape, 0)
        y = jnp.where((row >= lo_ref[w]) & (row < hi_ref[w]), y, 0.0)

        @pl.when(first_ref[w] == 1)
        def _():
            ys_ref[...] = _pack_rows(y)

        @pl.when(first_ref[w] == 0)
        def _():
            ys_ref[...] = _pack_rows(_unpack_rows(ys_ref[...]) + y)


def _experts(xs, items, w1, b1, w2, b2, tm):
    a = xs.shape[0]
    ne, d, dff2 = w1.shape
    dff = dff2 // 2
    n_items = a // tm + ne - 1
    tile_w, exp_w, lo_w, hi_w, first_w, nitem = items
    rows = pl.BlockSpec((tm,) + _row_tile(d), lambda w, tl, ex, lo, hi, fi, ni: (tl[w], 0, 0))
    per_expert = lambda w, tl, ex, lo, hi, fi, ni: (ex[w], 0, 0)
    grid_spec = pltpu.PrefetchScalarGridSpec(
        num_scalar_prefetch=6,
        grid=(n_items,),
        in_specs=[rows,
                  pl.BlockSpec((1, d, dff2), per_expert),
                  pl.BlockSpec((1, 1, dff2), per_expert),
                  pl.BlockSpec((1, dff, d), per_expert),
                  pl.BlockSpec((1, 1, d), per_expert)],
        out_specs=rows,
        scratch_shapes=[pltpu.VMEM((d, dff2), BF16), pltpu.VMEM((dff, d), BF16)],
    )
    return pl.pallas_call(
        _expert_kernel,
        grid_spec=grid_spec,
        out_shape=jax.ShapeDtypeStruct(xs.shape, jnp.uint32),
        compiler_params=_cparams(("arbitrary",)),
        name="moe_experts",
    )(tile_w, exp_w, lo_w, hi_w, first_w, nitem,
      xs, w1, b1.reshape(ne, 1, dff2), w2, b2.reshape(ne, 1, d))


def _work_items(counts, a, tm):
    ne = counts.shape[0]
    n_items = a // tm + ne - 1
    ends = jnp.cumsum(counts)
    starts = ends - counts
    first_tile = starts // tm
    last_tile = (ends - 1) // tm
    nvis = jnp.where(counts > 0, last_tile - first_tile + 1, 0)
    vis_end = jnp.cumsum(nvis)
    vis_start = vis_end - nvis
    total = vis_end[-1]
    w = jnp.minimum(jnp.arange(n_items, dtype=jnp.int32), total - 1)
    e_w = jnp.sum((vis_end[None, :] <= w[:, None]).astype(jnp.int32), axis=1)
    hot = e_w[:, None] == jnp.arange(ne, dtype=jnp.int32)[None, :]
    pick = lambda v: jnp.sum(jnp.where(hot, v[None, :], 0), axis=1)
    tile_w = (pick(first_tile) + (w - pick(vis_start))).astype(jnp.int32)
    lo_w = jnp.maximum(pick(starts) - tile_w * tm, 0).astype(jnp.int32)
    hi_w = jnp.minimum(pick(ends) - tile_w * tm, tm).astype(jnp.int32)
    prev = jnp.concatenate([jnp.full((1,), -1, jnp.int32), tile_w[:-1]])
    first_w = (tile_w != prev).astype(jnp.int32)
    return starts, (tile_w, e_w, lo_w, hi_w, first_w, total.reshape(1).astype(jnp.int32))


def _combine_kernel(alpha, n_ref, dst_ref, nn_ref, ndst_ref, pos_ref, gate_ref,
                    ys_ref, xm_ref, g2_ref, lg_ref, lb_ref, o_ref, srt_ref, sem):
    i = pl.program_id(0)
    last = pl.num_programs(0) - 1
    slot = i % 2
    td = xm_ref.shape[0]
    n = td * TOP_K

    def fetch(cnt_ref, from_ref, s):
        _strip_copies(cnt_ref, from_ref,
                      lambda row, src, size: pltpu.make_async_copy(
                          ys_ref.at[pl.ds(src, size)], srt_ref.at[s, pl.ds(row, size)], sem.at[s]),
                      td)

    @pl.when(i == 0)
    def _():
        fetch(n_ref, dst_ref, slot)

    @pl.when(i < last)
    def _():
        fetch(nn_ref, ndst_ref, 1 - slot)

    pltpu.make_async_copy(ys_ref.at[pl.ds(0, n)], srt_ref.at[slot], sem.at[slot]).wait()

    pos, gate = pos_ref[...], gate_ref[...]
    lanes = lax.broadcasted_iota(jnp.int32, (td, n), 1)
    wsel = jnp.where(lanes == pos[:, 0:1], gate[:, 0:1], 0.0)
    for k in range(1, TOP_K):
        wsel = wsel + jnp.where(lanes == pos[:, k:k + 1], gate[:, k:k + 1], 0.0)
    w_hi = wsel.astype(BF16)
    w_lo = (wsel - w_hi.astype(F32)).astype(BF16)
    y = _unpack_rows(srt_ref[slot]).astype(BF16)
    f = _bdot(w_hi, y) + _bdot(w_lo, y)
    o_ref[...] = _ln(alpha * xm_ref[...] + g2_ref[0] * f) * lg_ref[...] + lb_ref[...]


def _combine(ys, pos, gate, tile_n, tile_dst, x_mid, g2, lg, lb, alpha, td):
    t, d = x_mid.shape
    n = td * TOP_K
    nt = t // td
    per_b = nt // g2.shape[0]
    ne = tile_n.shape[-1]
    cur = lambda: pl.BlockSpec((1, 1, ne), lambda i: (i, 0, 0), memory_space=pltpu.SMEM)
    nxt = lambda: pl.BlockSpec((1, 1, ne), lambda i: (jnp.minimum(i + 1, nt - 1), 0, 0),
                               memory_space=pltpu.SMEM)
    const = lambda i: (0, 0)
    tile_n = tile_n.reshape(nt, 1, ne)
    tile_dst = tile_dst.reshape(nt, 1, ne)
    return pl.pallas_call(
        functools.partial(_combine_kernel, alpha),
        grid=(nt,),
        in_specs=[cur(), cur(), nxt(), nxt(),
                  pl.BlockSpec((td, TOP_K), lambda i: (i, 0)),
                  pl.BlockSpec((td, TOP_K), lambda i: (i, 0)),
                  pl.BlockSpec(memory_space=pl.ANY),
                  pl.BlockSpec((td, d), lambda i: (i, 0)),
                  pl.BlockSpec((1, 1, d), lambda i: (i // per_b, 0, 0)),
                  pl.BlockSpec((1, d), const), pl.BlockSpec((1, d), const)],
        out_specs=pl.BlockSpec((td, d), lambda i: (i, 0)),
        out_shape=jax.ShapeDtypeStruct((t, d), F32),
        scratch_shapes=[pltpu.VMEM((2, n) + _row_tile(d), jnp.uint32),
                        pltpu.SemaphoreType.DMA((2,))],
        compiler_params=_cparams(("arbitrary",)),
        name="moe_combine_ln",
    )(tile_n, tile_dst, tile_n, tile_dst, pos, gate, ys, x_mid, g2,
      lg.reshape(1, d), lb.reshape(1, d))


def _rope_tables(l):
    f32 = np.float32
    rows = l // GRID_W
    row = np.repeat(np.arange(rows, dtype=f32), GRID_W)
    col = np.tile(np.arange(GRID_W, dtype=f32), rows)
    n_freq = HEAD_DIM // 4
    inv_freq = np.power(f32(ROPE_BASE), -np.arange(n_freq, dtype=f32) / f32(n_freq)).astype(f32)
    ang_r = (row[:, None] * inv_freq).astype(f32)
    ang_c = (col[:, None] * inv_freq).astype(f32)
    zero = np.zeros_like(ang_r)
    cos_r, sin_r, cos_c, sin_c = np.cos(ang_r), np.sin(ang_r), np.cos(ang_c), np.sin(ang_c)
    cos_h = np.concatenate([cos_r, cos_r, cos_c, cos_c], axis=1)
    sa_h = np.concatenate([-sin_r, zero, -sin_c, zero], axis=1)
    sb_h = np.concatenate([zero, sin_r, zero, sin_c], axis=1)
    rep = 128 // HEAD_DIM
    return tuple(np.tile(a, (1, rep)).astype(f32) for a in (cos_h, sa_h, sb_h))


def _filter_features(l):
    f32 = np.float32
    bands = (HY_EMB_DIM - 1) // 2
    t = np.linspace(0.0, 1.0, l, dtype=f32)[:, None]
    omega = (f32(2.0 * math.pi) * np.arange(l, dtype=f32)[:, None] / f32(l)).astype(f32)
    f = np.linspace(1e-4, bands - 1, bands, dtype=f32)[None, :]
    ang = (f * omega).astype(f32)
    z = np.concatenate([t, np.cos(ang), -np.sin(ang)], axis=-1).astype(f32)
    min_decay = math.log(HY_DECAY_TARGET) / HY_FAST_DECAY_PCT
    max_decay = math.log(HY_DECAY_TARGET) / HY_SLOW_DECAY_PCT
    deltas = np.abs(np.linspace(min_decay, max_decay, HY_WIDTH, dtype=f32))
    decay = np.exp(-t * deltas).astype(f32)
    return z, decay


def kernel(x, c, ctx, c_ctx, w_mod, b_mod, w_in, attn_sink, hy_conv_w, hy_conv_b, hy_filt_w1,
           hy_filt_b1, hy_filt_w2, hy_filt_b2, hy_filt_w3, hy_skip, w_branch_attn, w_branch_hyena,
           w_out, ln1_g, ln1_b, router_w, router_b, exp_w1, exp_b1, exp_w2, exp_b2, ln2_g, ln2_b):
    depth = w_mod.shape[0]
    assert depth == 1, "only the single-layer configuration is implemented"
    b, l, d = x.shape
    t = b * l
    alpha = (2 * depth) ** 0.25
    hy_wc = 256
    expert_tm = 256

    n_cond = b + 1
    pad = (-n_cond) % 8
    cond = jnp.concatenate([c, c_ctx[None], jnp.zeros((pad, d), F32)], axis=0)
    mod = _modulation(cond, w_mod[0], b_mod[0])
    mod_x = mod[:b].reshape(b, 1, 6, d)
    sh1, sc1, g1, sh2, sc2, g2 = (mod_x[:, :, i] for i in range(6))
    mod_c = mod[b:b + 1].reshape(1, 1, 6, d)
    csh1, csc1 = mod_c[:, :, 0], mod_c[:, :, 1]

    w_in_b = w_in[0].astype(BF16)
    cos_t, sa_t, sb_t = (jnp.asarray(a) for a in _rope_tables(l))
    q, k, v, u_hy, gate_x = _in_projection(x, sh1, sc1, w_in_b, cos_t, sa_t, sb_t)
    k_c, v_c = _ctx_kv(ctx, csh1, csc1, w_in_b[:, K_OFF:HY_OFF])
    att = _attention(q, k, v, k_c, v_c, attn_sink[0])

    tabs_np = _dft_tables(l, hy_wc)
    tabs = (jnp.asarray(tabs_np[0]).astype(BF16), jnp.asarray(tabs_np[1]).astype(BF16)) + tuple(
        jnp.asarray(a) for a in tabs_np[2:])
    zfeat, decay = _filter_features(l)
    emb_pad = (-HY_EMB_DIM) % 128
    zfeat = np.pad(zfeat, ((0, 0), (0, emb_pad)))
    fw1 = jnp.pad(hy_filt_w1[0], ((0, emb_pad), (0, 0)))
    spectra = _hyena_filters(jnp.asarray(zfeat[0::2]), jnp.asarray(zfeat[1::2]), fw1, hy_filt_b1[0],
                             hy_filt_w2[0], hy_filt_b2[0], hy_filt_w3[0], jnp.asarray(decay[0::2]),
                             jnp.asarray(decay[1::2]), tabs, hy_wc)
    hy = _hyena(u_hy, hy_conv_w[0], hy_conv_b[0], hy_skip[0], spectra, tabs, hy_wc)

    x_mid, h2, logits_t = _merge(att, hy, gate_x, x, g1, sh2, sc2,
                                 w_branch_attn[0].astype(BF16), w_branch_hyena[0].astype(BF16),
                                 w_out[0].astype(BF16), ln1_g[0], ln1_b[0],
                                 jnp.transpose(router_w[0]), router_b[0], alpha)

    moe_td = 256
    gate_t, pos_t, tile_cnt, tile_base, counts = _route(logits_t, moe_td)
    starts, items = _work_items(counts[:, 0], t * TOP_K, expert_tm)
    tile_n = tile_cnt[:, :, 0]
    tile_dst = starts[None, :] + tile_base[:, :, 0]
    xs = _dispatch(h2.reshape(t, d), pos_t, tile_n, tile_dst, moe_td)
    ys = _experts(xs, items, exp_w1[0], exp_b1[0], exp_w2[0], exp_b2[0], expert_tm)
    out = _combine(ys, jnp.transpose(pos_t), jnp.transpose(gate_t), tile_n, tile_dst,
                   x_mid.reshape(t, d), g2, ln2_g[0], ln2_b[0], alpha, moe_td)
    return out.reshape(b, l, d)
```

```python
import functools
import math

import numpy as np
import jax
import jax.numpy as jnp
from jax import lax
from jax.experimental import pallas as pl
from jax.experimental.pallas import tpu as pltpu

F32 = jnp.float32
BF16 = jnp.bfloat16
HIGHEST = lax.Precision.HIGHEST

GRID_W = 64
N_HEADS = 8
N_KV_HEADS = 2
GQA_GROUP = N_HEADS // N_KV_HEADS
HEAD_DIM = 64
WINDOW = 128
ATTN_BLOCK = 128
ROPE_BASE = 10000.0

HY_WIDTH = 512
HY_ORDER = 2
HY_SHORT_CONV = 3
HY_EMB_DIM = 33
HY_DECAY_TARGET = 1e-2
HY_FAST_DECAY_PCT = 0.3
HY_SLOW_DECAY_PCT = 1.5

N_EXPERTS = 32
TOP_K = 4
SWIGLU_LIMIT = 7.0
SWIGLU_ALPHA = 1.702
LN_EPS = 1e-5

Q_W = N_HEADS * HEAD_DIM
KV_W = N_KV_HEADS * HEAD_DIM
K_OFF = Q_W
V_OFF = K_OFF + KV_W
HY_OFF = V_OFF + KV_W
GATE_OFF = HY_OFF + (HY_ORDER + 1) * HY_WIDTH

VMEM_LIMIT = 56 * 1024 * 1024
NEG_BIG = -1e30


def _cparams(sem):
    return pltpu.CompilerParams(dimension_semantics=sem, vmem_limit_bytes=VMEM_LIMIT)


def _ln(x):
    mu = jnp.mean(x, axis=-1, keepdims=True)
    xc = x - mu
    var = jnp.mean(xc * xc, axis=-1, keepdims=True)
    return xc * lax.rsqrt(var + LN_EPS)


def _bdot(a, b):
    return jnp.dot(a, b, preferred_element_type=F32)


def _fdot(a, b):
    return jnp.dot(a, b, preferred_element_type=F32, precision=HIGHEST)


def _mod_kernel(c_ref, w_ref, b_ref, o_ref):
    c = c_ref[...]
    s = c * jax.nn.sigmoid(c)
    o_ref[...] = _fdot(s, w_ref[...]) + b_ref[...]


def _modulation(cond, w, b, tn=512):
    r, d = cond.shape
    n = w.shape[1]
    return pl.pallas_call(
        _mod_kernel,
        grid=(n // tn,),
        in_specs=[pl.BlockSpec((r, d), lambda j: (0, 0)),
                  pl.BlockSpec((d, tn), lambda j: (0, j)),
                  pl.BlockSpec((1, tn), lambda j: (0, j))],
        out_specs=pl.BlockSpec((r, tn), lambda j: (0, j)),
        out_shape=jax.ShapeDtypeStruct((r, n), F32),
        compiler_params=_cparams(("arbitrary",)),
        name="modulation",
    )(cond, w, b.reshape(1, n))


def _rope(t, cos, sa, sb):
    n = t.shape[-1]
    return t * cos + pltpu.roll(t, n - 16, 1) * sa + pltpu.roll(t, 16, 1) * sb


def _inproj_kernel(x_ref, sh_ref, sc_ref, w_ref, cos_ref, sa_ref, sb_ref,
                   q_ref, k_ref, v_ref, u_ref, g_ref):
    h = _ln(x_ref[0]) * (1.0 + sc_ref[0]) + sh_ref[0]
    hb = h.astype(BF16)
    cos, sa, sb = cos_ref[...], sa_ref[...], sb_ref[...]
    scale = HEAD_DIM ** -0.5
    for j in range(Q_W // 128):
        t = _bdot(hb, w_ref[:, j * 128:(j + 1) * 128])
        q_ref[0, :, j * 128:(j + 1) * 128] = (_rope(t, cos, sa, sb) * scale).astype(BF16)
    t = _bdot(hb, w_ref[:, K_OFF:V_OFF])
    k_ref[0] = _rope(t, cos, sa, sb).astype(BF16)
    v_ref[0] = _bdot(hb, w_ref[:, V_OFF:HY_OFF]).astype(BF16)
    for j in range((GATE_OFF - HY_OFF) // 512):
        u_ref[0, :, j * 512:(j + 1) * 512] = _bdot(hb, w_ref[:, HY_OFF + j * 512:HY_OFF + (j + 1) * 512])
    n_gate = w_ref.shape[1] - GATE_OFF
    for j in range(n_gate // 512):
        g_ref[0, :, j * 512:(j + 1) * 512] = _bdot(
            hb, w_ref[:, GATE_OFF + j * 512:GATE_OFF + (j + 1) * 512])


def _in_projection(x, sh, sc, w_in_b, cos_t, sa_t, sb_t, tm=256):
    b, l, d = x.shape
    in_w = w_in_b.shape[1]
    hy_w = GATE_OFF - HY_OFF
    g_w = in_w - GATE_OFF
    row = lambda bi, i: (bi, i, 0)
    vec = lambda bi, i: (bi, 0, 0)
    tab = lambda bi, i: (i, 0)
    return pl.pallas_call(
        _inproj_kernel,
        grid=(b, l // tm),
        in_specs=[pl.BlockSpec((1, tm, d), row),
                  pl.BlockSpec((1, 1, d), vec),
                  pl.BlockSpec((1, 1, d), vec),
                  pl.BlockSpec((d, in_w), lambda bi, i: (0, 0)),
                  pl.BlockSpec((tm, 128), tab),
                  pl.BlockSpec((tm, 128), tab),
                  pl.BlockSpec((tm, 128), tab)],
        out_specs=[pl.BlockSpec((1, tm, Q_W), row),
                   pl.BlockSpec((1, tm, KV_W), row),
                   pl.BlockSpec((1, tm, KV_W), row),
                   pl.BlockSpec((1, tm, hy_w), row),
                   pl.BlockSpec((1, tm, g_w), row)],
        out_shape=[jax.ShapeDtypeStruct((b, l, Q_W), BF16),
                   jax.ShapeDtypeStruct((b, l, KV_W), BF16),
                   jax.ShapeDtypeStruct((b, l, KV_W), BF16),
                   jax.ShapeDtypeStruct((b, l, hy_w), F32),
                   jax.ShapeDtypeStruct((b, l, g_w), F32)],
        compiler_params=_cparams(("arbitrary", "arbitrary")),
        name="in_projection",
    )(x, sh, sc, w_in_b, cos_t, sa_t, sb_t)


def _ctx_kv_kernel(x_ref, sh_ref, sc_ref, w_ref, k_ref, v_ref):
    h = _ln(x_ref[0]) * (1.0 + sc_ref[0]) + sh_ref[0]
    kv = _bdot(h.astype(BF16), w_ref[...])
    k_ref[0] = kv[:, :KV_W].astype(BF16)
    v_ref[0] = kv[:, KV_W:].astype(BF16)


def _ctx_kv(ctx, sh, sc, w_kv_b):
    b, c, d = ctx.shape
    row = lambda bi: (bi, 0, 0)
    return pl.pallas_call(
        _ctx_kv_kernel,
        grid=(b,),
        in_specs=[pl.BlockSpec((1, c, d), row),
                  pl.BlockSpec((1, 1, d), lambda bi: (0, 0, 0)),
                  pl.BlockSpec((1, 1, d), lambda bi: (0, 0, 0)),
                  pl.BlockSpec((d, 2 * KV_W), lambda bi: (0, 0))],
        out_specs=[pl.BlockSpec((1, c, KV_W), row), pl.BlockSpec((1, c, KV_W), row)],
        out_shape=[jax.ShapeDtypeStruct((b, c, KV_W), BF16),
                   jax.ShapeDtypeStruct((b, c, KV_W), BF16)],
        compiler_params=_cparams(("arbitrary",)),
        name="ctx_kv",
    )(ctx, sh, sc, w_kv_b)


def _attn_kernel(sink_ref, q_ref, k_ref, v_ref, kc_ref, vc_ref, o_ref):
    l = k_ref.shape[1]
    span = ATTN_BLOCK + 2 * WINDOW
    j = pl.program_id(1)
    q0 = j * ATTN_BLOCK
    start = pl.multiple_of(jnp.clip(q0 - WINDOW, 0, l - span), ATTN_BLOCK)
    rows = GQA_GROUP * ATTN_BLOCK
    r_i = lax.broadcasted_iota(jnp.int32, (rows, span), 0)
    qpos = q0 + r_i % ATTN_BLOCK
    kpos = start + lax.broadcasted_iota(jnp.int32, (rows, span), 1)
    band = jnp.abs(kpos - qpos) <= WINDOW
    head_of_row = lax.broadcasted_iota(jnp.int32, (rows, 1), 0) // ATTN_BLOCK
    dn = (((1,), (1,)), ((), ()))
    outs = []
    for kv in range(N_KV_HEADS):
        ks = slice(kv * HEAD_DIM, (kv + 1) * HEAD_DIM)
        kl = k_ref[0, pl.ds(start, span), ks]
        vl = v_ref[0, pl.ds(start, span), ks]
        kc = kc_ref[0, :, ks]
        vc = vc_ref[0, :, ks]
        heads = [kv * GQA_GROUP + g for g in range(GQA_GROUP)]
        qg = jnp.concatenate([q_ref[0, :, h * HEAD_DIM:(h + 1) * HEAD_DIM] for h in heads], axis=0)
        sink = jnp.zeros((rows, 1), F32)
        for g, h in enumerate(heads):
            sink = jnp.where(head_of_row == g, sink_ref[h], sink)
        s_loc = lax.dot_general(qg, kl, dn, preferred_element_type=F32)
        s_loc = jnp.where(band, s_loc, NEG_BIG)
        s_ctx = lax.dot_general(qg, kc, dn, preferred_element_type=F32)
        m = jnp.maximum(jnp.max(s_loc, axis=1, keepdims=True),
                        jnp.max(s_ctx, axis=1, keepdims=True))
        m = jnp.maximum(m, sink)
        p_loc = jnp.exp(s_loc - m)
        p_ctx = jnp.exp(s_ctx - m)
        den = (jnp.sum(p_loc, axis=1, keepdims=True) + jnp.sum(p_ctx, axis=1, keepdims=True)
               + jnp.exp(sink - m))
        o = (_bdot(p_loc.astype(BF16), vl) + _bdot(p_ctx.astype(BF16), vc)) / den
        outs.extend(o[g * ATTN_BLOCK:(g + 1) * ATTN_BLOCK] for g in range(GQA_GROUP))
    o_ref[0] = jnp.concatenate(outs, axis=1).astype(BF16)


def _attention(q, k, v, kc, vc, sink):
    b, l, _ = q.shape
    c = kc.shape[1]
    full = lambda bi, j, s: (bi, 0, 0)
    grid_spec = pltpu.PrefetchScalarGridSpec(
        num_scalar_prefetch=1,
        grid=(b, l // ATTN_BLOCK),
        in_specs=[pl.BlockSpec((1, ATTN_BLOCK, Q_W), lambda bi, j, s: (bi, j, 0)),
                  pl.BlockSpec((1, l, KV_W), full),
                  pl.BlockSpec((1, l, KV_W), full),
                  pl.BlockSpec((1, c, KV_W), full),
                  pl.BlockSpec((1, c, KV_W), full)],
        out_specs=pl.BlockSpec((1, ATTN_BLOCK, Q_W), lambda bi, j, s: (bi, j, 0)),
    )
    return pl.pallas_call(
        _attn_kernel,
        grid_spec=grid_spec,
        out_shape=jax.ShapeDtypeStruct((b, l, Q_W), BF16),
        compiler_params=_cparams(("arbitrary", "arbitrary")),
        name="window_attention",
    )(sink, q, k, v, kc, vc)


def _dft_tables(l, wc):
    h = l // 2
    idx = np.arange(h, dtype=np.int64)
    ang = 2.0 * np.pi * ((idx[:, None] * idx[None, :]) % l).astype(np.float64) / l
    cm = np.cos(ang).astype(np.float32)
    sm = np.sin(ang).astype(np.float32)
    tw = 2.0 * np.pi * idx.astype(np.float64) / (2 * l)
    ones = np.ones((1, wc), np.float32)
    wr = np.cos(tw).astype(np.float32)[:, None] * ones
    wi = (-np.sin(tw)).astype(np.float32)[:, None] * ones
    sgn = np.where(idx % 2 == 0, 1.0, -1.0).astype(np.float32)[:, None] * ones
    wk = np.where(idx == 0, 1.0 / (2 * l), 2.0 / (2 * l)).astype(np.float32)[:, None] * ones
    return cm, sm, wr, wi, sgn, wk


def _half_spectrum(se, so, cm, sm, wr, wi, sgn):
    seb, sob = se.astype(BF16), so.astype(BF16)
    ce, ss_e = _bdot(cm, seb), _bdot(sm, seb)
    co, ss_o = _bdot(cm, sob), _bdot(sm, sob)
    vr = wr * co + wi * ss_o
    vi = wi * co - wr * ss_o
    zar, zai = ce + vr, vi - ss_e
    zbr, zbi = ce - vr, -ss_e - vi
    e_ny = jnp.sum(sgn * se, axis=0, keepdims=True)
    o_ny = jnp.sum(sgn * so, axis=0, keepdims=True)
    return zar, zai, zbr, zbi, e_ny, -o_ny


def _filter_kernel(ze_ref, zo_ref, w1_ref, b1_ref, w2_ref, b2_ref, w3f_ref, w3b_ref,
                   de_ref, do_ref, cm_ref, sm_ref, wr_ref, wi_ref, sgn_ref, wk_ref,
                   har_ref, hai_ref, hbr_ref, hbi_ref, hny_ref, ae_ref, ao_ref):
    cm, sm = cm_ref[...], sm_ref[...]
    wr, wi, sgn, wk = wr_ref[...], wi_ref[...], sgn_ref[...], wk_ref[...]
    h = cm.shape[0]

    @pl.when((pl.program_id(0) == 0) & (pl.program_id(1) == 0))
    def _():
        for z_ref, a_ref in ((ze_ref, ae_ref), (zo_ref, ao_ref)):
            a = jnp.sin(_fdot(z_ref[...], w1_ref[...]) + b1_ref[...])
            a_ref[...] = jnp.sin(_fdot(a, w2_ref[...]) + b2_ref[...])

    def taps(a_ref, w3_ref, d_ref):
        return _fdot(a_ref[...], w3_ref[0]) * d_ref[...]

    fe, fo = taps(ae_ref, w3f_ref, de_ref), taps(ao_ref, w3f_ref, do_ref)
    be, bo = taps(ae_ref, w3b_ref, de_ref), taps(ao_ref, w3b_ref, do_ref)
    row = lax.broadcasted_iota(jnp.int32, be.shape, 0)
    be = jnp.where(row == 0, 0.0, be)
    far, fai, fbr, fbi, fnr, fni = _half_spectrum(fe, fo, cm, sm, wr, wi, sgn)
    bar, bai, bbr, bbi, bnr, bni = _half_spectrum(be, bo, cm, sm, wr, wi, sgn)
    har_ref[0] = wk * (far + bar)
    hai_ref[0] = wk * (fai - bai)
    hbr_ref[0] = wk * (fbr + bbr)
    hbi_ref[0] = wk * (fbi - bbi)
    ny_scale = 2.0 / (4 * h)
    nr = ny_scale * (fnr + bnr)
    ni = ny_scale * (fni - bni)
    rows = lax.broadcasted_iota(jnp.int32, (8, nr.shape[1]), 0)
    hny_ref[0] = jnp.where(rows == 0, nr, jnp.where(rows == 1, ni, 0.0))


def _hyena_filters(zfe, zfo, w1, b1, w2, b2, w3, dec_e, dec_o, tabs, wc):
    cm, sm, wr, wi, sgn, wk = tabs
    h = cm.shape[0]
    emb = zfe.shape[1]
    hid = w2.shape[0]
    nblk = HY_WIDTH // wc
    w3r = w3.reshape(hid, HY_ORDER * 2, HY_WIDTH).transpose(1, 0, 2)
    const = lambda o, cb: (0, 0)
    chan = lambda o, cb: (0, cb)
    out_spec = pl.BlockSpec((1, h, wc), lambda o, cb: (o, 0, cb))
    out_shape = jax.ShapeDtypeStruct((HY_ORDER, h, HY_WIDTH), F32)
    return pl.pallas_call(
        _filter_kernel,
        grid=(HY_ORDER, nblk),
        in_specs=[pl.BlockSpec((h, emb), const), pl.BlockSpec((h, emb), const),
                  pl.BlockSpec((emb, hid), const), pl.BlockSpec((1, hid), const),
                  pl.BlockSpec((hid, hid), const), pl.BlockSpec((1, hid), const),
                  pl.BlockSpec((1, hid, wc), lambda o, cb: (2 * o, 0, cb)),
                  pl.BlockSpec((1, hid, wc), lambda o, cb: (2 * o + 1, 0, cb)),
                  pl.BlockSpec((h, wc), chan), pl.BlockSpec((h, wc), chan),
                  pl.BlockSpec((h, h), const), pl.BlockSpec((h, h), const),
                  pl.BlockSpec((h, wc), const), pl.BlockSpec((h, wc), const),
                  pl.BlockSpec((h, wc), const), pl.BlockSpec((h, wc), const)],
        out_specs=[out_spec, out_spec, out_spec, out_spec,
                   pl.BlockSpec((1, 8, wc), lambda o, cb: (o, 0, cb))],
        out_shape=[out_shape, out_shape, out_shape, out_shape,
                   jax.ShapeDtypeStruct((HY_ORDER, 8, HY_WIDTH), F32)],
        scratch_shapes=[pltpu.VMEM((h, hid), F32), pltpu.VMEM((h, hid), F32)],
        compiler_params=_cparams(("arbitrary", "arbitrary")),
        name="hyena_filter_spectrum",
    )(zfe, zfo, w1, b1.reshape(1, hid), w2, b2.reshape(1, hid), w3r, w3r,
      dec_e, dec_o, cm, sm, wr, wi, sgn, wk)


def _long_conv(ze, zo, har, hai, hbr, hbi, hny, cm, sm, wr, wi, sgn):
    zar, zai, zbr, zbi, znr, zni = _half_spectrum(ze, zo, cm, sm, wr, wi, sgn)
    yar, yai = zar * har - zai * hai, zar * hai + zai * har
    ybr, ybi = zbr * hbr - zbi * hbi, zbr * hbi + zbi * hbr
    dr, di = yar - ybr, yai - ybi
    sar, sai = (yar + ybr).astype(BF16), (yai + ybi).astype(BF16)
    sbr, sbi = (dr * wr + di * wi).astype(BF16), (di * wr - dr * wi).astype(BF16)
    hnr, hni = hny[0:1], hny[1:2]
    ynr = znr * hnr - zni * hni
    yni = znr * hni + zni * hnr
    ye = _bdot(cm, sar) - _bdot(sm, sai) + sgn * ynr
    yo = _bdot(cm, sbr) - _bdot(sm, sbi) - sgn * yni
    return ye, yo


def _hyena_kernel(*refs):
    nsub = (len(refs) - 18) // 4
    u_refs = [refs[s * nsub:(s + 1) * nsub] for s in range(HY_ORDER + 1)]
    rest = refs[(HY_ORDER + 1) * nsub:]
    (cw0_ref, cw1_ref, cw2_ref, cb0_ref, cb1_ref, cb2_ref, skip_ref,
     har_ref, hai_ref, hbr_ref, hbi_ref, hny_ref,
     cm_ref, sm_ref, wr_ref, wi_ref, sgn_ref, o_ref) = rest[:18]
    il_refs = rest[18:]
    h = cm_ref.shape[0]
    cm, sm = cm_ref[...], sm_ref[...]
    wr, wi, sgn = wr_ref[...], wi_ref[...], sgn_ref[...]
    row = lax.broadcasted_iota(jnp.int32, (h, wr.shape[1]), 0)

    def short_conv(u_slabs, cw_ref, cb_ref):
        ue = jnp.concatenate([r[0, pl.ds(0, h, stride=2), :] for r in u_slabs], axis=1)
        uo = jnp.concatenate([r[0, pl.ds(1, h, stride=2), :] for r in u_slabs], axis=1)
        w0, w1, w2 = cw_ref[0:1], cw_ref[1:2], cw_ref[2:3]
        uo_prev = jnp.where(row == 0, 0.0, pltpu.roll(uo, 1, 0))
        ue_next = jnp.where(row == h - 1, 0.0, pltpu.roll(ue, h - 1, 0))
        cb = cb_ref[...]
        se = cb + uo_prev * w0 + ue * w1 + uo * w2
        so = cb + ue * w0 + uo * w1 + ue_next * w2
        return se, so

    x_streams = [short_conv(u_refs[0], cw0_ref, cb0_ref), short_conv(u_refs[1], cw1_ref, cb1_ref)]
    ze, zo = short_conv(u_refs[2], cw2_ref, cb2_ref)
    for o in range(HY_ORDER):
        ce, co = _long_conv(ze, zo, har_ref[o], hai_ref[o], hbr_ref[o], hbi_ref[o], hny_ref[o],
                            cm, sm, wr, wi, sgn)
        sk = skip_ref[o:o + 1]
        xe, xo = x_streams[o]
        ze = xe * (ce + sk * ze)
        zo = xo * (co + sk * zo)
    for j, il in enumerate(il_refs):
        il[pl.ds(0, h, stride=2), :] = ze[:, j * 128:(j + 1) * 128]
        il[pl.ds(1, h, stride=2), :] = zo[:, j * 128:(j + 1) * 128]
        o_ref[0, :, j * 128:(j + 1) * 128] = il[...]


def _hyena(u, conv_w, conv_b, skip, spectra, tabs, wc):
    b, l, _ = u.shape
    har, hai, hbr, hbi, hny = spectra
    cm, sm, wr, wi, sgn, _ = tabs
    h = l // 2
    nblk = HY_WIDTH // wc
    nsub = wc // 128
    conv_b = conv_b.reshape(1, -1)

    def ublk(s, j):
        return pl.BlockSpec((1, l, 128), lambda cb, bi: (bi, 0, (cb + s * nblk) * nsub + j))

    def wblk(r, s):
        return pl.BlockSpec((r, wc), lambda cb, bi: (0, cb + s * nblk))

    const = lambda cb, bi: (0, 0)
    spec = pl.BlockSpec((HY_ORDER, h, wc), lambda cb, bi: (0, 0, cb))
    return pl.pallas_call(
        _hyena_kernel,
        grid=(nblk, b),
        in_specs=[ublk(s, j) for s in range(HY_ORDER + 1) for j in range(nsub)] + [
                  wblk(HY_SHORT_CONV, 0), wblk(HY_SHORT_CONV, 1), wblk(HY_SHORT_CONV, 2),
                  wblk(1, 0), wblk(1, 1), wblk(1, 2),
                  pl.BlockSpec((HY_ORDER, wc), lambda cb, bi: (0, cb)),
                  spec, spec, spec, spec,
                  pl.BlockSpec((HY_ORDER, 8, wc), lambda cb, bi: (0, 0, cb)),
                  pl.BlockSpec((h, h), const), pl.BlockSpec((h, h), const),
                  pl.BlockSpec((h, wc), const), pl.BlockSpec((h, wc), const),
                  pl.BlockSpec((h, wc), const)],
        out_specs=pl.BlockSpec((1, l, wc), lambda cb, bi: (bi, 0, cb)),
        out_shape=jax.ShapeDtypeStruct((b, l, HY_WIDTH), F32),
        scratch_shapes=[pltpu.VMEM((l, 128), F32) for _ in range(nsub)],
        compiler_params=_cparams(("arbitrary", "arbitrary")),
        name="hyena_long_conv",
    )(*([u] * ((HY_ORDER + 1) * nsub)), conv_w, conv_w, conv_w, conv_b, conv_b, conv_b, skip,
      har, hai, hbr, hbi, hny, cm, sm, wr, wi, sgn)


def _merge_kernel(alpha, att_ref, hy_ref, gate_ref, x_ref, g1_ref, sh2_ref, sc2_ref,
                  wba_ref, wbh_ref, wo_ref, l1g_ref, l1b_ref, rw_ref, rb_ref,
                  xmid_ref, h2_ref, logit_ref):
    tm, d = x_ref.shape[1], x_ref.shape[2]
    rw = rw_ref[...]
    r_hi = rw.astype(BF16)
    r_lo = (rw - r_hi.astype(F32)).astype(BF16)
    dn = (((1,), (1,)), ((), ()))
    nt_dot = lambda p, q: lax.dot_general(p, q, dn, preferred_element_type=F32)
    half = tm // 2
    for r in (slice(0, half), slice(half, tm)):
        a = _bdot(att_ref[0, r], wba_ref[...])
        hh = _bdot(hy_ref[0, r].astype(BF16), wbh_ref[...])
        ga = jax.nn.sigmoid(gate_ref[0, r, :d])
        gh = jax.nn.sigmoid(gate_ref[0, r, d:])
        y = _bdot((ga * a + gh * hh).astype(BF16), wo_ref[...])
        xm = _ln(alpha * x_ref[0, r] + g1_ref[0] * y) * l1g_ref[...] + l1b_ref[...]
        xmid_ref[0, r] = xm
        h2 = _ln(xm) * (1.0 + sc2_ref[0]) + sh2_ref[0]
        h_hi = h2.astype(BF16)
        h2_ref[0, r] = h_hi
        h_lo = (h2 - h_hi.astype(F32)).astype(BF16)
        logit_ref[:, r] = nt_dot(r_hi, h_hi) + nt_dot(r_hi, h_lo) + nt_dot(r_lo, h_hi) + rb_ref[...]


def _merge(att, hy, gate, x, g1, sh2, sc2, wba, wbh, wo, l1g, l1b, rw_t, rb, alpha, tm=512):
    b, l, d = x.shape
    ne = rw_t.shape[0]
    nt = l // tm
    row = lambda bi, i: (bi, i, 0)
    vec = lambda bi, i: (bi, 0, 0)
    const = lambda bi, i: (0, 0)
    return pl.pallas_call(
        functools.partial(_merge_kernel, alpha),
        grid=(b, l // tm),
        in_specs=[pl.BlockSpec((1, tm, Q_W), row),
                  pl.BlockSpec((1, tm, HY_WIDTH), row),
                  pl.BlockSpec((1, tm, 2 * d), row),
                  pl.BlockSpec((1, tm, d), row),
                  pl.BlockSpec((1, 1, d), vec), pl.BlockSpec((1, 1, d), vec),
                  pl.BlockSpec((1, 1, d), vec),
                  pl.BlockSpec((Q_W, d), const), pl.BlockSpec((HY_WIDTH, d), const),
                  pl.BlockSpec((d, d), const),
                  pl.BlockSpec((1, d), const), pl.BlockSpec((1, d), const),
                  pl.BlockSpec((ne, d), const), pl.BlockSpec((ne, 1), const)],
        out_specs=[pl.BlockSpec((1, tm, d), row), pl.BlockSpec((1, tm, d), row),
                   pl.BlockSpec((ne, tm), lambda bi, i: (0, bi * nt + i))],
        out_shape=[jax.ShapeDtypeStruct((b, l, d), F32),
                   jax.ShapeDtypeStruct((b, l, d), BF16),
                   jax.ShapeDtypeStruct((ne, b * l), F32)],
        compiler_params=_cparams(("arbitrary", "arbitrary")),
        name="merge_ln_router",
    )(att, hy, gate, x, g1, sh2, sc2, wba, wbh, wo, l1g.reshape(1, d), l1b.reshape(1, d),
      rw_t, rb.reshape(ne, 1))


def _route_kernel(lg_ref, gate_ref, pos_ref, tcnt_ref, tbase_ref, cnt_ref, run_ref):
    @pl.when(pl.program_id(0) == 0)
    def _():
        run_ref[...] = jnp.zeros_like(run_ref)

    lg = lg_ref[...]
    ne, tr = lg.shape
    sub = lax.broadcasted_iota(jnp.int32, (ne, tr), 0)
    work = lg
    vals, hots = [], []
    for _ in range(TOP_K):
        m = jnp.max(work, axis=0, keepdims=True)
        idx = jnp.min(jnp.where(work == m, sub, ne), axis=0, keepdims=True)
        hot = sub == idx
        vals.append(m)
        hots.append(hot)
        work = jnp.where(hot, -jnp.inf, work)
    exps = [jnp.exp(v - vals[0]) for v in vals]
    den = exps[0] + exps[1] + exps[2] + exps[3]
    member = jnp.zeros((ne, tr), F32)
    for hot in hots:
        member = member + jnp.where(hot, 1.0, 0.0)
    r_i = lax.broadcasted_iota(jnp.int32, (tr, tr), 0)
    c_i = lax.broadcasted_iota(jnp.int32, (tr, tr), 1)
    earlier = jnp.where(r_i < c_i, 1.0, 0.0).astype(BF16)
    prefix = _bdot(member.astype(BF16), earlier)
    cnt = jnp.broadcast_to(jnp.sum(member, axis=1, keepdims=True), (ne, 128))
    e_r = lax.broadcasted_iota(jnp.int32, (ne, ne), 0)
    e_c = lax.broadcasted_iota(jnp.int32, (ne, ne), 1)
    lower = jnp.where(e_c < e_r, 1.0, 0.0).astype(BF16)
    off = _bdot(lower, cnt.astype(BF16))
    base = off[:, 0:1] + prefix
    sub_k = lax.broadcasted_iota(jnp.int32, (TOP_K, tr), 0)
    gate = jnp.zeros((TOP_K, tr), F32)
    pos = jnp.zeros((TOP_K, tr), F32)
    for k in range(TOP_K):
        pk = jnp.sum(jnp.where(hots[k], base, 0.0), axis=0, keepdims=True)
        gate = jnp.where(sub_k == k, exps[k] / den, gate)
        pos = jnp.where(sub_k == k, pk, pos)
    gate_ref[...] = gate
    pos_ref[...] = pos.astype(jnp.int32)
    tcnt_ref[0] = cnt.astype(jnp.int32)
    tbase_ref[0] = run_ref[...].astype(jnp.int32)
    run_ref[...] = run_ref[...] + cnt
    cnt_ref[...] = run_ref[...].astype(jnp.int32)


def _route(logits_t, tr):
    ne, t = logits_t.shape
    nt = t // tr
    tok = lambda i: (0, i)
    tile = lambda i: (i, 0, 0)
    return pl.pallas_call(
        _route_kernel,
        grid=(nt,),
        in_specs=[pl.BlockSpec((ne, tr), tok)],
        out_specs=[pl.BlockSpec((TOP_K, tr), tok), pl.BlockSpec((TOP_K, tr), tok),
                   pl.BlockSpec((1, ne, 128), tile), pl.BlockSpec((1, ne, 128), tile),
                   pl.BlockSpec((ne, 128), lambda i: (0, 0))],
        out_shape=[jax.ShapeDtypeStruct((TOP_K, t), F32),
                   jax.ShapeDtypeStruct((TOP_K, t), jnp.int32),
                   jax.ShapeDtypeStruct((nt, ne, 128), jnp.int32),
                   jax.ShapeDtypeStruct((nt, ne, 128), jnp.int32),
                   jax.ShapeDtypeStruct((ne, 128), jnp.int32)],
        scratch_shapes=[pltpu.VMEM((ne, 128), F32)],
        compiler_params=_cparams(("arbitrary",)),
        name="route_topk",
    )(logits_t)


def _row_tile(d):
    return (d // 256, 128)


def _pack_rows(v):
    n, d = v.shape
    bits = lax.bitcast_convert_type(v.astype(BF16).astype(F32), jnp.uint32)
    word = (bits[:, d // 2:] & jnp.uint32(0xFFFF0000)) | (bits[:, :d // 2] >> 16)
    return word.reshape((n,) + _row_tile(d))


def _unpack_rows(w):
    n = w.shape[0]
    word = w.reshape(n, w.shape[1] * w.shape[2])
    lo = lax.bitcast_convert_type(word << 16, F32)
    hi = lax.bitcast_convert_type(word & jnp.uint32(0xFFFF0000), F32)
    return jnp.concatenate([lo, hi], axis=1)


def _strip_copies(n_ref, dst_ref, make_copy, max_rows):
    ne = n_ref.shape[2]
    bits = [1 << s for s in range(max_rows.bit_length() - 1, -1, -1)]
    big = pl.next_power_of_2(max(1, 2 * TOP_K * max_rows // ne))
    hi_bits = [b for b in bits if b >= big]
    lo_bits = [b for b in bits if b < big]

    def body(e, off):
        n = n_ref[0, 0, e]
        dst = dst_ref[0, 0, e]

        def chunks(sel, done):
            for bit in sel:
                part = n & bit

                @pl.when(part != 0)
                def _():
                    make_copy(off + done, dst + done, bit).start()

                done = done + part

        @pl.when(n >= big)
        def _():
            chunks(hi_bits, jnp.int32(0))

        chunks(lo_bits, n & -big)
        return off + n

    lax.fori_loop(0, ne, body, jnp.int32(0))


def _dispatch_kernel(n_ref, dst_ref, pos_ref, h_ref, xs_ref, srt_ref, sem):
    i = pl.program_id(0)
    slot = i % 2
    td = h_ref.shape[0]
    n = td * TOP_K
    pos = pos_ref[...]
    rows = lax.broadcasted_iota(jnp.int32, (n, td), 0)
    hit = rows == pos[0:1]
    for k in range(1, TOP_K):
        hit = hit | (rows == pos[k:k + 1])
    perm = jnp.where(hit, 1.0, 0.0).astype(BF16)
    srt_ref[slot] = _pack_rows(_bdot(perm, h_ref[...]))

    def all_rows(s):
        return pltpu.make_async_copy(srt_ref.at[s], xs_ref.at[pl.ds(0, n)], sem.at[s])

    _strip_copies(n_ref, dst_ref,
                  lambda src, dst, size: pltpu.make_async_copy(
                      srt_ref.at[slot, pl.ds(src, size)], xs_ref.at[pl.ds(dst, size)], sem.at[slot]),
                  td)

    @pl.when(i > 0)
    def _():
        all_rows(1 - slot).wait()

    @pl.when(i == pl.num_programs(0) - 1)
    def _():
        all_rows(slot).wait()


def _dispatch(h2, pos_t, tile_n, tile_dst, td):
    t, d = h2.shape
    a = t * TOP_K
    n = td * TOP_K
    nt = t // td
    ne = tile_n.shape[-1]
    smem = lambda: pl.BlockSpec((1, 1, ne), lambda i: (i, 0, 0), memory_space=pltpu.SMEM)
    return pl.pallas_call(
        _dispatch_kernel,
        grid=(nt,),
        in_specs=[smem(), smem(),
                  pl.BlockSpec((TOP_K, td), lambda i: (0, i)),
                  pl.BlockSpec((td, d), lambda i: (i, 0))],
        out_specs=pl.BlockSpec(memory_space=pl.ANY),
        out_shape=jax.ShapeDtypeStruct((a,) + _row_tile(d), jnp.uint32),
        scratch_shapes=[pltpu.VMEM((2, n) + _row_tile(d), jnp.uint32),
                        pltpu.SemaphoreType.DMA((2,))],
        compiler_params=_cparams(("arbitrary",)),
        name="moe_dispatch",
    )(tile_n.reshape(nt, 1, ne), tile_dst.reshape(nt, 1, ne), pos_t, h2)


def _expert_kernel(tile_ref, exp_ref, lo_ref, hi_ref, first_ref, nitem_ref,
                   xs_ref, w1_ref, b1_ref, w2_ref, b2_ref, ys_ref, w1b_ref, w2b_ref):
    w = pl.program_id(0)
    tm = xs_ref.shape[0]
    d, dff = w2_ref.shape[2], w2_ref.shape[1]

    @pl.when((w == 0) | (exp_ref[w] != exp_ref[jnp.maximum(w - 1, 0)]))
    def _():
        w1b_ref[...] = w1_ref[0].astype(BF16)
        w2b_ref[...] = w2_ref[0].astype(BF16)

    @pl.when(w < nitem_ref[0])
    def _():
        x = _unpack_rows(xs_ref[...]).astype(BF16)
        hb = _bdot(x, w1b_ref[...]) + b1_ref[0]
        glu = jnp.minimum(hb[:, :dff], SWIGLU_LIMIT)
        lin = jnp.clip(hb[:, dff:], -SWIGLU_LIMIT, SWIGLU_LIMIT)
        act = glu * jax.nn.sigmoid(SWIGLU_ALPHA * glu) * (lin + 1.0)
        y = _bdot(act.astype(BF16), w2b_ref[...]) + b2_ref[0]
        row = lax.broadcasted_iota(jnp.int32, y.shape, 0)
        y = jnp.where((row >= lo_ref[w]) & (row < hi_ref[w]), y, 0.0)

        @pl.when(first_ref[w] == 1)
        def _():
            ys_ref[...] = _pack_rows(y)

        @pl.when(first_ref[w] == 0)
        def _():
            ys_ref[...] = _pack_rows(_unpack_rows(ys_ref[...]) + y)


def _experts(xs, items, w1, b1, w2, b2, tm):
    a = xs.shape[0]
    ne, d, dff2 = w1.shape
    dff = dff2 // 2
    n_items = a // tm + ne - 1
    tile_w, exp_w, lo_w, hi_w, first_w, nitem = items
    rows = pl.BlockSpec((tm,) + _row_tile(d), lambda w, tl, ex, lo, hi, fi, ni: (tl[w], 0, 0))
    per_expert = lambda w, tl, ex, lo, hi, fi, ni: (ex[w], 0, 0)
    grid_spec = pltpu.PrefetchScalarGridSpec(
        num_scalar_prefetch=6,
        grid=(n_items,),
        in_specs=[rows,
                  pl.BlockSpec((1, d, dff2), per_expert),
                  pl.BlockSpec((1, 1, dff2), per_expert),
                  pl.BlockSpec((1, dff, d), per_expert),
                  pl.BlockSpec((1, 1, d), per_expert)],
        out_specs=rows,
        scratch_shapes=[pltpu.VMEM((d, dff2), BF16), pltpu.VMEM((dff, d), BF16)],
    )
    return pl.pallas_call(
        _expert_kernel,
        grid_spec=grid_spec,
        out_shape=jax.ShapeDtypeStruct(xs.shape, jnp.uint32),
        compiler_params=_cparams(("arbitrary",)),
        name="moe_experts",
    )(tile_w, exp_w, lo_w, hi_w, first_w, nitem,
      xs, w1, b1.reshape(ne, 1, dff2), w2, b2.reshape(ne, 1, d))


def _work_items(counts, a, tm):
    ne = counts.shape[0]
    n_items = a // tm + ne - 1
    ends = jnp.cumsum(counts)
    starts = ends - counts
    first_tile = starts // tm
    last_tile = (ends - 1) // tm
    nvis = jnp.where(counts > 0, last_tile - first_tile + 1, 0)
    vis_end = jnp.cumsum(nvis)
    vis_start = vis_end - nvis
    total = vis_end[-1]
    w = jnp.minimum(jnp.arange(n_items, dtype=jnp.int32), total - 1)
    e_w = jnp.sum((vis_end[None, :] <= w[:, None]).astype(jnp.int32), axis=1)
    hot = e_w[:, None] == jnp.arange(ne, dtype=jnp.int32)[None, :]
    pick = lambda v: jnp.sum(jnp.where(hot, v[None, :], 0), axis=1)
    tile_w = (pick(first_tile) + (w - pick(vis_start))).astype(jnp.int32)
    lo_w = jnp.maximum(pick(starts) - tile_w * tm, 0).astype(jnp.int32)
    hi_w = jnp.minimum(pick(ends) - tile_w * tm, tm).astype(jnp.int32)
    prev = jnp.concatenate([jnp.full((1,), -1, jnp.int32), tile_w[:-1]])
    first_w = (tile_w != prev).astype(jnp.int32)
    return starts, (tile_w, e_w, lo_w, hi_w, first_w, total.reshape(1).astype(jnp.int32))


def _combine_kernel(alpha, n_ref, dst_ref, nn_ref, ndst_ref, pos_ref, gate_ref,
                    ys_ref, xm_ref, g2_ref, lg_ref, lb_ref, o_ref, srt_ref, sem):
    i = pl.program_id(0)
    last = pl.num_programs(0) - 1
    slot = i % 2
    td = xm_ref.shape[0]
    n = td * TOP_K

    def fetch(cnt_ref, from_ref, s):
        _strip_copies(cnt_ref, from_ref,
                      lambda row, src, size: pltpu.make_async_copy(
                          ys_ref.at[pl.ds(src, size)], srt_ref.at[s, pl.ds(row, size)], sem.at[s]),
                      td)

    @pl.when(i == 0)
    def _():
        fetch(n_ref, dst_ref, slot)

    @pl.when(i < last)
    def _():
        fetch(nn_ref, ndst_ref, 1 - slot)

    pltpu.make_async_copy(ys_ref.at[pl.ds(0, n)], srt_ref.at[slot], sem.at[slot]).wait()

    pos, gate = pos_ref[...], gate_ref[...]
    lanes = lax.broadcasted_iota(jnp.int32, (td, n), 1)
    wsel = jnp.where(lanes == pos[:, 0:1], gate[:, 0:1], 0.0)
    for k in range(1, TOP_K):
        wsel = wsel + jnp.where(lanes == pos[:, k:k + 1], gate[:, k:k + 1], 0.0)
    w_hi = wsel.astype(BF16)
    w_lo = (wsel - w_hi.astype(F32)).astype(BF16)
    y = _unpack_rows(srt_ref[slot]).astype(BF16)
    f = _bdot(w_hi, y) + _bdot(w_lo, y)
    o_ref[...] = _ln(alpha * xm_ref[...] + g2_ref[0] * f) * lg_ref[...] + lb_ref[...]


def _combine(ys, pos, gate, tile_n, tile_dst, x_mid, g2, lg, lb, alpha, td):
    t, d = x_mid.shape
    n = td * TOP_K
    nt = t // td
    per_b = nt // g2.shape[0]
    ne = tile_n.shape[-1]
    cur = lambda: pl.BlockSpec((1, 1, ne), lambda i: (i, 0, 0), memory_space=pltpu.SMEM)
    nxt = lambda: pl.BlockSpec((1, 1, ne), lambda i: (jnp.minimum(i + 1, nt - 1), 0, 0),
                               memory_space=pltpu.SMEM)
    const = lambda i: (0, 0)
    tile_n = tile_n.reshape(nt, 1, ne)
    tile_dst = tile_dst.reshape(nt, 1, ne)
    return pl.pallas_call(
        functools.partial(_combine_kernel, alpha),
        grid=(nt,),
        in_specs=[cur(), cur(), nxt(), nxt(),
                  pl.BlockSpec((td, TOP_K), lambda i: (i, 0)),
                  pl.BlockSpec((td, TOP_K), lambda i: (i, 0)),
                  pl.BlockSpec(memory_space=pl.ANY),
                  pl.BlockSpec((td, d), lambda i: (i, 0)),
                  pl.BlockSpec((1, 1, d), lambda i: (i // per_b, 0, 0)),
                  pl.BlockSpec((1, d), const), pl.BlockSpec((1, d), const)],
        out_specs=pl.BlockSpec((td, d), lambda i: (i, 0)),
        out_shape=jax.ShapeDtypeStruct((t, d), F32),
        scratch_shapes=[pltpu.VMEM((2, n) + _row_tile(d), jnp.uint32),
                        pltpu.SemaphoreType.DMA((2,))],
        compiler_params=_cparams(("arbitrary",)),
        name="moe_combine_ln",
    )(tile_n, tile_dst, tile_n, tile_dst, pos, gate, ys, x_mid, g2,
      lg.reshape(1, d), lb.reshape(1, d))


def _rope_tables(l):
    f32 = np.float32
    rows = l // GRID_W
    row = np.repeat(np.arange(rows, dtype=f32), GRID_W)
    col = np.tile(np.arange(GRID_W, dtype=f32), rows)
    n_freq = HEAD_DIM // 4
    inv_freq = np.power(f32(ROPE_BASE), -np.arange(n_freq, dtype=f32) / f32(n_freq)).astype(f32)
    ang_r = (row[:, None] * inv_freq).astype(f32)
    ang_c = (col[:, None] * inv_freq).astype(f32)
    zero = np.zeros_like(ang_r)
    cos_r, sin_r, cos_c, sin_c = np.cos(ang_r), np.sin(ang_r), np.cos(ang_c), np.sin(ang_c)
    cos_h = np.concatenate([cos_r, cos_r, cos_c, cos_c], axis=1)
    sa_h = np.concatenate([-sin_r, zero, -sin_c, zero], axis=1)
    sb_h = np.concatenate([zero, sin_r, zero, sin_c], axis=1)
    rep = 128 // HEAD_DIM
    return tuple(np.tile(a, (1, rep)).astype(f32) for a in (cos_h, sa_h, sb_h))


def _filter_features(l):
    f32 = np.float32
    bands = (HY_EMB_DIM - 1) // 2
    t = np.linspace(0.0, 1.0, l, dtype=f32)[:, None]
    omega = (f32(2.0 * math.pi) * np.arange(l, dtype=f32)[:, None] / f32(l)).astype(f32)
    f = np.linspace(1e-4, bands - 1, bands, dtype=f32)[None, :]
    ang = (f * omega).astype(f32)
    z = np.concatenate([t, np.cos(ang), -np.sin(ang)], axis=-1).astype(f32)
    min_decay = math.log(HY_DECAY_TARGET) / HY_FAST_DECAY_PCT
    max_decay = math.log(HY_DECAY_TARGET) / HY_SLOW_DECAY_PCT
    deltas = np.abs(np.linspace(min_decay, max_decay, HY_WIDTH, dtype=f32))
    decay = np.exp(-t * deltas).astype(f32)
    return z, decay


def kernel(x, c, ctx, c_ctx, w_mod, b_mod, w_in, attn_sink, hy_conv_w, hy_conv_b, hy_filt_w1,
           hy_filt_b1, hy_filt_w2, hy_filt_b2, hy_filt_w3, hy_skip, w_branch_attn, w_branch_hyena,
           w_out, ln1_g, ln1_b, router_w, router_b, exp_w1, exp_b1, exp_w2, exp_b2, ln2_g, ln2_b):
    depth = w_mod.shape[0]
    assert depth == 1, "only the single-layer configuration is implemented"
    b, l, d = x.shape
    t = b * l
    alpha = (2 * depth) ** 0.25
    hy_wc = 256
    expert_tm = 512

    n_cond = b + 1
    pad = (-n_cond) % 8
    cond = jnp.concatenate([c, c_ctx[None], jnp.zeros((pad, d), F32)], axis=0)
    mod = _modulation(cond, w_mod[0], b_mod[0])
    mod_x = mod[:b].reshape(b, 1, 6, d)
    sh1, sc1, g1, sh2, sc2, g2 = (mod_x[:, :, i] for i in range(6))
    mod_c = mod[b:b + 1].reshape(1, 1, 6, d)
    csh1, csc1 = mod_c[:, :, 0], mod_c[:, :, 1]

    w_in_b = w_in[0].astype(BF16)
    cos_t, sa_t, sb_t = (jnp.asarray(a) for a in _rope_tables(l))
    q, k, v, u_hy, gate_x = _in_projection(x, sh1, sc1, w_in_b, cos_t, sa_t, sb_t)
    k_c, v_c = _ctx_kv(ctx, csh1, csc1, w_in_b[:, K_OFF:HY_OFF])
    att = _attention(q, k, v, k_c, v_c, attn_sink[0])

    tabs_np = _dft_tables(l, hy_wc)
    tabs = (jnp.asarray(tabs_np[0]).astype(BF16), jnp.asarray(tabs_np[1]).astype(BF16)) + tuple(
        jnp.asarray(a) for a in tabs_np[2:])
    zfeat, decay = _filter_features(l)
    emb_pad = (-HY_EMB_DIM) % 128
    zfeat = np.pad(zfeat, ((0, 0), (0, emb_pad)))
    fw1 = jnp.pad(hy_filt_w1[0], ((0, emb_pad), (0, 0)))
    spectra = _hyena_filters(jnp.asarray(zfeat[0::2]), jnp.asarray(zfeat[1::2]), fw1, hy_filt_b1[0],
                             hy_filt_w2[0], hy_filt_b2[0], hy_filt_w3[0], jnp.asarray(decay[0::2]),
                             jnp.asarray(decay[1::2]), tabs, hy_wc)
    hy = _hyena(u_hy, hy_conv_w[0], hy_conv_b[0], hy_skip[0], spectra, tabs, hy_wc)

    x_mid, h2, logits_t = _merge(att, hy, gate_x, x, g1, sh2, sc2,
                                 w_branch_attn[0].astype(BF16), w_branch_hyena[0].astype(BF16),
                                 w_out[0].astype(BF16), ln1_g[0], ln1_b[0],
                                 jnp.transpose(router_w[0]), router_b[0], alpha)

    moe_td = 256
    gate_t, pos_t, tile_cnt, tile_base, counts = _route(logits_t, moe_td)
    starts, items = _work_items(counts[:, 0], t * TOP_K, expert_tm)
    tile_n = tile_cnt[:, :, 0]
    tile_dst = starts[None, :] + tile_base[:, :, 0]
    xs = _dispatch(h2.reshape(t, d), pos_t, tile_n, tile_dst, moe_td)
    ys = _experts(xs, items, exp_w1[0], exp_b1[0], exp_w2[0], exp_b2[0], expert_tm)
    out = _combine(ys, jnp.transpose(pos_t), jnp.transpose(gate_t), tile_n, tile_dst,
                   x_mid.reshape(t, d), g2, ln2_g[0], ln2_b[0], alpha, moe_td)
    return out.reshape(b, l, d)
```

```python
import functools
import math

import numpy as np
import jax
import jax.numpy as jnp
from jax import lax
from jax.experimental import pallas as pl
from jax.experimental.pallas import tpu as pltpu

F32 = jnp.float32
BF16 = jnp.bfloat16
HIGHEST = lax.Precision.HIGHEST

GRID_W = 64
N_HEADS = 8
N_KV_HEADS = 2
GQA_GROUP = N_HEADS // N_KV_HEADS
HEAD_DIM = 64
WINDOW = 128
ATTN_BLOCK = 128
ROPE_BASE = 10000.0

HY_WIDTH = 512
HY_ORDER = 2
HY_SHORT_CONV = 3
HY_EMB_DIM = 33
HY_DECAY_TARGET = 1e-2
HY_FAST_DECAY_PCT = 0.3
HY_SLOW_DECAY_PCT = 1.5

N_EXPERTS = 32
TOP_K = 4
SWIGLU_LIMIT = 7.0
SWIGLU_ALPHA = 1.702
LN_EPS = 1e-5

Q_W = N_HEADS * HEAD_DIM
KV_W = N_KV_HEADS * HEAD_DIM
K_OFF = Q_W
V_OFF = K_OFF + KV_W
HY_OFF = V_OFF + KV_W
GATE_OFF = HY_OFF + (HY_ORDER + 1) * HY_WIDTH

VMEM_LIMIT = 56 * 1024 * 1024
NEG_BIG = -1e30
MERGE_CHUNKS = 2


def _cparams(sem):
    return pltpu.CompilerParams(dimension_semantics=sem, vmem_limit_bytes=VMEM_LIMIT)


def _ln(x):
    mu = jnp.mean(x, axis=-1, keepdims=True)
    xc = x - mu
    var = jnp.mean(xc * xc, axis=-1, keepdims=True)
    return xc * lax.rsqrt(var + LN_EPS)


def _bdot(a, b):
    return jnp.dot(a, b, preferred_element_type=F32)


def _fdot(a, b):
    return jnp.dot(a, b, preferred_element_type=F32, precision=HIGHEST)


def _mod_kernel(c_ref, w_ref, b_ref, o_ref):
    c = c_ref[...]
    s = c * jax.nn.sigmoid(c)
    o_ref[...] = _fdot(s, w_ref[...]) + b_ref[...]


def _modulation(cond, w, b, tn=512):
    r, d = cond.shape
    n = w.shape[1]
    return pl.pallas_call(
        _mod_kernel,
        grid=(n // tn,),
        in_specs=[pl.BlockSpec((r, d), lambda j: (0, 0)),
                  pl.BlockSpec((d, tn), lambda j: (0, j)),
                  pl.BlockSpec((1, tn), lambda j: (0, j))],
        out_specs=pl.BlockSpec((r, tn), lambda j: (0, j)),
        out_shape=jax.ShapeDtypeStruct((r, n), F32),
        compiler_params=_cparams(("arbitrary",)),
        name="modulation",
    )(cond, w, b.reshape(1, n))


def _rope(t, cos, sa, sb):
    n = t.shape[-1]
    return t * cos + pltpu.roll(t, n - 16, 1) * sa + pltpu.roll(t, 16, 1) * sb


def _inproj_kernel(x_ref, sh_ref, sc_ref, w_ref, cos_ref, sa_ref, sb_ref,
                   q_ref, k_ref, v_ref, u_ref, g_ref):
    h = _ln(x_ref[0]) * (1.0 + sc_ref[0]) + sh_ref[0]
    hb = h.astype(BF16)
    cos, sa, sb = cos_ref[...], sa_ref[...], sb_ref[...]
    scale = HEAD_DIM ** -0.5
    for j in range(Q_W // 128):
        t = _bdot(hb, w_ref[:, j * 128:(j + 1) * 128])
        q_ref[0, :, j * 128:(j + 1) * 128] = (_rope(t, cos, sa, sb) * scale).astype(BF16)
    t = _bdot(hb, w_ref[:, K_OFF:V_OFF])
    k_ref[0] = _rope(t, cos, sa, sb).astype(BF16)
    v_ref[0] = _bdot(hb, w_ref[:, V_OFF:HY_OFF]).astype(BF16)
    for j in range((GATE_OFF - HY_OFF) // 512):
        u_ref[0, :, j * 512:(j + 1) * 512] = _bdot(hb, w_ref[:, HY_OFF + j * 512:HY_OFF + (j + 1) * 512])
    n_gate = w_ref.shape[1] - GATE_OFF
    for j in range(n_gate // 512):
        g_ref[0, :, j * 512:(j + 1) * 512] = _bdot(
            hb, w_ref[:, GATE_OFF + j * 512:GATE_OFF + (j + 1) * 512])


def _in_projection(x, sh, sc, w_in_b, cos_t, sa_t, sb_t, tm=256):
    b, l, d = x.shape
    in_w = w_in_b.shape[1]
    hy_w = GATE_OFF - HY_OFF
    g_w = in_w - GATE_OFF
    row = lambda bi, i: (bi, i, 0)
    vec = lambda bi, i: (bi, 0, 0)
    tab = lambda bi, i: (i, 0)
    return pl.pallas_call(
        _inproj_kernel,
        grid=(b, l // tm),
        in_specs=[pl.BlockSpec((1, tm, d), row),
                  pl.BlockSpec((1, 1, d), vec),
                  pl.BlockSpec((1, 1, d), vec),
                  pl.BlockSpec((d, in_w), lambda bi, i: (0, 0)),
                  pl.BlockSpec((tm, 128), tab),
                  pl.BlockSpec((tm, 128), tab),
                  pl.BlockSpec((tm, 128), tab)],
        out_specs=[pl.BlockSpec((1, tm, Q_W), row),
                   pl.BlockSpec((1, tm, KV_W), row),
                   pl.BlockSpec((1, tm, KV_W), row),
                   pl.BlockSpec((1, tm, hy_w), row),
                   pl.BlockSpec((1, tm, g_w), row)],
        out_shape=[jax.ShapeDtypeStruct((b, l, Q_W), BF16),
                   jax.ShapeDtypeStruct((b, l, KV_W), BF16),
                   jax.ShapeDtypeStruct((b, l, KV_W), BF16),
                   jax.ShapeDtypeStruct((b, l, hy_w), F32),
                   jax.ShapeDtypeStruct((b, l, g_w), F32)],
        compiler_params=_cparams(("arbitrary", "arbitrary")),
        name="in_projection",
    )(x, sh, sc, w_in_b, cos_t, sa_t, sb_t)


def _ctx_kv_kernel(x_ref, sh_ref, sc_ref, w_ref, k_ref, v_ref):
    h = _ln(x_ref[0]) * (1.0 + sc_ref[0]) + sh_ref[0]
    kv = _bdot(h.astype(BF16), w_ref[...])
    k_ref[0] = kv[:, :KV_W].astype(BF16)
    v_ref[0] = kv[:, KV_W:].astype(BF16)


def _ctx_kv(ctx, sh, sc, w_kv_b):
    b, c, d = ctx.shape
    row = lambda bi: (bi, 0, 0)
    return pl.pallas_call(
        _ctx_kv_kernel,
        grid=(b,),
        in_specs=[pl.BlockSpec((1, c, d), row),
                  pl.BlockSpec((1, 1, d), lambda bi: (0, 0, 0)),
                  pl.BlockSpec((1, 1, d), lambda bi: (0, 0, 0)),
                  pl.BlockSpec((d, 2 * KV_W), lambda bi: (0, 0))],
        out_specs=[pl.BlockSpec((1, c, KV_W), row), pl.BlockSpec((1, c, KV_W), row)],
        out_shape=[jax.ShapeDtypeStruct((b, c, KV_W), BF16),
                   jax.ShapeDtypeStruct((b, c, KV_W), BF16)],
        compiler_params=_cparams(("arbitrary",)),
        name="ctx_kv",
    )(ctx, sh, sc, w_kv_b)


def _attn_kernel(sink_ref, q_ref, k_ref, v_ref, kc_ref, vc_ref, o_ref):
    l = k_ref.shape[1]
    span = ATTN_BLOCK + 2 * WINDOW
    j = pl.program_id(1)
    q0 = j * ATTN_BLOCK
    start = pl.multiple_of(jnp.clip(q0 - WINDOW, 0, l - span), ATTN_BLOCK)
    rows = GQA_GROUP * ATTN_BLOCK
    r_i = lax.broadcasted_iota(jnp.int32, (rows, span), 0)
    qpos = q0 + r_i % ATTN_BLOCK
    kpos = start + lax.broadcasted_iota(jnp.int32, (rows, span), 1)
    band = jnp.abs(kpos - qpos) <= WINDOW
    head_of_row = lax.broadcasted_iota(jnp.int32, (rows, 1), 0) // ATTN_BLOCK
    dn = (((1,), (1,)), ((), ()))
    outs = []
    for kv in range(N_KV_HEADS):
        ks = slice(kv * HEAD_DIM, (kv + 1) * HEAD_DIM)
        kl = k_ref[0, pl.ds(start, span), ks]
        vl = v_ref[0, pl.ds(start, span), ks]
        kc = kc_ref[0, :, ks]
        vc = vc_ref[0, :, ks]
        heads = [kv * GQA_GROUP + g for g in range(GQA_GROUP)]
        qg = jnp.concatenate([q_ref[0, :, h * HEAD_DIM:(h + 1) * HEAD_DIM] for h in heads], axis=0)
        sink = jnp.zeros((rows, 1), F32)
        for g, h in enumerate(heads):
            sink = jnp.where(head_of_row == g, sink_ref[h], sink)
        s_loc = lax.dot_general(qg, kl, dn, preferred_element_type=F32)
        s_loc = jnp.where(band, s_loc, NEG_BIG)
        s_ctx = lax.dot_general(qg, kc, dn, preferred_element_type=F32)
        m = jnp.maximum(jnp.max(s_loc, axis=1, keepdims=True),
                        jnp.max(s_ctx, axis=1, keepdims=True))
        m = jnp.maximum(m, sink)
        p_loc = jnp.exp(s_loc - m)
        p_ctx = jnp.exp(s_ctx - m)
        den = (jnp.sum(p_loc, axis=1, keepdims=True) + jnp.sum(p_ctx, axis=1, keepdims=True)
               + jnp.exp(sink - m))
        o = (_bdot(p_loc.astype(BF16), vl) + _bdot(p_ctx.astype(BF16), vc)) / den
        outs.extend(o[g * ATTN_BLOCK:(g + 1) * ATTN_BLOCK] for g in range(GQA_GROUP))
    o_ref[0] = jnp.concatenate(outs, axis=1).astype(BF16)


def _attention(q, k, v, kc, vc, sink):
    b, l, _ = q.shape
    c = kc.shape[1]
    full = lambda bi, j, s: (bi, 0, 0)
    grid_spec = pltpu.PrefetchScalarGridSpec(
        num_scalar_prefetch=1,
        grid=(b, l // ATTN_BLOCK),
        in_specs=[pl.BlockSpec((1, ATTN_BLOCK, Q_W), lambda bi, j, s: (bi, j, 0)),
                  pl.BlockSpec((1, l, KV_W), full),
                  pl.BlockSpec((1, l, KV_W), full),
                  pl.BlockSpec((1, c, KV_W), full),
                  pl.BlockSpec((1, c, KV_W), full)],
        out_specs=pl.BlockSpec((1, ATTN_BLOCK, Q_W), lambda bi, j, s: (bi, j, 0)),
    )
    return pl.pallas_call(
        _attn_kernel,
        grid_spec=grid_spec,
        out_shape=jax.ShapeDtypeStruct((b, l, Q_W), BF16),
        compiler_params=_cparams(("arbitrary", "arbitrary")),
        name="window_attention",
    )(sink, q, k, v, kc, vc)


def _dft_tables(l, wc):
    h = l // 2
    idx = np.arange(h, dtype=np.int64)
    ang = 2.0 * np.pi * ((idx[:, None] * idx[None, :]) % l).astype(np.float64) / l
    cm = np.cos(ang).astype(np.float32)
    sm = np.sin(ang).astype(np.float32)
    tw = 2.0 * np.pi * idx.astype(np.float64) / (2 * l)
    ones = np.ones((1, wc), np.float32)
    wr = np.cos(tw).astype(np.float32)[:, None] * ones
    wi = (-np.sin(tw)).astype(np.float32)[:, None] * ones
    sgn = np.where(idx % 2 == 0, 1.0, -1.0).astype(np.float32)[:, None] * ones
    wk = np.where(idx == 0, 1.0 / (2 * l), 2.0 / (2 * l)).astype(np.float32)[:, None] * ones
    return cm, sm, wr, wi, sgn, wk


def _half_spectrum(se, so, cm, sm, wr, wi, sgn):
    seb, sob = se.astype(BF16), so.astype(BF16)
    ce, ss_e = _bdot(cm, seb), _bdot(sm, seb)
    co, ss_o = _bdot(cm, sob), _bdot(sm, sob)
    vr = wr * co + wi * ss_o
    vi = wi * co - wr * ss_o
    zar, zai = ce + vr, vi - ss_e
    zbr, zbi = ce - vr, -ss_e - vi
    e_ny = jnp.sum(sgn * se, axis=0, keepdims=True)
    o_ny = jnp.sum(sgn * so, axis=0, keepdims=True)
    return zar, zai, zbr, zbi, e_ny, -o_ny


def _filter_kernel(ze_ref, zo_ref, w1_ref, b1_ref, w2_ref, b2_ref, w3f_ref, w3b_ref,
                   de_ref, do_ref, cm_ref, sm_ref, wr_ref, wi_ref, sgn_ref, wk_ref,
                   har_ref, hai_ref, hbr_ref, hbi_ref, hny_ref, ae_ref, ao_ref):
    cm, sm = cm_ref[...], sm_ref[...]
    wr, wi, sgn, wk = wr_ref[...], wi_ref[...], sgn_ref[...], wk_ref[...]
    h = cm.shape[0]

    @pl.when((pl.program_id(0) == 0) & (pl.program_id(1) == 0))
    def _():
        for z_ref, a_ref in ((ze_ref, ae_ref), (zo_ref, ao_ref)):
            a = jnp.sin(_fdot(z_ref[...], w1_ref[...]) + b1_ref[...])
            a_ref[...] = jnp.sin(_fdot(a, w2_ref[...]) + b2_ref[...])

    def taps(a_ref, w3_ref, d_ref):
        return _fdot(a_ref[...], w3_ref[0]) * d_ref[...]

    fe, fo = taps(ae_ref, w3f_ref, de_ref), taps(ao_ref, w3f_ref, do_ref)
    be, bo = taps(ae_ref, w3b_ref, de_ref), taps(ao_ref, w3b_ref, do_ref)
    row = lax.broadcasted_iota(jnp.int32, be.shape, 0)
    be = jnp.where(row == 0, 0.0, be)
    far, fai, fbr, fbi, fnr, fni = _half_spectrum(fe, fo, cm, sm, wr, wi, sgn)
    bar, bai, bbr, bbi, bnr, bni = _half_spectrum(be, bo, cm, sm, wr, wi, sgn)
    har_ref[0] = wk * (far + bar)
    hai_ref[0] = wk * (fai - bai)
    hbr_ref[0] = wk * (fbr + bbr)
    hbi_ref[0] = wk * (fbi - bbi)
    ny_scale = 2.0 / (4 * h)
    nr = ny_scale * (fnr + bnr)
    ni = ny_scale * (fni - bni)
    rows = lax.broadcasted_iota(jnp.int32, (8, nr.shape[1]), 0)
    hny_ref[0] = jnp.where(rows == 0, nr, jnp.where(rows == 1, ni, 0.0))


def _hyena_filters(zfe, zfo, w1, b1, w2, b2, w3, dec_e, dec_o, tabs, wc):
    cm, sm, wr, wi, sgn, wk = tabs
    h = cm.shape[0]
    emb = zfe.shape[1]
    hid = w2.shape[0]
    nblk = HY_WIDTH // wc
    w3r = w3.reshape(hid, HY_ORDER * 2, HY_WIDTH).transpose(1, 0, 2)
    const = lambda o, cb: (0, 0)
    chan = lambda o, cb: (0, cb)
    out_spec = pl.BlockSpec((1, h, wc), lambda o, cb: (o, 0, cb))
    out_shape = jax.ShapeDtypeStruct((HY_ORDER, h, HY_WIDTH), F32)
    return pl.pallas_call(
        _filter_kernel,
        grid=(HY_ORDER, nblk),
        in_specs=[pl.BlockSpec((h, emb), const), pl.BlockSpec((h, emb), const),
                  pl.BlockSpec((emb, hid), const), pl.BlockSpec((1, hid), const),
                  pl.BlockSpec((hid, hid), const), pl.BlockSpec((1, hid), const),
                  pl.BlockSpec((1, hid, wc), lambda o, cb: (2 * o, 0, cb)),
                  pl.BlockSpec((1, hid, wc), lambda o, cb: (2 * o + 1, 0, cb)),
                  pl.BlockSpec((h, wc), chan), pl.BlockSpec((h, wc), chan),
                  pl.BlockSpec((h, h), const), pl.BlockSpec((h, h), const),
                  pl.BlockSpec((h, wc), const), pl.BlockSpec((h, wc), const),
                  pl.BlockSpec((h, wc), const), pl.BlockSpec((h, wc), const)],
        out_specs=[out_spec, out_spec, out_spec, out_spec,
                   pl.BlockSpec((1, 8, wc), lambda o, cb: (o, 0, cb))],
        out_shape=[out_shape, out_shape, out_shape, out_shape,
                   jax.ShapeDtypeStruct((HY_ORDER, 8, HY_WIDTH), F32)],
        scratch_shapes=[pltpu.VMEM((h, hid), F32), pltpu.VMEM((h, hid), F32)],
        compiler_params=_cparams(("arbitrary", "arbitrary")),
        name="hyena_filter_spectrum",
    )(zfe, zfo, w1, b1.reshape(1, hid), w2, b2.reshape(1, hid), w3r, w3r,
      dec_e, dec_o, cm, sm, wr, wi, sgn, wk)


def _long_conv(ze, zo, har, hai, hbr, hbi, hny, cm, sm, wr, wi, sgn):
    zar, zai, zbr, zbi, znr, zni = _half_spectrum(ze, zo, cm, sm, wr, wi, sgn)
    yar, yai = zar * har - zai * hai, zar * hai + zai * har
    ybr, ybi = zbr * hbr - zbi * hbi, zbr * hbi + zbi * hbr
    dr, di = yar - ybr, yai - ybi
    sar, sai = (yar + ybr).astype(BF16), (yai + ybi).astype(BF16)
    sbr, sbi = (dr * wr + di * wi).astype(BF16), (di * wr - dr * wi).astype(BF16)
    hnr, hni = hny[0:1], hny[1:2]
    ynr = znr * hnr - zni * hni
    yni = znr * hni + zni * hnr
    ye = _bdot(cm, sar) - _bdot(sm, sai) + sgn * ynr
    yo = _bdot(cm, sbr) - _bdot(sm, sbi) - sgn * yni
    return ye, yo


def _hyena_kernel(*refs):
    nsub = (len(refs) - 18) // 4
    u_refs = [refs[s * nsub:(s + 1) * nsub] for s in range(HY_ORDER + 1)]
    rest = refs[(HY_ORDER + 1) * nsub:]
    (cw0_ref, cw1_ref, cw2_ref, cb0_ref, cb1_ref, cb2_ref, skip_ref,
     har_ref, hai_ref, hbr_ref, hbi_ref, hny_ref,
     cm_ref, sm_ref, wr_ref, wi_ref, sgn_ref, o_ref) = rest[:18]
    il_refs = rest[18:]
    h = cm_ref.shape[0]
    cm, sm = cm_ref[...], sm_ref[...]
    wr, wi, sgn = wr_ref[...], wi_ref[...], sgn_ref[...]
    row = lax.broadcasted_iota(jnp.int32, (h, wr.shape[1]), 0)

    def short_conv(u_slabs, cw_ref, cb_ref):
        ue = jnp.concatenate([r[0, pl.ds(0, h, stride=2), :] for r in u_slabs], axis=1)
        uo = jnp.concatenate([r[0, pl.ds(1, h, stride=2), :] for r in u_slabs], axis=1)
        w0, w1, w2 = cw_ref[0:1], cw_ref[1:2], cw_ref[2:3]
        uo_prev = jnp.where(row == 0, 0.0, pltpu.roll(uo, 1, 0))
        ue_next = jnp.where(row == h - 1, 0.0, pltpu.roll(ue, h - 1, 0))
        cb = cb_ref[...]
        se = cb + uo_prev * w0 + ue * w1 + uo * w2
        so = cb + ue * w0 + uo * w1 + ue_next * w2
        return se, so

    x_streams = [short_conv(u_refs[0], cw0_ref, cb0_ref), short_conv(u_refs[1], cw1_ref, cb1_ref)]
    ze, zo = short_conv(u_refs[2], cw2_ref, cb2_ref)
    for o in range(HY_ORDER):
        ce, co = _long_conv(ze, zo, har_ref[o], hai_ref[o], hbr_ref[o], hbi_ref[o], hny_ref[o],
                            cm, sm, wr, wi, sgn)
        sk = skip_ref[o:o + 1]
        xe, xo = x_streams[o]
        ze = xe * (ce + sk * ze)
        zo = xo * (co + sk * zo)
    for j, il in enumerate(il_refs):
        il[pl.ds(0, h, stride=2), :] = ze[:, j * 128:(j + 1) * 128]
        il[pl.ds(1, h, stride=2), :] = zo[:, j * 128:(j + 1) * 128]
        o_ref[0, :, j * 128:(j + 1) * 128] = il[...]


def _hyena(u, conv_w, conv_b, skip, spectra, tabs, wc):
    b, l, _ = u.shape
    har, hai, hbr, hbi, hny = spectra
    cm, sm, wr, wi, sgn, _ = tabs
    h = l // 2
    nblk = HY_WIDTH // wc
    nsub = wc // 128
    conv_b = conv_b.reshape(1, -1)

    def ublk(s, j):
        return pl.BlockSpec((1, l, 128), lambda cb, bi: (bi, 0, (cb + s * nblk) * nsub + j))

    def wblk(r, s):
        return pl.BlockSpec((r, wc), lambda cb, bi: (0, cb + s * nblk))

    const = lambda cb, bi: (0, 0)
    spec = pl.BlockSpec((HY_ORDER, h, wc), lambda cb, bi: (0, 0, cb))
    return pl.pallas_call(
        _hyena_kernel,
        grid=(nblk, b),
        in_specs=[ublk(s, j) for s in range(HY_ORDER + 1) for j in range(nsub)] + [
                  wblk(HY_SHORT_CONV, 0), wblk(HY_SHORT_CONV, 1), wblk(HY_SHORT_CONV, 2),
                  wblk(1, 0), wblk(1, 1), wblk(1, 2),
                  pl.BlockSpec((HY_ORDER, wc), lambda cb, bi: (0, cb)),
                  spec, spec, spec, spec,
                  pl.BlockSpec((HY_ORDER, 8, wc), lambda cb, bi: (0, 0, cb)),
                  pl.BlockSpec((h, h), const), pl.BlockSpec((h, h), const),
                  pl.BlockSpec((h, wc), const), pl.BlockSpec((h, wc), const),
                  pl.BlockSpec((h, wc), const)],
        out_specs=pl.BlockSpec((1, l, wc), lambda cb, bi: (bi, 0, cb)),
        out_shape=jax.ShapeDtypeStruct((b, l, HY_WIDTH), F32),
        scratch_shapes=[pltpu.VMEM((l, 128), F32) for _ in range(nsub)],
        compiler_params=_cparams(("arbitrary", "arbitrary")),
        name="hyena_long_conv",
    )(*([u] * ((HY_ORDER + 1) * nsub)), conv_w, conv_w, conv_w, conv_b, conv_b, conv_b, skip,
      har, hai, hbr, hbi, hny, cm, sm, wr, wi, sgn)


def _merge_kernel(alpha, att_ref, hy_ref, gate_ref, x_ref, g1_ref, sh2_ref, sc2_ref,
                  wba_ref, wbh_ref, wo_ref, l1g_ref, l1b_ref, rw_ref, rb_ref,
                  xmid_ref, h2_ref, logit_ref, res_ref):
    d = x_ref.shape[1]

    @pl.when(pl.program_id(0) == 0)
    def _():
        res_ref[...] = jnp.zeros_like(res_ref)

    rw = rw_ref[...]
    r_hi = rw.astype(BF16)
    r_lo = (rw - r_hi.astype(F32)).astype(BF16)
    dn = (((1,), (1,)), ((), ()))
    nt_dot = lambda p, q: lax.dot_general(p, q, dn, preferred_element_type=F32)
    tm = x_ref.shape[0]
    rc = tm // MERGE_CHUNKS
    for r in (slice(c * rc, (c + 1) * rc) for c in range(MERGE_CHUNKS)):
        xm = _ln(res_ref[r]) * l1g_ref[...] + l1b_ref[...]
        xmid_ref[r] = xm
        h2 = _ln(xm) * (1.0 + sc2_ref[0]) + sh2_ref[0]
        h_hi = h2.astype(BF16)
        h2_ref[r] = h_hi
        h_lo = (h2 - h_hi.astype(F32)).astype(BF16)
        logit_ref[:, r] = nt_dot(r_hi, h_hi) + nt_dot(r_hi, h_lo) + nt_dot(r_lo, h_hi) + rb_ref[...]

        a = _bdot(att_ref[r], wba_ref[...])
        hh = _bdot(hy_ref[r].astype(BF16), wbh_ref[...])
        ga = jax.nn.sigmoid(gate_ref[r, :d])
        gh = jax.nn.sigmoid(gate_ref[r, d:])
        y = _bdot((ga * a + gh * hh).astype(BF16), wo_ref[...])
        res_ref[r] = alpha * x_ref[r] + g1_ref[0] * y


def _merge(att, hy, gate, x, g1, sh2, sc2, wba, wbh, wo, l1g, l1b, rw_t, rb, alpha, tm=512):
    t, d = x.shape
    ne = rw_t.shape[0]
    nt = t // tm
    per_b = nt // g1.shape[0]
    t1 = lambda i: jnp.minimum(i, nt - 1)
    t2 = lambda i: jnp.maximum(i - 1, 0)
    row1 = lambda i: (t1(i), 0)
    row2 = lambda i: (t2(i), 0)
    const = lambda i: (0, 0)
    return pl.pallas_call(
        functools.partial(_merge_kernel, alpha),
        grid=(nt + 1,),
        in_specs=[pl.BlockSpec((tm, Q_W), row1),
                  pl.BlockSpec((tm, HY_WIDTH), row1),
                  pl.BlockSpec((tm, 2 * d), row1),
                  pl.BlockSpec((tm, d), row1),
                  pl.BlockSpec((1, 1, d), lambda i: (t1(i) // per_b, 0, 0)),
                  pl.BlockSpec((1, 1, d), lambda i: (t2(i) // per_b, 0, 0)),
                  pl.BlockSpec((1, 1, d), lambda i: (t2(i) // per_b, 0, 0)),
                  pl.BlockSpec((Q_W, d), const), pl.BlockSpec((HY_WIDTH, d), const),
                  pl.BlockSpec((d, d), const),
                  pl.BlockSpec((1, d), const), pl.BlockSpec((1, d), const),
                  pl.BlockSpec((ne, d), const), pl.BlockSpec((ne, 1), const)],
        out_specs=[pl.BlockSpec((tm, d), row2), pl.BlockSpec((tm, d), row2),
                   pl.BlockSpec((ne, tm), lambda i: (0, t2(i)))],
        out_shape=[jax.ShapeDtypeStruct((t, d), F32),
                   jax.ShapeDtypeStruct((t, d), BF16),
                   jax.ShapeDtypeStruct((ne, t), F32)],
        scratch_shapes=[pltpu.VMEM((tm, d), F32)],
        compiler_params=_cparams(("arbitrary",)),
        name="merge_ln_router",
    )(att, hy, gate, x, g1, sh2, sc2, wba, wbh, wo, l1g.reshape(1, d), l1b.reshape(1, d),
      rw_t, rb.reshape(ne, 1))


def _route_kernel(lg_ref, gate_ref, pos_ref, tcnt_ref, tbase_ref, cnt_ref, run_ref):
    @pl.when(pl.program_id(0) == 0)
    def _():
        run_ref[...] = jnp.zeros_like(run_ref)

    lg = lg_ref[...]
    ne, tr = lg.shape
    sub = lax.broadcasted_iota(jnp.int32, (ne, tr), 0)
    work = lg
    vals, hots = [], []
    for _ in range(TOP_K):
        m = jnp.max(work, axis=0, keepdims=True)
        idx = jnp.min(jnp.where(work == m, sub, ne), axis=0, keepdims=True)
        hot = sub == idx
        vals.append(m)
        hots.append(hot)
        work = jnp.where(hot, -jnp.inf, work)
    exps = [jnp.exp(v - vals[0]) for v in vals]
    den = exps[0] + exps[1] + exps[2] + exps[3]
    member = jnp.zeros((ne, tr), F32)
    for hot in hots:
        member = member + jnp.where(hot, 1.0, 0.0)
    r_i = lax.broadcasted_iota(jnp.int32, (tr, tr), 0)
    c_i = lax.broadcasted_iota(jnp.int32, (tr, tr), 1)
    earlier = jnp.where(r_i < c_i, 1.0, 0.0).astype(BF16)
    prefix = _bdot(member.astype(BF16), earlier)
    cnt = jnp.broadcast_to(jnp.sum(member, axis=1, keepdims=True), (ne, 128))
    e_r = lax.broadcasted_iota(jnp.int32, (ne, ne), 0)
    e_c = lax.broadcasted_iota(jnp.int32, (ne, ne), 1)
    lower = jnp.where(e_c < e_r, 1.0, 0.0).astype(BF16)
    off = _bdot(lower, cnt.astype(BF16))
    base = off[:, 0:1] + prefix
    sub_k = lax.broadcasted_iota(jnp.int32, (TOP_K, tr), 0)
    gate = jnp.zeros((TOP_K, tr), F32)
    pos = jnp.zeros((TOP_K, tr), F32)
    for k in range(TOP_K):
        pk = jnp.sum(jnp.where(hots[k], base, 0.0), axis=0, keepdims=True)
        gate = jnp.where(sub_k == k, exps[k] / den, gate)
        pos = jnp.where(sub_k == k, pk, pos)
    gate_ref[...] = gate
    pos_ref[...] = pos.astype(jnp.int32)
    tcnt_ref[0] = cnt.astype(jnp.int32)
    tbase_ref[0] = run_ref[...].astype(jnp.int32)
    run_ref[...] = run_ref[...] + cnt
    cnt_ref[...] = run_ref[...].astype(jnp.int32)


def _route(logits_t, tr):
    ne, t = logits_t.shape
    nt = t // tr
    tok = lambda i: (0, i)
    tile = lambda i: (i, 0, 0)
    return pl.pallas_call(
        _route_kernel,
        grid=(nt,),
        in_specs=[pl.BlockSpec((ne, tr), tok)],
        out_specs=[pl.BlockSpec((TOP_K, tr), tok), pl.BlockSpec((TOP_K, tr), tok),
                   pl.BlockSpec((1, ne, 128), tile), pl.BlockSpec((1, ne, 128), tile),
                   pl.BlockSpec((ne, 128), lambda i: (0, 0))],
        out_shape=[jax.ShapeDtypeStruct((TOP_K, t), F32),
                   jax.ShapeDtypeStruct((TOP_K, t), jnp.int32),
                   jax.ShapeDtypeStruct((nt, ne, 128), jnp.int32),
                   jax.ShapeDtypeStruct((nt, ne, 128), jnp.int32),
                   jax.ShapeDtypeStruct((ne, 128), jnp.int32)],
        scratch_shapes=[pltpu.VMEM((ne, 128), F32)],
        compiler_params=_cparams(("arbitrary",)),
        name="route_topk",
    )(logits_t)


def _row_tile(d):
    return (d // 256, 128)


def _pack_rows(v):
    n, d = v.shape
    bits = lax.bitcast_convert_type(v.astype(BF16).astype(F32), jnp.uint32)
    word = (bits[:, d // 2:] & jnp.uint32(0xFFFF0000)) | (bits[:, :d // 2] >> 16)
    return word.reshape((n,) + _row_tile(d))


def _unpack_rows(w):
    n = w.shape[0]
    word = w.reshape(n, w.shape[1] * w.shape[2])
    lo = lax.bitcast_convert_type(word << 16, F32)
    hi = lax.bitcast_convert_type(word & jnp.uint32(0xFFFF0000), F32)
    return jnp.concatenate([lo, hi], axis=1)


def _strip_copies(n_ref, dst_ref, make_copy, max_rows):
    ne = n_ref.shape[2]
    big = min(pl.next_power_of_2(max(1, 2 * TOP_K * max_rows // ne)), pl.next_power_of_2(max_rows))
    bits = [1 << s for s in range(big.bit_length() - 2, -1, -1)]

    def body(e, off):
        n = n_ref[0, 0, e]
        dst = dst_ref[0, 0, e]
        n_big = n // big

        @pl.when(n_big > 0)
        def _():
            def big_copy(i, carry):
                make_copy(off + i * big, dst + i * big, big).start()
                return carry

            lax.fori_loop(0, n_big, big_copy, 0)

        done = n_big * big
        for bit in bits:
            part = n & bit

            @pl.when(part != 0)
            def _():
                make_copy(off + done, dst + done, bit).start()

            done = done + part
        return off + n

    lax.fori_loop(0, ne, body, jnp.int32(0))


def _dispatch_kernel(n_ref, dst_ref, pos_ref, h_ref, xs_ref, srt_ref, sem):
    i = pl.program_id(0)
    slot = i % 2
    td = h_ref.shape[0]
    n = td * TOP_K
    pos = pos_ref[...]
    rows = lax.broadcasted_iota(jnp.int32, (n, td), 0)
    hit = rows == pos[0:1]
    for k in range(1, TOP_K):
        hit = hit | (rows == pos[k:k + 1])
    perm = jnp.where(hit, 1.0, 0.0).astype(BF16)
    srt_ref[slot] = _pack_rows(_bdot(perm, h_ref[...]))

    def all_rows(s):
        return pltpu.make_async_copy(srt_ref.at[s], xs_ref.at[pl.ds(0, n)], sem.at[s])

    _strip_copies(n_ref, dst_ref,
                  lambda src, dst, size: pltpu.make_async_copy(
                      srt_ref.at[slot, pl.ds(src, size)], xs_ref.at[pl.ds(dst, size)], sem.at[slot]),
                  td)

    @pl.when(i > 0)
    def _():
        all_rows(1 - slot).wait()

    @pl.when(i == pl.num_programs(0) - 1)
    def _():
        all_rows(slot).wait()


def _dispatch(h2, pos_t, tile_n, tile_dst, td):
    t, d = h2.shape
    a = t * TOP_K
    n = td * TOP_K
    nt = t // td
    ne = tile_n.shape[-1]
    smem = lambda: pl.BlockSpec((1, 1, ne), lambda i: (i, 0, 0), memory_space=pltpu.SMEM)
    return pl.pallas_call(
        _dispatch_kernel,
        grid=(nt,),
        in_specs=[smem(), smem(),
                  pl.BlockSpec((TOP_K, td), lambda i: (0, i)),
                  pl.BlockSpec((td, d), lambda i: (i, 0))],
        out_specs=pl.BlockSpec(memory_space=pl.ANY),
        out_shape=jax.ShapeDtypeStruct((a,) + _row_tile(d), jnp.uint32),
        scratch_shapes=[pltpu.VMEM((2, n) + _row_tile(d), jnp.uint32),
                        pltpu.SemaphoreType.DMA((2,))],
        compiler_params=_cparams(("arbitrary",)),
        name="moe_dispatch",
    )(tile_n.reshape(nt, 1, ne), tile_dst.reshape(nt, 1, ne), pos_t, h2)


def _expert_kernel(tile_ref, exp_ref, lo_ref, hi_ref, first_ref, nitem_ref,
                   xs_ref, w1_ref, b1_ref, w2_ref, b2_ref, ys_ref,
                   w1b_ref, w2b_ref, y_ref, acc_ref):
    w = pl.program_id(0)
    n_items = tile_ref.shape[0]
    cur = jnp.minimum(w, n_items - 1)
    prev = jnp.maximum(w - 1, 0)
    dff = w2_ref.shape[1]

    @pl.when(w == 0)
    def _():
        y_ref[...] = jnp.zeros_like(y_ref)
        acc_ref[...] = jnp.zeros_like(acc_ref)

    @pl.when((w == 0) | (exp_ref[cur] != exp_ref[prev]))
    def _():
        w1b_ref[...] = w1_ref[0].astype(BF16)
        w2b_ref[...] = w2_ref[0].astype(BF16)

    @pl.when(w <= nitem_ref[0])
    def _():
        y_prev = y_ref[...]
        row = lax.broadcasted_iota(jnp.int32, y_prev.shape, 0)
        y_prev = jnp.where((row >= lo_ref[prev]) & (row < hi_ref[prev]), y_prev, 0.0)
        acc = y_prev + jnp.where(first_ref[prev] == 1, 0.0, acc_ref[...])
        acc_ref[...] = acc
        ys_ref[...] = _pack_rows(acc)

        x = _unpack_rows(xs_ref[...]).astype(BF16)
        hb = _bdot(x, w1b_ref[...]) + b1_ref[0]
        glu = jnp.minimum(hb[:, :dff], SWIGLU_LIMIT)
        lin = jnp.clip(hb[:, dff:], -SWIGLU_LIMIT, SWIGLU_LIMIT)
        act = glu * jax.nn.sigmoid(SWIGLU_ALPHA * glu) * (lin + 1.0)
        y_ref[...] = _bdot(act.astype(BF16), w2b_ref[...]) + b2_ref[0]


def _experts(xs, items, w1, b1, w2, b2, tm):
    a = xs.shape[0]
    ne, d, dff2 = w1.shape
    dff = dff2 // 2
    n_items = a // tm + ne - 1
    tile_w, exp_w, lo_w, hi_w, first_w, nitem = items
    cur = lambda w: jnp.minimum(w, n_items - 1)
    rows_in = pl.BlockSpec((tm,) + _row_tile(d),
                           lambda w, tl, ex, lo, hi, fi, ni: (tl[cur(w)], 0, 0))
    rows_out = pl.BlockSpec((tm,) + _row_tile(d),
                            lambda w, tl, ex, lo, hi, fi, ni: (tl[jnp.maximum(w - 1, 0)], 0, 0))
    per_expert = lambda w, tl, ex, lo, hi, fi, ni: (ex[cur(w)], 0, 0)
    grid_spec = pltpu.PrefetchScalarGridSpec(
        num_scalar_prefetch=6,
        grid=(n_items + 1,),
        in_specs=[rows_in,
                  pl.BlockSpec((1, d, dff2), per_expert),
                  pl.BlockSpec((1, 1, dff2), per_expert),
                  pl.BlockSpec((1, dff, d), per_expert),
                  pl.BlockSpec((1, 1, d), per_expert)],
        out_specs=rows_out,
        scratch_shapes=[pltpu.VMEM((d, dff2), BF16), pltpu.VMEM((dff, d), BF16),
                        pltpu.VMEM((tm, d), F32), pltpu.VMEM((tm, d), F32)],
    )
    return pl.pallas_call(
        _expert_kernel,
        grid_spec=grid_spec,
        out_shape=jax.ShapeDtypeStruct(xs.shape, jnp.uint32),
        compiler_params=_cparams(("arbitrary",)),
        name="moe_experts",
    )(tile_w, exp_w, lo_w, hi_w, first_w, nitem,
      xs, w1, b1.reshape(ne, 1, dff2), w2, b2.reshape(ne, 1, d))


def _work_items(counts, a, tm):
    ne = counts.shape[0]
    n_items = a // tm + ne - 1
    ends = jnp.cumsum(counts)
    starts = ends - counts
    first_tile = starts // tm
    last_tile = (ends - 1) // tm
    nvis = jnp.where(counts > 0, last_tile - first_tile + 1, 0)
    vis_end = jnp.cumsum(nvis)
    vis_start = vis_end - nvis
    total = vis_end[-1]
    w = jnp.minimum(jnp.arange(n_items, dtype=jnp.int32), total - 1)
    e_w = jnp.sum((vis_end[None, :] <= w[:, None]).astype(jnp.int32), axis=1)
    hot = e_w[:, None] == jnp.arange(ne, dtype=jnp.int32)[None, :]
    pick = lambda v: jnp.sum(jnp.where(hot, v[None, :], 0), axis=1)
    tile_w = (pick(first_tile) + (w - pick(vis_start))).astype(jnp.int32)
    lo_w = jnp.maximum(pick(starts) - tile_w * tm, 0).astype(jnp.int32)
    hi_w = jnp.minimum(pick(ends) - tile_w * tm, tm).astype(jnp.int32)
    prev = jnp.concatenate([jnp.full((1,), -1, jnp.int32), tile_w[:-1]])
    first_w = (tile_w != prev).astype(jnp.int32)
    return starts, (tile_w, e_w, lo_w, hi_w, first_w, total.reshape(1).astype(jnp.int32))


def _combine_kernel(alpha, n_ref, dst_ref, nn_ref, ndst_ref, pos_ref, gate_ref,
                    ys_ref, xm_ref, g2_ref, lg_ref, lb_ref, o_ref, srt_ref, sem):
    i = pl.program_id(0)
    last = pl.num_programs(0) - 1
    slot = i % 2
    td = xm_ref.shape[0]
    n = td * TOP_K

    def fetch(cnt_ref, from_ref, s):
        _strip_copies(cnt_ref, from_ref,
                      lambda row, src, size: pltpu.make_async_copy(
                          ys_ref.at[pl.ds(src, size)], srt_ref.at[s, pl.ds(row, size)], sem.at[s]),
                      td)

    @pl.when(i == 0)
    def _():
        fetch(n_ref, dst_ref, slot)

    @pl.when(i < last)
    def _():
        fetch(nn_ref, ndst_ref, 1 - slot)

    pltpu.make_async_copy(ys_ref.at[pl.ds(0, n)], srt_ref.at[slot], sem.at[slot]).wait()

    pos, gate = pos_ref[...], gate_ref[...]
    lanes = lax.broadcasted_iota(jnp.int32, (td, n), 1)
    wsel = jnp.where(lanes == pos[:, 0:1], gate[:, 0:1], 0.0)
    for k in range(1, TOP_K):
        wsel = wsel + jnp.where(lanes == pos[:, k:k + 1], gate[:, k:k + 1], 0.0)
    w_hi = wsel.astype(BF16)
    w_lo = (wsel - w_hi.astype(F32)).astype(BF16)
    y = _unpack_rows(srt_ref[slot]).astype(BF16)
    f = _bdot(w_hi, y) + _bdot(w_lo, y)
    o_ref[...] = _ln(alpha * xm_ref[...] + g2_ref[0] * f) * lg_ref[...] + lb_ref[...]


def _combine(ys, pos, gate, tile_n, tile_dst, x_mid, g2, lg, lb, alpha, td):
    t, d = x_mid.shape
    n = td * TOP_K
    nt = t // td
    per_b = nt // g2.shape[0]
    ne = tile_n.shape[-1]
    cur = lambda: pl.BlockSpec((1, 1, ne), lambda i: (i, 0, 0), memory_space=pltpu.SMEM)
    nxt = lambda: pl.BlockSpec((1, 1, ne), lambda i: (jnp.minimum(i + 1, nt - 1), 0, 0),
                               memory_space=pltpu.SMEM)
    const = lambda i: (0, 0)
    tile_n = tile_n.reshape(nt, 1, ne)
    tile_dst = tile_dst.reshape(nt, 1, ne)
    return pl.pallas_call(
        functools.partial(_combine_kernel, alpha),
        grid=(nt,),
        in_specs=[cur(), cur(), nxt(), nxt(),
                  pl.BlockSpec((td, TOP_K), lambda i: (i, 0)),
                  pl.BlockSpec((td, TOP_K), lambda i: (i, 0)),
                  pl.BlockSpec(memory_space=pl.ANY),
                  pl.BlockSpec((td, d), lambda i: (i, 0)),
                  pl.BlockSpec((1, 1, d), lambda i: (i // per_b, 0, 0)),
                  pl.BlockSpec((1, d), const), pl.BlockSpec((1, d), const)],
        out_specs=pl.BlockSpec((td, d), lambda i: (i, 0)),
        out_shape=jax.ShapeDtypeStruct((t, d), F32),
        scratch_shapes=[pltpu.VMEM((2, n) + _row_tile(d), jnp.uint32),
                        pltpu.SemaphoreType.DMA((2,))],
        compiler_params=_cparams(("arbitrary",)),
        name="moe_combine_ln",
    )(tile_n, tile_dst, tile_n, tile_dst, pos, gate, ys, x_mid, g2,
      lg.reshape(1, d), lb.reshape(1, d))


def _rope_tables(l):
    f32 = np.float32
    rows = l // GRID_W
    row = np.repeat(np.arange(rows, dtype=f32), GRID_W)
    col = np.tile(np.arange(GRID_W, dtype=f32), rows)
    n_freq = HEAD_DIM // 4
    inv_freq = np.power(f32(ROPE_BASE), -np.arange(n_freq, dtype=f32) / f32(n_freq)).astype(f32)
    ang_r = (row[:, None] * inv_freq).astype(f32)
    ang_c = (col[:, None] * inv_freq).astype(f32)
    zero = np.zeros_like(ang_r)
    cos_r, sin_r, cos_c, sin_c = np.cos(ang_r), np.sin(ang_r), np.cos(ang_c), np.sin(ang_c)
    cos_h = np.concatenate([cos_r, cos_r, cos_c, cos_c], axis=1)
    sa_h = np.concatenate([-sin_r, zero, -sin_c, zero], axis=1)
    sb_h = np.concatenate([zero, sin_r, zero, sin_c], axis=1)
    rep = 128 // HEAD_DIM
    return tuple(np.tile(a, (1, rep)).astype(f32) for a in (cos_h, sa_h, sb_h))


def _filter_features(l):
    f32 = np.float32
    bands = (HY_EMB_DIM - 1) // 2
    t = np.linspace(0.0, 1.0, l, dtype=f32)[:, None]
    omega = (f32(2.0 * math.pi) * np.arange(l, dtype=f32)[:, None] / f32(l)).astype(f32)
    f = np.linspace(1e-4, bands - 1, bands, dtype=f32)[None, :]
    ang = (f * omega).astype(f32)
    z = np.concatenate([t, np.cos(ang), -np.sin(ang)], axis=-1).astype(f32)
    min_decay = math.log(HY_DECAY_TARGET) / HY_FAST_DECAY_PCT
    max_decay = math.log(HY_DECAY_TARGET) / HY_SLOW_DECAY_PCT
    deltas = np.abs(np.linspace(min_decay, max_decay, HY_WIDTH, dtype=f32))
    decay = np.exp(-t * deltas).astype(f32)
    return z, decay


def kernel(x, c, ctx, c_ctx, w_mod, b_mod, w_in, attn_sink, hy_conv_w, hy_conv_b, hy_filt_w1,
           hy_filt_b1, hy_filt_w2, hy_filt_b2, hy_filt_w3, hy_skip, w_branch_attn, w_branch_hyena,
           w_out, ln1_g, ln1_b, router_w, router_b, exp_w1, exp_b1, exp_w2, exp_b2, ln2_g, ln2_b):
    depth = w_mod.shape[0]
    assert depth == 1, "only the single-layer configuration is implemented"
    b, l, d = x.shape
    t = b * l
    alpha = (2 * depth) ** 0.25
    hy_wc = 256
    expert_tm = 512

    n_cond = b + 1
    pad = (-n_cond) % 8
    cond = jnp.concatenate([c, c_ctx[None], jnp.zeros((pad, d), F32)], axis=0)
    mod = _modulation(cond, w_mod[0], b_mod[0])
    mod_x = mod[:b].reshape(b, 1, 6, d)
    sh1, sc1, g1, sh2, sc2, g2 = (mod_x[:, :, i] for i in range(6))
    mod_c = mod[b:b + 1].reshape(1, 1, 6, d)
    csh1, csc1 = mod_c[:, :, 0], mod_c[:, :, 1]

    w_in_b = w_in[0].astype(BF16)
    cos_t, sa_t, sb_t = (jnp.asarray(a) for a in _rope_tables(l))
    q, k, v, u_hy, gate_x = _in_projection(x, sh1, sc1, w_in_b, cos_t, sa_t, sb_t)
    k_c, v_c = _ctx_kv(ctx, csh1, csc1, w_in_b[:, K_OFF:HY_OFF])
    att = _attention(q, k, v, k_c, v_c, attn_sink[0])

    tabs_np = _dft_tables(l, hy_wc)
    tabs = (jnp.asarray(tabs_np[0]).astype(BF16), jnp.asarray(tabs_np[1]).astype(BF16)) + tuple(
        jnp.asarray(a) for a in tabs_np[2:])
    zfeat, decay = _filter_features(l)
    emb_pad = (-HY_EMB_DIM) % 128
    zfeat = np.pad(zfeat, ((0, 0), (0, emb_pad)))
    fw1 = jnp.pad(hy_filt_w1[0], ((0, emb_pad), (0, 0)))
    spectra = _hyena_filters(jnp.asarray(zfeat[0::2]), jnp.asarray(zfeat[1::2]), fw1, hy_filt_b1[0],
                             hy_filt_w2[0], hy_filt_b2[0], hy_filt_w3[0], jnp.asarray(decay[0::2]),
                             jnp.asarray(decay[1::2]), tabs, hy_wc)
    hy = _hyena(u_hy, hy_conv_w[0], hy_conv_b[0], hy_skip[0], spectra, tabs, hy_wc)

    x_mid, h2, logits_t = _merge(att.reshape(t, Q_W), hy.reshape(t, HY_WIDTH),
                                 gate_x.reshape(t, 2 * d), x.reshape(t, d), g1, sh2, sc2,
                                 w_branch_attn[0].astype(BF16), w_branch_hyena[0].astype(BF16),
                                 w_out[0].astype(BF16), ln1_g[0], ln1_b[0],
                                 jnp.transpose(router_w[0]), router_b[0], alpha)

    moe_td = 256
    gate_t, pos_t, tile_cnt, tile_base, counts = _route(logits_t, moe_td)
    starts, items = _work_items(counts[:, 0], t * TOP_K, expert_tm)
    tile_n = tile_cnt[:, :, 0]
    tile_dst = starts[None, :] + tile_base[:, :, 0]
    xs = _dispatch(h2.reshape(t, d), pos_t, tile_n, tile_dst, moe_td)
    ys = _experts(xs, items, exp_w1[0], exp_b1[0], exp_w2[0], exp_b2[0], expert_tm)
    out = _combine(ys, jnp.transpose(pos_t), jnp.transpose(gate_t), tile_n, tile_dst,
                   x_mid.reshape(t, d), g2, ln2_g[0], ln2_b[0], alpha, moe_td)
    return out.reshape(b, l, d)
```

```python
import functools
import math

import numpy as np
import jax
import jax.numpy as jnp
from jax import lax
from jax.experimental import pallas as pl
from jax.experimental.pallas import tpu as pltpu

F32 = jnp.float32
BF16 = jnp.bfloat16
HIGHEST = lax.Precision.HIGHEST

GRID_W = 64
N_HEADS = 8
N_KV_HEADS = 2
GQA_GROUP = N_HEADS // N_KV_HEADS
HEAD_DIM = 64
WINDOW = 128
ATTN_BLOCK = 128
ROPE_BASE = 10000.0

HY_WIDTH = 512
HY_ORDER = 2
HY_SHORT_CONV = 3
HY_EMB_DIM = 33
HY_DECAY_TARGET = 1e-2
HY_FAST_DECAY_PCT = 0.3
HY_SLOW_DECAY_PCT = 1.5

N_EXPERTS = 32
TOP_K = 4
SWIGLU_LIMIT = 7.0
SWIGLU_ALPHA = 1.702
LN_EPS = 1e-5

Q_W = N_HEADS * HEAD_DIM
KV_W = N_KV_HEADS * HEAD_DIM
K_OFF = Q_W
V_OFF = K_OFF + KV_W
HY_OFF = V_OFF + KV_W
GATE_OFF = HY_OFF + (HY_ORDER + 1) * HY_WIDTH

VMEM_LIMIT = 56 * 1024 * 1024
NEG_BIG = -1e30
MERGE_CHUNKS = 2


def _cparams(sem):
    return pltpu.CompilerParams(dimension_semantics=sem, vmem_limit_bytes=VMEM_LIMIT)


def _ln(x):
    mu = jnp.mean(x, axis=-1, keepdims=True)
    xc = x - mu
    var = jnp.mean(xc * xc, axis=-1, keepdims=True)
    return xc * lax.rsqrt(var + LN_EPS)


def _bdot(a, b):
    return jnp.dot(a, b, preferred_element_type=F32)


def _fdot(a, b):
    return jnp.dot(a, b, preferred_element_type=F32, precision=HIGHEST)


def _mod_kernel(c_ref, w_ref, b_ref, o_ref):
    c = c_ref[...]
    s = c * jax.nn.sigmoid(c)
    o_ref[...] = _fdot(s, w_ref[...]) + b_ref[...]


def _modulation(cond, w, b, tn=512):
    r, d = cond.shape
    n = w.shape[1]
    return pl.pallas_call(
        _mod_kernel,
        grid=(n // tn,),
        in_specs=[pl.BlockSpec((r, d), lambda j: (0, 0)),
                  pl.BlockSpec((d, tn), lambda j: (0, j)),
                  pl.BlockSpec((1, tn), lambda j: (0, j))],
        out_specs=pl.BlockSpec((r, tn), lambda j: (0, j)),
        out_shape=jax.ShapeDtypeStruct((r, n), F32),
        compiler_params=_cparams(("arbitrary",)),
        name="modulation",
    )(cond, w, b.reshape(1, n))


def _rope(t, cos, sa, sb):
    n = t.shape[-1]
    return t * cos + pltpu.roll(t, n - 16, 1) * sa + pltpu.roll(t, 16, 1) * sb


def _inproj_kernel(x_ref, sh_ref, sc_ref, w_ref, cos_ref, sa_ref, sb_ref,
                   q_ref, k_ref, v_ref, u_ref, g_ref):
    h = _ln(x_ref[0]) * (1.0 + sc_ref[0]) + sh_ref[0]
    hb = h.astype(BF16)
    cos, sa, sb = cos_ref[...], sa_ref[...], sb_ref[...]
    scale = HEAD_DIM ** -0.5
    for j in range(Q_W // 128):
        t = _bdot(hb, w_ref[:, j * 128:(j + 1) * 128])
        q_ref[0, :, j * 128:(j + 1) * 128] = (_rope(t, cos, sa, sb) * scale).astype(BF16)
    t = _bdot(hb, w_ref[:, K_OFF:V_OFF])
    k_ref[0] = _rope(t, cos, sa, sb).astype(BF16)
    v_ref[0] = _bdot(hb, w_ref[:, V_OFF:HY_OFF]).astype(BF16)
    for j in range((GATE_OFF - HY_OFF) // 512):
        u_ref[0, :, j * 512:(j + 1) * 512] = _bdot(hb, w_ref[:, HY_OFF + j * 512:HY_OFF + (j + 1) * 512])
    n_gate = w_ref.shape[1] - GATE_OFF
    for j in range(n_gate // 512):
        g_ref[0, :, j * 512:(j + 1) * 512] = jax.nn.sigmoid(_bdot(
            hb, w_ref[:, GATE_OFF + j * 512:GATE_OFF + (j + 1) * 512])).astype(BF16)


def _in_projection(x, sh, sc, w_in_b, cos_t, sa_t, sb_t, tm=256):
    b, l, d = x.shape
    in_w = w_in_b.shape[1]
    hy_w = GATE_OFF - HY_OFF
    g_w = in_w - GATE_OFF
    row = lambda bi, i: (bi, i, 0)
    vec = lambda bi, i: (bi, 0, 0)
    tab = lambda bi, i: (i, 0)
    return pl.pallas_call(
        _inproj_kernel,
        grid=(b, l // tm),
        in_specs=[pl.BlockSpec((1, tm, d), row),
                  pl.BlockSpec((1, 1, d), vec),
                  pl.BlockSpec((1, 1, d), vec),
                  pl.BlockSpec((d, in_w), lambda bi, i: (0, 0)),
                  pl.BlockSpec((tm, 128), tab),
                  pl.BlockSpec((tm, 128), tab),
                  pl.BlockSpec((tm, 128), tab)],
        out_specs=[pl.BlockSpec((1, tm, Q_W), row),
                   pl.BlockSpec((1, tm, KV_W), row),
                   pl.BlockSpec((1, tm, KV_W), row),
                   pl.BlockSpec((1, tm, hy_w), row),
                   pl.BlockSpec((1, tm, g_w), row)],
        out_shape=[jax.ShapeDtypeStruct((b, l, Q_W), BF16),
                   jax.ShapeDtypeStruct((b, l, KV_W), BF16),
                   jax.ShapeDtypeStruct((b, l, KV_W), BF16),
                   jax.ShapeDtypeStruct((b, l, hy_w), F32),
                   jax.ShapeDtypeStruct((b, l, g_w), BF16)],
        compiler_params=_cparams(("arbitrary", "arbitrary")),
        name="in_projection",
    )(x, sh, sc, w_in_b, cos_t, sa_t, sb_t)


def _ctx_kv_kernel(x_ref, sh_ref, sc_ref, w_ref, k_ref, v_ref):
    h = _ln(x_ref[0]) * (1.0 + sc_ref[0]) + sh_ref[0]
    kv = _bdot(h.astype(BF16), w_ref[...])
    k_ref[0] = kv[:, :KV_W].astype(BF16)
    v_ref[0] = kv[:, KV_W:].astype(BF16)


def _ctx_kv(ctx, sh, sc, w_kv_b):
    b, c, d = ctx.shape
    row = lambda bi: (bi, 0, 0)
    return pl.pallas_call(
        _ctx_kv_kernel,
        grid=(b,),
        in_specs=[pl.BlockSpec((1, c, d), row),
                  pl.BlockSpec((1, 1, d), lambda bi: (0, 0, 0)),
                  pl.BlockSpec((1, 1, d), lambda bi: (0, 0, 0)),
                  pl.BlockSpec((d, 2 * KV_W), lambda bi: (0, 0))],
        out_specs=[pl.BlockSpec((1, c, KV_W), row), pl.BlockSpec((1, c, KV_W), row)],
        out_shape=[jax.ShapeDtypeStruct((b, c, KV_W), BF16),
                   jax.ShapeDtypeStruct((b, c, KV_W), BF16)],
        compiler_params=_cparams(("arbitrary",)),
        name="ctx_kv",
    )(ctx, sh, sc, w_kv_b)


def _attn_kernel(sink_ref, q_ref, k_ref, v_ref, kc_ref, vc_ref, o_ref):
    for qb in range(q_ref.shape[1] // ATTN_BLOCK):
        rows = slice(qb * ATTN_BLOCK, (qb + 1) * ATTN_BLOCK)
        j = pl.program_id(1) * (q_ref.shape[1] // ATTN_BLOCK) + qb
        o_ref[0, rows] = _attn_block(j, sink_ref, q_ref.at[0, rows], k_ref, v_ref, kc_ref, vc_ref)


def _attn_block(j, sink_ref, q_ref, k_ref, v_ref, kc_ref, vc_ref):
    l = k_ref.shape[1]
    span = ATTN_BLOCK + 2 * WINDOW
    q0 = j * ATTN_BLOCK
    start = pl.multiple_of(jnp.clip(q0 - WINDOW, 0, l - span), ATTN_BLOCK)
    rows = GQA_GROUP * ATTN_BLOCK
    r_i = lax.broadcasted_iota(jnp.int32, (rows, span), 0)
    qpos = q0 + r_i % ATTN_BLOCK
    kpos = start + lax.broadcasted_iota(jnp.int32, (rows, span), 1)
    band = jnp.abs(kpos - qpos) <= WINDOW
    head_of_row = lax.broadcasted_iota(jnp.int32, (rows, 1), 0) // ATTN_BLOCK
    dn = (((1,), (1,)), ((), ()))
    outs = []
    for kv in range(N_KV_HEADS):
        ks = slice(kv * HEAD_DIM, (kv + 1) * HEAD_DIM)
        kl = k_ref[0, pl.ds(start, span), ks]
        vl = v_ref[0, pl.ds(start, span), ks]
        kc = kc_ref[0, :, ks]
        vc = vc_ref[0, :, ks]
        heads = [kv * GQA_GROUP + g for g in range(GQA_GROUP)]
        qg = jnp.concatenate([q_ref[:, h * HEAD_DIM:(h + 1) * HEAD_DIM] for h in heads], axis=0)
        sink = jnp.zeros((rows, 1), F32)
        for g, h in enumerate(heads):
            sink = jnp.where(head_of_row == g, sink_ref[h], sink)
        s_loc = lax.dot_general(qg, kl, dn, preferred_element_type=F32)
        s_loc = jnp.where(band, s_loc, NEG_BIG)
        s_ctx = lax.dot_general(qg, kc, dn, preferred_element_type=F32)
        blocks = [s[:, c:c + 128] for s in (s_loc, s_ctx) for c in range(0, s.shape[1], 128)]
        folded = functools.reduce(jnp.maximum, blocks)
        m = jnp.maximum(jnp.max(folded, axis=1, keepdims=True), sink)
        p_loc = jnp.exp(s_loc - m).astype(BF16)
        p_ctx = jnp.exp(s_ctx - m).astype(BF16)
        ones_l = jnp.ones((span, HEAD_DIM), BF16)
        ones_c = jnp.ones((kc.shape[0], HEAD_DIM), BF16)
        pv = (_bdot(p_loc, jnp.concatenate([vl, ones_l], axis=1))
              + _bdot(p_ctx, jnp.concatenate([vc, ones_c], axis=1)))
        den = pv[:, HEAD_DIM:HEAD_DIM + 1] + jnp.exp(sink - m)
        o = pv[:, :HEAD_DIM] / den
        outs.extend(o[g * ATTN_BLOCK:(g + 1) * ATTN_BLOCK] for g in range(GQA_GROUP))
    return jnp.concatenate(outs, axis=1).astype(BF16)


def _attention(q, k, v, kc, vc, sink, q_blocks=4):
    b, l, _ = q.shape
    c = kc.shape[1]
    tq = q_blocks * ATTN_BLOCK
    full = lambda bi, j, s: (bi, 0, 0)
    grid_spec = pltpu.PrefetchScalarGridSpec(
        num_scalar_prefetch=1,
        grid=(b, l // tq),
        in_specs=[pl.BlockSpec((1, tq, Q_W), lambda bi, j, s: (bi, j, 0)),
                  pl.BlockSpec((1, l, KV_W), full),
                  pl.BlockSpec((1, l, KV_W), full),
                  pl.BlockSpec((1, c, KV_W), full),
                  pl.BlockSpec((1, c, KV_W), full)],
        out_specs=pl.BlockSpec((1, tq, Q_W), lambda bi, j, s: (bi, j, 0)),
    )
    return pl.pallas_call(
        _attn_kernel,
        grid_spec=grid_spec,
        out_shape=jax.ShapeDtypeStruct((b, l, Q_W), BF16),
        compiler_params=_cparams(("arbitrary", "arbitrary")),
        name="window_attention",
    )(sink, q, k, v, kc, vc)


def _dft_tables(l, wc):
    h = l // 2
    idx = np.arange(h, dtype=np.int64)
    ang = 2.0 * np.pi * ((idx[:, None] * idx[None, :]) % l).astype(np.float64) / l
    cm = np.cos(ang).astype(np.float32)
    sm = np.sin(ang).astype(np.float32)
    tw = 2.0 * np.pi * idx.astype(np.float64) / (2 * l)
    ones = np.ones((1, wc), np.float32)
    wr = np.cos(tw).astype(np.float32)[:, None] * ones
    wi = (-np.sin(tw)).astype(np.float32)[:, None] * ones
    sgn = np.where(idx % 2 == 0, 1.0, -1.0).astype(np.float32)[:, None] * ones
    wk = np.where(idx == 0, 1.0 / (2 * l), 2.0 / (2 * l)).astype(np.float32)[:, None] * ones
    return cm, sm, wr, wi, sgn, wk


def _half_spectrum(se, so, cm, sm, wr, wi, sgn):
    seb, sob = se.astype(BF16), so.astype(BF16)
    ce, ss_e = _bdot(cm, seb), _bdot(sm, seb)
    co, ss_o = _bdot(cm, sob), _bdot(sm, sob)
    vr = wr * co + wi * ss_o
    vi = wi * co - wr * ss_o
    zar, zai = ce + vr, vi - ss_e
    zbr, zbi = ce - vr, -ss_e - vi
    e_ny = jnp.sum(sgn * se, axis=0, keepdims=True)
    o_ny = jnp.sum(sgn * so, axis=0, keepdims=True)
    return zar, zai, zbr, zbi, e_ny, -o_ny


def _filter_kernel(ze_ref, zo_ref, w1_ref, b1_ref, w2_ref, b2_ref, w3f_ref, w3b_ref,
                   de_ref, do_ref, cm_ref, sm_ref, wr_ref, wi_ref, sgn_ref, wk_ref,
                   har_ref, hai_ref, hbr_ref, hbi_ref, hny_ref, ae_ref, ao_ref):
    cm, sm = cm_ref[...], sm_ref[...]
    wr, wi, sgn, wk = wr_ref[...], wi_ref[...], sgn_ref[...], wk_ref[...]
    h = cm.shape[0]

    @pl.when((pl.program_id(0) == 0) & (pl.program_id(1) == 0))
    def _():
        for z_ref, a_ref in ((ze_ref, ae_ref), (zo_ref, ao_ref)):
            a = jnp.sin(_fdot(z_ref[...], w1_ref[...]) + b1_ref[...])
            a_ref[...] = jnp.sin(_fdot(a, w2_ref[...]) + b2_ref[...])

    def taps(a_ref, w3_ref, d_ref):
        return _fdot(a_ref[...], w3_ref[0]) * d_ref[...]

    fe, fo = taps(ae_ref, w3f_ref, de_ref), taps(ao_ref, w3f_ref, do_ref)
    be, bo = taps(ae_ref, w3b_ref, de_ref), taps(ao_ref, w3b_ref, do_ref)
    row = lax.broadcasted_iota(jnp.int32, be.shape, 0)
    be = jnp.where(row == 0, 0.0, be)
    far, fai, fbr, fbi, fnr, fni = _half_spectrum(fe, fo, cm, sm, wr, wi, sgn)
    bar, bai, bbr, bbi, bnr, bni = _half_spectrum(be, bo, cm, sm, wr, wi, sgn)
    har_ref[0] = wk * (far + bar)
    hai_ref[0] = wk * (fai - bai)
    hbr_ref[0] = wk * (fbr + bbr)
    hbi_ref[0] = wk * (fbi - bbi)
    ny_scale = 2.0 / (4 * h)
    nr = ny_scale * (fnr + bnr)
    ni = ny_scale * (fni - bni)
    rows = lax.broadcasted_iota(jnp.int32, (8, nr.shape[1]), 0)
    hny_ref[0] = jnp.where(rows == 0, nr, jnp.where(rows == 1, ni, 0.0))


def _hyena_filters(zfe, zfo, w1, b1, w2, b2, w3, dec_e, dec_o, tabs, wc):
    cm, sm, wr, wi, sgn, wk = tabs
    h = cm.shape[0]
    emb = zfe.shape[1]
    hid = w2.shape[0]
    nblk = HY_WIDTH // wc
    w3r = w3.reshape(hid, HY_ORDER * 2, HY_WIDTH).transpose(1, 0, 2)
    const = lambda o, cb: (0, 0)
    chan = lambda o, cb: (0, cb)
    out_spec = pl.BlockSpec((1, h, wc), lambda o, cb: (o, 0, cb))
    out_shape = jax.ShapeDtypeStruct((HY_ORDER, h, HY_WIDTH), F32)
    return pl.pallas_call(
        _filter_kernel,
        grid=(HY_ORDER, nblk),
        in_specs=[pl.BlockSpec((h, emb), const), pl.BlockSpec((h, emb), const),
                  pl.BlockSpec((emb, hid), const), pl.BlockSpec((1, hid), const),
                  pl.BlockSpec((hid, hid), const), pl.BlockSpec((1, hid), const),
                  pl.BlockSpec((1, hid, wc), lambda o, cb: (2 * o, 0, cb)),
                  pl.BlockSpec((1, hid, wc), lambda o, cb: (2 * o + 1, 0, cb)),
                  pl.BlockSpec((h, wc), chan), pl.BlockSpec((h, wc), chan),
                  pl.BlockSpec((h, h), const), pl.BlockSpec((h, h), const),
                  pl.BlockSpec((h, wc), const), pl.BlockSpec((h, wc), const),
                  pl.BlockSpec((h, wc), const), pl.BlockSpec((h, wc), const)],
        out_specs=[out_spec, out_spec, out_spec, out_spec,
                   pl.BlockSpec((1, 8, wc), lambda o, cb: (o, 0, cb))],
        out_shape=[out_shape, out_shape, out_shape, out_shape,
                   jax.ShapeDtypeStruct((HY_ORDER, 8, HY_WIDTH), F32)],
        scratch_shapes=[pltpu.VMEM((h, hid), F32), pltpu.VMEM((h, hid), F32)],
        compiler_params=_cparams(("arbitrary", "arbitrary")),
        name="hyena_filter_spectrum",
    )(zfe, zfo, w1, b1.reshape(1, hid), w2, b2.reshape(1, hid), w3r, w3r,
      dec_e, dec_o, cm, sm, wr, wi, sgn, wk)


def _long_conv(ze, zo, har, hai, hbr, hbi, hny, cm, sm, wr, wi, sgn):
    zar, zai, zbr, zbi, znr, zni = _half_spectrum(ze, zo, cm, sm, wr, wi, sgn)
    yar, yai = zar * har - zai * hai, zar * hai + zai * har
    ybr, ybi = zbr * hbr - zbi * hbi, zbr * hbi + zbi * hbr
    dr, di = yar - ybr, yai - ybi
    sar, sai = (yar + ybr).astype(BF16), (yai + ybi).astype(BF16)
    sbr, sbi = (dr * wr + di * wi).astype(BF16), (di * wr - dr * wi).astype(BF16)
    hnr, hni = hny[0:1], hny[1:2]
    ynr = znr * hnr - zni * hni
    yni = znr * hni + zni * hnr
    ye = _bdot(cm, sar) - _bdot(sm, sai) + sgn * ynr
    yo = _bdot(cm, sbr) - _bdot(sm, sbi) - sgn * yni
    return ye, yo


def _hyena_kernel(*refs):
    nsub = (len(refs) - 18) // 4
    u_refs = [refs[s * nsub:(s + 1) * nsub] for s in range(HY_ORDER + 1)]
    rest = refs[(HY_ORDER + 1) * nsub:]
    (cw0_ref, cw1_ref, cw2_ref, cb0_ref, cb1_ref, cb2_ref, skip_ref,
     har_ref, hai_ref, hbr_ref, hbi_ref, hny_ref,
     cm_ref, sm_ref, wr_ref, wi_ref, sgn_ref, o_ref) = rest[:18]
    il_refs = rest[18:]
    h = cm_ref.shape[0]
    cm, sm = cm_ref[...], sm_ref[...]
    wr, wi, sgn = wr_ref[...], wi_ref[...], sgn_ref[...]
    row = lax.broadcasted_iota(jnp.int32, (h, wr.shape[1]), 0)

    def short_conv(u_slabs, cw_ref, cb_ref):
        ue = jnp.concatenate([r[0, pl.ds(0, h, stride=2), :] for r in u_slabs], axis=1)
        uo = jnp.concatenate([r[0, pl.ds(1, h, stride=2), :] for r in u_slabs], axis=1)
        w0, w1, w2 = cw_ref[0:1], cw_ref[1:2], cw_ref[2:3]
        uo_prev = jnp.where(row == 0, 0.0, pltpu.roll(uo, 1, 0))
        ue_next = jnp.where(row == h - 1, 0.0, pltpu.roll(ue, h - 1, 0))
        cb = cb_ref[...]
        se = cb + uo_prev * w0 + ue * w1 + uo * w2
        so = cb + ue * w0 + uo * w1 + ue_next * w2
        return se, so

    x_streams = [short_conv(u_refs[0], cw0_ref, cb0_ref), short_conv(u_refs[1], cw1_ref, cb1_ref)]
    ze, zo = short_conv(u_refs[2], cw2_ref, cb2_ref)
    for o in range(HY_ORDER):
        ce, co = _long_conv(ze, zo, har_ref[o], hai_ref[o], hbr_ref[o], hbi_ref[o], hny_ref[o],
                            cm, sm, wr, wi, sgn)
        sk = skip_ref[o:o + 1]
        xe, xo = x_streams[o]
        ze = xe * (ce + sk * ze)
        zo = xo * (co + sk * zo)
    for j, il in enumerate(il_refs):
        il[pl.ds(0, h, stride=2), :] = ze[:, j * 128:(j + 1) * 128]
        il[pl.ds(1, h, stride=2), :] = zo[:, j * 128:(j + 1) * 128]
        o_ref[0, :, j * 128:(j + 1) * 128] = il[...]


def _hyena(u, conv_w, conv_b, skip, spectra, tabs, wc):
    b, l, _ = u.shape
    har, hai, hbr, hbi, hny = spectra
    cm, sm, wr, wi, sgn, _ = tabs
    h = l // 2
    nblk = HY_WIDTH // wc
    nsub = wc // 128
    conv_b = conv_b.reshape(1, -1)

    def ublk(s, j):
        return pl.BlockSpec((1, l, 128), lambda cb, bi: (bi, 0, (cb + s * nblk) * nsub + j))

    def wblk(r, s):
        return pl.BlockSpec((r, wc), lambda cb, bi: (0, cb + s * nblk))

    const = lambda cb, bi: (0, 0)
    spec = pl.BlockSpec((HY_ORDER, h, wc), lambda cb, bi: (0, 0, cb))
    return pl.pallas_call(
        _hyena_kernel,
        grid=(nblk, b),
        in_specs=[ublk(s, j) for s in range(HY_ORDER + 1) for j in range(nsub)] + [
                  wblk(HY_SHORT_CONV, 0), wblk(HY_SHORT_CONV, 1), wblk(HY_SHORT_CONV, 2),
                  wblk(1, 0), wblk(1, 1), wblk(1, 2),
                  pl.BlockSpec((HY_ORDER, wc), lambda cb, bi: (0, cb)),
                  spec, spec, spec, spec,
                  pl.BlockSpec((HY_ORDER, 8, wc), lambda cb, bi: (0, 0, cb)),
                  pl.BlockSpec((h, h), const), pl.BlockSpec((h, h), const),
                  pl.BlockSpec((h, wc), const), pl.BlockSpec((h, wc), const),
                  pl.BlockSpec((h, wc), const)],
        out_specs=pl.BlockSpec((1, l, wc), lambda cb, bi: (bi, 0, cb)),
        out_shape=jax.ShapeDtypeStruct((b, l, HY_WIDTH), F32),
        scratch_shapes=[pltpu.VMEM((l, 128), F32) for _ in range(nsub)],
        compiler_params=_cparams(("arbitrary", "arbitrary")),
        name="hyena_long_conv",
    )(*([u] * ((HY_ORDER + 1) * nsub)), conv_w, conv_w, conv_w, conv_b, conv_b, conv_b, skip,
      har, hai, hbr, hbi, hny, cm, sm, wr, wi, sgn)


def _merge_kernel(alpha, att_ref, hy_ref, gate_ref, x_ref, g1_ref, sh2_ref, sc2_ref,
                  wba_ref, wbh_ref, wo_ref, l1g_ref, l1b_ref, rw_ref, rb_ref,
                  xmid_ref, h2_ref, logit_ref, res_ref):
    d = x_ref.shape[1]

    @pl.when(pl.program_id(0) == 0)
    def _():
        res_ref[...] = jnp.zeros_like(res_ref)

    rw = rw_ref[...]
    r_hi = rw.astype(BF16)
    r_lo = (rw - r_hi.astype(F32)).astype(BF16)
    dn = (((1,), (1,)), ((), ()))
    nt_dot = lambda p, q: lax.dot_general(p, q, dn, preferred_element_type=F32)
    tm = x_ref.shape[0]
    rc = tm // MERGE_CHUNKS
    for r in (slice(c * rc, (c + 1) * rc) for c in range(MERGE_CHUNKS)):
        xm = _ln(res_ref[r]) * l1g_ref[...] + l1b_ref[...]
        xmid_ref[r] = xm
        h2 = _ln(xm) * (1.0 + sc2_ref[0]) + sh2_ref[0]
        h_hi = h2.astype(BF16)
        h2_ref[r] = h_hi
        h_lo = (h2 - h_hi.astype(F32)).astype(BF16)
        logit_ref[:, r] = nt_dot(r_hi, h_hi) + nt_dot(r_hi, h_lo) + nt_dot(r_lo, h_hi) + rb_ref[...]

        a = _bdot(att_ref[r], wba_ref[...])
        hh = _bdot(hy_ref[r].astype(BF16), wbh_ref[...])
        ga = gate_ref[r, :d].astype(F32)
        gh = gate_ref[r, d:].astype(F32)
        y = _bdot((ga * a + gh * hh).astype(BF16), wo_ref[...])
        res_ref[r] = alpha * x_ref[r] + g1_ref[0] * y


def _merge(att, hy, gate, x, g1, sh2, sc2, wba, wbh, wo, l1g, l1b, rw_t, rb, alpha, tm=512):
    t, d = x.shape
    ne = rw_t.shape[0]
    nt = t // tm
    per_b = nt // g1.shape[0]
    t1 = lambda i: jnp.minimum(i, nt - 1)
    t2 = lambda i: jnp.maximum(i - 1, 0)
    row1 = lambda i: (t1(i), 0)
    row2 = lambda i: (t2(i), 0)
    const = lambda i: (0, 0)
    return pl.pallas_call(
        functools.partial(_merge_kernel, alpha),
        grid=(nt + 1,),
        in_specs=[pl.BlockSpec((tm, Q_W), row1),
                  pl.BlockSpec((tm, HY_WIDTH), row1),
                  pl.BlockSpec((tm, 2 * d), row1),
                  pl.BlockSpec((tm, d), row1),
                  pl.BlockSpec((1, 1, d), lambda i: (t1(i) // per_b, 0, 0)),
                  pl.BlockSpec((1, 1, d), lambda i: (t2(i) // per_b, 0, 0)),
                  pl.BlockSpec((1, 1, d), lambda i: (t2(i) // per_b, 0, 0)),
                  pl.BlockSpec((Q_W, d), const), pl.BlockSpec((HY_WIDTH, d), const),
                  pl.BlockSpec((d, d), const),
                  pl.BlockSpec((1, d), const), pl.BlockSpec((1, d), const),
                  pl.BlockSpec((ne, d), const), pl.BlockSpec((ne, 1), const)],
        out_specs=[pl.BlockSpec((tm, d), row2), pl.BlockSpec((tm, d), row2),
                   pl.BlockSpec((ne, tm), lambda i: (0, t2(i)))],
        out_shape=[jax.ShapeDtypeStruct((t, d), F32),
                   jax.ShapeDtypeStruct((t, d), BF16),
                   jax.ShapeDtypeStruct((ne, t), F32)],
        scratch_shapes=[pltpu.VMEM((tm, d), F32)],
        compiler_params=_cparams(("arbitrary",)),
        name="merge_ln_router",
    )(att, hy, gate, x, g1, sh2, sc2, wba, wbh, wo, l1g.reshape(1, d), l1b.reshape(1, d),
      rw_t, rb.reshape(ne, 1))


def _route_kernel(lg_ref, gate_ref, pos_ref, tcnt_ref, tbase_ref, cnt_ref, run_ref):
    @pl.when(pl.program_id(0) == 0)
    def _():
        run_ref[...] = jnp.zeros_like(run_ref)

    lg = lg_ref[...]
    ne, tr = lg.shape
    sub = lax.broadcasted_iota(jnp.int32, (ne, tr), 0)
    work = lg
    vals, hots = [], []
    for _ in range(TOP_K):
        m = jnp.max(work, axis=0, keepdims=True)
        idx = jnp.min(jnp.where(work == m, sub, ne), axis=0, keepdims=True)
        hot = sub == idx
        vals.append(m)
        hots.append(hot)
        work = jnp.where(hot, -jnp.inf, work)
    exps = [jnp.exp(v - vals[0]) for v in vals]
    den = exps[0] + exps[1] + exps[2] + exps[3]
    member = jnp.zeros((ne, tr), F32)
    for hot in hots:
        member = member + jnp.where(hot, 1.0, 0.0)
    r_i = lax.broadcasted_iota(jnp.int32, (tr, tr), 0)
    c_i = lax.broadcasted_iota(jnp.int32, (tr, tr), 1)
    earlier = jnp.where(r_i < c_i, 1.0, 0.0).astype(BF16)
    prefix = _bdot(member.astype(BF16), earlier)
    cnt = jnp.broadcast_to(jnp.sum(member, axis=1, keepdims=True), (ne, 128))
    e_r = lax.broadcasted_iota(jnp.int32, (ne, ne), 0)
    e_c = lax.broadcasted_iota(jnp.int32, (ne, ne), 1)
    lower = jnp.where(e_c < e_r, 1.0, 0.0).astype(BF16)
    off = _bdot(lower, cnt.astype(BF16))
    base = off[:, 0:1] + prefix
    sub_k = lax.broadcasted_iota(jnp.int32, (TOP_K, tr), 0)
    gate = jnp.zeros((TOP_K, tr), F32)
    pos = jnp.zeros((TOP_K, tr), F32)
    for k in range(TOP_K):
        pk = jnp.sum(jnp.where(hots[k], base, 0.0), axis=0, keepdims=True)
        gate = jnp.where(sub_k == k, exps[k] / den, gate)
        pos = jnp.where(sub_k == k, pk, pos)
    gate_ref[...] = gate
    pos_ref[...] = pos.astype(jnp.int32)
    tcnt_ref[0] = cnt.astype(jnp.int32)
    tbase_ref[0] = run_ref[...].astype(jnp.int32)
    run_ref[...] = run_ref[...] + cnt
    cnt_ref[...] = run_ref[...].astype(jnp.int32)


def _route(logits_t, tr):
    ne, t = logits_t.shape
    nt = t // tr
    tok = lambda i: (0, i)
    tile = lambda i: (i, 0, 0)
    return pl.pallas_call(
        _route_kernel,
        grid=(nt,),
        in_specs=[pl.BlockSpec((ne, tr), tok)],
        out_specs=[pl.BlockSpec((TOP_K, tr), tok), pl.BlockSpec((TOP_K, tr), tok),
                   pl.BlockSpec((1, ne, 128), tile), pl.BlockSpec((1, ne, 128), tile),
                   pl.BlockSpec((ne, 128), lambda i: (0, 0))],
        out_shape=[jax.ShapeDtypeStruct((TOP_K, t), F32),
                   jax.ShapeDtypeStruct((TOP_K, t), jnp.int32),
                   jax.ShapeDtypeStruct((nt, ne, 128), jnp.int32),
                   jax.ShapeDtypeStruct((nt, ne, 128), jnp.int32),
                   jax.ShapeDtypeStruct((ne, 128), jnp.int32)],
        scratch_shapes=[pltpu.VMEM((ne, 128), F32)],
        compiler_params=_cparams(("arbitrary",)),
        name="route_topk",
    )(logits_t)


def _row_tile(d):
    return (d // 256, 128)


def _pack_rows(v):
    n, d = v.shape
    bits = lax.bitcast_convert_type(v.astype(BF16).astype(F32), jnp.uint32)
    word = (bits[:, d // 2:] & jnp.uint32(0xFFFF0000)) | (bits[:, :d // 2] >> 16)
    return word.reshape((n,) + _row_tile(d))


def _unpack_rows(w):
    n = w.shape[0]
    word = w.reshape(n, w.shape[1] * w.shape[2])
    lo = lax.bitcast_convert_type(word << 16, F32)
    hi = lax.bitcast_convert_type(word & jnp.uint32(0xFFFF0000), F32)
    return jnp.concatenate([lo, hi], axis=1)


def _strip_copies(n_ref, dst_ref, make_copy, max_rows):
    ne = dst_ref.shape[2]
    all_bits = [1 << s for s in range(max_rows.bit_length() - 1, -1, -1)]
    small = pl.next_power_of_2(max(1, 2 * TOP_K * max_rows // ne))
    longest = n_ref[0, 0, ne]

    def run(bits):
        def body(e, off):
            n = n_ref[0, 0, e]
            dst = dst_ref[0, 0, e]
            done = jnp.int32(0)
            for bit in bits:
                part = n & bit

                @pl.when(part != 0)
                def _():
                    make_copy(off + done, dst + done, bit).start()

                done = done + part
            return off + n

        lax.fori_loop(0, ne, body, jnp.int32(0))

    @pl.when(longest < small)
    def _():
        run([b for b in all_bits if b < small])

    @pl.when(longest >= small)
    def _():
        run(all_bits)


def _dispatch_kernel(n_ref, dst_ref, pos_ref, h_ref, xs_ref, srt_ref, sem):
    i = pl.program_id(0)
    slot = i % 2
    td = h_ref.shape[0]
    n = td * TOP_K
    pos = pos_ref[...]
    rows = lax.broadcasted_iota(jnp.int32, (n, td), 0)
    hit = rows == pos[0:1]
    for k in range(1, TOP_K):
        hit = hit | (rows == pos[k:k + 1])
    perm = jnp.where(hit, 1.0, 0.0).astype(BF16)
    srt_ref[slot] = _pack_rows(_bdot(perm, h_ref[...]))

    def all_rows(s):
        return pltpu.make_async_copy(srt_ref.at[s], xs_ref.at[pl.ds(0, n)], sem.at[s])

    _strip_copies(n_ref, dst_ref,
                  lambda src, dst, size: pltpu.make_async_copy(
                      srt_ref.at[slot, pl.ds(src, size)], xs_ref.at[pl.ds(dst, size)], sem.at[slot]),
                  td)

    @pl.when(i > 0)
    def _():
        all_rows(1 - slot).wait()

    @pl.when(i == pl.num_programs(0) - 1)
    def _():
        all_rows(slot).wait()


def _dispatch(h2, pos_t, tile_n, tile_dst, td):
    t, d = h2.shape
    a = t * TOP_K
    n = td * TOP_K
    nt = t // td
    smem = lambda v: pl.BlockSpec((1, 1, v.shape[-1]), lambda i: (i, 0, 0),
                                  memory_space=pltpu.SMEM)
    return pl.pallas_call(
        _dispatch_kernel,
        grid=(nt,),
        in_specs=[smem(tile_n), smem(tile_dst),
                  pl.BlockSpec((TOP_K, td), lambda i: (0, i)),
                  pl.BlockSpec((td, d), lambda i: (i, 0))],
        out_specs=pl.BlockSpec(memory_space=pl.ANY),
        out_shape=jax.ShapeDtypeStruct((a,) + _row_tile(d), jnp.uint32),
        scratch_shapes=[pltpu.VMEM((2, n) + _row_tile(d), jnp.uint32),
                        pltpu.SemaphoreType.DMA((2,))],
        compiler_params=_cparams(("arbitrary",)),
        name="moe_dispatch",
    )(tile_n[:, None, :], tile_dst[:, None, :], pos_t, h2)


def _expert_kernel(tile_ref, exp_ref, lo_ref, hi_ref, first_ref, nitem_ref,
                   xs_ref, w1_ref, b1_ref, w2_ref, b2_ref, ys_ref,
                   w1b_ref, w2b_ref, y_ref, acc_ref):
    w = pl.program_id(0)
    n_items = tile_ref.shape[0]
    cur = jnp.minimum(w, n_items - 1)
    prev = jnp.maximum(w - 1, 0)
    dff = w2_ref.shape[1]

    @pl.when(w == 0)
    def _():
        y_ref[...] = jnp.zeros_like(y_ref)
        acc_ref[...] = jnp.zeros_like(acc_ref)

    @pl.when((w == 0) | (exp_ref[cur] != exp_ref[prev]))
    def _():
        w1b_ref[...] = w1_ref[0].astype(BF16)
        w2b_ref[...] = w2_ref[0].astype(BF16)

    @pl.when(w <= nitem_ref[0])
    def _():
        y_prev = y_ref[...]
        row = lax.broadcasted_iota(jnp.int32, y_prev.shape, 0)
        y_prev = jnp.where((row >= lo_ref[prev]) & (row < hi_ref[prev]), y_prev, 0.0)
        acc = y_prev + jnp.where(first_ref[prev] == 1, 0.0, acc_ref[...])
        acc_ref[...] = acc
        ys_ref[...] = _pack_rows(acc)

        x = _unpack_rows(xs_ref[...]).astype(BF16)
        hb = _bdot(x, w1b_ref[...]) + b1_ref[0]
        glu = jnp.minimum(hb[:, :dff], SWIGLU_LIMIT)
        lin = jnp.clip(hb[:, dff:], -SWIGLU_LIMIT, SWIGLU_LIMIT)
        act = glu * jax.nn.sigmoid(SWIGLU_ALPHA * glu) * (lin + 1.0)
        y_ref[...] = _bdot(act.astype(BF16), w2b_ref[...]) + b2_ref[0]


def _experts(xs, items, w1, b1, w2, b2, tm):
    a = xs.shape[0]
    ne, d, dff2 = w1.shape
    dff = dff2 // 2
    n_items = a // tm + ne - 1
    tile_w, exp_w, lo_w, hi_w, first_w, nitem = items
    cur = lambda w: jnp.minimum(w, n_items - 1)
    rows_in = pl.BlockSpec((tm,) + _row_tile(d),
                           lambda w, tl, ex, lo, hi, fi, ni: (tl[cur(w)], 0, 0))
    rows_out = pl.BlockSpec((tm,) + _row_tile(d),
                            lambda w, tl, ex, lo, hi, fi, ni: (tl[jnp.maximum(w - 1, 0)], 0, 0))
    per_expert = lambda w, tl, ex, lo, hi, fi, ni: (ex[cur(w)], 0, 0)
    grid_spec = pltpu.PrefetchScalarGridSpec(
        num_scalar_prefetch=6,
        grid=(n_items + 1,),
        in_specs=[rows_in,
                  pl.BlockSpec((1, d, dff2), per_expert),
                  pl.BlockSpec((1, 1, dff2), per_expert),
                  pl.BlockSpec((1, dff, d), per_expert),
                  pl.BlockSpec((1, 1, d), per_expert)],
        out_specs=rows_out,
        scratch_shapes=[pltpu.VMEM((d, dff2), BF16), pltpu.VMEM((dff, d), BF16),
                        pltpu.VMEM((tm, d), F32), pltpu.VMEM((tm, d), F32)],
    )
    return pl.pallas_call(
        _expert_kernel,
        grid_spec=grid_spec,
        out_shape=jax.ShapeDtypeStruct(xs.shape, jnp.uint32),
        compiler_params=_cparams(("arbitrary",)),
        name="moe_experts",
    )(tile_w, exp_w, lo_w, hi_w, first_w, nitem,
      xs, w1, b1.reshape(ne, 1, dff2), w2, b2.reshape(ne, 1, d))


def _work_items(counts, a, tm):
    ne = counts.shape[0]
    n_items = a // tm + ne - 1
    ends = jnp.cumsum(counts)
    starts = ends - counts
    first_tile = starts // tm
    last_tile = (ends - 1) // tm
    nvis = jnp.where(counts > 0, last_tile - first_tile + 1, 0)
    vis_end = jnp.cumsum(nvis)
    vis_start = vis_end - nvis
    total = vis_end[-1]
    w = jnp.minimum(jnp.arange(n_items, dtype=jnp.int32), total - 1)
    e_w = jnp.sum((vis_end[None, :] <= w[:, None]).astype(jnp.int32), axis=1)
    hot = e_w[:, None] == jnp.arange(ne, dtype=jnp.int32)[None, :]
    pick = lambda v: jnp.sum(jnp.where(hot, v[None, :], 0), axis=1)
    tile_w = (pick(first_tile) + (w - pick(vis_start))).astype(jnp.int32)
    lo_w = jnp.maximum(pick(starts) - tile_w * tm, 0).astype(jnp.int32)
    hi_w = jnp.minimum(pick(ends) - tile_w * tm, tm).astype(jnp.int32)
    prev = jnp.concatenate([jnp.full((1,), -1, jnp.int32), tile_w[:-1]])
    first_w = (tile_w != prev).astype(jnp.int32)
    return starts, (tile_w, e_w, lo_w, hi_w, first_w, total.reshape(1).astype(jnp.int32))


def _combine_kernel(alpha, n_ref, dst_ref, nn_ref, ndst_ref, pos_ref, gate_ref,
                    ys_ref, xm_ref, g2_ref, lg_ref, lb_ref, o_ref, srt_ref, sem):
    i = pl.program_id(0)
    last = pl.num_programs(0) - 1
    slot = i % 2
    td = xm_ref.shape[0]
    n = td * TOP_K

    def fetch(cnt_ref, from_ref, s):
        _strip_copies(cnt_ref, from_ref,
                      lambda row, src, size: pltpu.make_async_copy(
                          ys_ref.at[pl.ds(src, size)], srt_ref.at[s, pl.ds(row, size)], sem.at[s]),
                      td)

    @pl.when(i == 0)
    def _():
        fetch(n_ref, dst_ref, slot)

    @pl.when(i < last)
    def _():
        fetch(nn_ref, ndst_ref, 1 - slot)

    pltpu.make_async_copy(ys_ref.at[pl.ds(0, n)], srt_ref.at[slot], sem.at[slot]).wait()

    pos, gate = pos_ref[...], gate_ref[...]
    lanes = lax.broadcasted_iota(jnp.int32, (td, n), 1)
    wsel = jnp.where(lanes == pos[:, 0:1], gate[:, 0:1], 0.0)
    for k in range(1, TOP_K):
        wsel = wsel + jnp.where(lanes == pos[:, k:k + 1], gate[:, k:k + 1], 0.0)
    w_hi = wsel.astype(BF16)
    w_lo = (wsel - w_hi.astype(F32)).astype(BF16)
    y = _unpack_rows(srt_ref[slot]).astype(BF16)
    f = _bdot(w_hi, y) + _bdot(w_lo, y)
    o_ref[...] = _ln(alpha * xm_ref[...] + g2_ref[0] * f) * lg_ref[...] + lb_ref[...]


def _combine(ys, pos, gate, tile_n, tile_dst, x_mid, g2, lg, lb, alpha, td):
    t, d = x_mid.shape
    n = td * TOP_K
    nt = t // td
    per_b = nt // g2.shape[0]
    first = lambda v: pl.BlockSpec((1, 1, v.shape[-1]), lambda i: (0, 0, 0),
                                   memory_space=pltpu.SMEM)
    nxt = lambda v: pl.BlockSpec((1, 1, v.shape[-1]),
                                 lambda i: (jnp.minimum(i + 1, nt - 1), 0, 0),
                                 memory_space=pltpu.SMEM)
    const = lambda i: (0, 0)
    tile_n = tile_n[:, None, :]
    tile_dst = tile_dst[:, None, :]
    return pl.pallas_call(
        functools.partial(_combine_kernel, alpha),
        grid=(nt,),
        in_specs=[first(tile_n), first(tile_dst), nxt(tile_n), nxt(tile_dst),
                  pl.BlockSpec((td, TOP_K), lambda i: (i, 0)),
                  pl.BlockSpec((td, TOP_K), lambda i: (i, 0)),
                  pl.BlockSpec(memory_space=pl.ANY),
                  pl.BlockSpec((td, d), lambda i: (i, 0)),
                  pl.BlockSpec((1, 1, d), lambda i: (i // per_b, 0, 0)),
                  pl.BlockSpec((1, d), const), pl.BlockSpec((1, d), const)],
        out_specs=pl.BlockSpec((td, d), lambda i: (i, 0)),
        out_shape=jax.ShapeDtypeStruct((t, d), F32),
        scratch_shapes=[pltpu.VMEM((2, n) + _row_tile(d), jnp.uint32),
                        pltpu.SemaphoreType.DMA((2,))],
        compiler_params=_cparams(("arbitrary",)),
        name="moe_combine_ln",
    )(tile_n, tile_dst, tile_n, tile_dst, pos, gate, ys, x_mid, g2,
      lg.reshape(1, d), lb.reshape(1, d))


def _rope_tables(l):
    f32 = np.float32
    rows = l // GRID_W
    row = np.repeat(np.arange(rows, dtype=f32), GRID_W)
    col = np.tile(np.arange(GRID_W, dtype=f32), rows)
    n_freq = HEAD_DIM // 4
    inv_freq = np.power(f32(ROPE_BASE), -np.arange(n_freq, dtype=f32) / f32(n_freq)).astype(f32)
    ang_r = (row[:, None] * inv_freq).astype(f32)
    ang_c = (col[:, None] * inv_freq).astype(f32)
    zero = np.zeros_like(ang_r)
    cos_r, sin_r, cos_c, sin_c = np.cos(ang_r), np.sin(ang_r), np.cos(ang_c), np.sin(ang_c)
    cos_h = np.concatenate([cos_r, cos_r, cos_c, cos_c], axis=1)
    sa_h = np.concatenate([-sin_r, zero, -sin_c, zero], axis=1)
    sb_h = np.concatenate([zero, sin_r, zero, sin_c], axis=1)
    rep = 128 // HEAD_DIM
    return tuple(np.tile(a, (1, rep)).astype(f32) for a in (cos_h, sa_h, sb_h))


def _filter_features(l):
    f32 = np.float32
    bands = (HY_EMB_DIM - 1) // 2
    t = np.linspace(0.0, 1.0, l, dtype=f32)[:, None]
    omega = (f32(2.0 * math.pi) * np.arange(l, dtype=f32)[:, None] / f32(l)).astype(f32)
    f = np.linspace(1e-4, bands - 1, bands, dtype=f32)[None, :]
    ang = (f * omega).astype(f32)
    z = np.concatenate([t, np.cos(ang), -np.sin(ang)], axis=-1).astype(f32)
    min_decay = math.log(HY_DECAY_TARGET) / HY_FAST_DECAY_PCT
    max_decay = math.log(HY_DECAY_TARGET) / HY_SLOW_DECAY_PCT
    deltas = np.abs(np.linspace(min_decay, max_decay, HY_WIDTH, dtype=f32))
    decay = np.exp(-t * deltas).astype(f32)
    return z, decay


def kernel(x, c, ctx, c_ctx, w_mod, b_mod, w_in, attn_sink, hy_conv_w, hy_conv_b, hy_filt_w1,
           hy_filt_b1, hy_filt_w2, hy_filt_b2, hy_filt_w3, hy_skip, w_branch_attn, w_branch_hyena,
           w_out, ln1_g, ln1_b, router_w, router_b, exp_w1, exp_b1, exp_w2, exp_b2, ln2_g, ln2_b):
    depth = w_mod.shape[0]
    assert depth == 1, "only the single-layer configuration is implemented"
    b, l, d = x.shape
    t = b * l
    alpha = (2 * depth) ** 0.25
    hy_wc = 256
    expert_tm = 512

    n_cond = b + 1
    pad = (-n_cond) % 8
    cond = jnp.concatenate([c, c_ctx[None], jnp.zeros((pad, d), F32)], axis=0)
    mod = _modulation(cond, w_mod[0], b_mod[0])
    mod_x = mod[:b].reshape(b, 1, 6, d)
    sh1, sc1, g1, sh2, sc2, g2 = (mod_x[:, :, i] for i in range(6))
    mod_c = mod[b:b + 1].reshape(1, 1, 6, d)
    csh1, csc1 = mod_c[:, :, 0], mod_c[:, :, 1]

    w_in_b = w_in[0].astype(BF16)
    cos_t, sa_t, sb_t = (jnp.asarray(a) for a in _rope_tables(l))
    q, k, v, u_hy, gate_x = _in_projection(x, sh1, sc1, w_in_b, cos_t, sa_t, sb_t)
    k_c, v_c = _ctx_kv(ctx, csh1, csc1, w_in_b[:, K_OFF:HY_OFF])
    att = _attention(q, k, v, k_c, v_c, attn_sink[0])

    tabs_np = _dft_tables(l, hy_wc)
    tabs = (jnp.asarray(tabs_np[0]).astype(BF16), jnp.asarray(tabs_np[1]).astype(BF16)) + tuple(
        jnp.asarray(a) for a in tabs_np[2:])
    zfeat, decay = _filter_features(l)
    emb_pad = (-HY_EMB_DIM) % 128
    zfeat = np.pad(zfeat, ((0, 0), (0, emb_pad)))
    fw1 = jnp.pad(hy_filt_w1[0], ((0, emb_pad), (0, 0)))
    spectra = _hyena_filters(jnp.asarray(zfeat[0::2]), jnp.asarray(zfeat[1::2]), fw1, hy_filt_b1[0],
                             hy_filt_w2[0], hy_filt_b2[0], hy_filt_w3[0], jnp.asarray(decay[0::2]),
                             jnp.asarray(decay[1::2]), tabs, hy_wc)
    hy = _hyena(u_hy, hy_conv_w[0], hy_conv_b[0], hy_skip[0], spectra, tabs, hy_wc)

    x_mid, h2, logits_t = _merge(att.reshape(t, Q_W), hy.reshape(t, HY_WIDTH),
                                 gate_x.reshape(t, 2 * d), x.reshape(t, d), g1, sh2, sc2,
                                 w_branch_attn[0].astype(BF16), w_branch_hyena[0].astype(BF16),
                                 w_out[0].astype(BF16), ln1_g[0], ln1_b[0],
                                 jnp.transpose(router_w[0]), router_b[0], alpha)

    moe_td = 256
    gate_t, pos_t, tile_cnt, tile_base, counts = _route(logits_t, moe_td)
    starts, items = _work_items(counts[:, 0], t * TOP_K, expert_tm)
    tile_n = tile_cnt[:, :, 0]
    tile_n = jnp.concatenate([tile_n, jnp.max(tile_n, axis=1, keepdims=True)], axis=1)
    tile_dst = starts[None, :] + tile_base[:, :, 0]
    xs = _dispatch(h2.reshape(t, d), pos_t, tile_n, tile_dst, moe_td)
    ys = _experts(xs, items, exp_w1[0], exp_b1[0], exp_w2[0], exp_b2[0], expert_tm)
    out = _combine(ys, jnp.transpose(pos_t), jnp.transpose(gate_t), tile_n, tile_dst,
                   x_mid.reshape(t, d), g2, ln2_g[0], ln2_b[0], alpha, moe_td)
    return out.reshape(b, l, d)
```

```python
import functools
import math

import numpy as np
import jax
import jax.numpy as jnp
from jax import lax
from jax.experimental import pallas as pl
from jax.experimental.pallas import tpu as pltpu

F32 = jnp.float32
BF16 = jnp.bfloat16
HIGHEST = lax.Precision.HIGHEST

GRID_W = 64
N_HEADS = 8
N_KV_HEADS = 2
GQA_GROUP = N_HEADS // N_KV_HEADS
HEAD_DIM = 64
WINDOW = 128
ATTN_BLOCK = 128
ROPE_BASE = 10000.0

HY_WIDTH = 512
HY_ORDER = 2
HY_SHORT_CONV = 3
HY_EMB_DIM = 33
HY_DECAY_TARGET = 1e-2
HY_FAST_DECAY_PCT = 0.3
HY_SLOW_DECAY_PCT = 1.5

N_EXPERTS = 32
TOP_K = 4
SWIGLU_LIMIT = 7.0
SWIGLU_ALPHA = 1.702
LN_EPS = 1e-5

Q_W = N_HEADS * HEAD_DIM
KV_W = N_KV_HEADS * HEAD_DIM
K_OFF = Q_W
V_OFF = K_OFF + KV_W
HY_OFF = V_OFF + KV_W
GATE_OFF = HY_OFF + (HY_ORDER + 1) * HY_WIDTH

VMEM_LIMIT = 56 * 1024 * 1024
NEG_BIG = -1e30
MERGE_CHUNKS = 2
EXPERT_TILE = 512
EXPERT_LAST_TILES = (256, 512)


def _cparams(sem):
    return pltpu.CompilerParams(dimension_semantics=sem, vmem_limit_bytes=VMEM_LIMIT)


def _ln(x):
    mu = jnp.mean(x, axis=-1, keepdims=True)
    xc = x - mu
    var = jnp.mean(xc * xc, axis=-1, keepdims=True)
    return xc * lax.rsqrt(var + LN_EPS)


def _bdot(a, b):
    return jnp.dot(a, b, preferred_element_type=F32)


def _fdot(a, b):
    return jnp.dot(a, b, preferred_element_type=F32, precision=HIGHEST)


def _mod_kernel(c_ref, w_ref, b_ref, o_ref):
    c = c_ref[...]
    s = c * jax.nn.sigmoid(c)
    o_ref[...] = _fdot(s, w_ref[...]) + b_ref[...]


def _modulation(cond, w, b, tn=512):
    r, d = cond.shape
    n = w.shape[1]
    return pl.pallas_call(
        _mod_kernel,
        grid=(n // tn,),
        in_specs=[pl.BlockSpec((r, d), lambda j: (0, 0)),
                  pl.BlockSpec((d, tn), lambda j: (0, j)),
                  pl.BlockSpec((1, tn), lambda j: (0, j))],
        out_specs=pl.BlockSpec((r, tn), lambda j: (0, j)),
        out_shape=jax.ShapeDtypeStruct((r, n), F32),
        compiler_params=_cparams(("arbitrary",)),
        name="modulation",
    )(cond, w, b.reshape(1, n))


def _rope(t, cos, sa, sb):
    n = t.shape[-1]
    return t * cos + pltpu.roll(t, n - 16, 1) * sa + pltpu.roll(t, 16, 1) * sb


def _inproj_kernel(x_ref, sh_ref, sc_ref, w_ref, cos_ref, sa_ref, sb_ref,
                   q_ref, k_ref, v_ref, u_ref, g_ref):
    h = _ln(x_ref[0]) * (1.0 + sc_ref[0]) + sh_ref[0]
    hb = h.astype(BF16)
    cos, sa, sb = cos_ref[...], sa_ref[...], sb_ref[...]
    scale = HEAD_DIM ** -0.5
    for j in range(Q_W // 128):
        t = _bdot(hb, w_ref[:, j * 128:(j + 1) * 128])
        q_ref[0, :, j * 128:(j + 1) * 128] = (_rope(t, cos, sa, sb) * scale).astype(BF16)
    t = _bdot(hb, w_ref[:, K_OFF:V_OFF])
    k_ref[0] = _rope(t, cos, sa, sb).astype(BF16)
    v_ref[0] = _bdot(hb, w_ref[:, V_OFF:HY_OFF]).astype(BF16)
    for j in range((GATE_OFF - HY_OFF) // 512):
        u_ref[0, :, j * 512:(j + 1) * 512] = _bdot(hb, w_ref[:, HY_OFF + j * 512:HY_OFF + (j + 1) * 512])
    n_gate = w_ref.shape[1] - GATE_OFF
    for j in range(n_gate // 512):
        g_ref[0, :, j * 512:(j + 1) * 512] = jax.nn.sigmoid(_bdot(
            hb, w_ref[:, GATE_OFF + j * 512:GATE_OFF + (j + 1) * 512])).astype(BF16)


def _in_projection(x, sh, sc, w_in_b, cos_t, sa_t, sb_t, tm=256):
    b, l, d = x.shape
    in_w = w_in_b.shape[1]
    hy_w = GATE_OFF - HY_OFF
    g_w = in_w - GATE_OFF
    row = lambda bi, i: (bi, i, 0)
    vec = lambda bi, i: (bi, 0, 0)
    tab = lambda bi, i: (i, 0)
    return pl.pallas_call(
        _inproj_kernel,
        grid=(b, l // tm),
        in_specs=[pl.BlockSpec((1, tm, d), row),
                  pl.BlockSpec((1, 1, d), vec),
                  pl.BlockSpec((1, 1, d), vec),
                  pl.BlockSpec((d, in_w), lambda bi, i: (0, 0)),
                  pl.BlockSpec((tm, 128), tab),
                  pl.BlockSpec((tm, 128), tab),
                  pl.BlockSpec((tm, 128), tab)],
        out_specs=[pl.BlockSpec((1, tm, Q_W), row),
                   pl.BlockSpec((1, tm, KV_W), row),
                   pl.BlockSpec((1, tm, KV_W), row),
                   pl.BlockSpec((1, tm, hy_w), row),
                   pl.BlockSpec((1, tm, g_w), row)],
        out_shape=[jax.ShapeDtypeStruct((b, l, Q_W), BF16),
                   jax.ShapeDtypeStruct((b, l, KV_W), BF16),
                   jax.ShapeDtypeStruct((b, l, KV_W), BF16),
                   jax.ShapeDtypeStruct((b, l, hy_w), F32),
                   jax.ShapeDtypeStruct((b, l, g_w), BF16)],
        compiler_params=_cparams(("arbitrary", "arbitrary")),
        name="in_projection",
    )(x, sh, sc, w_in_b, cos_t, sa_t, sb_t)


def _ctx_kv_kernel(x_ref, sh_ref, sc_ref, w_ref, k_ref, v_ref):
    h = _ln(x_ref[0]) * (1.0 + sc_ref[0]) + sh_ref[0]
    kv = _bdot(h.astype(BF16), w_ref[...])
    k_ref[0] = kv[:, :KV_W].astype(BF16)
    v_ref[0] = kv[:, KV_W:].astype(BF16)


def _ctx_kv(ctx, sh, sc, w_kv_b):
    b, c, d = ctx.shape
    row = lambda bi: (bi, 0, 0)
    return pl.pallas_call(
        _ctx_kv_kernel,
        grid=(b,),
        in_specs=[pl.BlockSpec((1, c, d), row),
                  pl.BlockSpec((1, 1, d), lambda bi: (0, 0, 0)),
                  pl.BlockSpec((1, 1, d), lambda bi: (0, 0, 0)),
                  pl.BlockSpec((d, 2 * KV_W), lambda bi: (0, 0))],
        out_specs=[pl.BlockSpec((1, c, KV_W), row), pl.BlockSpec((1, c, KV_W), row)],
        out_shape=[jax.ShapeDtypeStruct((b, c, KV_W), BF16),
                   jax.ShapeDtypeStruct((b, c, KV_W), BF16)],
        compiler_params=_cparams(("arbitrary",)),
        name="ctx_kv",
    )(ctx, sh, sc, w_kv_b)


def _attn_kernel(sink_ref, q_ref, k_ref, v_ref, kc_ref, vc_ref, o_ref):
    for qb in range(q_ref.shape[1] // ATTN_BLOCK):
        rows = slice(qb * ATTN_BLOCK, (qb + 1) * ATTN_BLOCK)
        j = pl.program_id(1) * (q_ref.shape[1] // ATTN_BLOCK) + qb
        o_ref[0, rows] = _attn_block(j, sink_ref, q_ref.at[0, rows], k_ref, v_ref, kc_ref, vc_ref)


def _attn_block(j, sink_ref, q_ref, k_ref, v_ref, kc_ref, vc_ref):
    l = k_ref.shape[1]
    span = ATTN_BLOCK + 2 * WINDOW
    q0 = j * ATTN_BLOCK
    start = pl.multiple_of(jnp.clip(q0 - WINDOW, 0, l - span), ATTN_BLOCK)
    rows = GQA_GROUP * ATTN_BLOCK
    r_i = lax.broadcasted_iota(jnp.int32, (rows, span), 0)
    qpos = q0 + r_i % ATTN_BLOCK
    kpos = start + lax.broadcasted_iota(jnp.int32, (rows, span), 1)
    band = jnp.abs(kpos - qpos) <= WINDOW
    head_of_row = lax.broadcasted_iota(jnp.int32, (rows, 1), 0) // ATTN_BLOCK
    dn = (((1,), (1,)), ((), ()))
    outs = []
    for kv in range(N_KV_HEADS):
        ks = slice(kv * HEAD_DIM, (kv + 1) * HEAD_DIM)
        kl = k_ref[0, pl.ds(start, span), ks]
        vl = v_ref[0, pl.ds(start, span), ks]
        kc = kc_ref[0, :, ks]
        vc = vc_ref[0, :, ks]
        heads = [kv * GQA_GROUP + g for g in range(GQA_GROUP)]
        qg = jnp.concatenate([q_ref[:, h * HEAD_DIM:(h + 1) * HEAD_DIM] for h in heads], axis=0)
        sink = jnp.zeros((rows, 1), F32)
        for g, h in enumerate(heads):
            sink = jnp.where(head_of_row == g, sink_ref[h], sink)
        s_loc = lax.dot_general(qg, kl, dn, preferred_element_type=F32)
        s_loc = jnp.where(band, s_loc, NEG_BIG)
        s_ctx = lax.dot_general(qg, kc, dn, preferred_element_type=F32)
        blocks = [s[:, c:c + 128] for s in (s_loc, s_ctx) for c in range(0, s.shape[1], 128)]
        folded = functools.reduce(jnp.maximum, blocks)
        m = jnp.maximum(jnp.max(folded, axis=1, keepdims=True), sink)
        p_loc = jnp.exp(s_loc - m).astype(BF16)
        p_ctx = jnp.exp(s_ctx - m).astype(BF16)
        ones_l = jnp.ones((span, HEAD_DIM), BF16)
        ones_c = jnp.ones((kc.shape[0], HEAD_DIM), BF16)
        pv = (_bdot(p_loc, jnp.concatenate([vl, ones_l], axis=1))
              + _bdot(p_ctx, jnp.concatenate([vc, ones_c], axis=1)))
        den = pv[:, HEAD_DIM:HEAD_DIM + 1] + jnp.exp(sink - m)
        o = pv[:, :HEAD_DIM] / den
        outs.extend(o[g * ATTN_BLOCK:(g + 1) * ATTN_BLOCK] for g in range(GQA_GROUP))
    return jnp.concatenate(outs, axis=1).astype(BF16)


def _attention(q, k, v, kc, vc, sink, q_blocks=4):
    b, l, _ = q.shape
    c = kc.shape[1]
    tq = q_blocks * ATTN_BLOCK
    full = lambda bi, j, s: (bi, 0, 0)
    grid_spec = pltpu.PrefetchScalarGridSpec(
        num_scalar_prefetch=1,
        grid=(b, l // tq),
        in_specs=[pl.BlockSpec((1, tq, Q_W), lambda bi, j, s: (bi, j, 0)),
                  pl.BlockSpec((1, l, KV_W), full),
                  pl.BlockSpec((1, l, KV_W), full),
                  pl.BlockSpec((1, c, KV_W), full),
                  pl.BlockSpec((1, c, KV_W), full)],
        out_specs=pl.BlockSpec((1, tq, Q_W), lambda bi, j, s: (bi, j, 0)),
    )
    return pl.pallas_call(
        _attn_kernel,
        grid_spec=grid_spec,
        out_shape=jax.ShapeDtypeStruct((b, l, Q_W), BF16),
        compiler_params=_cparams(("arbitrary", "arbitrary")),
        name="window_attention",
    )(sink, q, k, v, kc, vc)


def _dft_tables(l, wc):
    h = l // 2
    idx = np.arange(h, dtype=np.int64)
    ang = 2.0 * np.pi * ((idx[:, None] * idx[None, :]) % l).astype(np.float64) / l
    cm = np.cos(ang).astype(np.float32)
    sm = np.sin(ang).astype(np.float32)
    tw = 2.0 * np.pi * idx.astype(np.float64) / (2 * l)
    ones = np.ones((1, wc), np.float32)
    wr = np.cos(tw).astype(np.float32)[:, None] * ones
    wi = (-np.sin(tw)).astype(np.float32)[:, None] * ones
    sgn = np.where(idx % 2 == 0, 1.0, -1.0).astype(np.float32)[:, None] * ones
    wk = np.where(idx == 0, 1.0 / (2 * l), 2.0 / (2 * l)).astype(np.float32)[:, None] * ones
    return cm, sm, wr, wi, sgn, wk


def _half_spectrum(se, so, cm, sm, wr, wi, sgn):
    seb, sob = se.astype(BF16), so.astype(BF16)
    ce, ss_e = _bdot(cm, seb), _bdot(sm, seb)
    co, ss_o = _bdot(cm, sob), _bdot(sm, sob)
    vr = wr * co + wi * ss_o
    vi = wi * co - wr * ss_o
    zar, zai = ce + vr, vi - ss_e
    zbr, zbi = ce - vr, -ss_e - vi
    e_ny = jnp.sum(sgn * se, axis=0, keepdims=True)
    o_ny = jnp.sum(sgn * so, axis=0, keepdims=True)
    return zar, zai, zbr, zbi, e_ny, -o_ny


def _filter_kernel(ze_ref, zo_ref, w1_ref, b1_ref, w2_ref, b2_ref, w3f_ref, w3b_ref,
                   de_ref, do_ref, cm_ref, sm_ref, wr_ref, wi_ref, sgn_ref, wk_ref,
                   har_ref, hai_ref, hbr_ref, hbi_ref, hny_ref, ae_ref, ao_ref):
    cm, sm = cm_ref[...], sm_ref[...]
    wr, wi, sgn, wk = wr_ref[...], wi_ref[...], sgn_ref[...], wk_ref[...]
    h = cm.shape[0]

    @pl.when((pl.program_id(0) == 0) & (pl.program_id(1) == 0))
    def _():
        for z_ref, a_ref in ((ze_ref, ae_ref), (zo_ref, ao_ref)):
            a = jnp.sin(_fdot(z_ref[...], w1_ref[...]) + b1_ref[...])
            a_ref[...] = jnp.sin(_fdot(a, w2_ref[...]) + b2_ref[...])

    def taps(a_ref, w3_ref, d_ref):
        return _fdot(a_ref[...], w3_ref[0]) * d_ref[...]

    fe, fo = taps(ae_ref, w3f_ref, de_ref), taps(ao_ref, w3f_ref, do_ref)
    be, bo = taps(ae_ref, w3b_ref, de_ref), taps(ao_ref, w3b_ref, do_ref)
    row = lax.broadcasted_iota(jnp.int32, be.shape, 0)
    be = jnp.where(row == 0, 0.0, be)
    far, fai, fbr, fbi, fnr, fni = _half_spectrum(fe, fo, cm, sm, wr, wi, sgn)
    bar, bai, bbr, bbi, bnr, bni = _half_spectrum(be, bo, cm, sm, wr, wi, sgn)
    har_ref[0] = wk * (far + bar)
    hai_ref[0] = wk * (fai - bai)
    hbr_ref[0] = wk * (fbr + bbr)
    hbi_ref[0] = wk * (fbi - bbi)
    ny_scale = 2.0 / (4 * h)
    nr = ny_scale * (fnr + bnr)
    ni = ny_scale * (fni - bni)
    rows = lax.broadcasted_iota(jnp.int32, (8, nr.shape[1]), 0)
    hny_ref[0] = jnp.where(rows == 0, nr, jnp.where(rows == 1, ni, 0.0))


def _hyena_filters(zfe, zfo, w1, b1, w2, b2, w3, dec_e, dec_o, tabs, wc):
    cm, sm, wr, wi, sgn, wk = tabs
    h = cm.shape[0]
    emb = zfe.shape[1]
    hid = w2.shape[0]
    nblk = HY_WIDTH // wc
    w3r = w3.reshape(hid, HY_ORDER * 2, HY_WIDTH).transpose(1, 0, 2)
    const = lambda o, cb: (0, 0)
    chan = lambda o, cb: (0, cb)
    out_spec = pl.BlockSpec((1, h, wc), lambda o, cb: (o, 0, cb))
    out_shape = jax.ShapeDtypeStruct((HY_ORDER, h, HY_WIDTH), F32)
    return pl.pallas_call(
        _filter_kernel,
        grid=(HY_ORDER, nblk),
        in_specs=[pl.BlockSpec((h, emb), const), pl.BlockSpec((h, emb), const),
                  pl.BlockSpec((emb, hid), const), pl.BlockSpec((1, hid), const),
                  pl.BlockSpec((hid, hid), const), pl.BlockSpec((1, hid), const),
                  pl.BlockSpec((1, hid, wc), lambda o, cb: (2 * o, 0, cb)),
                  pl.BlockSpec((1, hid, wc), lambda o, cb: (2 * o + 1, 0, cb)),
                  pl.BlockSpec((h, wc), chan), pl.BlockSpec((h, wc), chan),
                  pl.BlockSpec((h, h), const), pl.BlockSpec((h, h), const),
                  pl.BlockSpec((h, wc), const), pl.BlockSpec((h, wc), const),
                  pl.BlockSpec((h, wc), const), pl.BlockSpec((h, wc), const)],
        out_specs=[out_spec, out_spec, out_spec, out_spec,
                   pl.BlockSpec((1, 8, wc), lambda o, cb: (o, 0, cb))],
        out_shape=[out_shape, out_shape, out_shape, out_shape,
                   jax.ShapeDtypeStruct((HY_ORDER, 8, HY_WIDTH), F32)],
        scratch_shapes=[pltpu.VMEM((h, hid), F32), pltpu.VMEM((h, hid), F32)],
        compiler_params=_cparams(("arbitrary", "arbitrary")),
        name="hyena_filter_spectrum",
    )(zfe, zfo, w1, b1.reshape(1, hid), w2, b2.reshape(1, hid), w3r, w3r,
      dec_e, dec_o, cm, sm, wr, wi, sgn, wk)


def _long_conv(ze, zo, har, hai, hbr, hbi, hny, cm, sm, wr, wi, sgn):
    zar, zai, zbr, zbi, znr, zni = _half_spectrum(ze, zo, cm, sm, wr, wi, sgn)
    yar, yai = zar * har - zai * hai, zar * hai + zai * har
    ybr, ybi = zbr * hbr - zbi * hbi, zbr * hbi + zbi * hbr
    dr, di = yar - ybr, yai - ybi
    sar, sai = (yar + ybr).astype(BF16), (yai + ybi).astype(BF16)
    sbr, sbi = (dr * wr + di * wi).astype(BF16), (di * wr - dr * wi).astype(BF16)
    hnr, hni = hny[0:1], hny[1:2]
    ynr = znr * hnr - zni * hni
    yni = znr * hni + zni * hnr
    ye = _bdot(cm, sar) - _bdot(sm, sai) + sgn * ynr
    yo = _bdot(cm, sbr) - _bdot(sm, sbi) - sgn * yni
    return ye, yo


def _hyena_kernel(*refs):
    nsub = (len(refs) - 18) // 4
    u_refs = [refs[s * nsub:(s + 1) * nsub] for s in range(HY_ORDER + 1)]
    rest = refs[(HY_ORDER + 1) * nsub:]
    (cw0_ref, cw1_ref, cw2_ref, cb0_ref, cb1_ref, cb2_ref, skip_ref,
     har_ref, hai_ref, hbr_ref, hbi_ref, hny_ref,
     cm_ref, sm_ref, wr_ref, wi_ref, sgn_ref, o_ref) = rest[:18]
    il_refs = rest[18:]
    h = cm_ref.shape[0]
    cm, sm = cm_ref[...], sm_ref[...]
    wr, wi, sgn = wr_ref[...], wi_ref[...], sgn_ref[...]
    row = lax.broadcasted_iota(jnp.int32, (h, wr.shape[1]), 0)

    def short_conv(u_slabs, cw_ref, cb_ref):
        ue = jnp.concatenate([r[0, pl.ds(0, h, stride=2), :] for r in u_slabs], axis=1)
        uo = jnp.concatenate([r[0, pl.ds(1, h, stride=2), :] for r in u_slabs], axis=1)
        w0, w1, w2 = cw_ref[0:1], cw_ref[1:2], cw_ref[2:3]
        uo_prev = jnp.where(row == 0, 0.0, pltpu.roll(uo, 1, 0))
        ue_next = jnp.where(row == h - 1, 0.0, pltpu.roll(ue, h - 1, 0))
        cb = cb_ref[...]
        se = cb + uo_prev * w0 + ue * w1 + uo * w2
        so = cb + ue * w0 + uo * w1 + ue_next * w2
        return se, so

    x_streams = [short_conv(u_refs[0], cw0_ref, cb0_ref), short_conv(u_refs[1], cw1_ref, cb1_ref)]
    ze, zo = short_conv(u_refs[2], cw2_ref, cb2_ref)
    for o in range(HY_ORDER):
        ce, co = _long_conv(ze, zo, har_ref[o], hai_ref[o], hbr_ref[o], hbi_ref[o], hny_ref[o],
                            cm, sm, wr, wi, sgn)
        sk = skip_ref[o:o + 1]
        xe, xo = x_streams[o]
        ze = xe * (ce + sk * ze)
        zo = xo * (co + sk * zo)
    for j, il in enumerate(il_refs):
        il[pl.ds(0, h, stride=2), :] = ze[:, j * 128:(j + 1) * 128]
        il[pl.ds(1, h, stride=2), :] = zo[:, j * 128:(j + 1) * 128]
        o_ref[0, :, j * 128:(j + 1) * 128] = il[...]


def _hyena(u, conv_w, conv_b, skip, spectra, tabs, wc):
    b, l, _ = u.shape
    har, hai, hbr, hbi, hny = spectra
    cm, sm, wr, wi, sgn, _ = tabs
    h = l // 2
    nblk = HY_WIDTH // wc
    nsub = wc // 128
    conv_b = conv_b.reshape(1, -1)

    def ublk(s, j):
        return pl.BlockSpec((1, l, 128), lambda cb, bi: (bi, 0, (cb + s * nblk) * nsub + j))

    def wblk(r, s):
        return pl.BlockSpec((r, wc), lambda cb, bi: (0, cb + s * nblk))

    const = lambda cb, bi: (0, 0)
    spec = pl.BlockSpec((HY_ORDER, h, wc), lambda cb, bi: (0, 0, cb))
    return pl.pallas_call(
        _hyena_kernel,
        grid=(nblk, b),
        in_specs=[ublk(s, j) for s in range(HY_ORDER + 1) for j in range(nsub)] + [
                  wblk(HY_SHORT_CONV, 0), wblk(HY_SHORT_CONV, 1), wblk(HY_SHORT_CONV, 2),
                  wblk(1, 0), wblk(1, 1), wblk(1, 2),
                  pl.BlockSpec((HY_ORDER, wc), lambda cb, bi: (0, cb)),
                  spec, spec, spec, spec,
                  pl.BlockSpec((HY_ORDER, 8, wc), lambda cb, bi: (0, 0, cb)),
                  pl.BlockSpec((h, h), const), pl.BlockSpec((h, h), const),
                  pl.BlockSpec((h, wc), const), pl.BlockSpec((h, wc), const),
                  pl.BlockSpec((h, wc), const)],
        out_specs=pl.BlockSpec((1, l, wc), lambda cb, bi: (bi, 0, cb)),
        out_shape=jax.ShapeDtypeStruct((b, l, HY_WIDTH), F32),
        scratch_shapes=[pltpu.VMEM((l, 128), F32) for _ in range(nsub)],
        compiler_params=_cparams(("arbitrary", "arbitrary")),
        name="hyena_long_conv",
    )(*([u] * ((HY_ORDER + 1) * nsub)), conv_w, conv_w, conv_w, conv_b, conv_b, conv_b, skip,
      har, hai, hbr, hbi, hny, cm, sm, wr, wi, sgn)


def _merge_kernel(alpha, att_ref, hy_ref, gate_ref, x_ref, g1_ref, sh2_ref, sc2_ref,
                  wba_ref, wbh_ref, wo_ref, l1g_ref, l1b_ref, rw_ref, rb_ref,
                  xmid_ref, h2_ref, logit_ref, res_ref):
    d = x_ref.shape[1]

    @pl.when(pl.program_id(0) == 0)
    def _():
        res_ref[...] = jnp.zeros_like(res_ref)

    rw = rw_ref[...]
    r_hi = rw.astype(BF16)
    r_lo = (rw - r_hi.astype(F32)).astype(BF16)
    dn = (((1,), (1,)), ((), ()))
    nt_dot = lambda p, q: lax.dot_general(p, q, dn, preferred_element_type=F32)
    tm = x_ref.shape[0]
    rc = tm // MERGE_CHUNKS
    for r in (slice(c * rc, (c + 1) * rc) for c in range(MERGE_CHUNKS)):
        xm = _ln(res_ref[r]) * l1g_ref[...] + l1b_ref[...]
        xmid_ref[r] = xm
        h2 = _ln(xm) * (1.0 + sc2_ref[0]) + sh2_ref[0]
        h_hi = h2.astype(BF16)
        h2_ref[r] = h_hi
        h_lo = (h2 - h_hi.astype(F32)).astype(BF16)
        logit_ref[:, r] = nt_dot(r_hi, h_hi) + nt_dot(r_hi, h_lo) + nt_dot(r_lo, h_hi) + rb_ref[...]

        a = _bdot(att_ref[r], wba_ref[...])
        hh = _bdot(hy_ref[r].astype(BF16), wbh_ref[...])
        ga = gate_ref[r, :d].astype(F32)
        gh = gate_ref[r, d:].astype(F32)
        y = _bdot((ga * a + gh * hh).astype(BF16), wo_ref[...])
        res_ref[r] = alpha * x_ref[r] + g1_ref[0] * y


def _merge(att, hy, gate, x, g1, sh2, sc2, wba, wbh, wo, l1g, l1b, rw_t, rb, alpha, tm=512):
    t, d = x.shape
    ne = rw_t.shape[0]
    nt = t // tm
    per_b = nt // g1.shape[0]
    t1 = lambda i: jnp.minimum(i, nt - 1)
    t2 = lambda i: jnp.maximum(i - 1, 0)
    row1 = lambda i: (t1(i), 0)
    row2 = lambda i: (t2(i), 0)
    const = lambda i: (0, 0)
    return pl.pallas_call(
        functools.partial(_merge_kernel, alpha),
        grid=(nt + 1,),
        in_specs=[pl.BlockSpec((tm, Q_W), row1),
                  pl.BlockSpec((tm, HY_WIDTH), row1),
                  pl.BlockSpec((tm, 2 * d), row1),
                  pl.BlockSpec((tm, d), row1),
                  pl.BlockSpec((1, 1, d), lambda i: (t1(i) // per_b, 0, 0)),
                  pl.BlockSpec((1, 1, d), lambda i: (t2(i) // per_b, 0, 0)),
                  pl.BlockSpec((1, 1, d), lambda i: (t2(i) // per_b, 0, 0)),
                  pl.BlockSpec((Q_W, d), const), pl.BlockSpec((HY_WIDTH, d), const),
                  pl.BlockSpec((d, d), const),
                  pl.BlockSpec((1, d), const), pl.BlockSpec((1, d), const),
                  pl.BlockSpec((ne, d), const), pl.BlockSpec((ne, 1), const)],
        out_specs=[pl.BlockSpec((tm, d), row2), pl.BlockSpec((tm, d), row2),
                   pl.BlockSpec((ne, tm), lambda i: (0, t2(i)))],
        out_shape=[jax.ShapeDtypeStruct((t, d), F32),
                   jax.ShapeDtypeStruct((t, d), BF16),
                   jax.ShapeDtypeStruct((ne, t), F32)],
        scratch_shapes=[pltpu.VMEM((tm, d), F32)],
        compiler_params=_cparams(("arbitrary",)),
        name="merge_ln_router",
    )(att, hy, gate, x, g1, sh2, sc2, wba, wbh, wo, l1g.reshape(1, d), l1b.reshape(1, d),
      rw_t, rb.reshape(ne, 1))


def _route_kernel(lg_ref, gate_ref, pos_ref, tcnt_ref, tbase_ref, cnt_ref, run_ref):
    @pl.when(pl.program_id(0) == 0)
    def _():
        run_ref[...] = jnp.zeros_like(run_ref)

    lg = lg_ref[...]
    ne, tr = lg.shape
    sub = lax.broadcasted_iota(jnp.int32, (ne, tr), 0)
    work = lg
    vals, hots = [], []
    for _ in range(TOP_K):
        m = jnp.max(work, axis=0, keepdims=True)
        idx = jnp.min(jnp.where(work == m, sub, ne), axis=0, keepdims=True)
        hot = sub == idx
        vals.append(m)
        hots.append(hot)
        work = jnp.where(hot, -jnp.inf, work)
    exps = [jnp.exp(v - vals[0]) for v in vals]
    den = exps[0] + exps[1] + exps[2] + exps[3]
    member = jnp.zeros((ne, tr), F32)
    for hot in hots:
        member = member + jnp.where(hot, 1.0, 0.0)
    r_i = lax.broadcasted_iota(jnp.int32, (tr, tr), 0)
    c_i = lax.broadcasted_iota(jnp.int32, (tr, tr), 1)
    earlier = jnp.where(r_i < c_i, 1.0, 0.0).astype(BF16)
    prefix = _bdot(member.astype(BF16), earlier)
    cnt = jnp.broadcast_to(jnp.sum(member, axis=1, keepdims=True), (ne, 128))
    e_r = lax.broadcasted_iota(jnp.int32, (ne, ne), 0)
    e_c = lax.broadcasted_iota(jnp.int32, (ne, ne), 1)
    lower = jnp.where(e_c < e_r, 1.0, 0.0).astype(BF16)
    off = _bdot(lower, cnt.astype(BF16))
    base = off[:, 0:1] + prefix
    sub_k = lax.broadcasted_iota(jnp.int32, (TOP_K, tr), 0)
    gate = jnp.zeros((TOP_K, tr), F32)
    pos = jnp.zeros((TOP_K, tr), F32)
    for k in range(TOP_K):
        pk = jnp.sum(jnp.where(hots[k], base, 0.0), axis=0, keepdims=True)
        gate = jnp.where(sub_k == k, exps[k] / den, gate)
        pos = jnp.where(sub_k == k, pk, pos)
    gate_ref[...] = gate
    pos_ref[...] = pos.astype(jnp.int32)
    tcnt_ref[0] = cnt.astype(jnp.int32)
    tbase_ref[0] = run_ref[...].astype(jnp.int32)
    run_ref[...] = run_ref[...] + cnt
    cnt_ref[...] = run_ref[...].astype(jnp.int32)


def _route(logits_t, tr):
    ne, t = logits_t.shape
    nt = t // tr
    tok = lambda i: (0, i)
    tile = lambda i: (i, 0, 0)
    return pl.pallas_call(
        _route_kernel,
        grid=(nt,),
        in_specs=[pl.BlockSpec((ne, tr), tok)],
        out_specs=[pl.BlockSpec((TOP_K, tr), tok), pl.BlockSpec((TOP_K, tr), tok),
                   pl.BlockSpec((1, ne, 128), tile), pl.BlockSpec((1, ne, 128), tile),
                   pl.BlockSpec((ne, 128), lambda i: (0, 0))],
        out_shape=[jax.ShapeDtypeStruct((TOP_K, t), F32),
                   jax.ShapeDtypeStruct((TOP_K, t), jnp.int32),
                   jax.ShapeDtypeStruct((nt, ne, 128), jnp.int32),
                   jax.ShapeDtypeStruct((nt, ne, 128), jnp.int32),
                   jax.ShapeDtypeStruct((ne, 128), jnp.int32)],
        scratch_shapes=[pltpu.VMEM((ne, 128), F32)],
        compiler_params=_cparams(("arbitrary",)),
        name="route_topk",
    )(logits_t)


def _row_tile(d):
    return (d // 256, 128)


def _pack_rows(v):
    n, d = v.shape
    bits = lax.bitcast_convert_type(v.astype(BF16).astype(F32), jnp.uint32)
    word = (bits[:, d // 2:] & jnp.uint32(0xFFFF0000)) | (bits[:, :d // 2] >> 16)
    return word.reshape((n,) + _row_tile(d))


def _unpack_rows(w):
    n = w.shape[0]
    word = w.reshape(n, w.shape[1] * w.shape[2])
    lo = lax.bitcast_convert_type(word << 16, F32)
    hi = lax.bitcast_convert_type(word & jnp.uint32(0xFFFF0000), F32)
    return jnp.concatenate([lo, hi], axis=1)


def _strip_copies(n_ref, dst_ref, make_copy, max_rows):
    ne = dst_ref.shape[2]
    all_bits = [1 << s for s in range(max_rows.bit_length() - 1, -1, -1)]
    small = pl.next_power_of_2(max(1, 2 * TOP_K * max_rows // ne))
    longest = n_ref[0, 0, ne]

    def run(bits):
        def body(e, off):
            n = n_ref[0, 0, e]
            dst = dst_ref[0, 0, e]
            done = jnp.int32(0)
            for bit in bits:
                part = n & bit

                @pl.when(part != 0)
                def _():
                    make_copy(off + done, dst + done, bit).start()

                done = done + part
            return off + n

        lax.fori_loop(0, ne, body, jnp.int32(0))

    @pl.when(longest < small)
    def _():
        run([b for b in all_bits if b < small])

    @pl.when(longest >= small)
    def _():
        run(all_bits)


def _dispatch_kernel(n_ref, dst_ref, pos_ref, h_ref, xs_ref, srt_ref, sem):
    i = pl.program_id(0)
    slot = i % 2
    td = h_ref.shape[0]
    n = td * TOP_K

    @pl.when(i == 0)
    def _():
        tail = xs_ref.shape[0] - EXPERT_TILE
        srt_ref[1, 0:EXPERT_TILE] = jnp.zeros((EXPERT_TILE,) + srt_ref.shape[2:], srt_ref.dtype)
        zero_tail = pltpu.make_async_copy(srt_ref.at[1, pl.ds(0, EXPERT_TILE)],
                                          xs_ref.at[pl.ds(tail, EXPERT_TILE)], sem.at[1])
        zero_tail.start()
        zero_tail.wait()

    pos = pos_ref[...]
    rows = lax.broadcasted_iota(jnp.int32, (n, td), 0)
    hit = rows == pos[0:1]
    for k in range(1, TOP_K):
        hit = hit | (rows == pos[k:k + 1])
    perm = jnp.where(hit, 1.0, 0.0).astype(BF16)
    srt_ref[slot] = _pack_rows(_bdot(perm, h_ref[...]))

    def all_rows(s):
        return pltpu.make_async_copy(srt_ref.at[s], xs_ref.at[pl.ds(0, n)], sem.at[s])

    _strip_copies(n_ref, dst_ref,
                  lambda src, dst, size: pltpu.make_async_copy(
                      srt_ref.at[slot, pl.ds(src, size)], xs_ref.at[pl.ds(dst, size)], sem.at[slot]),
                  td)

    @pl.when(i > 0)
    def _():
        all_rows(1 - slot).wait()

    @pl.when(i == pl.num_programs(0) - 1)
    def _():
        all_rows(slot).wait()


def _dispatch(h2, pos_t, tile_n, tile_dst, td):
    t, d = h2.shape
    a = t * TOP_K
    n = td * TOP_K
    nt = t // td
    smem = lambda v: pl.BlockSpec((1, 1, v.shape[-1]), lambda i: (i, 0, 0),
                                  memory_space=pltpu.SMEM)
    return pl.pallas_call(
        _dispatch_kernel,
        grid=(nt,),
        in_specs=[smem(tile_n), smem(tile_dst),
                  pl.BlockSpec((TOP_K, td), lambda i: (0, i)),
                  pl.BlockSpec((td, d), lambda i: (i, 0))],
        out_specs=pl.BlockSpec(memory_space=pl.ANY),
        out_shape=jax.ShapeDtypeStruct((a + EXPERT_TILE,) + _row_tile(d), jnp.uint32),
        scratch_shapes=[pltpu.VMEM((2, n) + _row_tile(d), jnp.uint32),
                        pltpu.SemaphoreType.DMA((2,))],
        compiler_params=_cparams(("arbitrary",)),
        name="moe_dispatch",
    )(tile_n[:, None, :], tile_dst[:, None, :], pos_t, h2)


def _expert_kernel(exp_ref, row_ref, cls_ref, nitem_ref,
                   xs_ref, w1_ref, b1_ref, w2_ref, b2_ref, ys_ref,
                   w1b_ref, w2b_ref, xbuf_ref, ybuf_ref, sem_in, sem_out):
    w = pl.program_id(0)
    n_items = nitem_ref[0]
    last = exp_ref.shape[0] - 1
    slot = w % 2
    dff = w2_ref.shape[1]

    def fetch(item, s):
        return pltpu.make_async_copy(xs_ref.at[pl.ds(row_ref[item], EXPERT_TILE)],
                                     xbuf_ref.at[s], sem_in.at[s])

    def write(item, s, rows):
        return pltpu.make_async_copy(ybuf_ref.at[s, pl.ds(0, rows)],
                                     ys_ref.at[pl.ds(row_ref[item], rows)], sem_out.at[s])

    def for_tile_size(item, fn):
        for c, rows in enumerate(EXPERT_LAST_TILES):
            @pl.when(cls_ref[item] == c)
            def _():
                fn(rows)

    @pl.when(w == 0)
    def _():
        fetch(0, 0).start()
        ybuf_ref[1] = jnp.zeros(ybuf_ref.shape[1:], ybuf_ref.dtype)
        zero_tail = pltpu.make_async_copy(
            ybuf_ref.at[1], ys_ref.at[pl.ds(ys_ref.shape[0] - EXPERT_TILE, EXPERT_TILE)],
            sem_out.at[1])
        zero_tail.start()
        zero_tail.wait()

    @pl.when(w + 1 < n_items)
    def _():
        fetch(jnp.minimum(w + 1, last), 1 - slot).start()

    @pl.when((w == 0) | (exp_ref[w] != exp_ref[jnp.maximum(w - 1, 0)]))
    def _():
        w1b_ref[...] = w1_ref[0].astype(BF16)
        w2b_ref[...] = w2_ref[0].astype(BF16)

    @pl.when(w < n_items)
    def _():
        fetch(w, slot).wait()

        def compute(rows):
            x = _unpack_rows(xbuf_ref[slot, 0:rows]).astype(BF16)
            hb = _bdot(x, w1b_ref[...]) + b1_ref[0]
            glu = jnp.minimum(hb[:, :dff], SWIGLU_LIMIT)
            lin = jnp.clip(hb[:, dff:], -SWIGLU_LIMIT, SWIGLU_LIMIT)
            act = glu * jax.nn.sigmoid(SWIGLU_ALPHA * glu) * (lin + 1.0)
            ybuf_ref[slot, 0:rows] = _pack_rows(_bdot(act.astype(BF16), w2b_ref[...]) + b2_ref[0])

        for_tile_size(w, compute)

        @pl.when(w > 0)
        def _():
            for_tile_size(w - 1, lambda rows: write(w - 1, 1 - slot, rows).wait())

        for_tile_size(w, lambda rows: write(w, slot, rows).start())

        @pl.when(w == n_items - 1)
        def _():
            for_tile_size(w, lambda rows: write(w, slot, rows).wait())


def _experts(xs, items, w1, b1, w2, b2):
    ne, d, dff2 = w1.shape
    dff = dff2 // 2
    exp_w, row_w, cls_w, nitem = items
    per_expert = lambda w, ex, ro, cl, ni: (ex[w], 0, 0)
    tile = (EXPERT_TILE,) + _row_tile(d)
    grid_spec = pltpu.PrefetchScalarGridSpec(
        num_scalar_prefetch=4,
        grid=(exp_w.shape[0],),
        in_specs=[pl.BlockSpec(memory_space=pl.ANY),
                  pl.BlockSpec((1, d, dff2), per_expert),
                  pl.BlockSpec((1, 1, dff2), per_expert),
                  pl.BlockSpec((1, dff, d), per_expert),
                  pl.BlockSpec((1, 1, d), per_expert)],
        out_specs=pl.BlockSpec(memory_space=pl.ANY),
        scratch_shapes=[pltpu.VMEM((d, dff2), BF16), pltpu.VMEM((dff, d), BF16),
                        pltpu.VMEM((2,) + tile, jnp.uint32), pltpu.VMEM((2,) + tile, jnp.uint32),
                        pltpu.SemaphoreType.DMA((2,)), pltpu.SemaphoreType.DMA((2,))],
    )
    return pl.pallas_call(
        _expert_kernel,
        grid_spec=grid_spec,
        out_shape=jax.ShapeDtypeStruct(xs.shape, jnp.uint32),
        compiler_params=_cparams(("arbitrary",)),
        name="moe_experts",
    )(exp_w, row_w, cls_w, nitem, xs, w1, b1.reshape(ne, 1, dff2), w2, b2.reshape(ne, 1, d))


def _work_items(counts, a):
    ne = counts.shape[0]
    n_items = a // EXPERT_TILE + ne
    ends = jnp.cumsum(counts)
    starts = ends - counts
    n_full = counts // EXPERT_TILE
    rem = counts - n_full * EXPERT_TILE
    n_tiles = n_full + (rem > 0).astype(jnp.int32)
    tile_end = jnp.cumsum(n_tiles)
    tile_start = tile_end - n_tiles
    total = tile_end[-1]
    w = jnp.minimum(jnp.arange(n_items, dtype=jnp.int32), total - 1)
    e_w = jnp.sum((tile_end[None, :] <= w[:, None]).astype(jnp.int32), axis=1)
    hot = e_w[:, None] == jnp.arange(ne, dtype=jnp.int32)[None, :]
    pick = lambda v: jnp.sum(jnp.where(hot, v[None, :], 0), axis=1)
    k = w - pick(tile_start)
    row_w = (pick(starts) + k * EXPERT_TILE).astype(jnp.int32)
    rem_w = pick(rem)
    rem_cls = jnp.zeros_like(rem_w)
    for c, rows in enumerate(EXPERT_LAST_TILES[:-1]):
        rem_cls = rem_cls + (rem_w > rows).astype(jnp.int32)
    full_cls = len(EXPERT_LAST_TILES) - 1
    cls_w = jnp.where(k < pick(n_full), full_cls, rem_cls).astype(jnp.int32)
    return starts, (e_w.astype(jnp.int32), row_w, cls_w, total.reshape(1).astype(jnp.int32))


def _combine_kernel(alpha, n_ref, dst_ref, nn_ref, ndst_ref, pos_ref, gate_ref,
                    ys_ref, xm_ref, g2_ref, lg_ref, lb_ref, o_ref, srt_ref, sem):
    i = pl.program_id(0)
    last = pl.num_programs(0) - 1
    slot = i % 2
    td = xm_ref.shape[0]
    n = td * TOP_K

    def fetch(cnt_ref, from_ref, s):
        _strip_copies(cnt_ref, from_ref,
                      lambda row, src, size: pltpu.make_async_copy(
                          ys_ref.at[pl.ds(src, size)], srt_ref.at[s, pl.ds(row, size)], sem.at[s]),
                      td)

    @pl.when(i == 0)
    def _():
        fetch(n_ref, dst_ref, slot)

    @pl.when(i < last)
    def _():
        fetch(nn_ref, ndst_ref, 1 - slot)

    pltpu.make_async_copy(ys_ref.at[pl.ds(0, n)], srt_ref.at[slot], sem.at[slot]).wait()

    pos, gate = pos_ref[...], gate_ref[...]
    lanes = lax.broadcasted_iota(jnp.int32, (td, n), 1)
    wsel = jnp.where(lanes == pos[:, 0:1], gate[:, 0:1], 0.0)
    for k in range(1, TOP_K):
        wsel = wsel + jnp.where(lanes == pos[:, k:k + 1], gate[:, k:k + 1], 0.0)
    w_hi = wsel.astype(BF16)
    w_lo = (wsel - w_hi.astype(F32)).astype(BF16)
    y = _unpack_rows(srt_ref[slot]).astype(BF16)
    f = _bdot(w_hi, y) + _bdot(w_lo, y)
    o_ref[...] = _ln(alpha * xm_ref[...] + g2_ref[0] * f) * lg_ref[...] + lb_ref[...]


def _combine(ys, pos, gate, tile_n, tile_dst, x_mid, g2, lg, lb, alpha, td):
    t, d = x_mid.shape
    n = td * TOP_K
    nt = t // td
    per_b = nt // g2.shape[0]
    first = lambda v: pl.BlockSpec((1, 1, v.shape[-1]), lambda i: (0, 0, 0),
                                   memory_space=pltpu.SMEM)
    nxt = lambda v: pl.BlockSpec((1, 1, v.shape[-1]),
                                 lambda i: (jnp.minimum(i + 1, nt - 1), 0, 0),
                                 memory_space=pltpu.SMEM)
    const = lambda i: (0, 0)
    tile_n = tile_n[:, None, :]
    tile_dst = tile_dst[:, None, :]
    return pl.pallas_call(
        functools.partial(_combine_kernel, alpha),
        grid=(nt,),
        in_specs=[first(tile_n), first(tile_dst), nxt(tile_n), nxt(tile_dst),
                  pl.BlockSpec((td, TOP_K), lambda i: (i, 0)),
                  pl.BlockSpec((td, TOP_K), lambda i: (i, 0)),
                  pl.BlockSpec(memory_space=pl.ANY),
                  pl.BlockSpec((td, d), lambda i: (i, 0)),
                  pl.BlockSpec((1, 1, d), lambda i: (i // per_b, 0, 0)),
                  pl.BlockSpec((1, d), const), pl.BlockSpec((1, d), const)],
        out_specs=pl.BlockSpec((td, d), lambda i: (i, 0)),
        out_shape=jax.ShapeDtypeStruct((t, d), F32),
        scratch_shapes=[pltpu.VMEM((2, n) + _row_tile(d), jnp.uint32),
                        pltpu.SemaphoreType.DMA((2,))],
        compiler_params=_cparams(("arbitrary",)),
        name="moe_combine_ln",
    )(tile_n, tile_dst, tile_n, tile_dst, pos, gate, ys, x_mid, g2,
      lg.reshape(1, d), lb.reshape(1, d))


def _rope_tables(l):
    f32 = np.float32
    rows = l // GRID_W
    row = np.repeat(np.arange(rows, dtype=f32), GRID_W)
    col = np.tile(np.arange(GRID_W, dtype=f32), rows)
    n_freq = HEAD_DIM // 4
    inv_freq = np.power(f32(ROPE_BASE), -np.arange(n_freq, dtype=f32) / f32(n_freq)).astype(f32)
    ang_r = (row[:, None] * inv_freq).astype(f32)
    ang_c = (col[:, None] * inv_freq).astype(f32)
    zero = np.zeros_like(ang_r)
    cos_r, sin_r, cos_c, sin_c = np.cos(ang_r), np.sin(ang_r), np.cos(ang_c), np.sin(ang_c)
    cos_h = np.concatenate([cos_r, cos_r, cos_c, cos_c], axis=1)
    sa_h = np.concatenate([-sin_r, zero, -sin_c, zero], axis=1)
    sb_h = np.concatenate([zero, sin_r, zero, sin_c], axis=1)
    rep = 128 // HEAD_DIM
    return tuple(np.tile(a, (1, rep)).astype(f32) for a in (cos_h, sa_h, sb_h))


def _filter_features(l):
    f32 = np.float32
    bands = (HY_EMB_DIM - 1) // 2
    t = np.linspace(0.0, 1.0, l, dtype=f32)[:, None]
    omega = (f32(2.0 * math.pi) * np.arange(l, dtype=f32)[:, None] / f32(l)).astype(f32)
    f = np.linspace(1e-4, bands - 1, bands, dtype=f32)[None, :]
    ang = (f * omega).astype(f32)
    z = np.concatenate([t, np.cos(ang), -np.sin(ang)], axis=-1).astype(f32)
    min_decay = math.log(HY_DECAY_TARGET) / HY_FAST_DECAY_PCT
    max_decay = math.log(HY_DECAY_TARGET) / HY_SLOW_DECAY_PCT
    deltas = np.abs(np.linspace(min_decay, max_decay, HY_WIDTH, dtype=f32))
    decay = np.exp(-t * deltas).astype(f32)
    return z, decay


def kernel(x, c, ctx, c_ctx, w_mod, b_mod, w_in, attn_sink, hy_conv_w, hy_conv_b, hy_filt_w1,
           hy_filt_b1, hy_filt_w2, hy_filt_b2, hy_filt_w3, hy_skip, w_branch_attn, w_branch_hyena,
           w_out, ln1_g, ln1_b, router_w, router_b, exp_w1, exp_b1, exp_w2, exp_b2, ln2_g, ln2_b):
    depth = w_mod.shape[0]
    assert depth == 1, "only the single-layer configuration is implemented"
    b, l, d = x.shape
    t = b * l
    alpha = (2 * depth) ** 0.25
    hy_wc = 256

    n_cond = b + 1
    pad = (-n_cond) % 8
    cond = jnp.concatenate([c, c_ctx[None], jnp.zeros((pad, d), F32)], axis=0)
    mod = _modulation(cond, w_mod[0], b_mod[0])
    mod_x = mod[:b].reshape(b, 1, 6, d)
    sh1, sc1, g1, sh2, sc2, g2 = (mod_x[:, :, i] for i in range(6))
    mod_c = mod[b:b + 1].reshape(1, 1, 6, d)
    csh1, csc1 = mod_c[:, :, 0], mod_c[:, :, 1]

    w_in_b = w_in[0].astype(BF16)
    cos_t, sa_t, sb_t = (jnp.asarray(a) for a in _rope_tables(l))
    q, k, v, u_hy, gate_x = _in_projection(x, sh1, sc1, w_in_b, cos_t, sa_t, sb_t)
    k_c, v_c = _ctx_kv(ctx, csh1, csc1, w_in_b[:, K_OFF:HY_OFF])
    att = _attention(q, k, v, k_c, v_c, attn_sink[0])

    tabs_np = _dft_tables(l, hy_wc)
    tabs = (jnp.asarray(tabs_np[0]).astype(BF16), jnp.asarray(tabs_np[1]).astype(BF16)) + tuple(
        jnp.asarray(a) for a in tabs_np[2:])
    zfeat, decay = _filter_features(l)
    emb_pad = (-HY_EMB_DIM) % 128
    zfeat = np.pad(zfeat, ((0, 0), (0, emb_pad)))
    fw1 = jnp.pad(hy_filt_w1[0], ((0, emb_pad), (0, 0)))
    spectra = _hyena_filters(jnp.asarray(zfeat[0::2]), jnp.asarray(zfeat[1::2]), fw1, hy_filt_b1[0],
                             hy_filt_w2[0], hy_filt_b2[0], hy_filt_w3[0], jnp.asarray(decay[0::2]),
                             jnp.asarray(decay[1::2]), tabs, hy_wc)
    hy = _hyena(u_hy, hy_conv_w[0], hy_conv_b[0], hy_skip[0], spectra, tabs, hy_wc)

    x_mid, h2, logits_t = _merge(att.reshape(t, Q_W), hy.reshape(t, HY_WIDTH),
                                 gate_x.reshape(t, 2 * d), x.reshape(t, d), g1, sh2, sc2,
                                 w_branch_attn[0].astype(BF16), w_branch_hyena[0].astype(BF16),
                                 w_out[0].astype(BF16), ln1_g[0], ln1_b[0],
                                 jnp.transpose(router_w[0]), router_b[0], alpha)

    moe_td = 256
    gate_t, pos_t, tile_cnt, tile_base, counts = _route(logits_t, moe_td)
    starts, items = _work_items(counts[:, 0], t * TOP_K)
    tile_n = tile_cnt[:, :, 0]
    tile_n = jnp.concatenate([tile_n, jnp.max(tile_n, axis=1, keepdims=True)], axis=1)
    tile_dst = starts[None, :] + tile_base[:, :, 0]
    xs = _dispatch(h2.reshape(t, d), pos_t, tile_n, tile_dst, moe_td)
    ys = _experts(xs, items, exp_w1[0], exp_b1[0], exp_w2[0], exp_b2[0])
    out = _combine(ys, jnp.transpose(pos_t), jnp.transpose(gate_t), tile_n, tile_dst,
                   x_mid.reshape(t, d), g2, ln2_g[0], ln2_b[0], alpha, moe_td)
    return out.reshape(b, l, d)
```

```python
import functools
import math

import numpy as np
import jax
import jax.numpy as jnp
from jax import lax
from jax.experimental import pallas as pl
from jax.experimental.pallas import tpu as pltpu

F32 = jnp.float32
BF16 = jnp.bfloat16
HIGHEST = lax.Precision.HIGHEST

GRID_W = 64
N_HEADS = 8
N_KV_HEADS = 2
GQA_GROUP = N_HEADS // N_KV_HEADS
HEAD_DIM = 64
WINDOW = 128
ATTN_BLOCK = 128
ROPE_BASE = 10000.0

HY_WIDTH = 512
HY_ORDER = 2
HY_SHORT_CONV = 3
HY_EMB_DIM = 33
HY_DECAY_TARGET = 1e-2
HY_FAST_DECAY_PCT = 0.3
HY_SLOW_DECAY_PCT = 1.5

N_EXPERTS = 32
TOP_K = 4
SWIGLU_LIMIT = 7.0
SWIGLU_ALPHA = 1.702
LN_EPS = 1e-5

Q_W = N_HEADS * HEAD_DIM
KV_W = N_KV_HEADS * HEAD_DIM
K_OFF = Q_W
V_OFF = K_OFF + KV_W
HY_OFF = V_OFF + KV_W
GATE_OFF = HY_OFF + (HY_ORDER + 1) * HY_WIDTH

VMEM_LIMIT = 56 * 1024 * 1024
NEG_BIG = -1e30
MERGE_CHUNKS = 2
EXPERT_TILE = 1024
EXPERT_LAST_TILES = (256, 512, 1024)


def _cparams(sem):
    return pltpu.CompilerParams(dimension_semantics=sem, vmem_limit_bytes=VMEM_LIMIT)


def _ln(x):
    mu = jnp.mean(x, axis=-1, keepdims=True)
    xc = x - mu
    var = jnp.mean(xc * xc, axis=-1, keepdims=True)
    return xc * lax.rsqrt(var + LN_EPS)


def _bdot(a, b):
    return jnp.dot(a, b, preferred_element_type=F32)


def _fdot(a, b):
    return jnp.dot(a, b, preferred_element_type=F32, precision=HIGHEST)


def _mod_kernel(c_ref, w_ref, b_ref, o_ref):
    c = c_ref[...]
    s = c * jax.nn.sigmoid(c)
    o_ref[...] = _fdot(s, w_ref[...]) + b_ref[...]


def _modulation(cond, w, b, tn=512):
    r, d = cond.shape
    n = w.shape[1]
    return pl.pallas_call(
        _mod_kernel,
        grid=(n // tn,),
        in_specs=[pl.BlockSpec((r, d), lambda j: (0, 0)),
                  pl.BlockSpec((d, tn), lambda j: (0, j)),
                  pl.BlockSpec((1, tn), lambda j: (0, j))],
        out_specs=pl.BlockSpec((r, tn), lambda j: (0, j)),
        out_shape=jax.ShapeDtypeStruct((r, n), F32),
        compiler_params=_cparams(("arbitrary",)),
        name="modulation",
    )(cond, w, b.reshape(1, n))


def _rope(t, cos, sa, sb):
    n = t.shape[-1]
    return t * cos + pltpu.roll(t, n - 16, 1) * sa + pltpu.roll(t, 16, 1) * sb


def _inproj_kernel(x_ref, sh_ref, sc_ref, w_ref, cos_ref, sa_ref, sb_ref,
                   q_ref, k_ref, v_ref, u_ref, g_ref):
    h = _ln(x_ref[0]) * (1.0 + sc_ref[0]) + sh_ref[0]
    hb = h.astype(BF16)
    cos, sa, sb = cos_ref[...], sa_ref[...], sb_ref[...]
    scale = HEAD_DIM ** -0.5
    for j in range(Q_W // 128):
        t = _bdot(hb, w_ref[:, j * 128:(j + 1) * 128])
        q_ref[0, :, j * 128:(j + 1) * 128] = (_rope(t, cos, sa, sb) * scale).astype(BF16)
    t = _bdot(hb, w_ref[:, K_OFF:V_OFF])
    k_ref[0] = _rope(t, cos, sa, sb).astype(BF16)
    v_ref[0] = _bdot(hb, w_ref[:, V_OFF:HY_OFF]).astype(BF16)
    for j in range((GATE_OFF - HY_OFF) // 512):
        u_ref[0, :, j * 512:(j + 1) * 512] = _bdot(hb, w_ref[:, HY_OFF + j * 512:HY_OFF + (j + 1) * 512])
    n_gate = w_ref.shape[1] - GATE_OFF
    for j in range(n_gate // 512):
        g_ref[0, :, j * 512:(j + 1) * 512] = jax.nn.sigmoid(_bdot(
            hb, w_ref[:, GATE_OFF + j * 512:GATE_OFF + (j + 1) * 512])).astype(BF16)


def _in_projection(x, sh, sc, w_in_b, cos_t, sa_t, sb_t, tm=512):
    b, l, d = x.shape
    in_w = w_in_b.shape[1]
    hy_w = GATE_OFF - HY_OFF
    g_w = in_w - GATE_OFF
    row = lambda bi, i: (bi, i, 0)
    vec = lambda bi, i: (bi, 0, 0)
    tab = lambda bi, i: (i, 0)
    return pl.pallas_call(
        _inproj_kernel,
        grid=(b, l // tm),
        in_specs=[pl.BlockSpec((1, tm, d), row),
                  pl.BlockSpec((1, 1, d), vec),
                  pl.BlockSpec((1, 1, d), vec),
                  pl.BlockSpec((d, in_w), lambda bi, i: (0, 0)),
                  pl.BlockSpec((tm, 128), tab),
                  pl.BlockSpec((tm, 128), tab),
                  pl.BlockSpec((tm, 128), tab)],
        out_specs=[pl.BlockSpec((1, tm, Q_W), row),
                   pl.BlockSpec((1, tm, KV_W), row),
                   pl.BlockSpec((1, tm, KV_W), row),
                   pl.BlockSpec((1, tm, hy_w), row),
                   pl.BlockSpec((1, tm, g_w), row)],
        out_shape=[jax.ShapeDtypeStruct((b, l, Q_W), BF16),
                   jax.ShapeDtypeStruct((b, l, KV_W), BF16),
                   jax.ShapeDtypeStruct((b, l, KV_W), BF16),
                   jax.ShapeDtypeStruct((b, l, hy_w), F32),
                   jax.ShapeDtypeStruct((b, l, g_w), BF16)],
        compiler_params=_cparams(("arbitrary", "arbitrary")),
        name="in_projection",
    )(x, sh, sc, w_in_b, cos_t, sa_t, sb_t)


def _ctx_kv_kernel(x_ref, sh_ref, sc_ref, w_ref, k_ref, v_ref):
    h = _ln(x_ref[0]) * (1.0 + sc_ref[0]) + sh_ref[0]
    kv = _bdot(h.astype(BF16), w_ref[...])
    k_ref[0] = kv[:, :KV_W].astype(BF16)
    v_ref[0] = kv[:, KV_W:].astype(BF16)


def _ctx_kv(ctx, sh, sc, w_kv_b):
    b, c, d = ctx.shape
    row = lambda bi: (bi, 0, 0)
    return pl.pallas_call(
        _ctx_kv_kernel,
        grid=(b,),
        in_specs=[pl.BlockSpec((1, c, d), row),
                  pl.BlockSpec((1, 1, d), lambda bi: (0, 0, 0)),
                  pl.BlockSpec((1, 1, d), lambda bi: (0, 0, 0)),
                  pl.BlockSpec((d, 2 * KV_W), lambda bi: (0, 0))],
        out_specs=[pl.BlockSpec((1, c, KV_W), row), pl.BlockSpec((1, c, KV_W), row)],
        out_shape=[jax.ShapeDtypeStruct((b, c, KV_W), BF16),
                   jax.ShapeDtypeStruct((b, c, KV_W), BF16)],
        compiler_params=_cparams(("arbitrary",)),
        name="ctx_kv",
    )(ctx, sh, sc, w_kv_b)


def _attn_kernel(sink_ref, q_ref, k_ref, v_ref, kc_ref, vc_ref, o_ref):
    for qb in range(q_ref.shape[1] // ATTN_BLOCK):
        rows = slice(qb * ATTN_BLOCK, (qb + 1) * ATTN_BLOCK)
        j = pl.program_id(1) * (q_ref.shape[1] // ATTN_BLOCK) + qb
        o_ref[0, rows] = _attn_block(j, sink_ref, q_ref.at[0, rows], k_ref, v_ref, kc_ref, vc_ref)


def _attn_block(j, sink_ref, q_ref, k_ref, v_ref, kc_ref, vc_ref):
    l = k_ref.shape[1]
    span = ATTN_BLOCK + 2 * WINDOW
    q0 = j * ATTN_BLOCK
    start = pl.multiple_of(jnp.clip(q0 - WINDOW, 0, l - span), ATTN_BLOCK)
    rows = GQA_GROUP * ATTN_BLOCK
    r_i = lax.broadcasted_iota(jnp.int32, (rows, span), 0)
    qpos = q0 + r_i % ATTN_BLOCK
    kpos = start + lax.broadcasted_iota(jnp.int32, (rows, span), 1)
    band = jnp.abs(kpos - qpos) <= WINDOW
    head_of_row = lax.broadcasted_iota(jnp.int32, (rows, 1), 0) // ATTN_BLOCK
    dn = (((1,), (1,)), ((), ()))
    outs = []
    for kv in range(N_KV_HEADS):
        ks = slice(kv * HEAD_DIM, (kv + 1) * HEAD_DIM)
        kl = k_ref[0, pl.ds(start, span), ks]
        vl = v_ref[0, pl.ds(start, span), ks]
        kc = kc_ref[0, :, ks]
        vc = vc_ref[0, :, ks]
        heads = [kv * GQA_GROUP + g for g in range(GQA_GROUP)]
        qg = jnp.concatenate([q_ref[:, h * HEAD_DIM:(h + 1) * HEAD_DIM] for h in heads], axis=0)
        sink = jnp.zeros((rows, 1), F32)
        for g, h in enumerate(heads):
            sink = jnp.where(head_of_row == g, sink_ref[h], sink)
        s_loc = lax.dot_general(qg, kl, dn, preferred_element_type=F32)
        s_loc = jnp.where(band, s_loc, NEG_BIG)
        s_ctx = lax.dot_general(qg, kc, dn, preferred_element_type=F32)
        blocks = [s[:, c:c + 128] for s in (s_loc, s_ctx) for c in range(0, s.shape[1], 128)]
        folded = functools.reduce(jnp.maximum, blocks)
        m = jnp.maximum(jnp.max(folded, axis=1, keepdims=True), sink)
        p_loc = jnp.exp(s_loc - m).astype(BF16)
        p_ctx = jnp.exp(s_ctx - m).astype(BF16)
        ones_l = jnp.ones((span, HEAD_DIM), BF16)
        ones_c = jnp.ones((kc.shape[0], HEAD_DIM), BF16)
        pv = (_bdot(p_loc, jnp.concatenate([vl, ones_l], axis=1))
              + _bdot(p_ctx, jnp.concatenate([vc, ones_c], axis=1)))
        den = pv[:, HEAD_DIM:HEAD_DIM + 1] + jnp.exp(sink - m)
        o = pv[:, :HEAD_DIM] / den
        outs.extend(o[g * ATTN_BLOCK:(g + 1) * ATTN_BLOCK] for g in range(GQA_GROUP))
    return jnp.concatenate(outs, axis=1).astype(BF16)


def _attention(q, k, v, kc, vc, sink, q_blocks=4):
    b, l, _ = q.shape
    c = kc.shape[1]
    tq = q_blocks * ATTN_BLOCK
    full = lambda bi, j, s: (bi, 0, 0)
    grid_spec = pltpu.PrefetchScalarGridSpec(
        num_scalar_prefetch=1,
        grid=(b, l // tq),
        in_specs=[pl.BlockSpec((1, tq, Q_W), lambda bi, j, s: (bi, j, 0)),
                  pl.BlockSpec((1, l, KV_W), full),
                  pl.BlockSpec((1, l, KV_W), full),
                  pl.BlockSpec((1, c, KV_W), full),
                  pl.BlockSpec((1, c, KV_W), full)],
        out_specs=pl.BlockSpec((1, tq, Q_W), lambda bi, j, s: (bi, j, 0)),
    )
    return pl.pallas_call(
        _attn_kernel,
        grid_spec=grid_spec,
        out_shape=jax.ShapeDtypeStruct((b, l, Q_W), BF16),
        compiler_params=_cparams(("arbitrary", "arbitrary")),
        name="window_attention",
    )(sink, q, k, v, kc, vc)


def _dft_tables(l, wc):
    h = l // 2
    idx = np.arange(h, dtype=np.int64)
    ang = 2.0 * np.pi * ((idx[:, None] * idx[None, :]) % l).astype(np.float64) / l
    cm = np.cos(ang).astype(np.float32)
    sm = np.sin(ang).astype(np.float32)
    tw = 2.0 * np.pi * idx.astype(np.float64) / (2 * l)
    ones = np.ones((1, wc), np.float32)
    wr = np.cos(tw).astype(np.float32)[:, None] * ones
    wi = (-np.sin(tw)).astype(np.float32)[:, None] * ones
    sgn = np.where(idx % 2 == 0, 1.0, -1.0).astype(np.float32)[:, None] * ones
    wk = np.where(idx == 0, 1.0 / (2 * l), 2.0 / (2 * l)).astype(np.float32)[:, None] * ones
    return cm, sm, wr, wi, sgn, wk


def _half_spectrum(se, so, cm, sm, wr, wi, sgn):
    seb, sob = se.astype(BF16), so.astype(BF16)
    ce, ss_e = _bdot(cm, seb), _bdot(sm, seb)
    co, ss_o = _bdot(cm, sob), _bdot(sm, sob)
    vr = wr * co + wi * ss_o
    vi = wi * co - wr * ss_o
    zar, zai = ce + vr, vi - ss_e
    zbr, zbi = ce - vr, -ss_e - vi
    e_ny = jnp.sum(sgn * se, axis=0, keepdims=True)
    o_ny = jnp.sum(sgn * so, axis=0, keepdims=True)
    return zar, zai, zbr, zbi, e_ny, -o_ny


def _filter_kernel(ze_ref, zo_ref, w1_ref, b1_ref, w2_ref, b2_ref, w3f_ref, w3b_ref,
                   de_ref, do_ref, cm_ref, sm_ref, wr_ref, wi_ref, sgn_ref, wk_ref,
                   har_ref, hai_ref, hbr_ref, hbi_ref, hny_ref, ae_ref, ao_ref):
    cm, sm = cm_ref[...], sm_ref[...]
    wr, wi, sgn, wk = wr_ref[...], wi_ref[...], sgn_ref[...], wk_ref[...]
    h = cm.shape[0]

    @pl.when((pl.program_id(0) == 0) & (pl.program_id(1) == 0))
    def _():
        for z_ref, a_ref in ((ze_ref, ae_ref), (zo_ref, ao_ref)):
            a = jnp.sin(_fdot(z_ref[...], w1_ref[...]) + b1_ref[...])
            a_ref[...] = jnp.sin(_fdot(a, w2_ref[...]) + b2_ref[...])

    def taps(a_ref, w3_ref, d_ref):
        return _fdot(a_ref[...], w3_ref[0]) * d_ref[...]

    fe, fo = taps(ae_ref, w3f_ref, de_ref), taps(ao_ref, w3f_ref, do_ref)
    be, bo = taps(ae_ref, w3b_ref, de_ref), taps(ao_ref, w3b_ref, do_ref)
    row = lax.broadcasted_iota(jnp.int32, be.shape, 0)
    be = jnp.where(row == 0, 0.0, be)
    far, fai, fbr, fbi, fnr, fni = _half_spectrum(fe, fo, cm, sm, wr, wi, sgn)
    bar, bai, bbr, bbi, bnr, bni = _half_spectrum(be, bo, cm, sm, wr, wi, sgn)
    har_ref[0] = wk * (far + bar)
    hai_ref[0] = wk * (fai - bai)
    hbr_ref[0] = wk * (fbr + bbr)
    hbi_ref[0] = wk * (fbi - bbi)
    ny_scale = 2.0 / (4 * h)
    nr = ny_scale * (fnr + bnr)
    ni = ny_scale * (fni - bni)
    rows = lax.broadcasted_iota(jnp.int32, (8, nr.shape[1]), 0)
    hny_ref[0] = jnp.where(rows == 0, nr, jnp.where(rows == 1, ni, 0.0))


def _hyena_filters(zfe, zfo, w1, b1, w2, b2, w3, dec_e, dec_o, tabs, wc):
    cm, sm, wr, wi, sgn, wk = tabs
    h = cm.shape[0]
    emb = zfe.shape[1]
    hid = w2.shape[0]
    nblk = HY_WIDTH // wc
    w3r = w3.reshape(hid, HY_ORDER * 2, HY_WIDTH).transpose(1, 0, 2)
    const = lambda o, cb: (0, 0)
    chan = lambda o, cb: (0, cb)
    out_spec = pl.BlockSpec((1, h, wc), lambda o, cb: (o, 0, cb))
    out_shape = jax.ShapeDtypeStruct((HY_ORDER, h, HY_WIDTH), F32)
    return pl.pallas_call(
        _filter_kernel,
        grid=(HY_ORDER, nblk),
        in_specs=[pl.BlockSpec((h, emb), const), pl.BlockSpec((h, emb), const),
                  pl.BlockSpec((emb, hid), const), pl.BlockSpec((1, hid), const),
                  pl.BlockSpec((hid, hid), const), pl.BlockSpec((1, hid), const),
                  pl.BlockSpec((1, hid, wc), lambda o, cb: (2 * o, 0, cb)),
                  pl.BlockSpec((1, hid, wc), lambda o, cb: (2 * o + 1, 0, cb)),
                  pl.BlockSpec((h, wc), chan), pl.BlockSpec((h, wc), chan),
                  pl.BlockSpec((h, h), const), pl.BlockSpec((h, h), const),
                  pl.BlockSpec((h, wc), const), pl.BlockSpec((h, wc), const),
                  pl.BlockSpec((h, wc), const), pl.BlockSpec((h, wc), const)],
        out_specs=[out_spec, out_spec, out_spec, out_spec,
                   pl.BlockSpec((1, 8, wc), lambda o, cb: (o, 0, cb))],
        out_shape=[out_shape, out_shape, out_shape, out_shape,
                   jax.ShapeDtypeStruct((HY_ORDER, 8, HY_WIDTH), F32)],
        scratch_shapes=[pltpu.VMEM((h, hid), F32), pltpu.VMEM((h, hid), F32)],
        compiler_params=_cparams(("arbitrary", "arbitrary")),
        name="hyena_filter_spectrum",
    )(zfe, zfo, w1, b1.reshape(1, hid), w2, b2.reshape(1, hid), w3r, w3r,
      dec_e, dec_o, cm, sm, wr, wi, sgn, wk)


def _long_conv(ze, zo, har, hai, hbr, hbi, hny, cm, sm, wr, wi, sgn):
    zar, zai, zbr, zbi, znr, zni = _half_spectrum(ze, zo, cm, sm, wr, wi, sgn)
    yar, yai = zar * har - zai * hai, zar * hai + zai * har
    ybr, ybi = zbr * hbr - zbi * hbi, zbr * hbi + zbi * hbr
    dr, di = yar - ybr, yai - ybi
    sar, sai = (yar + ybr).astype(BF16), (yai + ybi).astype(BF16)
    sbr, sbi = (dr * wr + di * wi).astype(BF16), (di * wr - dr * wi).astype(BF16)
    hnr, hni = hny[0:1], hny[1:2]
    ynr = znr * hnr - zni * hni
    yni = znr * hni + zni * hnr
    ye = _bdot(cm, sar) - _bdot(sm, sai) + sgn * ynr
    yo = _bdot(cm, sbr) - _bdot(sm, sbi) - sgn * yni
    return ye, yo


def _hyena_kernel(*refs):
    nsub = (len(refs) - 18) // 4
    u_refs = [refs[s * nsub:(s + 1) * nsub] for s in range(HY_ORDER + 1)]
    rest = refs[(HY_ORDER + 1) * nsub:]
    (cw0_ref, cw1_ref, cw2_ref, cb0_ref, cb1_ref, cb2_ref, skip_ref,
     har_ref, hai_ref, hbr_ref, hbi_ref, hny_ref,
     cm_ref, sm_ref, wr_ref, wi_ref, sgn_ref, o_ref) = rest[:18]
    il_refs = rest[18:]
    h = cm_ref.shape[0]
    cm, sm = cm_ref[...], sm_ref[...]
    wr, wi, sgn = wr_ref[...], wi_ref[...], sgn_ref[...]
    row = lax.broadcasted_iota(jnp.int32, (h, wr.shape[1]), 0)

    def short_conv(u_slabs, cw_ref, cb_ref):
        ue = jnp.concatenate([r[0, pl.ds(0, h, stride=2), :] for r in u_slabs], axis=1)
        uo = jnp.concatenate([r[0, pl.ds(1, h, stride=2), :] for r in u_slabs], axis=1)
        w0, w1, w2 = cw_ref[0:1], cw_ref[1:2], cw_ref[2:3]
        uo_prev = jnp.where(row == 0, 0.0, pltpu.roll(uo, 1, 0))
        ue_next = jnp.where(row == h - 1, 0.0, pltpu.roll(ue, h - 1, 0))
        cb = cb_ref[...]
        se = cb + uo_prev * w0 + ue * w1 + uo * w2
        so = cb + ue * w0 + uo * w1 + ue_next * w2
        return se, so

    x_streams = [short_conv(u_refs[0], cw0_ref, cb0_ref), short_conv(u_refs[1], cw1_ref, cb1_ref)]
    ze, zo = short_conv(u_refs[2], cw2_ref, cb2_ref)
    for o in range(HY_ORDER):
        ce, co = _long_conv(ze, zo, har_ref[o], hai_ref[o], hbr_ref[o], hbi_ref[o], hny_ref[o],
                            cm, sm, wr, wi, sgn)
        sk = skip_ref[o:o + 1]
        xe, xo = x_streams[o]
        ze = xe * (ce + sk * ze)
        zo = xo * (co + sk * zo)
    for j, il in enumerate(il_refs):
        il[pl.ds(0, h, stride=2), :] = ze[:, j * 128:(j + 1) * 128]
        il[pl.ds(1, h, stride=2), :] = zo[:, j * 128:(j + 1) * 128]
        o_ref[0, :, j * 128:(j + 1) * 128] = il[...]


def _hyena(u, conv_w, conv_b, skip, spectra, tabs, wc):
    b, l, _ = u.shape
    har, hai, hbr, hbi, hny = spectra
    cm, sm, wr, wi, sgn, _ = tabs
    h = l // 2
    nblk = HY_WIDTH // wc
    nsub = wc // 128
    conv_b = conv_b.reshape(1, -1)

    def ublk(s, j):
        return pl.BlockSpec((1, l, 128), lambda cb, bi: (bi, 0, (cb + s * nblk) * nsub + j))

    def wblk(r, s):
        return pl.BlockSpec((r, wc), lambda cb, bi: (0, cb + s * nblk))

    const = lambda cb, bi: (0, 0)
    spec = pl.BlockSpec((HY_ORDER, h, wc), lambda cb, bi: (0, 0, cb))
    return pl.pallas_call(
        _hyena_kernel,
        grid=(nblk, b),
        in_specs=[ublk(s, j) for s in range(HY_ORDER + 1) for j in range(nsub)] + [
                  wblk(HY_SHORT_CONV, 0), wblk(HY_SHORT_CONV, 1), wblk(HY_SHORT_CONV, 2),
                  wblk(1, 0), wblk(1, 1), wblk(1, 2),
                  pl.BlockSpec((HY_ORDER, wc), lambda cb, bi: (0, cb)),
                  spec, spec, spec, spec,
                  pl.BlockSpec((HY_ORDER, 8, wc), lambda cb, bi: (0, 0, cb)),
                  pl.BlockSpec((h, h), const), pl.BlockSpec((h, h), const),
                  pl.BlockSpec((h, wc), const), pl.BlockSpec((h, wc), const),
                  pl.BlockSpec((h, wc), const)],
        out_specs=pl.BlockSpec((1, l, wc), lambda cb, bi: (bi, 0, cb)),
        out_shape=jax.ShapeDtypeStruct((b, l, HY_WIDTH), F32),
        scratch_shapes=[pltpu.VMEM((l, 128), F32) for _ in range(nsub)],
        compiler_params=_cparams(("arbitrary", "arbitrary")),
        name="hyena_long_conv",
    )(*([u] * ((HY_ORDER + 1) * nsub)), conv_w, conv_w, conv_w, conv_b, conv_b, conv_b, skip,
      har, hai, hbr, hbi, hny, cm, sm, wr, wi, sgn)


def _merge_kernel(alpha, att_ref, hy_ref, gate_ref, x_ref, g1_ref, sh2_ref, sc2_ref,
                  wba_ref, wbh_ref, wo_ref, l1g_ref, l1b_ref, rw_ref, rb_ref,
                  xmid_ref, h2_ref, logit_ref, res_ref):
    d = x_ref.shape[1]

    @pl.when(pl.program_id(0) == 0)
    def _():
        res_ref[...] = jnp.zeros_like(res_ref)

    rw = rw_ref[...]
    r_hi = rw.astype(BF16)
    r_lo = (rw - r_hi.astype(F32)).astype(BF16)
    dn = (((1,), (1,)), ((), ()))
    nt_dot = lambda p, q: lax.dot_general(p, q, dn, preferred_element_type=F32)
    tm = x_ref.shape[0]
    rc = tm // MERGE_CHUNKS
    for r in (slice(c * rc, (c + 1) * rc) for c in range(MERGE_CHUNKS)):
        xm = _ln(res_ref[r]) * l1g_ref[...] + l1b_ref[...]
        xmid_ref[r] = xm
        h2 = _ln(xm) * (1.0 + sc2_ref[0]) + sh2_ref[0]
        h_hi = h2.astype(BF16)
        h2_ref[r] = h_hi
        h_lo = (h2 - h_hi.astype(F32)).astype(BF16)
        logit_ref[:, r] = nt_dot(r_hi, h_hi) + nt_dot(r_hi, h_lo) + nt_dot(r_lo, h_hi) + rb_ref[...]

        a = _bdot(att_ref[r], wba_ref[...])
        hh = _bdot(hy_ref[r].astype(BF16), wbh_ref[...])
        ga = gate_ref[r, :d].astype(F32)
        gh = gate_ref[r, d:].astype(F32)
        y = _bdot((ga * a + gh * hh).astype(BF16), wo_ref[...])
        res_ref[r] = alpha * x_ref[r] + g1_ref[0] * y


def _merge(att, hy, gate, x, g1, sh2, sc2, wba, wbh, wo, l1g, l1b, rw_t, rb, alpha, tm=512):
    t, d = x.shape
    ne = rw_t.shape[0]
    nt = t // tm
    per_b = nt // g1.shape[0]
    t1 = lambda i: jnp.minimum(i, nt - 1)
    t2 = lambda i: jnp.maximum(i - 1, 0)
    row1 = lambda i: (t1(i), 0)
    row2 = lambda i: (t2(i), 0)
    const = lambda i: (0, 0)
    return pl.pallas_call(
        functools.partial(_merge_kernel, alpha),
        grid=(nt + 1,),
        in_specs=[pl.BlockSpec((tm, Q_W), row1),
                  pl.BlockSpec((tm, HY_WIDTH), row1),
                  pl.BlockSpec((tm, 2 * d), row1),
                  pl.BlockSpec((tm, d), row1),
                  pl.BlockSpec((1, 1, d), lambda i: (t1(i) // per_b, 0, 0)),
                  pl.BlockSpec((1, 1, d), lambda i: (t2(i) // per_b, 0, 0)),
                  pl.BlockSpec((1, 1, d), lambda i: (t2(i) // per_b, 0, 0)),
                  pl.BlockSpec((Q_W, d), const), pl.BlockSpec((HY_WIDTH, d), const),
                  pl.BlockSpec((d, d), const),
                  pl.BlockSpec((1, d), const), pl.BlockSpec((1, d), const),
                  pl.BlockSpec((ne, d), const), pl.BlockSpec((ne, 1), const)],
        out_specs=[pl.BlockSpec((tm, d), row2), pl.BlockSpec((tm, d), row2),
                   pl.BlockSpec((ne, tm), lambda i: (0, t2(i)))],
        out_shape=[jax.ShapeDtypeStruct((t, d), F32),
                   jax.ShapeDtypeStruct((t, d), BF16),
                   jax.ShapeDtypeStruct((ne, t), F32)],
        scratch_shapes=[pltpu.VMEM((tm, d), F32)],
        compiler_params=_cparams(("arbitrary",)),
        name="merge_ln_router",
    )(att, hy, gate, x, g1, sh2, sc2, wba, wbh, wo, l1g.reshape(1, d), l1b.reshape(1, d),
      rw_t, rb.reshape(ne, 1))


def _route_kernel(lg_ref, gate_ref, pos_ref, tcnt_ref, tbase_ref, cnt_ref, run_ref):
    @pl.when(pl.program_id(0) == 0)
    def _():
        run_ref[...] = jnp.zeros_like(run_ref)

    tiles = tcnt_ref.shape[0]
    tr = lg_ref.shape[1] // tiles
    for s in range(tiles):
        cols = slice(s * tr, (s + 1) * tr)
        gate, pos, cnt = _route_tile(lg_ref[:, cols])
        gate_ref[:, cols] = gate
        pos_ref[:, cols] = pos
        tcnt_ref[s] = cnt.astype(jnp.int32)
        tbase_ref[s] = run_ref[...].astype(jnp.int32)
        run_ref[...] = run_ref[...] + cnt
    cnt_ref[...] = run_ref[...].astype(jnp.int32)


def _route_tile(lg):
    ne, tr = lg.shape
    sub = lax.broadcasted_iota(jnp.int32, (ne, tr), 0)
    work = lg
    vals, hots = [], []
    for _ in range(TOP_K):
        m = jnp.max(work, axis=0, keepdims=True)
        idx = jnp.min(jnp.where(work == m, sub, ne), axis=0, keepdims=True)
        hot = sub == idx
        vals.append(m)
        hots.append(hot)
        work = jnp.where(hot, -jnp.inf, work)
    exps = [jnp.exp(v - vals[0]) for v in vals]
    den = exps[0] + exps[1] + exps[2] + exps[3]
    member = jnp.zeros((ne, tr), F32)
    for hot in hots:
        member = member + jnp.where(hot, 1.0, 0.0)
    r_i = lax.broadcasted_iota(jnp.int32, (tr, tr), 0)
    c_i = lax.broadcasted_iota(jnp.int32, (tr, tr), 1)
    earlier = jnp.where(r_i < c_i, 1.0, 0.0).astype(BF16)
    prefix = _bdot(member.astype(BF16), earlier)
    cnt = jnp.broadcast_to(jnp.sum(member, axis=1, keepdims=True), (ne, 128))
    e_r = lax.broadcasted_iota(jnp.int32, (ne, ne), 0)
    e_c = lax.broadcasted_iota(jnp.int32, (ne, ne), 1)
    lower = jnp.where(e_c < e_r, 1.0, 0.0).astype(BF16)
    off = _bdot(lower, cnt.astype(BF16))
    base = off[:, 0:1] + prefix
    sub_k = lax.broadcasted_iota(jnp.int32, (TOP_K, tr), 0)
    gate = jnp.zeros((TOP_K, tr), F32)
    pos = jnp.zeros((TOP_K, tr), F32)
    for k in range(TOP_K):
        pk = jnp.sum(jnp.where(hots[k], base, 0.0), axis=0, keepdims=True)
        gate = jnp.where(sub_k == k, exps[k] / den, gate)
        pos = jnp.where(sub_k == k, pk, pos)
    return gate, pos.astype(jnp.int32), cnt


def _route(logits_t, tr, tiles_per_step=4):
    ne, t = logits_t.shape
    nt = t // tr
    tb = tr * tiles_per_step
    tok = lambda i: (0, i)
    tile = lambda i: (i, 0, 0)
    return pl.pallas_call(
        _route_kernel,
        grid=(nt // tiles_per_step,),
        in_specs=[pl.BlockSpec((ne, tb), tok)],
        out_specs=[pl.BlockSpec((TOP_K, tb), tok), pl.BlockSpec((TOP_K, tb), tok),
                   pl.BlockSpec((tiles_per_step, ne, 128), tile),
                   pl.BlockSpec((tiles_per_step, ne, 128), tile),
                   pl.BlockSpec((ne, 128), lambda i: (0, 0))],
        out_shape=[jax.ShapeDtypeStruct((TOP_K, t), F32),
                   jax.ShapeDtypeStruct((TOP_K, t), jnp.int32),
                   jax.ShapeDtypeStruct((nt, ne, 128), jnp.int32),
                   jax.ShapeDtypeStruct((nt, ne, 128), jnp.int32),
                   jax.ShapeDtypeStruct((ne, 128), jnp.int32)],
        scratch_shapes=[pltpu.VMEM((ne, 128), F32)],
        compiler_params=_cparams(("arbitrary",)),
        name="route_topk",
    )(logits_t)


def _row_tile(d):
    return (d // 256, 128)


def _pack_rows(v):
    n, d = v.shape
    bits = lax.bitcast_convert_type(v.astype(BF16).astype(F32), jnp.uint32)
    word = (bits[:, d // 2:] & jnp.uint32(0xFFFF0000)) | (bits[:, :d // 2] >> 16)
    return word.reshape((n,) + _row_tile(d))


def _unpack_rows(w):
    n = w.shape[0]
    word = w.reshape(n, w.shape[1] * w.shape[2])
    lo = lax.bitcast_convert_type(word << 16, F32)
    hi = lax.bitcast_convert_type(word & jnp.uint32(0xFFFF0000), F32)
    return jnp.concatenate([lo, hi], axis=1)


def _strip_copies(n_ref, dst_ref, make_copy, max_rows):
    ne = dst_ref.shape[2]
    all_bits = [1 << s for s in range(max_rows.bit_length() - 1, -1, -1)]
    small = pl.next_power_of_2(max(1, 2 * TOP_K * max_rows // ne))
    longest = n_ref[0, 0, ne]

    def run(bits):
        def body(e, off):
            n = n_ref[0, 0, e]
            dst = dst_ref[0, 0, e]
            done = jnp.int32(0)
            for bit in bits:
                part = n & bit

                @pl.when(part != 0)
                def _():
                    make_copy(off + done, dst + done, bit).start()

                done = done + part
            return off + n

        lax.fori_loop(0, ne, body, jnp.int32(0))

    @pl.when(longest < small)
    def _():
        run([b for b in all_bits if b < small])

    @pl.when(longest >= small)
    def _():
        run(all_bits)


def _dispatch_kernel(n_ref, dst_ref, pos_ref, h_ref, xs_ref, srt_ref, sem):
    i = pl.program_id(0)
    slot = i % 2
    td = h_ref.shape[0]
    n = td * TOP_K

    @pl.when(i == 0)
    def _():
        tail = xs_ref.shape[0] - EXPERT_TILE
        srt_ref[1, 0:EXPERT_TILE] = jnp.zeros((EXPERT_TILE,) + srt_ref.shape[2:], srt_ref.dtype)
        zero_tail = pltpu.make_async_copy(srt_ref.at[1, pl.ds(0, EXPERT_TILE)],
                                          xs_ref.at[pl.ds(tail, EXPERT_TILE)], sem.at[1])
        zero_tail.start()
        zero_tail.wait()

    pos = pos_ref[...]
    rows = lax.broadcasted_iota(jnp.int32, (n, td), 0)
    hit = rows == pos[0:1]
    for k in range(1, TOP_K):
        hit = hit | (rows == pos[k:k + 1])
    perm = jnp.where(hit, 1.0, 0.0).astype(BF16)
    srt_ref[slot] = _pack_rows(_bdot(perm, h_ref[...]))

    def all_rows(s):
        return pltpu.make_async_copy(srt_ref.at[s], xs_ref.at[pl.ds(0, n)], sem.at[s])

    _strip_copies(n_ref, dst_ref,
                  lambda src, dst, size: pltpu.make_async_copy(
                      srt_ref.at[slot, pl.ds(src, size)], xs_ref.at[pl.ds(dst, size)], sem.at[slot]),
                  td)

    @pl.when(i > 0)
    def _():
        all_rows(1 - slot).wait()

    @pl.when(i == pl.num_programs(0) - 1)
    def _():
        all_rows(slot).wait()


def _dispatch(h2, pos_t, tile_n, tile_dst, td):
    t, d = h2.shape
    a = t * TOP_K
    n = td * TOP_K
    nt = t // td
    smem = lambda v: pl.BlockSpec((1, 1, v.shape[-1]), lambda i: (i, 0, 0),
                                  memory_space=pltpu.SMEM)
    return pl.pallas_call(
        _dispatch_kernel,
        grid=(nt,),
        in_specs=[smem(tile_n), smem(tile_dst),
                  pl.BlockSpec((TOP_K, td), lambda i: (0, i)),
                  pl.BlockSpec((td, d), lambda i: (i, 0))],
        out_specs=pl.BlockSpec(memory_space=pl.ANY),
        out_shape=jax.ShapeDtypeStruct((a + EXPERT_TILE,) + _row_tile(d), jnp.uint32),
        scratch_shapes=[pltpu.VMEM((2, n) + _row_tile(d), jnp.uint32),
                        pltpu.SemaphoreType.DMA((2,))],
        compiler_params=_cparams(("arbitrary",)),
        name="moe_dispatch",
    )(tile_n[:, None, :], tile_dst[:, None, :], pos_t, h2)


def _expert_kernel(exp_ref, row_ref, cls_ref, nitem_ref,
                   xs_ref, w1_ref, b1_ref, w2_ref, b2_ref, ys_ref,
                   w1b_ref, w2b_ref, xbuf_ref, ybuf_ref, sem_in, sem_out):
    w = pl.program_id(0)
    n_items = nitem_ref[0]
    last = exp_ref.shape[0] - 1
    slot = w % 2
    dff = w2_ref.shape[1]

    def fetch(item, s):
        return pltpu.make_async_copy(xs_ref.at[pl.ds(row_ref[item], EXPERT_TILE)],
                                     xbuf_ref.at[s], sem_in.at[s])

    def write(item, s, rows):
        return pltpu.make_async_copy(ybuf_ref.at[s, pl.ds(0, rows)],
                                     ys_ref.at[pl.ds(row_ref[item], rows)], sem_out.at[s])

    def for_tile_size(item, fn):
        for c, rows in enumerate(EXPERT_LAST_TILES):
            @pl.when(cls_ref[item] == c)
            def _():
                fn(rows)

    @pl.when(w == 0)
    def _():
        fetch(0, 0).start()
        ybuf_ref[1] = jnp.zeros(ybuf_ref.shape[1:], ybuf_ref.dtype)
        zero_tail = pltpu.make_async_copy(
            ybuf_ref.at[1], ys_ref.at[pl.ds(ys_ref.shape[0] - EXPERT_TILE, EXPERT_TILE)],
            sem_out.at[1])
        zero_tail.start()
        zero_tail.wait()

    @pl.when(w + 1 < n_items)
    def _():
        fetch(jnp.minimum(w + 1, last), 1 - slot).start()

    @pl.when((w == 0) | (exp_ref[w] != exp_ref[jnp.maximum(w - 1, 0)]))
    def _():
        w1b_ref[...] = w1_ref[0].astype(BF16)
        w2b_ref[...] = w2_ref[0].astype(BF16)

    @pl.when(w < n_items)
    def _():
        fetch(w, slot).wait()

        def compute(rows):
            x = _unpack_rows(xbuf_ref[slot, 0:rows]).astype(BF16)
            hb = _bdot(x, w1b_ref[...]) + b1_ref[0]
            glu = jnp.minimum(hb[:, :dff], SWIGLU_LIMIT)
            lin = jnp.clip(hb[:, dff:], -SWIGLU_LIMIT, SWIGLU_LIMIT)
            act = glu * jax.nn.sigmoid(SWIGLU_ALPHA * glu) * (lin + 1.0)
            ybuf_ref[slot, 0:rows] = _pack_rows(_bdot(act.astype(BF16), w2b_ref[...]) + b2_ref[0])

        for_tile_size(w, compute)

        @pl.when(w > 0)
        def _():
            for_tile_size(w - 1, lambda rows: write(w - 1, 1 - slot, rows).wait())

        for_tile_size(w, lambda rows: write(w, slot, rows).start())

        @pl.when(w == n_items - 1)
        def _():
            for_tile_size(w, lambda rows: write(w, slot, rows).wait())


def _experts(xs, items, w1, b1, w2, b2):
    ne, d, dff2 = w1.shape
    dff = dff2 // 2
    exp_w, row_w, cls_w, nitem = items
    per_expert = lambda w, ex, ro, cl, ni: (ex[w], 0, 0)
    tile = (EXPERT_TILE,) + _row_tile(d)
    grid_spec = pltpu.PrefetchScalarGridSpec(
        num_scalar_prefetch=4,
        grid=(exp_w.shape[0],),
        in_specs=[pl.BlockSpec(memory_space=pl.ANY),
                  pl.BlockSpec((1, d, dff2), per_expert),
                  pl.BlockSpec((1, 1, dff2), per_expert),
                  pl.BlockSpec((1, dff, d), per_expert),
                  pl.BlockSpec((1, 1, d), per_expert)],
        out_specs=pl.BlockSpec(memory_space=pl.ANY),
        scratch_shapes=[pltpu.VMEM((d, dff2), BF16), pltpu.VMEM((dff, d), BF16),
                        pltpu.VMEM((2,) + tile, jnp.uint32), pltpu.VMEM((2,) + tile, jnp.uint32),
                        pltpu.SemaphoreType.DMA((2,)), pltpu.SemaphoreType.DMA((2,))],
    )
    return pl.pallas_call(
        _expert_kernel,
        grid_spec=grid_spec,
        out_shape=jax.ShapeDtypeStruct(xs.shape, jnp.uint32),
        compiler_params=_cparams(("arbitrary",)),
        name="moe_experts",
    )(exp_w, row_w, cls_w, nitem, xs, w1, b1.reshape(ne, 1, dff2), w2, b2.reshape(ne, 1, d))


def _work_items(counts, a):
    ne = counts.shape[0]
    n_items = a // EXPERT_TILE + ne
    ends = jnp.cumsum(counts)
    starts = ends - counts
    n_full = counts // EXPERT_TILE
    rem = counts - n_full * EXPERT_TILE
    n_tiles = n_full + (rem > 0).astype(jnp.int32)
    tile_end = jnp.cumsum(n_tiles)
    tile_start = tile_end - n_tiles
    total = tile_end[-1]
    w = jnp.minimum(jnp.arange(n_items, dtype=jnp.int32), total - 1)
    e_w = jnp.sum((tile_end[None, :] <= w[:, None]).astype(jnp.int32), axis=1)
    hot = e_w[:, None] == jnp.arange(ne, dtype=jnp.int32)[None, :]
    pick = lambda v: jnp.sum(jnp.where(hot, v[None, :], 0), axis=1)
    k = w - pick(tile_start)
    row_w = (pick(starts) + k * EXPERT_TILE).astype(jnp.int32)
    rem_w = pick(rem)
    rem_cls = jnp.zeros_like(rem_w)
    for c, rows in enumerate(EXPERT_LAST_TILES[:-1]):
        rem_cls = rem_cls + (rem_w > rows).astype(jnp.int32)
    full_cls = len(EXPERT_LAST_TILES) - 1
    cls_w = jnp.where(k < pick(n_full), full_cls, rem_cls).astype(jnp.int32)
    return starts, (e_w.astype(jnp.int32), row_w, cls_w, total.reshape(1).astype(jnp.int32))


def _combine_kernel(alpha, n_ref, dst_ref, nn_ref, ndst_ref, pos_ref, gate_ref,
                    ys_ref, xm_ref, g2_ref, lg_ref, lb_ref, o_ref, srt_ref, sem):
    i = pl.program_id(0)
    last = pl.num_programs(0) - 1
    slot = i % 2
    td = xm_ref.shape[0]
    n = td * TOP_K

    def fetch(cnt_ref, from_ref, s):
        _strip_copies(cnt_ref, from_ref,
                      lambda row, src, size: pltpu.make_async_copy(
                          ys_ref.at[pl.ds(src, size)], srt_ref.at[s, pl.ds(row, size)], sem.at[s]),
                      td)

    @pl.when(i == 0)
    def _():
        fetch(n_ref, dst_ref, slot)

    @pl.when(i < last)
    def _():
        fetch(nn_ref, ndst_ref, 1 - slot)

    pltpu.make_async_copy(ys_ref.at[pl.ds(0, n)], srt_ref.at[slot], sem.at[slot]).wait()

    pos, gate = pos_ref[...], gate_ref[...]
    lanes = lax.broadcasted_iota(jnp.int32, (td, n), 1)
    wsel = jnp.where(lanes == pos[:, 0:1], gate[:, 0:1], 0.0)
    for k in range(1, TOP_K):
        wsel = wsel + jnp.where(lanes == pos[:, k:k + 1], gate[:, k:k + 1], 0.0)
    w_hi = wsel.astype(BF16)
    w_lo = (wsel - w_hi.astype(F32)).astype(BF16)
    y = _unpack_rows(srt_ref[slot]).astype(BF16)
    f = _bdot(w_hi, y) + _bdot(w_lo, y)
    o_ref[...] = _ln(alpha * xm_ref[...] + g2_ref[0] * f) * lg_ref[...] + lb_ref[...]


def _combine(ys, pos, gate, tile_n, tile_dst, x_mid, g2, lg, lb, alpha, td):
    t, d = x_mid.shape
    n = td * TOP_K
    nt = t // td
    per_b = nt // g2.shape[0]
    first = lambda v: pl.BlockSpec((1, 1, v.shape[-1]), lambda i: (0, 0, 0),
                                   memory_space=pltpu.SMEM)
    nxt = lambda v: pl.BlockSpec((1, 1, v.shape[-1]),
                                 lambda i: (jnp.minimum(i + 1, nt - 1), 0, 0),
                                 memory_space=pltpu.SMEM)
    const = lambda i: (0, 0)
    tile_n = tile_n[:, None, :]
    tile_dst = tile_dst[:, None, :]
    return pl.pallas_call(
        functools.partial(_combine_kernel, alpha),
        grid=(nt,),
        in_specs=[first(tile_n), first(tile_dst), nxt(tile_n), nxt(tile_dst),
                  pl.BlockSpec((td, TOP_K), lambda i: (i, 0)),
                  pl.BlockSpec((td, TOP_K), lambda i: (i, 0)),
                  pl.BlockSpec(memory_space=pl.ANY),
                  pl.BlockSpec((td, d), lambda i: (i, 0)),
                  pl.BlockSpec((1, 1, d), lambda i: (i // per_b, 0, 0)),
                  pl.BlockSpec((1, d), const), pl.BlockSpec((1, d), const)],
        out_specs=pl.BlockSpec((td, d), lambda i: (i, 0)),
        out_shape=jax.ShapeDtypeStruct((t, d), F32),
        scratch_shapes=[pltpu.VMEM((2, n) + _row_tile(d), jnp.uint32),
                        pltpu.SemaphoreType.DMA((2,))],
        compiler_params=_cparams(("arbitrary",)),
        name="moe_combine_ln",
    )(tile_n, tile_dst, tile_n, tile_dst, pos, gate, ys, x_mid, g2,
      lg.reshape(1, d), lb.reshape(1, d))


def _rope_tables(l):
    f32 = np.float32
    rows = l // GRID_W
    row = np.repeat(np.arange(rows, dtype=f32), GRID_W)
    col = np.tile(np.arange(GRID_W, dtype=f32), rows)
    n_freq = HEAD_DIM // 4
    inv_freq = np.power(f32(ROPE_BASE), -np.arange(n_freq, dtype=f32) / f32(n_freq)).astype(f32)
    ang_r = (row[:, None] * inv_freq).astype(f32)
    ang_c = (col[:, None] * inv_freq).astype(f32)
    zero = np.zeros_like(ang_r)
    cos_r, sin_r, cos_c, sin_c = np.cos(ang_r), np.sin(ang_r), np.cos(ang_c), np.sin(ang_c)
    cos_h = np.concatenate([cos_r, cos_r, cos_c, cos_c], axis=1)
    sa_h = np.concatenate([-sin_r, zero, -sin_c, zero], axis=1)
    sb_h = np.concatenate([zero, sin_r, zero, sin_c], axis=1)
    rep = 128 // HEAD_DIM
    return tuple(np.tile(a, (1, rep)).astype(f32) for a in (cos_h, sa_h, sb_h))


def _filter_features(l):
    f32 = np.float32
    bands = (HY_EMB_DIM - 1) // 2
    t = np.linspace(0.0, 1.0, l, dtype=f32)[:, None]
    omega = (f32(2.0 * math.pi) * np.arange(l, dtype=f32)[:, None] / f32(l)).astype(f32)
    f = np.linspace(1e-4, bands - 1, bands, dtype=f32)[None, :]
    ang = (f * omega).astype(f32)
    z = np.concatenate([t, np.cos(ang), -np.sin(ang)], axis=-1).astype(f32)
    min_decay = math.log(HY_DECAY_TARGET) / HY_FAST_DECAY_PCT
    max_decay = math.log(HY_DECAY_TARGET) / HY_SLOW_DECAY_PCT
    deltas = np.abs(np.linspace(min_decay, max_decay, HY_WIDTH, dtype=f32))
    decay = np.exp(-t * deltas).astype(f32)
    return z, decay


def kernel(x, c, ctx, c_ctx, w_mod, b_mod, w_in, attn_sink, hy_conv_w, hy_conv_b, hy_filt_w1,
           hy_filt_b1, hy_filt_w2, hy_filt_b2, hy_filt_w3, hy_skip, w_branch_attn, w_branch_hyena,
           w_out, ln1_g, ln1_b, router_w, router_b, exp_w1, exp_b1, exp_w2, exp_b2, ln2_g, ln2_b):
    depth = w_mod.shape[0]
    assert depth == 1, "only the single-layer configuration is implemented"
    b, l, d = x.shape
    t = b * l
    alpha = (2 * depth) ** 0.25
    hy_wc = 256

    n_cond = b + 1
    pad = (-n_cond) % 8
    cond = jnp.concatenate([c, c_ctx[None], jnp.zeros((pad, d), F32)], axis=0)
    mod = _modulation(cond, w_mod[0], b_mod[0])
    mod_x = mod[:b].reshape(b, 1, 6, d)
    sh1, sc1, g1, sh2, sc2, g2 = (mod_x[:, :, i] for i in range(6))
    mod_c = mod[b:b + 1].reshape(1, 1, 6, d)
    csh1, csc1 = mod_c[:, :, 0], mod_c[:, :, 1]

    w_in_b = w_in[0].astype(BF16)
    cos_t, sa_t, sb_t = (jnp.asarray(a) for a in _rope_tables(l))
    q, k, v, u_hy, gate_x = _in_projection(x, sh1, sc1, w_in_b, cos_t, sa_t, sb_t)
    k_c, v_c = _ctx_kv(ctx, csh1, csc1, w_in_b[:, K_OFF:HY_OFF])
    att = _attention(q, k, v, k_c, v_c, attn_sink[0])

    tabs_np = _dft_tables(l, hy_wc)
    tabs = (jnp.asarray(tabs_np[0]).astype(BF16), jnp.asarray(tabs_np[1]).astype(BF16)) + tuple(
        jnp.asarray(a) for a in tabs_np[2:])
    zfeat, decay = _filter_features(l)
    emb_pad = (-HY_EMB_DIM) % 128
    zfeat = np.pad(zfeat, ((0, 0), (0, emb_pad)))
    fw1 = jnp.pad(hy_filt_w1[0], ((0, emb_pad), (0, 0)))
    spectra = _hyena_filters(jnp.asarray(zfeat[0::2]), jnp.asarray(zfeat[1::2]), fw1, hy_filt_b1[0],
                             hy_filt_w2[0], hy_filt_b2[0], hy_filt_w3[0], jnp.asarray(decay[0::2]),
                             jnp.asarray(decay[1::2]), tabs, hy_wc)
    hy = _hyena(u_hy, hy_conv_w[0], hy_conv_b[0], hy_skip[0], spectra, tabs, hy_wc)

    x_mid, h2, logits_t = _merge(att.reshape(t, Q_W), hy.reshape(t, HY_WIDTH),
                                 gate_x.reshape(t, 2 * d), x.reshape(t, d), g1, sh2, sc2,
                                 w_branch_attn[0].astype(BF16), w_branch_hyena[0].astype(BF16),
                                 w_out[0].astype(BF16), ln1_g[0], ln1_b[0],
                                 jnp.transpose(router_w[0]), router_b[0], alpha)

    moe_td = 256
    gate_t, pos_t, tile_cnt, tile_base, counts = _route(logits_t, moe_td)
    starts, items = _work_items(counts[:, 0], t * TOP_K)
    tile_n = tile_cnt[:, :, 0]
    tile_n = jnp.concatenate([tile_n, jnp.max(tile_n, axis=1, keepdims=True)], axis=1)
    tile_dst = starts[None, :] + tile_base[:, :, 0]
    xs = _dispatch(h2.reshape(t, d), pos_t, tile_n, tile_dst, moe_td)
    ys = _experts(xs, items, exp_w1[0], exp_b1[0], exp_w2[0], exp_b2[0])
    out = _combine(ys, jnp.transpose(pos_t), jnp.transpose(gate_t), tile_n, tile_dst,
                   x_mid.reshape(t, d), g2, ln2_g[0], ln2_b[0], alpha, moe_td)
    return out.reshape(b, l, d)
```

```python
import functools
import math

import numpy as np
import jax
import jax.numpy as jnp
from jax import lax
from jax.experimental import pallas as pl
from jax.experimental.pallas import tpu as pltpu

F32 = jnp.float32
BF16 = jnp.bfloat16
HIGHEST = lax.Precision.HIGHEST

GRID_W = 64
N_HEADS = 8
N_KV_HEADS = 2
GQA_GROUP = N_HEADS // N_KV_HEADS
HEAD_DIM = 64
WINDOW = 128
ATTN_BLOCK = 128
ROPE_BASE = 10000.0

HY_WIDTH = 512
HY_ORDER = 2
HY_SHORT_CONV = 3
HY_EMB_DIM = 33
HY_DECAY_TARGET = 1e-2
HY_FAST_DECAY_PCT = 0.3
HY_SLOW_DECAY_PCT = 1.5

N_EXPERTS = 32
TOP_K = 4
SWIGLU_LIMIT = 7.0
SWIGLU_ALPHA = 1.702
LN_EPS = 1e-5

Q_W = N_HEADS * HEAD_DIM
KV_W = N_KV_HEADS * HEAD_DIM
K_OFF = Q_W
V_OFF = K_OFF + KV_W
HY_OFF = V_OFF + KV_W
GATE_OFF = HY_OFF + (HY_ORDER + 1) * HY_WIDTH

VMEM_LIMIT = 56 * 1024 * 1024
NEG_BIG = -1e30
MERGE_CHUNKS = 2
EXPERT_TILE = 1024
EXPERT_LAST_TILES = (256, 512, 1024)


def _cparams(sem):
    return pltpu.CompilerParams(dimension_semantics=sem, vmem_limit_bytes=VMEM_LIMIT)


def _ln(x):
    mu = jnp.mean(x, axis=-1, keepdims=True)
    xc = x - mu
    var = jnp.mean(xc * xc, axis=-1, keepdims=True)
    return xc * lax.rsqrt(var + LN_EPS)


def _bdot(a, b):
    return jnp.dot(a, b, preferred_element_type=F32)


def _fdot(a, b):
    return jnp.dot(a, b, preferred_element_type=F32, precision=HIGHEST)


def _mod_kernel(c_ref, w_ref, b_ref, o_ref):
    c = c_ref[...]
    s = c * jax.nn.sigmoid(c)
    o_ref[...] = _fdot(s, w_ref[...]) + b_ref[...]


def _modulation(cond, w, b, tn=512):
    r, d = cond.shape
    n = w.shape[1]
    return pl.pallas_call(
        _mod_kernel,
        grid=(n // tn,),
        in_specs=[pl.BlockSpec((r, d), lambda j: (0, 0)),
                  pl.BlockSpec((d, tn), lambda j: (0, j)),
                  pl.BlockSpec((1, tn), lambda j: (0, j))],
        out_specs=pl.BlockSpec((r, tn), lambda j: (0, j)),
        out_shape=jax.ShapeDtypeStruct((r, n), F32),
        compiler_params=_cparams(("arbitrary",)),
        name="modulation",
    )(cond, w, b.reshape(1, n))


def _rope(t, cos, sa, sb):
    n = t.shape[-1]
    return t * cos + pltpu.roll(t, n - 16, 1) * sa + pltpu.roll(t, 16, 1) * sb


def _inproj_kernel(x_ref, sh_ref, sc_ref, w_ref, cos_ref, sa_ref, sb_ref,
                   q_ref, k_ref, v_ref, u_ref, g_ref, hb_ref):
    @pl.when(pl.program_id(0) == 0)
    def _():
        hb_ref[...] = jnp.zeros_like(hb_ref)

    tm = x_ref.shape[1]
    scale = HEAD_DIM ** -0.5
    n_gate = w_ref.shape[1] - GATE_OFF
    rc = tm // MERGE_CHUNKS
    for r in (slice(c * rc, (c + 1) * rc) for c in range(MERGE_CHUNKS)):
        hb = hb_ref[r]
        cos, sa, sb = cos_ref[r], sa_ref[r], sb_ref[r]
        for j in range(Q_W // 128):
            t = _bdot(hb, w_ref[:, j * 128:(j + 1) * 128])
            q_ref[0, r, j * 128:(j + 1) * 128] = (_rope(t, cos, sa, sb) * scale).astype(BF16)
        t = _bdot(hb, w_ref[:, K_OFF:V_OFF])
        k_ref[0, r] = _rope(t, cos, sa, sb).astype(BF16)
        v_ref[0, r] = _bdot(hb, w_ref[:, V_OFF:HY_OFF]).astype(BF16)
        for j in range((GATE_OFF - HY_OFF) // 512):
            u_ref[0, r, j * 512:(j + 1) * 512] = _bdot(
                hb, w_ref[:, HY_OFF + j * 512:HY_OFF + (j + 1) * 512])
        for j in range(n_gate // 512):
            g_ref[0, r, j * 512:(j + 1) * 512] = jax.nn.sigmoid(_bdot(
                hb, w_ref[:, GATE_OFF + j * 512:GATE_OFF + (j + 1) * 512])).astype(BF16)

        h = _ln(x_ref[0, r]) * (1.0 + sc_ref[0]) + sh_ref[0]
        hb_ref[r] = h.astype(BF16)


def _in_projection(x, sh, sc, w_in_b, cos_t, sa_t, sb_t, tm=512):
    b, l, d = x.shape
    in_w = w_in_b.shape[1]
    hy_w = GATE_OFF - HY_OFF
    g_w = in_w - GATE_OFF
    per_b = l // tm
    nt = b * per_b
    t1 = lambda i: jnp.minimum(i, nt - 1)
    t2 = lambda i: jnp.maximum(i - 1, 0)
    row1 = lambda i: (t1(i) // per_b, t1(i) % per_b, 0)
    row2 = lambda i: (t2(i) // per_b, t2(i) % per_b, 0)
    vec1 = lambda i: (t1(i) // per_b, 0, 0)
    tab2 = lambda i: (t2(i) % per_b, 0)
    return pl.pallas_call(
        _inproj_kernel,
        grid=(nt + 1,),
        in_specs=[pl.BlockSpec((1, tm, d), row1),
                  pl.BlockSpec((1, 1, d), vec1),
                  pl.BlockSpec((1, 1, d), vec1),
                  pl.BlockSpec((d, in_w), lambda i: (0, 0)),
                  pl.BlockSpec((tm, 128), tab2),
                  pl.BlockSpec((tm, 128), tab2),
                  pl.BlockSpec((tm, 128), tab2)],
        out_specs=[pl.BlockSpec((1, tm, Q_W), row2),
                   pl.BlockSpec((1, tm, KV_W), row2),
                   pl.BlockSpec((1, tm, KV_W), row2),
                   pl.BlockSpec((1, tm, hy_w), row2),
                   pl.BlockSpec((1, tm, g_w), row2)],
        out_shape=[jax.ShapeDtypeStruct((b, l, Q_W), BF16),
                   jax.ShapeDtypeStruct((b, l, KV_W), BF16),
                   jax.ShapeDtypeStruct((b, l, KV_W), BF16),
                   jax.ShapeDtypeStruct((b, l, hy_w), F32),
                   jax.ShapeDtypeStruct((b, l, g_w), BF16)],
        scratch_shapes=[pltpu.VMEM((tm, d), BF16)],
        compiler_params=_cparams(("arbitrary",)),
        name="in_projection",
    )(x, sh, sc, w_in_b, cos_t, sa_t, sb_t)


def _ctx_kv_kernel(x_ref, sh_ref, sc_ref, w_ref, k_ref, v_ref):
    h = _ln(x_ref[0]) * (1.0 + sc_ref[0]) + sh_ref[0]
    kv = _bdot(h.astype(BF16), w_ref[...])
    k_ref[0] = kv[:, :KV_W].astype(BF16)
    v_ref[0] = kv[:, KV_W:].astype(BF16)


def _ctx_kv(ctx, sh, sc, w_kv_b):
    b, c, d = ctx.shape
    row = lambda bi: (bi, 0, 0)
    return pl.pallas_call(
        _ctx_kv_kernel,
        grid=(b,),
        in_specs=[pl.BlockSpec((1, c, d), row),
                  pl.BlockSpec((1, 1, d), lambda bi: (0, 0, 0)),
                  pl.BlockSpec((1, 1, d), lambda bi: (0, 0, 0)),
                  pl.BlockSpec((d, 2 * KV_W), lambda bi: (0, 0))],
        out_specs=[pl.BlockSpec((1, c, KV_W), row), pl.BlockSpec((1, c, KV_W), row)],
        out_shape=[jax.ShapeDtypeStruct((b, c, KV_W), BF16),
                   jax.ShapeDtypeStruct((b, c, KV_W), BF16)],
        compiler_params=_cparams(("arbitrary",)),
        name="ctx_kv",
    )(ctx, sh, sc, w_kv_b)


def _attn_kernel(sink_ref, q_ref, k_ref, v_ref, kc_ref, vc_ref, bias_ref, o_ref):
    for qb in range(q_ref.shape[1] // ATTN_BLOCK):
        rows = slice(qb * ATTN_BLOCK, (qb + 1) * ATTN_BLOCK)
        j = pl.program_id(1) * (q_ref.shape[1] // ATTN_BLOCK) + qb
        o_ref[0, rows] = _attn_block(j, sink_ref, q_ref.at[0, rows], k_ref, v_ref, kc_ref, vc_ref,
                                     bias_ref)


def _band_bias():
    span = ATTN_BLOCK + 2 * WINDOW
    q = np.arange(GQA_GROUP * ATTN_BLOCK)[None, :, None] % ATTN_BLOCK
    k = np.arange(span)[None, None, :] - ATTN_BLOCK * np.arange(span // ATTN_BLOCK)[:, None, None]
    return np.where(np.abs(k - q) <= WINDOW, 0.0, NEG_BIG).astype(np.float32)


def _attn_block(j, sink_ref, q_ref, k_ref, v_ref, kc_ref, vc_ref, bias_ref):
    l = k_ref.shape[1]
    span = ATTN_BLOCK + 2 * WINDOW
    q0 = j * ATTN_BLOCK
    start = pl.multiple_of(jnp.clip(q0 - WINDOW, 0, l - span), ATTN_BLOCK)
    rows = GQA_GROUP * ATTN_BLOCK
    bias = bias_ref[(q0 - start) // ATTN_BLOCK]
    head_of_row = lax.broadcasted_iota(jnp.int32, (rows, 1), 0) // ATTN_BLOCK
    dn = (((1,), (1,)), ((), ()))
    outs = []
    for kv in range(N_KV_HEADS):
        ks = slice(kv * HEAD_DIM, (kv + 1) * HEAD_DIM)
        kl = k_ref[0, pl.ds(start, span), ks]
        vl = v_ref[0, pl.ds(start, span), ks]
        kc = kc_ref[0, :, ks]
        vc = vc_ref[0, :, ks]
        heads = [kv * GQA_GROUP + g for g in range(GQA_GROUP)]
        qg = jnp.concatenate([q_ref[:, h * HEAD_DIM:(h + 1) * HEAD_DIM] for h in heads], axis=0)
        sink = jnp.zeros((rows, 1), F32)
        for g, h in enumerate(heads):
            sink = jnp.where(head_of_row == g, sink_ref[h], sink)
        s_loc = lax.dot_general(qg, kl, dn, preferred_element_type=F32) + bias
        s_ctx = lax.dot_general(qg, kc, dn, preferred_element_type=F32)
        blocks = [s[:, c:c + 128] for s in (s_loc, s_ctx) for c in range(0, s.shape[1], 128)]
        folded = functools.reduce(jnp.maximum, blocks)
        m = jnp.maximum(jnp.max(folded, axis=1, keepdims=True), sink)
        p_loc = jnp.exp(s_loc - m).astype(BF16)
        p_ctx = jnp.exp(s_ctx - m).astype(BF16)
        ones_l = jnp.ones((span, HEAD_DIM), BF16)
        ones_c = jnp.ones((kc.shape[0], HEAD_DIM), BF16)
        pv = (_bdot(p_loc, jnp.concatenate([vl, ones_l], axis=1))
              + _bdot(p_ctx, jnp.concatenate([vc, ones_c], axis=1)))
        den = pv[:, HEAD_DIM:HEAD_DIM + 1] + jnp.exp(sink - m)
        o = pv[:, :HEAD_DIM] / den
        outs.extend(o[g * ATTN_BLOCK:(g + 1) * ATTN_BLOCK] for g in range(GQA_GROUP))
    return jnp.concatenate(outs, axis=1).astype(BF16)


def _attention(q, k, v, kc, vc, sink, q_blocks=4):
    b, l, _ = q.shape
    c = kc.shape[1]
    tq = q_blocks * ATTN_BLOCK
    bias = jnp.asarray(_band_bias())
    full = lambda bi, j, s: (bi, 0, 0)
    grid_spec = pltpu.PrefetchScalarGridSpec(
        num_scalar_prefetch=1,
        grid=(b, l // tq),
        in_specs=[pl.BlockSpec((1, tq, Q_W), lambda bi, j, s: (bi, j, 0)),
                  pl.BlockSpec((1, l, KV_W), full),
                  pl.BlockSpec((1, l, KV_W), full),
                  pl.BlockSpec((1, c, KV_W), full),
                  pl.BlockSpec((1, c, KV_W), full),
                  pl.BlockSpec(bias.shape, lambda bi, j, s: (0, 0, 0))],
        out_specs=pl.BlockSpec((1, tq, Q_W), lambda bi, j, s: (bi, j, 0)),
    )
    return pl.pallas_call(
        _attn_kernel,
        grid_spec=grid_spec,
        out_shape=jax.ShapeDtypeStruct((b, l, Q_W), BF16),
        compiler_params=_cparams(("arbitrary", "arbitrary")),
        name="window_attention",
    )(sink, q, k, v, kc, vc, bias)


def _dft_tables(l, wc):
    h = l // 2
    idx = np.arange(h, dtype=np.int64)
    ang = 2.0 * np.pi * ((idx[:, None] * idx[None, :]) % l).astype(np.float64) / l
    cm = np.cos(ang).astype(np.float32)
    sm = np.sin(ang).astype(np.float32)
    tw = 2.0 * np.pi * idx.astype(np.float64) / (2 * l)
    ones = np.ones((1, wc), np.float32)
    wr = np.cos(tw).astype(np.float32)[:, None] * ones
    wi = (-np.sin(tw)).astype(np.float32)[:, None] * ones
    sgn = np.where(idx % 2 == 0, 1.0, -1.0).astype(np.float32)[:, None] * ones
    wk = np.where(idx == 0, 1.0 / (2 * l), 2.0 / (2 * l)).astype(np.float32)[:, None] * ones
    return cm, sm, wr, wi, sgn, wk


def _half_spectrum(se, so, cm, sm, wr, wi, sgn):
    seb, sob = se.astype(BF16), so.astype(BF16)
    ce, ss_e = _bdot(cm, seb), _bdot(sm, seb)
    co, ss_o = _bdot(cm, sob), _bdot(sm, sob)
    vr = wr * co + wi * ss_o
    vi = wi * co - wr * ss_o
    zar, zai = ce + vr, vi - ss_e
    zbr, zbi = ce - vr, -ss_e - vi
    e_ny = jnp.sum(sgn * se, axis=0, keepdims=True)
    o_ny = jnp.sum(sgn * so, axis=0, keepdims=True)
    return zar, zai, zbr, zbi, e_ny, -o_ny


def _filter_kernel(ze_ref, zo_ref, w1_ref, b1_ref, w2_ref, b2_ref, w3f_ref, w3b_ref,
                   de_ref, do_ref, cm_ref, sm_ref, wr_ref, wi_ref, sgn_ref, wk_ref,
                   har_ref, hai_ref, hbr_ref, hbi_ref, hny_ref, ae_ref, ao_ref):
    cm, sm = cm_ref[...], sm_ref[...]
    wr, wi, sgn, wk = wr_ref[...], wi_ref[...], sgn_ref[...], wk_ref[...]
    h = cm.shape[0]

    @pl.when((pl.program_id(0) == 0) & (pl.program_id(1) == 0))
    def _():
        for z_ref, a_ref in ((ze_ref, ae_ref), (zo_ref, ao_ref)):
            a = jnp.sin(_fdot(z_ref[...], w1_ref[...]) + b1_ref[...])
            a_ref[...] = jnp.sin(_fdot(a, w2_ref[...]) + b2_ref[...])

    def taps(a_ref, w3_ref, d_ref):
        return _fdot(a_ref[...], w3_ref[0]) * d_ref[...]

    fe, fo = taps(ae_ref, w3f_ref, de_ref), taps(ao_ref, w3f_ref, do_ref)
    be, bo = taps(ae_ref, w3b_ref, de_ref), taps(ao_ref, w3b_ref, do_ref)
    row = lax.broadcasted_iota(jnp.int32, be.shape, 0)
    be = jnp.where(row == 0, 0.0, be)
    far, fai, fbr, fbi, fnr, fni = _half_spectrum(fe, fo, cm, sm, wr, wi, sgn)
    bar, bai, bbr, bbi, bnr, bni = _half_spectrum(be, bo, cm, sm, wr, wi, sgn)
    har_ref[0] = wk * (far + bar)
    hai_ref[0] = wk * (fai - bai)
    hbr_ref[0] = wk * (fbr + bbr)
    hbi_ref[0] = wk * (fbi - bbi)
    ny_scale = 2.0 / (4 * h)
    nr = ny_scale * (fnr + bnr)
    ni = ny_scale * (fni - bni)
    rows = lax.broadcasted_iota(jnp.int32, (8, nr.shape[1]), 0)
    hny_ref[0] = jnp.where(rows == 0, nr, jnp.where(rows == 1, ni, 0.0))


def _hyena_filters(zfe, zfo, w1, b1, w2, b2, w3, dec_e, dec_o, tabs, wc):
    cm, sm, wr, wi, sgn, wk = tabs
    h = cm.shape[0]
    emb = zfe.shape[1]
    hid = w2.shape[0]
    nblk = HY_WIDTH // wc
    w3r = w3.reshape(hid, HY_ORDER * 2, HY_WIDTH).transpose(1, 0, 2)
    const = lambda o, cb: (0, 0)
    chan = lambda o, cb: (0, cb)
    out_spec = pl.BlockSpec((1, h, wc), lambda o, cb: (o, 0, cb))
    out_shape = jax.ShapeDtypeStruct((HY_ORDER, h, HY_WIDTH), F32)
    return pl.pallas_call(
        _filter_kernel,
        grid=(HY_ORDER, nblk),
        in_specs=[pl.BlockSpec((h, emb), const), pl.BlockSpec((h, emb), const),
                  pl.BlockSpec((emb, hid), const), pl.BlockSpec((1, hid), const),
                  pl.BlockSpec((hid, hid), const), pl.BlockSpec((1, hid), const),
                  pl.BlockSpec((1, hid, wc), lambda o, cb: (2 * o, 0, cb)),
                  pl.BlockSpec((1, hid, wc), lambda o, cb: (2 * o + 1, 0, cb)),
                  pl.BlockSpec((h, wc), chan), pl.BlockSpec((h, wc), chan),
                  pl.BlockSpec((h, h), const), pl.BlockSpec((h, h), const),
                  pl.BlockSpec((h, wc), const), pl.BlockSpec((h, wc), const),
                  pl.BlockSpec((h, wc), const), pl.BlockSpec((h, wc), const)],
        out_specs=[out_spec, out_spec, out_spec, out_spec,
                   pl.BlockSpec((1, 8, wc), lambda o, cb: (o, 0, cb))],
        out_shape=[out_shape, out_shape, out_shape, out_shape,
                   jax.ShapeDtypeStruct((HY_ORDER, 8, HY_WIDTH), F32)],
        scratch_shapes=[pltpu.VMEM((h, hid), F32), pltpu.VMEM((h, hid), F32)],
        compiler_params=_cparams(("arbitrary", "arbitrary")),
        name="hyena_filter_spectrum",
    )(zfe, zfo, w1, b1.reshape(1, hid), w2, b2.reshape(1, hid), w3r, w3r,
      dec_e, dec_o, cm, sm, wr, wi, sgn, wk)


def _long_conv(ze, zo, har, hai, hbr, hbi, hny, cm, sm, wr, wi, sgn):
    zar, zai, zbr, zbi, znr, zni = _half_spectrum(ze, zo, cm, sm, wr, wi, sgn)
    yar, yai = zar * har - zai * hai, zar * hai + zai * har
    ybr, ybi = zbr * hbr - zbi * hbi, zbr * hbi + zbi * hbr
    dr, di = yar - ybr, yai - ybi
    sar, sai = (yar + ybr).astype(BF16), (yai + ybi).astype(BF16)
    sbr, sbi = (dr * wr + di * wi).astype(BF16), (di * wr - dr * wi).astype(BF16)
    hnr, hni = hny[0:1], hny[1:2]
    ynr = znr * hnr - zni * hni
    yni = znr * hni + zni * hnr
    ye = _bdot(cm, sar) - _bdot(sm, sai) + sgn * ynr
    yo = _bdot(cm, sbr) - _bdot(sm, sbi) - sgn * yni
    return ye, yo


def _hyena_kernel(*refs):
    nsub = (len(refs) - 18) // 4
    u_refs = [refs[s * nsub:(s + 1) * nsub] for s in range(HY_ORDER + 1)]
    rest = refs[(HY_ORDER + 1) * nsub:]
    (cw0_ref, cw1_ref, cw2_ref, cb0_ref, cb1_ref, cb2_ref, skip_ref,
     har_ref, hai_ref, hbr_ref, hbi_ref, hny_ref,
     cm_ref, sm_ref, wr_ref, wi_ref, sgn_ref, o_ref) = rest[:18]
    il_refs = rest[18:]
    h = cm_ref.shape[0]
    cm, sm = cm_ref[...], sm_ref[...]
    wr, wi, sgn = wr_ref[...], wi_ref[...], sgn_ref[...]
    row = lax.broadcasted_iota(jnp.int32, (h, wr.shape[1]), 0)

    def short_conv(u_slabs, cw_ref, cb_ref):
        ue = jnp.concatenate([r[0, pl.ds(0, h, stride=2), :] for r in u_slabs], axis=1)
        uo = jnp.concatenate([r[0, pl.ds(1, h, stride=2), :] for r in u_slabs], axis=1)
        w0, w1, w2 = cw_ref[0:1], cw_ref[1:2], cw_ref[2:3]
        uo_prev = jnp.where(row == 0, 0.0, pltpu.roll(uo, 1, 0))
        ue_next = jnp.where(row == h - 1, 0.0, pltpu.roll(ue, h - 1, 0))
        cb = cb_ref[...]
        se = cb + uo_prev * w0 + ue * w1 + uo * w2
        so = cb + ue * w0 + uo * w1 + ue_next * w2
        return se, so

    x_streams = [short_conv(u_refs[0], cw0_ref, cb0_ref), short_conv(u_refs[1], cw1_ref, cb1_ref)]
    ze, zo = short_conv(u_refs[2], cw2_ref, cb2_ref)
    for o in range(HY_ORDER):
        ce, co = _long_conv(ze, zo, har_ref[o], hai_ref[o], hbr_ref[o], hbi_ref[o], hny_ref[o],
                            cm, sm, wr, wi, sgn)
        sk = skip_ref[o:o + 1]
        xe, xo = x_streams[o]
        ze = xe * (ce + sk * ze)
        zo = xo * (co + sk * zo)
    for j, il in enumerate(il_refs):
        il[pl.ds(0, h, stride=2), :] = ze[:, j * 128:(j + 1) * 128]
        il[pl.ds(1, h, stride=2), :] = zo[:, j * 128:(j + 1) * 128]
        o_ref[0, :, j * 128:(j + 1) * 128] = il[...]


def _hyena(u, conv_w, conv_b, skip, spectra, tabs, wc):
    b, l, _ = u.shape
    har, hai, hbr, hbi, hny = spectra
    cm, sm, wr, wi, sgn, _ = tabs
    h = l // 2
    nblk = HY_WIDTH // wc
    nsub = wc // 128
    conv_b = conv_b.reshape(1, -1)

    def ublk(s, j):
        return pl.BlockSpec((1, l, 128), lambda cb, bi: (bi, 0, (cb + s * nblk) * nsub + j))

    def wblk(r, s):
        return pl.BlockSpec((r, wc), lambda cb, bi: (0, cb + s * nblk))

    const = lambda cb, bi: (0, 0)
    spec = pl.BlockSpec((HY_ORDER, h, wc), lambda cb, bi: (0, 0, cb))
    return pl.pallas_call(
        _hyena_kernel,
        grid=(nblk, b),
        in_specs=[ublk(s, j) for s in range(HY_ORDER + 1) for j in range(nsub)] + [
                  wblk(HY_SHORT_CONV, 0), wblk(HY_SHORT_CONV, 1), wblk(HY_SHORT_CONV, 2),
                  wblk(1, 0), wblk(1, 1), wblk(1, 2),
                  pl.BlockSpec((HY_ORDER, wc), lambda cb, bi: (0, cb)),
                  spec, spec, spec, spec,
                  pl.BlockSpec((HY_ORDER, 8, wc), lambda cb, bi: (0, 0, cb)),
                  pl.BlockSpec((h, h), const), pl.BlockSpec((h, h), const),
                  pl.BlockSpec((h, wc), const), pl.BlockSpec((h, wc), const),
                  pl.BlockSpec((h, wc), const)],
        out_specs=pl.BlockSpec((1, l, wc), lambda cb, bi: (bi, 0, cb)),
        out_shape=jax.ShapeDtypeStruct((b, l, HY_WIDTH), F32),
        scratch_shapes=[pltpu.VMEM((l, 128), F32) for _ in range(nsub)],
        compiler_params=_cparams(("arbitrary", "arbitrary")),
        name="hyena_long_conv",
    )(*([u] * ((HY_ORDER + 1) * nsub)), conv_w, conv_w, conv_w, conv_b, conv_b, conv_b, skip,
      har, hai, hbr, hbi, hny, cm, sm, wr, wi, sgn)


def _merge_kernel(alpha, att_ref, hy_ref, gate_ref, x_ref, g1_ref, sh2_ref, sc2_ref,
                  wba_ref, wbh_ref, wo_ref, l1g_ref, l1b_ref, rw_ref, rb_ref,
                  xmid_ref, h2_ref, logit_ref, res_ref):
    d = x_ref.shape[1]

    @pl.when(pl.program_id(0) == 0)
    def _():
        res_ref[...] = jnp.zeros_like(res_ref)

    rw = rw_ref[...]
    r_hi = rw.astype(BF16)
    r_lo = (rw - r_hi.astype(F32)).astype(BF16)
    dn = (((1,), (1,)), ((), ()))
    nt_dot = lambda p, q: lax.dot_general(p, q, dn, preferred_element_type=F32)
    tm = x_ref.shape[0]
    rc = tm // MERGE_CHUNKS
    for r in (slice(c * rc, (c + 1) * rc) for c in range(MERGE_CHUNKS)):
        xm = _ln(res_ref[r]) * l1g_ref[...] + l1b_ref[...]
        xmid_ref[r] = xm
        h2 = _ln(xm) * (1.0 + sc2_ref[0]) + sh2_ref[0]
        h_hi = h2.astype(BF16)
        h2_ref[r] = h_hi
        h_lo = (h2 - h_hi.astype(F32)).astype(BF16)
        logit_ref[:, r] = nt_dot(r_hi, h_hi) + nt_dot(r_hi, h_lo) + nt_dot(r_lo, h_hi) + rb_ref[...]

        a = _bdot(att_ref[r], wba_ref[...])
        hh = _bdot(hy_ref[r].astype(BF16), wbh_ref[...])
        ga = gate_ref[r, :d].astype(F32)
        gh = gate_ref[r, d:].astype(F32)
        y = _bdot((ga * a + gh * hh).astype(BF16), wo_ref[...])
        res_ref[r] = alpha * x_ref[r] + g1_ref[0] * y


def _merge(att, hy, gate, x, g1, sh2, sc2, wba, wbh, wo, l1g, l1b, rw_t, rb, alpha, tm=512):
    t, d = x.shape
    ne = rw_t.shape[0]
    nt = t // tm
    per_b = nt // g1.shape[0]
    t1 = lambda i: jnp.minimum(i, nt - 1)
    t2 = lambda i: jnp.maximum(i - 1, 0)
    row1 = lambda i: (t1(i), 0)
    row2 = lambda i: (t2(i), 0)
    const = lambda i: (0, 0)
    return pl.pallas_call(
        functools.partial(_merge_kernel, alpha),
        grid=(nt + 1,),
        in_specs=[pl.BlockSpec((tm, Q_W), row1),
                  pl.BlockSpec((tm, HY_WIDTH), row1),
                  pl.BlockSpec((tm, 2 * d), row1),
                  pl.BlockSpec((tm, d), row1),
                  pl.BlockSpec((1, 1, d), lambda i: (t1(i) // per_b, 0, 0)),
                  pl.BlockSpec((1, 1, d), lambda i: (t2(i) // per_b, 0, 0)),
                  pl.BlockSpec((1, 1, d), lambda i: (t2(i) // per_b, 0, 0)),
                  pl.BlockSpec((Q_W, d), const), pl.BlockSpec((HY_WIDTH, d), const),
                  pl.BlockSpec((d, d), const),
                  pl.BlockSpec((1, d), const), pl.BlockSpec((1, d), const),
                  pl.BlockSpec((ne, d), const), pl.BlockSpec((ne, 1), const)],
        out_specs=[pl.BlockSpec((tm, d), row2), pl.BlockSpec((tm, d), row2),
                   pl.BlockSpec((ne, tm), lambda i: (0, t2(i)))],
        out_shape=[jax.ShapeDtypeStruct((t, d), F32),
                   jax.ShapeDtypeStruct((t, d), BF16),
                   jax.ShapeDtypeStruct((ne, t), F32)],
        scratch_shapes=[pltpu.VMEM((tm, d), F32)],
        compiler_params=_cparams(("arbitrary",)),
        name="merge_ln_router",
    )(att, hy, gate, x, g1, sh2, sc2, wba, wbh, wo, l1g.reshape(1, d), l1b.reshape(1, d),
      rw_t, rb.reshape(ne, 1))


def _route_kernel(lg_ref, gate_ref, pos_ref, tcnt_ref, tbase_ref, cnt_ref, run_ref):
    @pl.when(pl.program_id(0) == 0)
    def _():
        run_ref[...] = jnp.zeros_like(run_ref)

    tiles = tcnt_ref.shape[0]
    tr = lg_ref.shape[1] // tiles
    for s in range(tiles):
        cols = slice(s * tr, (s + 1) * tr)
        gate, pos, cnt_row = _route_tile(lg_ref[:, cols])
        gate_ref[:, cols] = gate
        pos_ref[:, cols] = pos
        tcnt_ref[s] = cnt_row.astype(jnp.int32)
        tbase_ref[s] = run_ref[...].astype(jnp.int32)
        run_ref[...] = run_ref[...] + cnt_row
    cnt_ref[...] = run_ref[...].astype(jnp.int32)


def _route_tile(lg):
    ne, tr = lg.shape
    sub = lax.broadcasted_iota(jnp.int32, (ne, tr), 0)
    work = lg
    vals, hots = [], []
    for _ in range(TOP_K):
        m = jnp.max(work, axis=0, keepdims=True)
        idx = jnp.min(jnp.where(work == m, sub, ne), axis=0, keepdims=True)
        hot = sub == idx
        vals.append(m)
        hots.append(hot)
        work = jnp.where(hot, -jnp.inf, work)
    exps = [jnp.exp(v - vals[0]) for v in vals]
    den = exps[0] + exps[1] + exps[2] + exps[3]
    member = jnp.zeros((ne, tr), F32)
    for hot in hots:
        member = member + jnp.where(hot, 1.0, 0.0)
    r_i = lax.broadcasted_iota(jnp.int32, (tr, tr), 0)
    c_i = lax.broadcasted_iota(jnp.int32, (tr, tr), 1)
    earlier = jnp.where(r_i < c_i, 1.0, 0.0).astype(BF16)
    prefix = _bdot(member.astype(BF16), earlier)
    cnt = jnp.broadcast_to(jnp.sum(member, axis=1, keepdims=True), (ne, 128))
    e_r = lax.broadcasted_iota(jnp.int32, (ne, ne), 0)
    e_c = lax.broadcasted_iota(jnp.int32, (ne, ne), 1)
    lower = jnp.where(e_c < e_r, 1.0, 0.0).astype(BF16)
    off = _bdot(lower, cnt.astype(BF16))
    base = off[:, 0:1] + prefix
    sub_k = lax.broadcasted_iota(jnp.int32, (TOP_K, tr), 0)
    gate = jnp.zeros((TOP_K, tr), F32)
    pos = jnp.zeros((TOP_K, tr), F32)
    for k in range(TOP_K):
        pk = jnp.sum(jnp.where(hots[k], base, 0.0), axis=0, keepdims=True)
        gate = jnp.where(sub_k == k, exps[k] / den, gate)
        pos = jnp.where(sub_k == k, pk, pos)
    cnt_row = lax.dot_general(jnp.ones((8, tr), BF16), member.astype(BF16),
                              (((1,), (1,)), ((), ())), preferred_element_type=F32)
    return gate, pos.astype(jnp.int32), cnt_row


def _route(logits_t, tr, tiles_per_step=4):
    ne, t = logits_t.shape
    nt = t // tr
    tb = tr * tiles_per_step
    tok = lambda i: (0, i)
    tile = lambda i: (i, 0, 0)
    return pl.pallas_call(
        _route_kernel,
        grid=(nt // tiles_per_step,),
        in_specs=[pl.BlockSpec((ne, tb), tok)],
        out_specs=[pl.BlockSpec((TOP_K, tb), tok), pl.BlockSpec((TOP_K, tb), tok),
                   pl.BlockSpec((tiles_per_step, 8, ne), tile),
                   pl.BlockSpec((tiles_per_step, 8, ne), tile),
                   pl.BlockSpec((8, ne), lambda i: (0, 0))],
        out_shape=[jax.ShapeDtypeStruct((TOP_K, t), F32),
                   jax.ShapeDtypeStruct((TOP_K, t), jnp.int32),
                   jax.ShapeDtypeStruct((nt, 8, ne), jnp.int32),
                   jax.ShapeDtypeStruct((nt, 8, ne), jnp.int32),
                   jax.ShapeDtypeStruct((8, ne), jnp.int32)],
        scratch_shapes=[pltpu.VMEM((8, ne), F32)],
        compiler_params=_cparams(("arbitrary",)),
        name="route_topk",
    )(logits_t)


def _row_tile(d):
    return (d // 256, 128)


def _pack_rows(v):
    n, d = v.shape
    bits = lax.bitcast_convert_type(v.astype(BF16).astype(F32), jnp.uint32)
    word = (bits[:, d // 2:] & jnp.uint32(0xFFFF0000)) | (bits[:, :d // 2] >> 16)
    return word.reshape((n,) + _row_tile(d))


def _unpack_rows(w):
    n = w.shape[0]
    word = w.reshape(n, w.shape[1] * w.shape[2])
    lo = lax.bitcast_convert_type(word << 16, F32)
    hi = lax.bitcast_convert_type(word & jnp.uint32(0xFFFF0000), F32)
    return jnp.concatenate([lo, hi], axis=1)


def _strip_copies(n_ref, dst_ref, make_copy, max_rows):
    ne = dst_ref.shape[2]
    all_bits = [1 << s for s in range(max_rows.bit_length() - 1, -1, -1)]
    small = pl.next_power_of_2(max(1, 2 * TOP_K * max_rows // ne))
    longest = n_ref[0, 0, ne]

    def run(bits):
        def body(e, off):
            n = n_ref[0, 0, e]
            dst = dst_ref[0, 0, e]
            done = jnp.int32(0)
            for bit in bits:
                part = n & bit

                @pl.when(part != 0)
                def _():
                    make_copy(off + done, dst + done, bit).start()

                done = done + part
            return off + n

        lax.fori_loop(0, ne, body, jnp.int32(0))

    @pl.when(longest < small)
    def _():
        run([b for b in all_bits if b < small])

    @pl.when(longest >= small)
    def _():
        run(all_bits)


def _dispatch_kernel(n_ref, dst_ref, pos_ref, h_ref, xs_ref, srt_ref, sem):
    i = pl.program_id(0)
    slot = i % 2
    td = h_ref.shape[0]
    n = td * TOP_K

    @pl.when(i == 0)
    def _():
        tail = xs_ref.shape[0] - EXPERT_TILE
        srt_ref[1, 0:EXPERT_TILE] = jnp.zeros((EXPERT_TILE,) + srt_ref.shape[2:], srt_ref.dtype)
        zero_tail = pltpu.make_async_copy(srt_ref.at[1, pl.ds(0, EXPERT_TILE)],
                                          xs_ref.at[pl.ds(tail, EXPERT_TILE)], sem.at[1])
        zero_tail.start()
        zero_tail.wait()

    pos = pos_ref[...]
    rows = lax.broadcasted_iota(jnp.int32, (n, td), 0)
    hit = rows == pos[0:1]
    for k in range(1, TOP_K):
        hit = hit | (rows == pos[k:k + 1])
    perm = jnp.where(hit, 1.0, 0.0).astype(BF16)
    srt_ref[slot] = _pack_rows(_bdot(perm, h_ref[...]))

    def all_rows(s):
        return pltpu.make_async_copy(srt_ref.at[s], xs_ref.at[pl.ds(0, n)], sem.at[s])

    _strip_copies(n_ref, dst_ref,
                  lambda src, dst, size: pltpu.make_async_copy(
                      srt_ref.at[slot, pl.ds(src, size)], xs_ref.at[pl.ds(dst, size)], sem.at[slot]),
                  td)

    @pl.when(i > 0)
    def _():
        all_rows(1 - slot).wait()

    @pl.when(i == pl.num_programs(0) - 1)
    def _():
        all_rows(slot).wait()


def _dispatch(h2, pos_t, tile_n, tile_dst, td):
    t, d = h2.shape
    a = t * TOP_K
    n = td * TOP_K
    nt = t // td
    smem = lambda v: pl.BlockSpec((1, 1, v.shape[-1]), lambda i: (i, 0, 0),
                                  memory_space=pltpu.SMEM)
    return pl.pallas_call(
        _dispatch_kernel,
        grid=(nt,),
        in_specs=[smem(tile_n), smem(tile_dst),
                  pl.BlockSpec((TOP_K, td), lambda i: (0, i)),
                  pl.BlockSpec((td, d), lambda i: (i, 0))],
        out_specs=pl.BlockSpec(memory_space=pl.ANY),
        out_shape=jax.ShapeDtypeStruct((a + EXPERT_TILE,) + _row_tile(d), jnp.uint32),
        scratch_shapes=[pltpu.VMEM((2, n) + _row_tile(d), jnp.uint32),
                        pltpu.SemaphoreType.DMA((2,))],
        compiler_params=_cparams(("arbitrary",)),
        name="moe_dispatch",
    )(tile_n[:, None, :], tile_dst[:, None, :], pos_t, h2)


def _expert_kernel(exp_ref, row_ref, cls_ref, nitem_ref,
                   xs_ref, w1_ref, b1_ref, w2_ref, b2_ref, ys_ref,
                   w1b_ref, w2b_ref, xbuf_ref, ybuf_ref, sem_in, sem_out):
    w = pl.program_id(0)
    n_items = nitem_ref[0]
    last = exp_ref.shape[0] - 1
    slot = w % 2
    dff = w2_ref.shape[1]

    def fetch(item, s):
        return pltpu.make_async_copy(xs_ref.at[pl.ds(row_ref[item], EXPERT_TILE)],
                                     xbuf_ref.at[s], sem_in.at[s])

    def write(item, s, rows):
        return pltpu.make_async_copy(ybuf_ref.at[s, pl.ds(0, rows)],
                                     ys_ref.at[pl.ds(row_ref[item], rows)], sem_out.at[s])

    def for_tile_size(item, fn):
        for c, rows in enumerate(EXPERT_LAST_TILES):
            @pl.when(cls_ref[item] == c)
            def _():
                fn(rows)

    @pl.when(w == 0)
    def _():
        fetch(0, 0).start()
        ybuf_ref[1] = jnp.zeros(ybuf_ref.shape[1:], ybuf_ref.dtype)
        zero_tail = pltpu.make_async_copy(
            ybuf_ref.at[1], ys_ref.at[pl.ds(ys_ref.shape[0] - EXPERT_TILE, EXPERT_TILE)],
            sem_out.at[1])
        zero_tail.start()
        zero_tail.wait()

    @pl.when(w + 1 < n_items)
    def _():
        fetch(jnp.minimum(w + 1, last), 1 - slot).start()

    @pl.when((w == 0) | (exp_ref[w] != exp_ref[jnp.maximum(w - 1, 0)]))
    def _():
        w1b_ref[...] = w1_ref[0].astype(BF16)
        w2b_ref[...] = w2_ref[0].astype(BF16)

    @pl.when(w < n_items)
    def _():
        fetch(w, slot).wait()

        def compute(rows):
            x = _unpack_rows(xbuf_ref[slot, 0:rows]).astype(BF16)
            hb = _bdot(x, w1b_ref[...]) + b1_ref[0]
            glu = jnp.minimum(hb[:, :dff], SWIGLU_LIMIT)
            lin = jnp.clip(hb[:, dff:], -SWIGLU_LIMIT, SWIGLU_LIMIT)
            act = glu * jax.nn.sigmoid(SWIGLU_ALPHA * glu) * (lin + 1.0)
            ybuf_ref[slot, 0:rows] = _pack_rows(_bdot(act.astype(BF16), w2b_ref[...]) + b2_ref[0])

        for_tile_size(w, compute)

        @pl.when(w > 0)
        def _():
            for_tile_size(w - 1, lambda rows: write(w - 1, 1 - slot, rows).wait())

        for_tile_size(w, lambda rows: write(w, slot, rows).start())

        @pl.when(w == n_items - 1)
        def _():
            for_tile_size(w, lambda rows: write(w, slot, rows).wait())


def _experts(xs, items, w1, b1, w2, b2):
    ne, d, dff2 = w1.shape
    dff = dff2 // 2
    exp_w, row_w, cls_w, nitem = items
    per_expert = lambda w, ex, ro, cl, ni: (ex[w], 0, 0)
    tile = (EXPERT_TILE,) + _row_tile(d)
    grid_spec = pltpu.PrefetchScalarGridSpec(
        num_scalar_prefetch=4,
        grid=(exp_w.shape[0],),
        in_specs=[pl.BlockSpec(memory_space=pl.ANY),
                  pl.BlockSpec((1, d, dff2), per_expert),
                  pl.BlockSpec((1, 1, dff2), per_expert),
                  pl.BlockSpec((1, dff, d), per_expert),
                  pl.BlockSpec((1, 1, d), per_expert)],
        out_specs=pl.BlockSpec(memory_space=pl.ANY),
        scratch_shapes=[pltpu.VMEM((d, dff2), BF16), pltpu.VMEM((dff, d), BF16),
                        pltpu.VMEM((2,) + tile, jnp.uint32), pltpu.VMEM((2,) + tile, jnp.uint32),
                        pltpu.SemaphoreType.DMA((2,)), pltpu.SemaphoreType.DMA((2,))],
    )
    return pl.pallas_call(
        _expert_kernel,
        grid_spec=grid_spec,
        out_shape=jax.ShapeDtypeStruct(xs.shape, jnp.uint32),
        compiler_params=_cparams(("arbitrary",)),
        name="moe_experts",
    )(exp_w, row_w, cls_w, nitem, xs, w1, b1.reshape(ne, 1, dff2), w2, b2.reshape(ne, 1, d))


def _work_items(counts, a):
    ne = counts.shape[0]
    n_items = a // EXPERT_TILE + ne
    ends = jnp.cumsum(counts)
    starts = ends - counts
    n_full = counts // EXPERT_TILE
    rem = counts - n_full * EXPERT_TILE
    n_tiles = n_full + (rem > 0).astype(jnp.int32)
    tile_end = jnp.cumsum(n_tiles)
    tile_start = tile_end - n_tiles
    total = tile_end[-1]
    w = jnp.minimum(jnp.arange(n_items, dtype=jnp.int32), total - 1)
    e_w = jnp.sum((tile_end[None, :] <= w[:, None]).astype(jnp.int32), axis=1)
    hot = e_w[:, None] == jnp.arange(ne, dtype=jnp.int32)[None, :]
    pick = lambda v: jnp.sum(jnp.where(hot, v[None, :], 0), axis=1)
    k = w - pick(tile_start)
    row_w = (pick(starts) + k * EXPERT_TILE).astype(jnp.int32)
    rem_w = pick(rem)
    rem_cls = jnp.zeros_like(rem_w)
    for c, rows in enumerate(EXPERT_LAST_TILES[:-1]):
        rem_cls = rem_cls + (rem_w > rows).astype(jnp.int32)
    full_cls = len(EXPERT_LAST_TILES) - 1
    cls_w = jnp.where(k < pick(n_full), full_cls, rem_cls).astype(jnp.int32)
    return starts, (e_w.astype(jnp.int32), row_w, cls_w, total.reshape(1).astype(jnp.int32))


def _combine_kernel(alpha, n_ref, dst_ref, nn_ref, ndst_ref, pos_ref, gate_ref,
                    ys_ref, xm_ref, g2_ref, lg_ref, lb_ref, o_ref, srt_ref, sem):
    i = pl.program_id(0)
    last = pl.num_programs(0) - 1
    slot = i % 2
    td = xm_ref.shape[0]
    n = td * TOP_K

    def fetch(cnt_ref, from_ref, s):
        _strip_copies(cnt_ref, from_ref,
                      lambda row, src, size: pltpu.make_async_copy(
                          ys_ref.at[pl.ds(src, size)], srt_ref.at[s, pl.ds(row, size)], sem.at[s]),
                      td)

    @pl.when(i == 0)
    def _():
        fetch(n_ref, dst_ref, slot)

    @pl.when(i < last)
    def _():
        fetch(nn_ref, ndst_ref, 1 - slot)

    pltpu.make_async_copy(ys_ref.at[pl.ds(0, n)], srt_ref.at[slot], sem.at[slot]).wait()

    pos, gate = pos_ref[...], gate_ref[...]
    lanes = lax.broadcasted_iota(jnp.int32, (td, n), 1)
    wsel = jnp.where(lanes == pos[:, 0:1], gate[:, 0:1], 0.0)
    for k in range(1, TOP_K):
        wsel = wsel + jnp.where(lanes == pos[:, k:k + 1], gate[:, k:k + 1], 0.0)
    w_hi = wsel.astype(BF16)
    w_lo = (wsel - w_hi.astype(F32)).astype(BF16)
    y = _unpack_rows(srt_ref[slot]).astype(BF16)
    f = _bdot(w_hi, y) + _bdot(w_lo, y)
    o_ref[...] = _ln(alpha * xm_ref[...] + g2_ref[0] * f) * lg_ref[...] + lb_ref[...]


def _combine(ys, pos, gate, tile_n, tile_dst, x_mid, g2, lg, lb, alpha, td):
    t, d = x_mid.shape
    n = td * TOP_K
    nt = t // td
    per_b = nt // g2.shape[0]
    first = lambda v: pl.BlockSpec((1, 1, v.shape[-1]), lambda i: (0, 0, 0),
                                   memory_space=pltpu.SMEM)
    nxt = lambda v: pl.BlockSpec((1, 1, v.shape[-1]),
                                 lambda i: (jnp.minimum(i + 1, nt - 1), 0, 0),
                                 memory_space=pltpu.SMEM)
    const = lambda i: (0, 0)
    tile_n = tile_n[:, None, :]
    tile_dst = tile_dst[:, None, :]
    return pl.pallas_call(
        functools.partial(_combine_kernel, alpha),
        grid=(nt,),
        in_specs=[first(tile_n), first(tile_dst), nxt(tile_n), nxt(tile_dst),
                  pl.BlockSpec((td, TOP_K), lambda i: (i, 0)),
                  pl.BlockSpec((td, TOP_K), lambda i: (i, 0)),
                  pl.BlockSpec(memory_space=pl.ANY),
                  pl.BlockSpec((td, d), lambda i: (i, 0)),
                  pl.BlockSpec((1, 1, d), lambda i: (i // per_b, 0, 0)),
                  pl.BlockSpec((1, d), const), pl.BlockSpec((1, d), const)],
        out_specs=pl.BlockSpec((td, d), lambda i: (i, 0)),
        out_shape=jax.ShapeDtypeStruct((t, d), F32),
        scratch_shapes=[pltpu.VMEM((2, n) + _row_tile(d), jnp.uint32),
                        pltpu.SemaphoreType.DMA((2,))],
        compiler_params=_cparams(("arbitrary",)),
        name="moe_combine_ln",
    )(tile_n, tile_dst, tile_n, tile_dst, pos, gate, ys, x_mid, g2,
      lg.reshape(1, d), lb.reshape(1, d))


def _rope_tables(l):
    f32 = np.float32
    rows = l // GRID_W
    row = np.repeat(np.arange(rows, dtype=f32), GRID_W)
    col = np.tile(np.arange(GRID_W, dtype=f32), rows)
    n_freq = HEAD_DIM // 4
    inv_freq = np.power(f32(ROPE_BASE), -np.arange(n_freq, dtype=f32) / f32(n_freq)).astype(f32)
    ang_r = (row[:, None] * inv_freq).astype(f32)
    ang_c = (col[:, None] * inv_freq).astype(f32)
    zero = np.zeros_like(ang_r)
    cos_r, sin_r, cos_c, sin_c = np.cos(ang_r), np.sin(ang_r), np.cos(ang_c), np.sin(ang_c)
    cos_h = np.concatenate([cos_r, cos_r, cos_c, cos_c], axis=1)
    sa_h = np.concatenate([-sin_r, zero, -sin_c, zero], axis=1)
    sb_h = np.concatenate([zero, sin_r, zero, sin_c], axis=1)
    rep = 128 // HEAD_DIM
    return tuple(np.tile(a, (1, rep)).astype(f32) for a in (cos_h, sa_h, sb_h))


def _filter_features(l):
    f32 = np.float32
    bands = (HY_EMB_DIM - 1) // 2
    t = np.linspace(0.0, 1.0, l, dtype=f32)[:, None]
    omega = (f32(2.0 * math.pi) * np.arange(l, dtype=f32)[:, None] / f32(l)).astype(f32)
    f = np.linspace(1e-4, bands - 1, bands, dtype=f32)[None, :]
    ang = (f * omega).astype(f32)
    z = np.concatenate([t, np.cos(ang), -np.sin(ang)], axis=-1).astype(f32)
    min_decay = math.log(HY_DECAY_TARGET) / HY_FAST_DECAY_PCT
    max_decay = math.log(HY_DECAY_TARGET) / HY_SLOW_DECAY_PCT
    deltas = np.abs(np.linspace(min_decay, max_decay, HY_WIDTH, dtype=f32))
    decay = np.exp(-t * deltas).astype(f32)
    return z, decay


def kernel(x, c, ctx, c_ctx, w_mod, b_mod, w_in, attn_sink, hy_conv_w, hy_conv_b, hy_filt_w1,
           hy_filt_b1, hy_filt_w2, hy_filt_b2, hy_filt_w3, hy_skip, w_branch_attn, w_branch_hyena,
           w_out, ln1_g, ln1_b, router_w, router_b, exp_w1, exp_b1, exp_w2, exp_b2, ln2_g, ln2_b):
    depth = w_mod.shape[0]
    assert depth == 1, "only the single-layer configuration is implemented"
    b, l, d = x.shape
    t = b * l
    alpha = (2 * depth) ** 0.25
    hy_wc = 256

    n_cond = b + 1
    pad = (-n_cond) % 8
    cond = jnp.concatenate([c, c_ctx[None], jnp.zeros((pad, d), F32)], axis=0)
    mod = _modulation(cond, w_mod[0], b_mod[0])
    mod_x = mod[:b].reshape(b, 1, 6, d)
    sh1, sc1, g1, sh2, sc2, g2 = (mod_x[:, :, i] for i in range(6))
    mod_c = mod[b:b + 1].reshape(1, 1, 6, d)
    csh1, csc1 = mod_c[:, :, 0], mod_c[:, :, 1]

    w_in_b = w_in[0].astype(BF16)
    cos_t, sa_t, sb_t = (jnp.asarray(a) for a in _rope_tables(l))
    q, k, v, u_hy, gate_x = _in_projection(x, sh1, sc1, w_in_b, cos_t, sa_t, sb_t)
    k_c, v_c = _ctx_kv(ctx, csh1, csc1, w_in_b[:, K_OFF:HY_OFF])
    att = _attention(q, k, v, k_c, v_c, attn_sink[0])

    tabs_np = _dft_tables(l, hy_wc)
    tabs = (jnp.asarray(tabs_np[0]).astype(BF16), jnp.asarray(tabs_np[1]).astype(BF16)) + tuple(
        jnp.asarray(a) for a in tabs_np[2:])
    zfeat, decay = _filter_features(l)
    emb_pad = (-HY_EMB_DIM) % 128
    zfeat = np.pad(zfeat, ((0, 0), (0, emb_pad)))
    fw1 = jnp.pad(hy_filt_w1[0], ((0, emb_pad), (0, 0)))
    spectra = _hyena_filters(jnp.asarray(zfeat[0::2]), jnp.asarray(zfeat[1::2]), fw1, hy_filt_b1[0],
                             hy_filt_w2[0], hy_filt_b2[0], hy_filt_w3[0], jnp.asarray(decay[0::2]),
                             jnp.asarray(decay[1::2]), tabs, hy_wc)
    hy = _hyena(u_hy, hy_conv_w[0], hy_conv_b[0], hy_skip[0], spectra, tabs, hy_wc)

    x_mid, h2, logits_t = _merge(att.reshape(t, Q_W), hy.reshape(t, HY_WIDTH),
                                 gate_x.reshape(t, 2 * d), x.reshape(t, d), g1, sh2, sc2,
                                 w_branch_attn[0].astype(BF16), w_branch_hyena[0].astype(BF16),
                                 w_out[0].astype(BF16), ln1_g[0], ln1_b[0],
                                 jnp.transpose(router_w[0]), router_b[0], alpha)

    moe_td = 256
    gate_t, pos_t, tile_cnt, tile_base, counts = _route(logits_t, moe_td)
    starts, items = _work_items(counts[0], t * TOP_K)
    tile_n = tile_cnt[:, 0, :]
    tile_n = jnp.concatenate([tile_n, jnp.max(tile_n, axis=1, keepdims=True)], axis=1)
    tile_dst = starts[None, :] + tile_base[:, 0, :]
    xs = _dispatch(h2.reshape(t, d), pos_t, tile_n, tile_dst, moe_td)
    ys = _experts(xs, items, exp_w1[0], exp_b1[0], exp_w2[0], exp_b2[0])
    out = _combine(ys, jnp.transpose(pos_t), jnp.transpose(gate_t), tile_n, tile_dst,
                   x_mid.reshape(t, d), g2, ln2_g[0], ln2_b[0], alpha, moe_td)
    return out.reshape(b, l, d)
```

```python
import functools
import math

import numpy as np
import jax
import jax.numpy as jnp
from jax import lax
from jax.experimental import pallas as pl
from jax.experimental.pallas import tpu as pltpu

F32 = jnp.float32
BF16 = jnp.bfloat16
HIGHEST = lax.Precision.HIGHEST

GRID_W = 64
N_HEADS = 8
N_KV_HEADS = 2
GQA_GROUP = N_HEADS // N_KV_HEADS
HEAD_DIM = 64
WINDOW = 128
ATTN_BLOCK = 128
ROPE_BASE = 10000.0

HY_WIDTH = 512
HY_ORDER = 2
HY_SHORT_CONV = 3
HY_EMB_DIM = 33
HY_DECAY_TARGET = 1e-2
HY_FAST_DECAY_PCT = 0.3
HY_SLOW_DECAY_PCT = 1.5

N_EXPERTS = 32
TOP_K = 4
SWIGLU_LIMIT = 7.0
SWIGLU_ALPHA = 1.702
LN_EPS = 1e-5

Q_W = N_HEADS * HEAD_DIM
KV_W = N_KV_HEADS * HEAD_DIM
K_OFF = Q_W
V_OFF = K_OFF + KV_W
HY_OFF = V_OFF + KV_W
GATE_OFF = HY_OFF + (HY_ORDER + 1) * HY_WIDTH

VMEM_LIMIT = 56 * 1024 * 1024
NEG_BIG = -1e30
MERGE_CHUNKS = 2
EXPERT_TILE = 1024
EXPERT_LAST_TILES = (256, 512, 1024)


def _cparams(sem):
    return pltpu.CompilerParams(dimension_semantics=sem, vmem_limit_bytes=VMEM_LIMIT)


def _ln(x):
    mu = jnp.mean(x, axis=-1, keepdims=True)
    xc = x - mu
    var = jnp.mean(xc * xc, axis=-1, keepdims=True)
    return xc * lax.rsqrt(var + LN_EPS)


def _bdot(a, b):
    return jnp.dot(a, b, preferred_element_type=F32)


def _fdot(a, b):
    return jnp.dot(a, b, preferred_element_type=F32, precision=HIGHEST)


def _mod_kernel(c_ref, w_ref, b_ref, o_ref):
    c = c_ref[...]
    s = c * jax.nn.sigmoid(c)
    o_ref[...] = _fdot(s, w_ref[...]) + b_ref[...]


def _modulation(cond, w, b, tn=512):
    r, d = cond.shape
    n = w.shape[1]
    return pl.pallas_call(
        _mod_kernel,
        grid=(n // tn,),
        in_specs=[pl.BlockSpec((r, d), lambda j: (0, 0)),
                  pl.BlockSpec((d, tn), lambda j: (0, j)),
                  pl.BlockSpec((1, tn), lambda j: (0, j))],
        out_specs=pl.BlockSpec((r, tn), lambda j: (0, j)),
        out_shape=jax.ShapeDtypeStruct((r, n), F32),
        compiler_params=_cparams(("arbitrary",)),
        name="modulation",
    )(cond, w, b.reshape(1, n))


def _rope(t, cos, sa, sb):
    n = t.shape[-1]
    return t * cos + pltpu.roll(t, n - 16, 1) * sa + pltpu.roll(t, 16, 1) * sb


def _inproj_kernel(x_ref, sh_ref, sc_ref, w_ref, cos_ref, sa_ref, sb_ref,
                   q_ref, k_ref, v_ref, u_ref, g_ref, hb_ref):
    @pl.when(pl.program_id(0) == 0)
    def _():
        hb_ref[...] = jnp.zeros_like(hb_ref)

    tm = x_ref.shape[1]
    scale = HEAD_DIM ** -0.5
    n_gate = w_ref.shape[1] - GATE_OFF
    rc = tm // MERGE_CHUNKS
    for r in (slice(c * rc, (c + 1) * rc) for c in range(MERGE_CHUNKS)):
        hb = hb_ref[r]
        cos, sa, sb = cos_ref[r], sa_ref[r], sb_ref[r]
        for j in range(Q_W // 128):
            t = _bdot(hb, w_ref[:, j * 128:(j + 1) * 128])
            q_ref[0, r, j * 128:(j + 1) * 128] = (_rope(t, cos, sa, sb) * scale).astype(BF16)
        t = _bdot(hb, w_ref[:, K_OFF:V_OFF])
        k_ref[0, r] = _rope(t, cos, sa, sb).astype(BF16)
        v_ref[0, r] = _bdot(hb, w_ref[:, V_OFF:HY_OFF]).astype(BF16)
        for j in range((GATE_OFF - HY_OFF) // 512):
            u_ref[0, r, j * 512:(j + 1) * 512] = _bdot(
                hb, w_ref[:, HY_OFF + j * 512:HY_OFF + (j + 1) * 512])
        for j in range(n_gate // 512):
            g_ref[0, r, j * 512:(j + 1) * 512] = jax.nn.sigmoid(_bdot(
                hb, w_ref[:, GATE_OFF + j * 512:GATE_OFF + (j + 1) * 512])).astype(BF16)

        h = _ln(x_ref[0, r]) * (1.0 + sc_ref[0]) + sh_ref[0]
        hb_ref[r] = h.astype(BF16)


def _in_projection(x, sh, sc, w_in_b, cos_t, sa_t, sb_t, tm=512):
    b, l, d = x.shape
    in_w = w_in_b.shape[1]
    hy_w = GATE_OFF - HY_OFF
    g_w = in_w - GATE_OFF
    per_b = l // tm
    nt = b * per_b
    t1 = lambda i: jnp.minimum(i, nt - 1)
    t2 = lambda i: jnp.maximum(i - 1, 0)
    row1 = lambda i: (t1(i) // per_b, t1(i) % per_b, 0)
    row2 = lambda i: (t2(i) // per_b, t2(i) % per_b, 0)
    vec1 = lambda i: (t1(i) // per_b, 0, 0)
    tab2 = lambda i: (t2(i) % per_b, 0)
    return pl.pallas_call(
        _inproj_kernel,
        grid=(nt + 1,),
        in_specs=[pl.BlockSpec((1, tm, d), row1),
                  pl.BlockSpec((1, 1, d), vec1),
                  pl.BlockSpec((1, 1, d), vec1),
                  pl.BlockSpec((d, in_w), lambda i: (0, 0)),
                  pl.BlockSpec((tm, 128), tab2),
                  pl.BlockSpec((tm, 128), tab2),
                  pl.BlockSpec((tm, 128), tab2)],
        out_specs=[pl.BlockSpec((1, tm, Q_W), row2),
                   pl.BlockSpec((1, tm, KV_W), row2),
                   pl.BlockSpec((1, tm, KV_W), row2),
                   pl.BlockSpec((1, tm, hy_w), row2),
                   pl.BlockSpec((1, tm, g_w), row2)],
        out_shape=[jax.ShapeDtypeStruct((b, l, Q_W), BF16),
                   jax.ShapeDtypeStruct((b, l, KV_W), BF16),
                   jax.ShapeDtypeStruct((b, l, KV_W), BF16),
                   jax.ShapeDtypeStruct((b, l, hy_w), F32),
                   jax.ShapeDtypeStruct((b, l, g_w), BF16)],
        scratch_shapes=[pltpu.VMEM((tm, d), BF16)],
        compiler_params=_cparams(("arbitrary",)),
        name="in_projection",
    )(x, sh, sc, w_in_b, cos_t, sa_t, sb_t)


def _ctx_kv_kernel(x_ref, sh_ref, sc_ref, w_ref, k_ref, v_ref):
    h = _ln(x_ref[0]) * (1.0 + sc_ref[0]) + sh_ref[0]
    kv = _bdot(h.astype(BF16), w_ref[...])
    k_ref[0] = kv[:, :KV_W].astype(BF16)
    v_ref[0] = kv[:, KV_W:].astype(BF16)


def _ctx_kv(ctx, sh, sc, w_kv_b):
    b, c, d = ctx.shape
    row = lambda bi: (bi, 0, 0)
    return pl.pallas_call(
        _ctx_kv_kernel,
        grid=(b,),
        in_specs=[pl.BlockSpec((1, c, d), row),
                  pl.BlockSpec((1, 1, d), lambda bi: (0, 0, 0)),
                  pl.BlockSpec((1, 1, d), lambda bi: (0, 0, 0)),
                  pl.BlockSpec((d, 2 * KV_W), lambda bi: (0, 0))],
        out_specs=[pl.BlockSpec((1, c, KV_W), row), pl.BlockSpec((1, c, KV_W), row)],
        out_shape=[jax.ShapeDtypeStruct((b, c, KV_W), BF16),
                   jax.ShapeDtypeStruct((b, c, KV_W), BF16)],
        compiler_params=_cparams(("arbitrary",)),
        name="ctx_kv",
    )(ctx, sh, sc, w_kv_b)


def _attn_kernel(sink_ref, q_ref, k_ref, v_ref, kc_ref, vc_ref, bias_ref, o_ref):
    for qb in range(q_ref.shape[1] // ATTN_BLOCK):
        rows = slice(qb * ATTN_BLOCK, (qb + 1) * ATTN_BLOCK)
        j = pl.program_id(1) * (q_ref.shape[1] // ATTN_BLOCK) + qb
        o_ref[0, rows] = _attn_block(j, sink_ref, q_ref.at[0, rows], k_ref, v_ref, kc_ref, vc_ref,
                                     bias_ref)


def _band_bias():
    span = ATTN_BLOCK + 2 * WINDOW
    q = np.arange(GQA_GROUP * ATTN_BLOCK)[None, :, None] % ATTN_BLOCK
    k = np.arange(span)[None, None, :] - ATTN_BLOCK * np.arange(span // ATTN_BLOCK)[:, None, None]
    return np.where(np.abs(k - q) <= WINDOW, 0.0, NEG_BIG).astype(np.float32)


def _attn_block(j, sink_ref, q_ref, k_ref, v_ref, kc_ref, vc_ref, bias_ref):
    l = k_ref.shape[1]
    span = ATTN_BLOCK + 2 * WINDOW
    q0 = j * ATTN_BLOCK
    start = pl.multiple_of(jnp.clip(q0 - WINDOW, 0, l - span), ATTN_BLOCK)
    rows = GQA_GROUP * ATTN_BLOCK
    bias = bias_ref[(q0 - start) // ATTN_BLOCK]
    head_of_row = lax.broadcasted_iota(jnp.int32, (rows, 1), 0) // ATTN_BLOCK
    dn = (((1,), (1,)), ((), ()))
    outs = []
    for kv in range(N_KV_HEADS):
        ks = slice(kv * HEAD_DIM, (kv + 1) * HEAD_DIM)
        kl = k_ref[0, pl.ds(start, span), ks]
        vl = v_ref[0, pl.ds(start, span), ks]
        kc = kc_ref[0, :, ks]
        vc = vc_ref[0, :, ks]
        heads = [kv * GQA_GROUP + g for g in range(GQA_GROUP)]
        qg = jnp.concatenate([q_ref[:, h * HEAD_DIM:(h + 1) * HEAD_DIM] for h in heads], axis=0)
        sink = jnp.zeros((rows, 1), F32)
        for g, h in enumerate(heads):
            sink = jnp.where(head_of_row == g, sink_ref[h], sink)
        s_loc = lax.dot_general(qg, kl, dn, preferred_element_type=F32) + bias
        s_ctx = lax.dot_general(qg, kc, dn, preferred_element_type=F32)
        blocks = [s[:, c:c + 128] for s in (s_loc, s_ctx) for c in range(0, s.shape[1], 128)]
        folded = functools.reduce(jnp.maximum, blocks)
        m = jnp.maximum(jnp.max(folded, axis=1, keepdims=True), sink)
        p_loc = jnp.exp(s_loc - m).astype(BF16)
        p_ctx = jnp.exp(s_ctx - m).astype(BF16)
        ones_l = jnp.ones((span, HEAD_DIM), BF16)
        ones_c = jnp.ones((kc.shape[0], HEAD_DIM), BF16)
        pv = (_bdot(p_loc, jnp.concatenate([vl, ones_l], axis=1))
              + _bdot(p_ctx, jnp.concatenate([vc, ones_c], axis=1)))
        den = pv[:, HEAD_DIM:HEAD_DIM + 1] + jnp.exp(sink - m)
        o = pv[:, :HEAD_DIM] / den
        outs.extend(o[g * ATTN_BLOCK:(g + 1) * ATTN_BLOCK] for g in range(GQA_GROUP))
    return jnp.concatenate(outs, axis=1).astype(BF16)


def _attention(q, k, v, kc, vc, sink, q_blocks=8):
    b, l, _ = q.shape
    c = kc.shape[1]
    tq = q_blocks * ATTN_BLOCK
    bias = jnp.asarray(_band_bias())
    full = lambda bi, j, s: (bi, 0, 0)
    grid_spec = pltpu.PrefetchScalarGridSpec(
        num_scalar_prefetch=1,
        grid=(b, l // tq),
        in_specs=[pl.BlockSpec((1, tq, Q_W), lambda bi, j, s: (bi, j, 0)),
                  pl.BlockSpec((1, l, KV_W), full),
                  pl.BlockSpec((1, l, KV_W), full),
                  pl.BlockSpec((1, c, KV_W), full),
                  pl.BlockSpec((1, c, KV_W), full),
                  pl.BlockSpec(bias.shape, lambda bi, j, s: (0, 0, 0))],
        out_specs=pl.BlockSpec((1, tq, Q_W), lambda bi, j, s: (bi, j, 0)),
    )
    return pl.pallas_call(
        _attn_kernel,
        grid_spec=grid_spec,
        out_shape=jax.ShapeDtypeStruct((b, l, Q_W), BF16),
        compiler_params=_cparams(("arbitrary", "arbitrary")),
        name="window_attention",
    )(sink, q, k, v, kc, vc, bias)


def _dft_tables(l, wc):
    h = l // 2
    idx = np.arange(h, dtype=np.int64)
    ang = 2.0 * np.pi * ((idx[:, None] * idx[None, :]) % l).astype(np.float64) / l
    cm = np.cos(ang).astype(np.float32)
    sm = np.sin(ang).astype(np.float32)
    tw = 2.0 * np.pi * idx.astype(np.float64) / (2 * l)
    ones = np.ones((1, wc), np.float32)
    wr = np.cos(tw).astype(np.float32)[:, None] * ones
    wi = (-np.sin(tw)).astype(np.float32)[:, None] * ones
    sgn = np.where(idx % 2 == 0, 1.0, -1.0).astype(np.float32)[:, None] * ones
    wk = np.where(idx == 0, 1.0 / (2 * l), 2.0 / (2 * l)).astype(np.float32)[:, None] * ones
    return cm, sm, wr, wi, sgn, wk


def _half_spectrum(se, so, cm, sm, wr, wi, sgn):
    seb, sob = se.astype(BF16), so.astype(BF16)
    ce, ss_e = _bdot(cm, seb), _bdot(sm, seb)
    co, ss_o = _bdot(cm, sob), _bdot(sm, sob)
    vr = wr * co + wi * ss_o
    vi = wi * co - wr * ss_o
    zar, zai = ce + vr, vi - ss_e
    zbr, zbi = ce - vr, -ss_e - vi
    e_ny = jnp.sum(sgn * se, axis=0, keepdims=True)
    o_ny = jnp.sum(sgn * so, axis=0, keepdims=True)
    return zar, zai, zbr, zbi, e_ny, -o_ny


def _filter_kernel(ze_ref, zo_ref, w1_ref, b1_ref, w2_ref, b2_ref, w3f_ref, w3b_ref,
                   de_ref, do_ref, cm_ref, sm_ref, wr_ref, wi_ref, sgn_ref, wk_ref,
                   har_ref, hai_ref, hbr_ref, hbi_ref, hny_ref, ae_ref, ao_ref):
    cm, sm = cm_ref[...], sm_ref[...]
    wr, wi, sgn, wk = wr_ref[...], wi_ref[...], sgn_ref[...], wk_ref[...]
    h = cm.shape[0]

    @pl.when((pl.program_id(0) == 0) & (pl.program_id(1) == 0))
    def _():
        for z_ref, a_ref in ((ze_ref, ae_ref), (zo_ref, ao_ref)):
            a = jnp.sin(_fdot(z_ref[...], w1_ref[...]) + b1_ref[...])
            a_ref[...] = jnp.sin(_fdot(a, w2_ref[...]) + b2_ref[...])

    def taps(a_ref, w3_ref, d_ref):
        return _fdot(a_ref[...], w3_ref[0]) * d_ref[...]

    fe, fo = taps(ae_ref, w3f_ref, de_ref), taps(ao_ref, w3f_ref, do_ref)
    be, bo = taps(ae_ref, w3b_ref, de_ref), taps(ao_ref, w3b_ref, do_ref)
    row = lax.broadcasted_iota(jnp.int32, be.shape, 0)
    be = jnp.where(row == 0, 0.0, be)
    pe, po = (fe + be).astype(BF16), (fo + bo).astype(BF16)
    me, mo = (fe - be).astype(BF16), (fo - bo).astype(BF16)
    ce_p = _bdot(cm, pe)
    vr_p = wr * _bdot(cm, po) + wi * _bdot(sm, po)
    se_m = _bdot(sm, me)
    vi_m = wi * _bdot(cm, mo) - wr * _bdot(sm, mo)
    har_ref[0] = wk * (ce_p + vr_p)
    hai_ref[0] = wk * (vi_m - se_m)
    hbr_ref[0] = wk * (ce_p - vr_p)
    hbi_ref[0] = wk * (-se_m - vi_m)
    ny_scale = 2.0 / (4 * h)
    nr = ny_scale * jnp.sum(sgn * (fe + be), axis=0, keepdims=True)
    ni = -ny_scale * jnp.sum(sgn * (fo - bo), axis=0, keepdims=True)
    rows = lax.broadcasted_iota(jnp.int32, (8, nr.shape[1]), 0)
    hny_ref[0] = jnp.where(rows == 0, nr, jnp.where(rows == 1, ni, 0.0))


def _hyena_filters(zfe, zfo, w1, b1, w2, b2, w3, dec_e, dec_o, tabs, wc):
    cm, sm, wr, wi, sgn, wk = tabs
    h = cm.shape[0]
    emb = zfe.shape[1]
    hid = w2.shape[0]
    nblk = HY_WIDTH // wc
    w3r = w3.reshape(hid, HY_ORDER * 2, HY_WIDTH).transpose(1, 0, 2)
    const = lambda o, cb: (0, 0)
    chan = lambda o, cb: (0, cb)
    out_spec = pl.BlockSpec((1, h, wc), lambda o, cb: (o, 0, cb))
    out_shape = jax.ShapeDtypeStruct((HY_ORDER, h, HY_WIDTH), F32)
    return pl.pallas_call(
        _filter_kernel,
        grid=(HY_ORDER, nblk),
        in_specs=[pl.BlockSpec((h, emb), const), pl.BlockSpec((h, emb), const),
                  pl.BlockSpec((emb, hid), const), pl.BlockSpec((1, hid), const),
                  pl.BlockSpec((hid, hid), const), pl.BlockSpec((1, hid), const),
                  pl.BlockSpec((1, hid, wc), lambda o, cb: (2 * o, 0, cb)),
                  pl.BlockSpec((1, hid, wc), lambda o, cb: (2 * o + 1, 0, cb)),
                  pl.BlockSpec((h, wc), chan), pl.BlockSpec((h, wc), chan),
                  pl.BlockSpec((h, h), const), pl.BlockSpec((h, h), const),
                  pl.BlockSpec((h, wc), const), pl.BlockSpec((h, wc), const),
                  pl.BlockSpec((h, wc), const), pl.BlockSpec((h, wc), const)],
        out_specs=[out_spec, out_spec, out_spec, out_spec,
                   pl.BlockSpec((1, 8, wc), lambda o, cb: (o, 0, cb))],
        out_shape=[out_shape, out_shape, out_shape, out_shape,
                   jax.ShapeDtypeStruct((HY_ORDER, 8, HY_WIDTH), F32)],
        scratch_shapes=[pltpu.VMEM((h, hid), F32), pltpu.VMEM((h, hid), F32)],
        compiler_params=_cparams(("arbitrary", "arbitrary")),
        name="hyena_filter_spectrum",
    )(zfe, zfo, w1, b1.reshape(1, hid), w2, b2.reshape(1, hid), w3r, w3r,
      dec_e, dec_o, cm, sm, wr, wi, sgn, wk)


def _long_conv(ze, zo, har, hai, hbr, hbi, hny, cm, sm, wr, wi, sgn):
    zar, zai, zbr, zbi, znr, zni = _half_spectrum(ze, zo, cm, sm, wr, wi, sgn)
    yar, yai = zar * har - zai * hai, zar * hai + zai * har
    ybr, ybi = zbr * hbr - zbi * hbi, zbr * hbi + zbi * hbr
    dr, di = yar - ybr, yai - ybi
    sar, sai = (yar + ybr).astype(BF16), (yai + ybi).astype(BF16)
    sbr, sbi = (dr * wr + di * wi).astype(BF16), (di * wr - dr * wi).astype(BF16)
    hnr, hni = hny[0:1], hny[1:2]
    ynr = znr * hnr - zni * hni
    yni = znr * hni + zni * hnr
    ye = _bdot(cm, sar) - _bdot(sm, sai) + sgn * ynr
    yo = _bdot(cm, sbr) - _bdot(sm, sbi) - sgn * yni
    return ye, yo


def _hyena_kernel(*refs):
    nsub = (len(refs) - 18) // 4
    u_refs = [refs[s * nsub:(s + 1) * nsub] for s in range(HY_ORDER + 1)]
    rest = refs[(HY_ORDER + 1) * nsub:]
    (cw0_ref, cw1_ref, cw2_ref, cb0_ref, cb1_ref, cb2_ref, skip_ref,
     har_ref, hai_ref, hbr_ref, hbi_ref, hny_ref,
     cm_ref, sm_ref, wr_ref, wi_ref, sgn_ref, o_ref) = rest[:18]
    il_refs = rest[18:]
    h = cm_ref.shape[0]
    cm, sm = cm_ref[...], sm_ref[...]
    wr, wi, sgn = wr_ref[...], wi_ref[...], sgn_ref[...]
    row = lax.broadcasted_iota(jnp.int32, (h, wr.shape[1]), 0)

    def short_conv(u_slabs, cw_ref, cb_ref):
        ue = jnp.concatenate([r[0, pl.ds(0, h, stride=2), :] for r in u_slabs], axis=1)
        uo = jnp.concatenate([r[0, pl.ds(1, h, stride=2), :] for r in u_slabs], axis=1)
        w0, w1, w2 = cw_ref[0:1], cw_ref[1:2], cw_ref[2:3]
        uo_prev = jnp.where(row == 0, 0.0, pltpu.roll(uo, 1, 0))
        ue_next = jnp.where(row == h - 1, 0.0, pltpu.roll(ue, h - 1, 0))
        cb = cb_ref[...]
        se = cb + uo_prev * w0 + ue * w1 + uo * w2
        so = cb + ue * w0 + uo * w1 + ue_next * w2
        return se, so

    x_streams = [short_conv(u_refs[0], cw0_ref, cb0_ref), short_conv(u_refs[1], cw1_ref, cb1_ref)]
    ze, zo = short_conv(u_refs[2], cw2_ref, cb2_ref)
    for o in range(HY_ORDER):
        ce, co = _long_conv(ze, zo, har_ref[o], hai_ref[o], hbr_ref[o], hbi_ref[o], hny_ref[o],
                            cm, sm, wr, wi, sgn)
        sk = skip_ref[o:o + 1]
        xe, xo = x_streams[o]
        ze = xe * (ce + sk * ze)
        zo = xo * (co + sk * zo)
    for j, il in enumerate(il_refs):
        il[pl.ds(0, h, stride=2), :] = ze[:, j * 128:(j + 1) * 128]
        il[pl.ds(1, h, stride=2), :] = zo[:, j * 128:(j + 1) * 128]
        o_ref[0, :, j * 128:(j + 1) * 128] = il[...]


def _hyena(u, conv_w, conv_b, skip, spectra, tabs, wc):
    b, l, _ = u.shape
    har, hai, hbr, hbi, hny = spectra
    cm, sm, wr, wi, sgn, _ = tabs
    h = l // 2
    nblk = HY_WIDTH // wc
    nsub = wc // 128
    conv_b = conv_b.reshape(1, -1)

    def ublk(s, j):
        return pl.BlockSpec((1, l, 128), lambda cb, bi: (bi, 0, (cb + s * nblk) * nsub + j))

    def wblk(r, s):
        return pl.BlockSpec((r, wc), lambda cb, bi: (0, cb + s * nblk))

    const = lambda cb, bi: (0, 0)
    spec = pl.BlockSpec((HY_ORDER, h, wc), lambda cb, bi: (0, 0, cb))
    return pl.pallas_call(
        _hyena_kernel,
        grid=(nblk, b),
        in_specs=[ublk(s, j) for s in range(HY_ORDER + 1) for j in range(nsub)] + [
                  wblk(HY_SHORT_CONV, 0), wblk(HY_SHORT_CONV, 1), wblk(HY_SHORT_CONV, 2),
                  wblk(1, 0), wblk(1, 1), wblk(1, 2),
                  pl.BlockSpec((HY_ORDER, wc), lambda cb, bi: (0, cb)),
                  spec, spec, spec, spec,
                  pl.BlockSpec((HY_ORDER, 8, wc), lambda cb, bi: (0, 0, cb)),
                  pl.BlockSpec((h, h), const), pl.BlockSpec((h, h), const),
                  pl.BlockSpec((h, wc), const), pl.BlockSpec((h, wc), const),
                  pl.BlockSpec((h, wc), const)],
        out_specs=pl.BlockSpec((1, l, wc), lambda cb, bi: (bi, 0, cb)),
        out_shape=jax.ShapeDtypeStruct((b, l, HY_WIDTH), F32),
        scratch_shapes=[pltpu.VMEM((l, 128), F32) for _ in range(nsub)],
        compiler_params=_cparams(("arbitrary", "arbitrary")),
        name="hyena_long_conv",
    )(*([u] * ((HY_ORDER + 1) * nsub)), conv_w, conv_w, conv_w, conv_b, conv_b, conv_b, skip,
      har, hai, hbr, hbi, hny, cm, sm, wr, wi, sgn)


def _merge_kernel(alpha, att_ref, hy_ref, gate_ref, x_ref, g1_ref, sh2_ref, sc2_ref,
                  wba_ref, wbh_ref, wo_ref, l1g_ref, l1b_ref, rw_ref, rb_ref,
                  xmid_ref, h2_ref, logit_ref, res_ref):
    d = x_ref.shape[1]

    @pl.when(pl.program_id(0) == 0)
    def _():
        res_ref[...] = jnp.zeros_like(res_ref)

    rw = rw_ref[...]
    r_hi = rw.astype(BF16)
    r_lo = (rw - r_hi.astype(F32)).astype(BF16)
    dn = (((1,), (1,)), ((), ()))
    nt_dot = lambda p, q: lax.dot_general(p, q, dn, preferred_element_type=F32)
    tm = x_ref.shape[0]
    rc = tm // MERGE_CHUNKS
    for r in (slice(c * rc, (c + 1) * rc) for c in range(MERGE_CHUNKS)):
        xm = _ln(res_ref[r]) * l1g_ref[...] + l1b_ref[...]
        xmid_ref[r] = xm
        h2 = _ln(xm) * (1.0 + sc2_ref[0]) + sh2_ref[0]
        h_hi = h2.astype(BF16)
        h2_ref[r] = h_hi
        h_lo = (h2 - h_hi.astype(F32)).astype(BF16)
        logit_ref[:, r] = nt_dot(r_hi, h_hi) + nt_dot(r_hi, h_lo) + nt_dot(r_lo, h_hi) + rb_ref[...]

        a = _bdot(att_ref[r], wba_ref[...])
        hh = _bdot(hy_ref[r].astype(BF16), wbh_ref[...])
        ga = gate_ref[r, :d].astype(F32)
        gh = gate_ref[r, d:].astype(F32)
        y = _bdot((ga * a + gh * hh).astype(BF16), wo_ref[...])
        res_ref[r] = alpha * x_ref[r] + g1_ref[0] * y


def _merge(att, hy, gate, x, g1, sh2, sc2, wba, wbh, wo, l1g, l1b, rw_t, rb, alpha, tm=512):
    t, d = x.shape
    ne = rw_t.shape[0]
    nt = t // tm
    per_b = nt // g1.shape[0]
    t1 = lambda i: jnp.minimum(i, nt - 1)
    t2 = lambda i: jnp.maximum(i - 1, 0)
    row1 = lambda i: (t1(i), 0)
    row2 = lambda i: (t2(i), 0)
    const = lambda i: (0, 0)
    return pl.pallas_call(
        functools.partial(_merge_kernel, alpha),
        grid=(nt + 1,),
        in_specs=[pl.BlockSpec((tm, Q_W), row1),
                  pl.BlockSpec((tm, HY_WIDTH), row1),
                  pl.BlockSpec((tm, 2 * d), row1),
                  pl.BlockSpec((tm, d), row1),
                  pl.BlockSpec((1, 1, d), lambda i: (t1(i) // per_b, 0, 0)),
                  pl.BlockSpec((1, 1, d), lambda i: (t2(i) // per_b, 0, 0)),
                  pl.BlockSpec((1, 1, d), lambda i: (t2(i) // per_b, 0, 0)),
                  pl.BlockSpec((Q_W, d), const), pl.BlockSpec((HY_WIDTH, d), const),
                  pl.BlockSpec((d, d), const),
                  pl.BlockSpec((1, d), const), pl.BlockSpec((1, d), const),
                  pl.BlockSpec((ne, d), const), pl.BlockSpec((ne, 1), const)],
        out_specs=[pl.BlockSpec((tm, d), row2), pl.BlockSpec((tm, d), row2),
                   pl.BlockSpec((ne, tm), lambda i: (0, t2(i)))],
        out_shape=[jax.ShapeDtypeStruct((t, d), F32),
                   jax.ShapeDtypeStruct((t, d), BF16),
                   jax.ShapeDtypeStruct((ne, t), F32)],
        scratch_shapes=[pltpu.VMEM((tm, d), F32)],
        compiler_params=_cparams(("arbitrary",)),
        name="merge_ln_router",
    )(att, hy, gate, x, g1, sh2, sc2, wba, wbh, wo, l1g.reshape(1, d), l1b.reshape(1, d),
      rw_t, rb.reshape(ne, 1))


def _route_kernel(lg_ref, gate_ref, pos_ref, tcnt_ref, tbase_ref, cnt_ref, run_ref):
    @pl.when(pl.program_id(0) == 0)
    def _():
        run_ref[...] = jnp.zeros_like(run_ref)

    tiles = tcnt_ref.shape[0]
    tr = lg_ref.shape[1] // tiles
    for s in range(tiles):
        cols = slice(s * tr, (s + 1) * tr)
        gate, pos, cnt_row = _route_tile(lg_ref[:, cols])
        gate_ref[:, cols] = gate
        pos_ref[:, cols] = pos
        tcnt_ref[s] = cnt_row.astype(jnp.int32)
        tbase_ref[s] = run_ref[...].astype(jnp.int32)
        run_ref[...] = run_ref[...] + cnt_row
    cnt_ref[...] = run_ref[...].astype(jnp.int32)


def _route_tile(lg):
    ne, tr = lg.shape
    sub = lax.broadcasted_iota(jnp.int32, (ne, tr), 0)
    work = lg
    vals, hots = [], []
    for _ in range(TOP_K):
        m = jnp.max(work, axis=0, keepdims=True)
        idx = jnp.min(jnp.where(work == m, sub, ne), axis=0, keepdims=True)
        hot = sub == idx
        vals.append(m)
        hots.append(hot)
        work = jnp.where(hot, -jnp.inf, work)
    exps = [jnp.exp(v - vals[0]) for v in vals]
    den = exps[0] + exps[1] + exps[2] + exps[3]
    member = jnp.zeros((ne, tr), F32)
    for hot in hots:
        member = member + jnp.where(hot, 1.0, 0.0)
    r_i = lax.broadcasted_iota(jnp.int32, (tr, tr), 0)
    c_i = lax.broadcasted_iota(jnp.int32, (tr, tr), 1)
    earlier = jnp.where(r_i < c_i, 1.0, 0.0).astype(BF16)
    prefix = _bdot(member.astype(BF16), earlier)
    cnt = jnp.broadcast_to(jnp.sum(member, axis=1, keepdims=True), (ne, 128))
    e_r = lax.broadcasted_iota(jnp.int32, (ne, ne), 0)
    e_c = lax.broadcasted_iota(jnp.int32, (ne, ne), 1)
    lower = jnp.where(e_c < e_r, 1.0, 0.0).astype(BF16)
    off = _bdot(lower, cnt.astype(BF16))
    base = off[:, 0:1] + prefix
    sub_k = lax.broadcasted_iota(jnp.int32, (TOP_K, tr), 0)
    gate = jnp.zeros((TOP_K, tr), F32)
    pos = jnp.zeros((TOP_K, tr), F32)
    for k in range(TOP_K):
        pk = jnp.sum(jnp.where(hots[k], base, 0.0), axis=0, keepdims=True)
        gate = jnp.where(sub_k == k, exps[k] / den, gate)
        pos = jnp.where(sub_k == k, pk, pos)
    cnt_row = lax.dot_general(jnp.ones((8, tr), BF16), member.astype(BF16),
                              (((1,), (1,)), ((), ())), preferred_element_type=F32)
    return gate, pos.astype(jnp.int32), cnt_row


def _route(logits_t, tr, tiles_per_step=4):
    ne, t = logits_t.shape
    nt = t // tr
    tb = tr * tiles_per_step
    tok = lambda i: (0, i)
    tile = lambda i: (i, 0, 0)
    return pl.pallas_call(
        _route_kernel,
        grid=(nt // tiles_per_step,),
        in_specs=[pl.BlockSpec((ne, tb), tok)],
        out_specs=[pl.BlockSpec((TOP_K, tb), tok), pl.BlockSpec((TOP_K, tb), tok),
                   pl.BlockSpec((tiles_per_step, 8, ne), tile),
                   pl.BlockSpec((tiles_per_step, 8, ne), tile),
                   pl.BlockSpec((8, ne), lambda i: (0, 0))],
        out_shape=[jax.ShapeDtypeStruct((TOP_K, t), F32),
                   jax.ShapeDtypeStruct((TOP_K, t), jnp.int32),
                   jax.ShapeDtypeStruct((nt, 8, ne), jnp.int32),
                   jax.ShapeDtypeStruct((nt, 8, ne), jnp.int32),
                   jax.ShapeDtypeStruct((8, ne), jnp.int32)],
        scratch_shapes=[pltpu.VMEM((8, ne), F32)],
        compiler_params=_cparams(("arbitrary",)),
        name="route_topk",
    )(logits_t)


def _row_tile(d):
    return (d // 256, 128)


def _pack_rows(v, bf16_exact=False):
    n, d = v.shape
    if bf16_exact:
        bits = lax.bitcast_convert_type(v, jnp.uint32)
        word = bits[:, d // 2:] | (bits[:, :d // 2] >> 16)
    else:
        bits = lax.bitcast_convert_type(v.astype(BF16).astype(F32), jnp.uint32)
        word = (bits[:, d // 2:] & jnp.uint32(0xFFFF0000)) | (bits[:, :d // 2] >> 16)
    return word.reshape((n,) + _row_tile(d))


def _unpack_rows(w):
    n = w.shape[0]
    word = w.reshape(n, w.shape[1] * w.shape[2])
    lo = lax.bitcast_convert_type(word << 16, F32)
    hi = lax.bitcast_convert_type(word & jnp.uint32(0xFFFF0000), F32)
    return jnp.concatenate([lo, hi], axis=1)


def _strip_copies(n_ref, dst_ref, make_copy, max_rows):
    ne = dst_ref.shape[2]
    all_bits = [1 << s for s in range(max_rows.bit_length() - 1, -1, -1)]
    small = pl.next_power_of_2(max(1, 2 * TOP_K * max_rows // ne))
    longest = n_ref[0, 0, ne]

    def run(bits):
        def body(e, off):
            n = n_ref[0, 0, e]
            dst = dst_ref[0, 0, e]
            done = jnp.int32(0)
            for bit in bits:
                part = n & bit

                @pl.when(part != 0)
                def _():
                    make_copy(off + done, dst + done, bit).start()

                done = done + part
            return off + n

        lax.fori_loop(0, ne, body, jnp.int32(0))

    @pl.when(longest < small)
    def _():
        run([b for b in all_bits if b < small])

    @pl.when(longest >= small)
    def _():
        run(all_bits)


def _dispatch_kernel(n_ref, dst_ref, pos_ref, h_ref, xs_ref, srt_ref, sem):
    i = pl.program_id(0)
    slot = i % 2
    td = h_ref.shape[0]
    n = td * TOP_K

    @pl.when(i == 0)
    def _():
        tail = xs_ref.shape[0] - EXPERT_TILE
        srt_ref[1, 0:EXPERT_TILE] = jnp.zeros((EXPERT_TILE,) + srt_ref.shape[2:], srt_ref.dtype)
        zero_tail = pltpu.make_async_copy(srt_ref.at[1, pl.ds(0, EXPERT_TILE)],
                                          xs_ref.at[pl.ds(tail, EXPERT_TILE)], sem.at[1])
        zero_tail.start()
        zero_tail.wait()

    pos = pos_ref[...]
    rows = lax.broadcasted_iota(jnp.int32, (n, td), 0)
    hit = rows == pos[0:1]
    for k in range(1, TOP_K):
        hit = hit | (rows == pos[k:k + 1])
    perm = jnp.where(hit, 1.0, 0.0).astype(BF16)
    srt_ref[slot] = _pack_rows(_bdot(perm, h_ref[...]), bf16_exact=True)

    def all_rows(s):
        return pltpu.make_async_copy(srt_ref.at[s], xs_ref.at[pl.ds(0, n)], sem.at[s])

    _strip_copies(n_ref, dst_ref,
                  lambda src, dst, size: pltpu.make_async_copy(
                      srt_ref.at[slot, pl.ds(src, size)], xs_ref.at[pl.ds(dst, size)], sem.at[slot]),
                  td)

    @pl.when(i > 0)
    def _():
        all_rows(1 - slot).wait()

    @pl.when(i == pl.num_programs(0) - 1)
    def _():
        all_rows(slot).wait()


def _dispatch(h2, pos_t, tile_n, tile_dst, td):
    t, d = h2.shape
    a = t * TOP_K
    n = td * TOP_K
    nt = t // td
    smem = lambda v: pl.BlockSpec((1, 1, v.shape[-1]), lambda i: (i, 0, 0),
                                  memory_space=pltpu.SMEM)
    return pl.pallas_call(
        _dispatch_kernel,
        grid=(nt,),
        in_specs=[smem(tile_n), smem(tile_dst),
                  pl.BlockSpec((TOP_K, td), lambda i: (0, i)),
                  pl.BlockSpec((td, d), lambda i: (i, 0))],
        out_specs=pl.BlockSpec(memory_space=pl.ANY),
        out_shape=jax.ShapeDtypeStruct((a + EXPERT_TILE,) + _row_tile(d), jnp.uint32),
        scratch_shapes=[pltpu.VMEM((2, n) + _row_tile(d), jnp.uint32),
                        pltpu.SemaphoreType.DMA((2,))],
        compiler_params=_cparams(("arbitrary",)),
        name="moe_dispatch",
    )(tile_n[:, None, :], tile_dst[:, None, :], pos_t, h2)


def _expert_kernel(exp_ref, row_ref, cls_ref, nitem_ref,
                   xs_ref, w1_ref, b1_ref, w2_ref, b2_ref, ys_ref,
                   w1b_ref, w2b_ref, xbuf_ref, ybuf_ref, sem_in, sem_out):
    w = pl.program_id(0)
    n_items = nitem_ref[0]
    last = exp_ref.shape[0] - 1
    slot = w % 2
    dff = w2_ref.shape[1]

    def fetch(item, s):
        return pltpu.make_async_copy(xs_ref.at[pl.ds(row_ref[item], EXPERT_TILE)],
                                     xbuf_ref.at[s], sem_in.at[s])

    def write(item, s, rows):
        return pltpu.make_async_copy(ybuf_ref.at[s, pl.ds(0, rows)],
                                     ys_ref.at[pl.ds(row_ref[item], rows)], sem_out.at[s])

    def for_tile_size(item, fn):
        for c, rows in enumerate(EXPERT_LAST_TILES):
            @pl.when(cls_ref[item] == c)
            def _():
                fn(rows)

    @pl.when(w == 0)
    def _():
        fetch(0, 0).start()
        ybuf_ref[1] = jnp.zeros(ybuf_ref.shape[1:], ybuf_ref.dtype)
        zero_tail = pltpu.make_async_copy(
            ybuf_ref.at[1], ys_ref.at[pl.ds(ys_ref.shape[0] - EXPERT_TILE, EXPERT_TILE)],
            sem_out.at[1])
        zero_tail.start()
        zero_tail.wait()

    @pl.when(w + 1 < n_items)
    def _():
        fetch(jnp.minimum(w + 1, last), 1 - slot).start()

    @pl.when((w == 0) | (exp_ref[w] != exp_ref[jnp.maximum(w - 1, 0)]))
    def _():
        w1b_ref[...] = w1_ref[0].astype(BF16)
        w2b_ref[...] = w2_ref[0].astype(BF16)

    @pl.when(w < n_items)
    def _():
        fetch(w, slot).wait()

        def compute(rows):
            x = _unpack_rows(xbuf_ref[slot, 0:rows]).astype(BF16)
            hb = _bdot(x, w1b_ref[...]) + b1_ref[0]
            glu = jnp.minimum(hb[:, :dff], SWIGLU_LIMIT)
            lin = jnp.clip(hb[:, dff:], -SWIGLU_LIMIT, SWIGLU_LIMIT)
            act = glu * jax.nn.sigmoid(SWIGLU_ALPHA * glu) * (lin + 1.0)
            ybuf_ref[slot, 0:rows] = _pack_rows(_bdot(act.astype(BF16), w2b_ref[...]) + b2_ref[0])

        for_tile_size(w, compute)

        @pl.when(w > 0)
        def _():
            for_tile_size(w - 1, lambda rows: write(w - 1, 1 - slot, rows).wait())

        for_tile_size(w, lambda rows: write(w, slot, rows).start())

        @pl.when(w == n_items - 1)
        def _():
            for_tile_size(w, lambda rows: write(w, slot, rows).wait())


def _experts(xs, items, w1, b1, w2, b2):
    ne, d, dff2 = w1.shape
    dff = dff2 // 2
    exp_w, row_w, cls_w, nitem = items
    per_expert = lambda w, ex, ro, cl, ni: (ex[w], 0, 0)
    tile = (EXPERT_TILE,) + _row_tile(d)
    grid_spec = pltpu.PrefetchScalarGridSpec(
        num_scalar_prefetch=4,
        grid=(exp_w.shape[0],),
        in_specs=[pl.BlockSpec(memory_space=pl.ANY),
                  pl.BlockSpec((1, d, dff2), per_expert),
                  pl.BlockSpec((1, 1, dff2), per_expert),
                  pl.BlockSpec((1, dff, d), per_expert),
                  pl.BlockSpec((1, 1, d), per_expert)],
        out_specs=pl.BlockSpec(memory_space=pl.ANY),
        scratch_shapes=[pltpu.VMEM((d, dff2), BF16), pltpu.VMEM((dff, d), BF16),
                        pltpu.VMEM((2,) + tile, jnp.uint32), pltpu.VMEM((2,) + tile, jnp.uint32),
                        pltpu.SemaphoreType.DMA((2,)), pltpu.SemaphoreType.DMA((2,))],
    )
    return pl.pallas_call(
        _expert_kernel,
        grid_spec=grid_spec,
        out_shape=jax.ShapeDtypeStruct(xs.shape, jnp.uint32),
        compiler_params=_cparams(("arbitrary",)),
        name="moe_experts",
    )(exp_w, row_w, cls_w, nitem, xs, w1, b1.reshape(ne, 1, dff2), w2, b2.reshape(ne, 1, d))


def _work_items(counts, a):
    ne = counts.shape[0]
    n_items = a // EXPERT_TILE + ne
    ends = jnp.cumsum(counts)
    starts = ends - counts
    n_full = counts // EXPERT_TILE
    rem = counts - n_full * EXPERT_TILE
    n_tiles = n_full + (rem > 0).astype(jnp.int32)
    tile_end = jnp.cumsum(n_tiles)
    tile_start = tile_end - n_tiles
    total = tile_end[-1]
    w = jnp.minimum(jnp.arange(n_items, dtype=jnp.int32), total - 1)
    e_w = jnp.sum((tile_end[None, :] <= w[:, None]).astype(jnp.int32), axis=1)
    hot = e_w[:, None] == jnp.arange(ne, dtype=jnp.int32)[None, :]
    pick = lambda v: jnp.sum(jnp.where(hot, v[None, :], 0), axis=1)
    k = w - pick(tile_start)
    row_w = (pick(starts) + k * EXPERT_TILE).astype(jnp.int32)
    rem_w = pick(rem)
    rem_cls = jnp.zeros_like(rem_w)
    for c, rows in enumerate(EXPERT_LAST_TILES[:-1]):
        rem_cls = rem_cls + (rem_w > rows).astype(jnp.int32)
    full_cls = len(EXPERT_LAST_TILES) - 1
    cls_w = jnp.where(k < pick(n_full), full_cls, rem_cls).astype(jnp.int32)
    return starts, (e_w.astype(jnp.int32), row_w, cls_w, total.reshape(1).astype(jnp.int32))


def _combine_kernel(alpha, n_ref, dst_ref, nn_ref, ndst_ref, pos_ref, gate_ref,
                    ys_ref, xm_ref, g2_ref, lg_ref, lb_ref, o_ref, srt_ref, sem):
    i = pl.program_id(0)
    last = pl.num_programs(0) - 1
    slot = i % 2
    td = xm_ref.shape[0]
    n = td * TOP_K

    def fetch(cnt_ref, from_ref, s):
        _strip_copies(cnt_ref, from_ref,
                      lambda row, src, size: pltpu.make_async_copy(
                          ys_ref.at[pl.ds(src, size)], srt_ref.at[s, pl.ds(row, size)], sem.at[s]),
                      td)

    @pl.when(i == 0)
    def _():
        fetch(n_ref, dst_ref, slot)

    @pl.when(i < last)
    def _():
        fetch(nn_ref, ndst_ref, 1 - slot)

    pltpu.make_async_copy(ys_ref.at[pl.ds(0, n)], srt_ref.at[slot], sem.at[slot]).wait()

    pos, gate = jnp.transpose(pos_ref[...]), jnp.transpose(gate_ref[...])
    lanes = lax.broadcasted_iota(jnp.int32, (td, n), 1)
    wsel = jnp.where(lanes == pos[:, 0:1], gate[:, 0:1], 0.0)
    for k in range(1, TOP_K):
        wsel = wsel + jnp.where(lanes == pos[:, k:k + 1], gate[:, k:k + 1], 0.0)
    y = _unpack_rows(srt_ref[slot]).astype(BF16)
    f = _bdot(wsel.astype(BF16), y)
    o_ref[...] = _ln(alpha * xm_ref[...] + g2_ref[0] * f) * lg_ref[...] + lb_ref[...]


def _combine(ys, pos, gate, tile_n, tile_dst, x_mid, g2, lg, lb, alpha, td):
    t, d = x_mid.shape
    n = td * TOP_K
    nt = t // td
    per_b = nt // g2.shape[0]
    first = lambda v: pl.BlockSpec((1, 1, v.shape[-1]), lambda i: (0, 0, 0),
                                   memory_space=pltpu.SMEM)
    nxt = lambda v: pl.BlockSpec((1, 1, v.shape[-1]),
                                 lambda i: (jnp.minimum(i + 1, nt - 1), 0, 0),
                                 memory_space=pltpu.SMEM)
    const = lambda i: (0, 0)
    tile_n = tile_n[:, None, :]
    tile_dst = tile_dst[:, None, :]
    return pl.pallas_call(
        functools.partial(_combine_kernel, alpha),
        grid=(nt,),
        in_specs=[first(tile_n), first(tile_dst), nxt(tile_n), nxt(tile_dst),
                  pl.BlockSpec((TOP_K, td), lambda i: (0, i)),
                  pl.BlockSpec((TOP_K, td), lambda i: (0, i)),
                  pl.BlockSpec(memory_space=pl.ANY),
                  pl.BlockSpec((td, d), lambda i: (i, 0)),
                  pl.BlockSpec((1, 1, d), lambda i: (i // per_b, 0, 0)),
                  pl.BlockSpec((1, d), const), pl.BlockSpec((1, d), const)],
        out_specs=pl.BlockSpec((td, d), lambda i: (i, 0)),
        out_shape=jax.ShapeDtypeStruct((t, d), F32),
        scratch_shapes=[pltpu.VMEM((2, n) + _row_tile(d), jnp.uint32),
                        pltpu.SemaphoreType.DMA((2,))],
        compiler_params=_cparams(("arbitrary",)),
        name="moe_combine_ln",
    )(tile_n, tile_dst, tile_n, tile_dst, pos, gate, ys, x_mid, g2,
      lg.reshape(1, d), lb.reshape(1, d))


def _rope_tables(l):
    f32 = np.float32
    rows = l // GRID_W
    row = np.repeat(np.arange(rows, dtype=f32), GRID_W)
    col = np.tile(np.arange(GRID_W, dtype=f32), rows)
    n_freq = HEAD_DIM // 4
    inv_freq = np.power(f32(ROPE_BASE), -np.arange(n_freq, dtype=f32) / f32(n_freq)).astype(f32)
    ang_r = (row[:, None] * inv_freq).astype(f32)
    ang_c = (col[:, None] * inv_freq).astype(f32)
    zero = np.zeros_like(ang_r)
    cos_r, sin_r, cos_c, sin_c = np.cos(ang_r), np.sin(ang_r), np.cos(ang_c), np.sin(ang_c)
    cos_h = np.concatenate([cos_r, cos_r, cos_c, cos_c], axis=1)
    sa_h = np.concatenate([-sin_r, zero, -sin_c, zero], axis=1)
    sb_h = np.concatenate([zero, sin_r, zero, sin_c], axis=1)
    rep = 128 // HEAD_DIM
    return tuple(np.tile(a, (1, rep)).astype(f32) for a in (cos_h, sa_h, sb_h))


def _filter_features(l):
    f32 = np.float32
    bands = (HY_EMB_DIM - 1) // 2
    t = np.linspace(0.0, 1.0, l, dtype=f32)[:, None]
    omega = (f32(2.0 * math.pi) * np.arange(l, dtype=f32)[:, None] / f32(l)).astype(f32)
    f = np.linspace(1e-4, bands - 1, bands, dtype=f32)[None, :]
    ang = (f * omega).astype(f32)
    z = np.concatenate([t, np.cos(ang), -np.sin(ang)], axis=-1).astype(f32)
    min_decay = math.log(HY_DECAY_TARGET) / HY_FAST_DECAY_PCT
    max_decay = math.log(HY_DECAY_TARGET) / HY_SLOW_DECAY_PCT
    deltas = np.abs(np.linspace(min_decay, max_decay, HY_WIDTH, dtype=f32))
    decay = np.exp(-t * deltas).astype(f32)
    return z, decay


def kernel(x, c, ctx, c_ctx, w_mod, b_mod, w_in, attn_sink, hy_conv_w, hy_conv_b, hy_filt_w1,
           hy_filt_b1, hy_filt_w2, hy_filt_b2, hy_filt_w3, hy_skip, w_branch_attn, w_branch_hyena,
           w_out, ln1_g, ln1_b, router_w, router_b, exp_w1, exp_b1, exp_w2, exp_b2, ln2_g, ln2_b):
    depth = w_mod.shape[0]
    assert depth == 1, "only the single-layer configuration is implemented"
    b, l, d = x.shape
    t = b * l
    alpha = (2 * depth) ** 0.25
    hy_wc = 256

    n_cond = b + 1
    pad = (-n_cond) % 8
    cond = jnp.concatenate([c, c_ctx[None], jnp.zeros((pad, d), F32)], axis=0)
    mod = _modulation(cond, w_mod[0], b_mod[0])
    mod_x = mod[:b].reshape(b, 1, 6, d)
    sh1, sc1, g1, sh2, sc2, g2 = (mod_x[:, :, i] for i in range(6))
    mod_c = mod[b:b + 1].reshape(1, 1, 6, d)
    csh1, csc1 = mod_c[:, :, 0], mod_c[:, :, 1]

    w_in_b = w_in[0].astype(BF16)
    cos_t, sa_t, sb_t = (jnp.asarray(a) for a in _rope_tables(l))
    q, k, v, u_hy, gate_x = _in_projection(x, sh1, sc1, w_in_b, cos_t, sa_t, sb_t)
    k_c, v_c = _ctx_kv(ctx, csh1, csc1, w_in_b[:, K_OFF:HY_OFF])
    att = _attention(q, k, v, k_c, v_c, attn_sink[0])

    tabs_np = _dft_tables(l, hy_wc)
    tabs = (jnp.asarray(tabs_np[0]).astype(BF16), jnp.asarray(tabs_np[1]).astype(BF16)) + tuple(
        jnp.asarray(a) for a in tabs_np[2:])
    zfeat, decay = _filter_features(l)
    emb_pad = (-HY_EMB_DIM) % 128
    zfeat = np.pad(zfeat, ((0, 0), (0, emb_pad)))
    fw1 = jnp.pad(hy_filt_w1[0], ((0, emb_pad), (0, 0)))
    spectra = _hyena_filters(jnp.asarray(zfeat[0::2]), jnp.asarray(zfeat[1::2]), fw1, hy_filt_b1[0],
                             hy_filt_w2[0], hy_filt_b2[0], hy_filt_w3[0], jnp.asarray(decay[0::2]),
                             jnp.asarray(decay[1::2]), tabs, hy_wc)
    hy = _hyena(u_hy, hy_conv_w[0], hy_conv_b[0], hy_skip[0], spectra, tabs, hy_wc)

    x_mid, h2, logits_t = _merge(att.reshape(t, Q_W), hy.reshape(t, HY_WIDTH),
                                 gate_x.reshape(t, 2 * d), x.reshape(t, d), g1, sh2, sc2,
                                 w_branch_attn[0].astype(BF16), w_branch_hyena[0].astype(BF16),
                                 w_out[0].astype(BF16), ln1_g[0], ln1_b[0],
                                 jnp.transpose(router_w[0]), router_b[0], alpha)

    moe_td = 256
    gate_t, pos_t, tile_cnt, tile_base, counts = _route(logits_t, moe_td)
    starts, items = _work_items(counts[0], t * TOP_K)
    tile_n = tile_cnt[:, 0, :]
    tile_n = jnp.concatenate([tile_n, jnp.max(tile_n, axis=1, keepdims=True)], axis=1)
    tile_dst = starts[None, :] + tile_base[:, 0, :]
    xs = _dispatch(h2.reshape(t, d), pos_t, tile_n, tile_dst, moe_td)
    ys = _experts(xs, items, exp_w1[0], exp_b1[0], exp_w2[0], exp_b2[0])
    out = _combine(ys, pos_t, gate_t, tile_n, tile_dst,
                   x_mid.reshape(t, d), g2, ln2_g[0], ln2_b[0], alpha, moe_td)
    return out.reshape(b, l, d)
```

```python
import functools
import math

import numpy as np
import jax
import jax.numpy as jnp
from jax import lax
from jax.experimental import pallas as pl
from jax.experimental.pallas import tpu as pltpu

F32 = jnp.float32
BF16 = jnp.bfloat16
HIGHEST = lax.Precision.HIGHEST

GRID_W = 64
N_HEADS = 8
N_KV_HEADS = 2
GQA_GROUP = N_HEADS // N_KV_HEADS
HEAD_DIM = 64
WINDOW = 128
ATTN_BLOCK = 128
ROPE_BASE = 10000.0

HY_WIDTH = 512
HY_ORDER = 2
HY_SHORT_CONV = 3
HY_EMB_DIM = 33
HY_DECAY_TARGET = 1e-2
HY_FAST_DECAY_PCT = 0.3
HY_SLOW_DECAY_PCT = 1.5

N_EXPERTS = 32
TOP_K = 4
SWIGLU_LIMIT = 7.0
SWIGLU_ALPHA = 1.702
LN_EPS = 1e-5

Q_W = N_HEADS * HEAD_DIM
KV_W = N_KV_HEADS * HEAD_DIM
K_OFF = Q_W
V_OFF = K_OFF + KV_W
HY_OFF = V_OFF + KV_W
GATE_OFF = HY_OFF + (HY_ORDER + 1) * HY_WIDTH

VMEM_LIMIT = 56 * 1024 * 1024
NEG_BIG = -1e30
MERGE_CHUNKS = 2
EXPERT_TILE = 1024
EXPERT_LAST_TILES = (256, 512, 1024)


def _cparams(sem):
    return pltpu.CompilerParams(dimension_semantics=sem, vmem_limit_bytes=VMEM_LIMIT)


def _ln(x):
    mu = jnp.mean(x, axis=-1, keepdims=True)
    xc = x - mu
    var = jnp.mean(xc * xc, axis=-1, keepdims=True)
    return xc * lax.rsqrt(var + LN_EPS)


def _bdot(a, b):
    return jnp.dot(a, b, preferred_element_type=F32)


def _fdot(a, b):
    return jnp.dot(a, b, preferred_element_type=F32, precision=HIGHEST)


def _mod_kernel(c_ref, w_ref, b_ref, o_ref):
    c = c_ref[...]
    s = c * jax.nn.sigmoid(c)
    o_ref[...] = _fdot(s, w_ref[...]) + b_ref[...]


def _modulation(cond, w, b, tn=512):
    r, d = cond.shape
    n = w.shape[1]
    return pl.pallas_call(
        _mod_kernel,
        grid=(n // tn,),
        in_specs=[pl.BlockSpec((r, d), lambda j: (0, 0)),
                  pl.BlockSpec((d, tn), lambda j: (0, j)),
                  pl.BlockSpec((1, tn), lambda j: (0, j))],
        out_specs=pl.BlockSpec((r, tn), lambda j: (0, j)),
        out_shape=jax.ShapeDtypeStruct((r, n), F32),
        compiler_params=_cparams(("arbitrary",)),
        name="modulation",
    )(cond, w, b.reshape(1, n))


def _rope(t, cos, sa, sb):
    n = t.shape[-1]
    return t * cos + pltpu.roll(t, n - 16, 1) * sa + pltpu.roll(t, 16, 1) * sb


def _inproj_kernel(x_ref, sh_ref, sc_ref, w_ref, cos_ref, sa_ref, sb_ref,
                   q_ref, k_ref, v_ref, u_ref, g_ref, hb_ref):
    @pl.when(pl.program_id(0) == 0)
    def _():
        hb_ref[...] = jnp.zeros_like(hb_ref)

    tm = x_ref.shape[1]
    scale = HEAD_DIM ** -0.5
    n_gate = w_ref.shape[1] - GATE_OFF
    rc = tm // MERGE_CHUNKS
    for r in (slice(c * rc, (c + 1) * rc) for c in range(MERGE_CHUNKS)):
        hb = hb_ref[r]
        cos, sa, sb = cos_ref[r], sa_ref[r], sb_ref[r]
        for j in range(Q_W // 128):
            t = _bdot(hb, w_ref[:, j * 128:(j + 1) * 128])
            q_ref[0, r, j * 128:(j + 1) * 128] = (_rope(t, cos, sa, sb) * scale).astype(BF16)
        t = _bdot(hb, w_ref[:, K_OFF:V_OFF])
        k_ref[0, r] = _rope(t, cos, sa, sb).astype(BF16)
        v_ref[0, r] = _bdot(hb, w_ref[:, V_OFF:HY_OFF]).astype(BF16)
        for j in range((GATE_OFF - HY_OFF) // 512):
            u_ref[0, r, j * 512:(j + 1) * 512] = _bdot(
                hb, w_ref[:, HY_OFF + j * 512:HY_OFF + (j + 1) * 512])
        for j in range(n_gate // 512):
            g_ref[0, r, j * 512:(j + 1) * 512] = jax.nn.sigmoid(_bdot(
                hb, w_ref[:, GATE_OFF + j * 512:GATE_OFF + (j + 1) * 512])).astype(BF16)

        h = _ln(x_ref[0, r]) * (1.0 + sc_ref[0]) + sh_ref[0]
        hb_ref[r] = h.astype(BF16)


def _in_projection(x, sh, sc, w_in_b, cos_t, sa_t, sb_t, tm=512):
    b, l, d = x.shape
    in_w = w_in_b.shape[1]
    hy_w = GATE_OFF - HY_OFF
    g_w = in_w - GATE_OFF
    per_b = l // tm
    nt = b * per_b
    t1 = lambda i: jnp.minimum(i, nt - 1)
    t2 = lambda i: jnp.maximum(i - 1, 0)
    row1 = lambda i: (t1(i) // per_b, t1(i) % per_b, 0)
    row2 = lambda i: (t2(i) // per_b, t2(i) % per_b, 0)
    vec1 = lambda i: (t1(i) // per_b, 0, 0)
    tab2 = lambda i: (t2(i) % per_b, 0)
    return pl.pallas_call(
        _inproj_kernel,
        grid=(nt + 1,),
        in_specs=[pl.BlockSpec((1, tm, d), row1),
                  pl.BlockSpec((1, 1, d), vec1),
                  pl.BlockSpec((1, 1, d), vec1),
                  pl.BlockSpec((d, in_w), lambda i: (0, 0)),
                  pl.BlockSpec((tm, 128), tab2),
                  pl.BlockSpec((tm, 128), tab2),
                  pl.BlockSpec((tm, 128), tab2)],
        out_specs=[pl.BlockSpec((1, tm, Q_W), row2),
                   pl.BlockSpec((1, tm, KV_W), row2),
                   pl.BlockSpec((1, tm, KV_W), row2),
                   pl.BlockSpec((1, tm, hy_w), row2),
                   pl.BlockSpec((1, tm, g_w), row2)],
        out_shape=[jax.ShapeDtypeStruct((b, l, Q_W), BF16),
                   jax.ShapeDtypeStruct((b, l, KV_W), BF16),
                   jax.ShapeDtypeStruct((b, l, KV_W), BF16),
                   jax.ShapeDtypeStruct((b, l, hy_w), F32),
                   jax.ShapeDtypeStruct((b, l, g_w), BF16)],
        scratch_shapes=[pltpu.VMEM((tm, d), BF16)],
        compiler_params=_cparams(("arbitrary",)),
        name="in_projection",
    )(x, sh, sc, w_in_b, cos_t, sa_t, sb_t)


def _ctx_kv_kernel(x_ref, sh_ref, sc_ref, w_ref, k_ref, v_ref):
    h = _ln(x_ref[0]) * (1.0 + sc_ref[0]) + sh_ref[0]
    kv = _bdot(h.astype(BF16), w_ref[...])
    k_ref[0] = kv[:, :KV_W].astype(BF16)
    v_ref[0] = kv[:, KV_W:].astype(BF16)


def _ctx_kv(ctx, sh, sc, w_kv_b):
    b, c, d = ctx.shape
    row = lambda bi: (bi, 0, 0)
    return pl.pallas_call(
        _ctx_kv_kernel,
        grid=(b,),
        in_specs=[pl.BlockSpec((1, c, d), row),
                  pl.BlockSpec((1, 1, d), lambda bi: (0, 0, 0)),
                  pl.BlockSpec((1, 1, d), lambda bi: (0, 0, 0)),
                  pl.BlockSpec((d, 2 * KV_W), lambda bi: (0, 0))],
        out_specs=[pl.BlockSpec((1, c, KV_W), row), pl.BlockSpec((1, c, KV_W), row)],
        out_shape=[jax.ShapeDtypeStruct((b, c, KV_W), BF16),
                   jax.ShapeDtypeStruct((b, c, KV_W), BF16)],
        compiler_params=_cparams(("arbitrary",)),
        name="ctx_kv",
    )(ctx, sh, sc, w_kv_b)


def _attn_kernel(sink_ref, q_ref, k_ref, v_ref, kc_ref, vc_ref, bias_ref, o_ref):
    for qb in range(q_ref.shape[1] // ATTN_BLOCK):
        rows = slice(qb * ATTN_BLOCK, (qb + 1) * ATTN_BLOCK)
        j = pl.program_id(1) * (q_ref.shape[1] // ATTN_BLOCK) + qb
        o_ref[0, rows] = _attn_block(j, sink_ref, q_ref.at[0, rows], k_ref, v_ref, kc_ref, vc_ref,
                                     bias_ref)


def _band_bias():
    span = ATTN_BLOCK + 2 * WINDOW
    q = np.arange(GQA_GROUP * ATTN_BLOCK)[None, :, None] % ATTN_BLOCK
    k = np.arange(span)[None, None, :] - ATTN_BLOCK * np.arange(span // ATTN_BLOCK)[:, None, None]
    return np.where(np.abs(k - q) <= WINDOW, 0.0, NEG_BIG).astype(np.float32)


def _attn_block(j, sink_ref, q_ref, k_ref, v_ref, kc_ref, vc_ref, bias_ref):
    l = k_ref.shape[1]
    span = ATTN_BLOCK + 2 * WINDOW
    q0 = j * ATTN_BLOCK
    start = pl.multiple_of(jnp.clip(q0 - WINDOW, 0, l - span), ATTN_BLOCK)
    rows = GQA_GROUP * ATTN_BLOCK
    bias = bias_ref[(q0 - start) // ATTN_BLOCK]
    head_of_row = lax.broadcasted_iota(jnp.int32, (rows, 1), 0) // ATTN_BLOCK
    dn = (((1,), (1,)), ((), ()))
    outs = []
    for kv in range(N_KV_HEADS):
        ks = slice(kv * HEAD_DIM, (kv + 1) * HEAD_DIM)
        kl = k_ref[0, pl.ds(start, span), ks]
        vl = v_ref[0, pl.ds(start, span), ks]
        kc = kc_ref[0, :, ks]
        vc = vc_ref[0, :, ks]
        heads = [kv * GQA_GROUP + g for g in range(GQA_GROUP)]
        qg = jnp.concatenate([q_ref[:, h * HEAD_DIM:(h + 1) * HEAD_DIM] for h in heads], axis=0)
        sink = jnp.zeros((rows, 1), F32)
        for g, h in enumerate(heads):
            sink = jnp.where(head_of_row == g, sink_ref[h], sink)
        s_loc = lax.dot_general(qg, kl, dn, preferred_element_type=F32) + bias
        s_ctx = lax.dot_general(qg, kc, dn, preferred_element_type=F32)
        blocks = [s[:, c:c + 128] for s in (s_loc, s_ctx) for c in range(0, s.shape[1], 128)]
        folded = functools.reduce(jnp.maximum, blocks)
        m = jnp.maximum(jnp.max(folded, axis=1, keepdims=True), sink)
        p_loc = jnp.exp(s_loc - m).astype(BF16)
        p_ctx = jnp.exp(s_ctx - m).astype(BF16)
        ones_l = jnp.ones((span, HEAD_DIM), BF16)
        ones_c = jnp.ones((kc.shape[0], HEAD_DIM), BF16)
        pv = (_bdot(p_loc, jnp.concatenate([vl, ones_l], axis=1))
              + _bdot(p_ctx, jnp.concatenate([vc, ones_c], axis=1)))
        den = pv[:, HEAD_DIM:HEAD_DIM + 1] + jnp.exp(sink - m)
        o = pv[:, :HEAD_DIM] / den
        outs.extend(o[g * ATTN_BLOCK:(g + 1) * ATTN_BLOCK] for g in range(GQA_GROUP))
    return jnp.concatenate(outs, axis=1).astype(BF16)


def _attention(q, k, v, kc, vc, sink, q_blocks=8):
    b, l, _ = q.shape
    c = kc.shape[1]
    tq = q_blocks * ATTN_BLOCK
    bias = jnp.asarray(_band_bias())
    full = lambda bi, j, s: (bi, 0, 0)
    grid_spec = pltpu.PrefetchScalarGridSpec(
        num_scalar_prefetch=1,
        grid=(b, l // tq),
        in_specs=[pl.BlockSpec((1, tq, Q_W), lambda bi, j, s: (bi, j, 0)),
                  pl.BlockSpec((1, l, KV_W), full),
                  pl.BlockSpec((1, l, KV_W), full),
                  pl.BlockSpec((1, c, KV_W), full),
                  pl.BlockSpec((1, c, KV_W), full),
                  pl.BlockSpec(bias.shape, lambda bi, j, s: (0, 0, 0))],
        out_specs=pl.BlockSpec((1, tq, Q_W), lambda bi, j, s: (bi, j, 0)),
    )
    return pl.pallas_call(
        _attn_kernel,
        grid_spec=grid_spec,
        out_shape=jax.ShapeDtypeStruct((b, l, Q_W), BF16),
        compiler_params=_cparams(("arbitrary", "arbitrary")),
        name="window_attention",
    )(sink, q, k, v, kc, vc, bias)


def _dft_tables(l, wc):
    h = l // 2
    idx = np.arange(h, dtype=np.int64)
    ang = 2.0 * np.pi * ((idx[:, None] * idx[None, :]) % l).astype(np.float64) / l
    cm = np.cos(ang).astype(np.float32)
    sm = np.sin(ang).astype(np.float32)
    tw = 2.0 * np.pi * idx.astype(np.float64) / (2 * l)
    ones = np.ones((1, wc), np.float32)
    wr = np.cos(tw).astype(np.float32)[:, None] * ones
    wi = (-np.sin(tw)).astype(np.float32)[:, None] * ones
    sgn = np.where(idx % 2 == 0, 1.0, -1.0).astype(np.float32)[:, None] * ones
    wk = np.where(idx == 0, 1.0 / (2 * l), 2.0 / (2 * l)).astype(np.float32)[:, None] * ones
    return cm, sm, wr, wi, sgn, wk


def _half_spectrum(se, so, cm, sm, wr, wi, sgn):
    seb, sob = se.astype(BF16), so.astype(BF16)
    ce, ss_e = _bdot(cm, seb), _bdot(sm, seb)
    co, ss_o = _bdot(cm, sob), _bdot(sm, sob)
    vr = wr * co + wi * ss_o
    vi = wi * co - wr * ss_o
    zar, zai = ce + vr, vi - ss_e
    zbr, zbi = ce - vr, -ss_e - vi
    e_ny = jnp.sum(sgn * se, axis=0, keepdims=True)
    o_ny = jnp.sum(sgn * so, axis=0, keepdims=True)
    return zar, zai, zbr, zbi, e_ny, -o_ny


def _filter_kernel(ze_ref, zo_ref, w1_ref, b1_ref, w2_ref, b2_ref, w3f_ref, w3b_ref,
                   de_ref, do_ref, cm_ref, sm_ref, wr_ref, wi_ref, sgn_ref, wk_ref,
                   har_ref, hai_ref, hbr_ref, hbi_ref, hny_ref, ae_ref, ao_ref):
    cm, sm = cm_ref[...], sm_ref[...]
    wr, wi, sgn, wk = wr_ref[...], wi_ref[...], sgn_ref[...], wk_ref[...]
    h = cm.shape[0]

    @pl.when((pl.program_id(0) == 0) & (pl.program_id(1) == 0))
    def _():
        for z_ref, a_ref in ((ze_ref, ae_ref), (zo_ref, ao_ref)):
            a = jnp.sin(_fdot(z_ref[...], w1_ref[...]) + b1_ref[...])
            a_ref[...] = jnp.sin(_fdot(a, w2_ref[...]) + b2_ref[...])

    def taps(a_ref, w3_ref, d_ref):
        return _fdot(a_ref[...], w3_ref[0]) * d_ref[...]

    fe, fo = taps(ae_ref, w3f_ref, de_ref), taps(ao_ref, w3f_ref, do_ref)
    be, bo = taps(ae_ref, w3b_ref, de_ref), taps(ao_ref, w3b_ref, do_ref)
    row = lax.broadcasted_iota(jnp.int32, be.shape, 0)
    be = jnp.where(row == 0, 0.0, be)
    pe, po = (fe + be).astype(BF16), (fo + bo).astype(BF16)
    me, mo = (fe - be).astype(BF16), (fo - bo).astype(BF16)
    ce_p = _bdot(cm, pe)
    vr_p = wr * _bdot(cm, po) + wi * _bdot(sm, po)
    se_m = _bdot(sm, me)
    vi_m = wi * _bdot(cm, mo) - wr * _bdot(sm, mo)
    har_ref[0] = wk * (ce_p + vr_p)
    hai_ref[0] = wk * (vi_m - se_m)
    hbr_ref[0] = wk * (ce_p - vr_p)
    hbi_ref[0] = wk * (-se_m - vi_m)
    ny_scale = 2.0 / (4 * h)
    nr = ny_scale * jnp.sum(sgn * (fe + be), axis=0, keepdims=True)
    ni = -ny_scale * jnp.sum(sgn * (fo - bo), axis=0, keepdims=True)
    rows = lax.broadcasted_iota(jnp.int32, (8, nr.shape[1]), 0)
    hny_ref[0] = jnp.where(rows == 0, nr, jnp.where(rows == 1, ni, 0.0))


def _hyena_filters(zfe, zfo, w1, b1, w2, b2, w3, dec_e, dec_o, tabs, wc):
    cm, sm, wr, wi, sgn, wk = tabs
    h = cm.shape[0]
    emb = zfe.shape[1]
    hid = w2.shape[0]
    nblk = HY_WIDTH // wc
    w3r = w3.reshape(hid, HY_ORDER * 2, HY_WIDTH).transpose(1, 0, 2)
    const = lambda o, cb: (0, 0)
    chan = lambda o, cb: (0, cb)
    out_spec = pl.BlockSpec((1, h, wc), lambda o, cb: (o, 0, cb))
    out_shape = jax.ShapeDtypeStruct((HY_ORDER, h, HY_WIDTH), F32)
    return pl.pallas_call(
        _filter_kernel,
        grid=(HY_ORDER, nblk),
        in_specs=[pl.BlockSpec((h, emb), const), pl.BlockSpec((h, emb), const),
                  pl.BlockSpec((emb, hid), const), pl.BlockSpec((1, hid), const),
                  pl.BlockSpec((hid, hid), const), pl.BlockSpec((1, hid), const),
                  pl.BlockSpec((1, hid, wc), lambda o, cb: (2 * o, 0, cb)),
                  pl.BlockSpec((1, hid, wc), lambda o, cb: (2 * o + 1, 0, cb)),
                  pl.BlockSpec((h, wc), chan), pl.BlockSpec((h, wc), chan),
                  pl.BlockSpec((h, h), const), pl.BlockSpec((h, h), const),
                  pl.BlockSpec((h, wc), const), pl.BlockSpec((h, wc), const),
                  pl.BlockSpec((h, wc), const), pl.BlockSpec((h, wc), const)],
        out_specs=[out_spec, out_spec, out_spec, out_spec,
                   pl.BlockSpec((1, 8, wc), lambda o, cb: (o, 0, cb))],
        out_shape=[out_shape, out_shape, out_shape, out_shape,
                   jax.ShapeDtypeStruct((HY_ORDER, 8, HY_WIDTH), F32)],
        scratch_shapes=[pltpu.VMEM((h, hid), F32), pltpu.VMEM((h, hid), F32)],
        compiler_params=_cparams(("arbitrary", "arbitrary")),
        name="hyena_filter_spectrum",
    )(zfe, zfo, w1, b1.reshape(1, hid), w2, b2.reshape(1, hid), w3r, w3r,
      dec_e, dec_o, cm, sm, wr, wi, sgn, wk)


def _long_conv(ze, zo, har, hai, hbr, hbi, hny, cm, sm, wr, wi, sgn):
    zar, zai, zbr, zbi, znr, zni = _half_spectrum(ze, zo, cm, sm, wr, wi, sgn)
    yar, yai = zar * har - zai * hai, zar * hai + zai * har
    ybr, ybi = zbr * hbr - zbi * hbi, zbr * hbi + zbi * hbr
    dr, di = yar - ybr, yai - ybi
    sar, sai = (yar + ybr).astype(BF16), (yai + ybi).astype(BF16)
    sbr, sbi = (dr * wr + di * wi).astype(BF16), (di * wr - dr * wi).astype(BF16)
    hnr, hni = hny[0:1], hny[1:2]
    ynr = znr * hnr - zni * hni
    yni = znr * hni + zni * hnr
    ye = _bdot(cm, sar) - _bdot(sm, sai) + sgn * ynr
    yo = _bdot(cm, sbr) - _bdot(sm, sbi) - sgn * yni
    return ye, yo


def _hyena_kernel(*refs):
    nsub = (len(refs) - 18) // 4
    u_refs = [refs[s * nsub:(s + 1) * nsub] for s in range(HY_ORDER + 1)]
    rest = refs[(HY_ORDER + 1) * nsub:]
    (cw0_ref, cw1_ref, cw2_ref, cb0_ref, cb1_ref, cb2_ref, skip_ref,
     har_ref, hai_ref, hbr_ref, hbi_ref, hny_ref,
     cm_ref, sm_ref, wr_ref, wi_ref, sgn_ref, o_ref) = rest[:18]
    il_refs = rest[18:]
    h = cm_ref.shape[0]
    cm, sm = cm_ref[...], sm_ref[...]
    wr, wi, sgn = wr_ref[...], wi_ref[...], sgn_ref[...]
    row = lax.broadcasted_iota(jnp.int32, (h, wr.shape[1]), 0)

    def short_conv(u_slabs, cw_ref, cb_ref):
        ue = jnp.concatenate([r[0, pl.ds(0, h, stride=2), :] for r in u_slabs], axis=1)
        uo = jnp.concatenate([r[0, pl.ds(1, h, stride=2), :] for r in u_slabs], axis=1)
        w0, w1, w2 = cw_ref[0:1], cw_ref[1:2], cw_ref[2:3]
        uo_prev = jnp.where(row == 0, 0.0, pltpu.roll(uo, 1, 0))
        ue_next = jnp.where(row == h - 1, 0.0, pltpu.roll(ue, h - 1, 0))
        cb = cb_ref[...]
        se = cb + uo_prev * w0 + ue * w1 + uo * w2
        so = cb + ue * w0 + uo * w1 + ue_next * w2
        return se, so

    x_streams = [short_conv(u_refs[0], cw0_ref, cb0_ref), short_conv(u_refs[1], cw1_ref, cb1_ref)]
    ze, zo = short_conv(u_refs[2], cw2_ref, cb2_ref)
    for o in range(HY_ORDER):
        ce, co = _long_conv(ze, zo, har_ref[o], hai_ref[o], hbr_ref[o], hbi_ref[o], hny_ref[o],
                            cm, sm, wr, wi, sgn)
        sk = skip_ref[o:o + 1]
        xe, xo = x_streams[o]
        ze = xe * (ce + sk * ze)
        zo = xo * (co + sk * zo)
    for j, il in enumerate(il_refs):
        il[pl.ds(0, h, stride=2), :] = ze[:, j * 128:(j + 1) * 128]
        il[pl.ds(1, h, stride=2), :] = zo[:, j * 128:(j + 1) * 128]
        o_ref[0, :, j * 128:(j + 1) * 128] = il[...]


def _hyena(u, conv_w, conv_b, skip, spectra, tabs, wc):
    b, l, _ = u.shape
    har, hai, hbr, hbi, hny = spectra
    cm, sm, wr, wi, sgn, _ = tabs
    h = l // 2
    nblk = HY_WIDTH // wc
    nsub = wc // 128
    conv_b = conv_b.reshape(1, -1)

    def ublk(s, j):
        return pl.BlockSpec((1, l, 128), lambda cb, bi: (bi, 0, (cb + s * nblk) * nsub + j))

    def wblk(r, s):
        return pl.BlockSpec((r, wc), lambda cb, bi: (0, cb + s * nblk))

    const = lambda cb, bi: (0, 0)
    spec = pl.BlockSpec((HY_ORDER, h, wc), lambda cb, bi: (0, 0, cb))
    return pl.pallas_call(
        _hyena_kernel,
        grid=(nblk, b),
        in_specs=[ublk(s, j) for s in range(HY_ORDER + 1) for j in range(nsub)] + [
                  wblk(HY_SHORT_CONV, 0), wblk(HY_SHORT_CONV, 1), wblk(HY_SHORT_CONV, 2),
                  wblk(1, 0), wblk(1, 1), wblk(1, 2),
                  pl.BlockSpec((HY_ORDER, wc), lambda cb, bi: (0, cb)),
                  spec, spec, spec, spec,
                  pl.BlockSpec((HY_ORDER, 8, wc), lambda cb, bi: (0, 0, cb)),
                  pl.BlockSpec((h, h), const), pl.BlockSpec((h, h), const),
                  pl.BlockSpec((h, wc), const), pl.BlockSpec((h, wc), const),
                  pl.BlockSpec((h, wc), const)],
        out_specs=pl.BlockSpec((1, l, wc), lambda cb, bi: (bi, 0, cb)),
        out_shape=jax.ShapeDtypeStruct((b, l, HY_WIDTH), F32),
        scratch_shapes=[pltpu.VMEM((l, 128), F32) for _ in range(nsub)],
        compiler_params=_cparams(("arbitrary", "arbitrary")),
        name="hyena_long_conv",
    )(*([u] * ((HY_ORDER + 1) * nsub)), conv_w, conv_w, conv_w, conv_b, conv_b, conv_b, skip,
      har, hai, hbr, hbi, hny, cm, sm, wr, wi, sgn)


def _merge_kernel(alpha, att_ref, hy_ref, gate_ref, x_ref, g1_ref, sh2_ref, sc2_ref,
                  wba_ref, wbh_ref, wo_ref, l1g_ref, l1b_ref, rw_ref, rb_ref,
                  xmid_ref, h2_ref, logit_ref, res_ref):
    d = x_ref.shape[1]

    @pl.when(pl.program_id(0) == 0)
    def _():
        res_ref[...] = jnp.zeros_like(res_ref)

    rw = rw_ref[...]
    r_hi = rw.astype(BF16)
    r_lo = (rw - r_hi.astype(F32)).astype(BF16)
    dn = (((1,), (1,)), ((), ()))
    nt_dot = lambda p, q: lax.dot_general(p, q, dn, preferred_element_type=F32)
    tm = x_ref.shape[0]
    rc = tm // MERGE_CHUNKS
    for r in (slice(c * rc, (c + 1) * rc) for c in range(MERGE_CHUNKS)):
        xm = _ln(res_ref[r]) * l1g_ref[...] + l1b_ref[...]
        xmid_ref[r] = xm
        h2 = _ln(xm) * (1.0 + sc2_ref[0]) + sh2_ref[0]
        h_hi = h2.astype(BF16)
        h2_ref[r] = h_hi
        h_lo = (h2 - h_hi.astype(F32)).astype(BF16)
        logit_ref[:, r] = nt_dot(r_hi, h_hi) + nt_dot(r_hi, h_lo) + nt_dot(r_lo, h_hi) + rb_ref[...]

        a = _bdot(att_ref[r], wba_ref[...])
        hh = _bdot(hy_ref[r].astype(BF16), wbh_ref[...])
        ga = gate_ref[r, :d].astype(F32)
        gh = gate_ref[r, d:].astype(F32)
        y = _bdot((ga * a + gh * hh).astype(BF16), wo_ref[...])
        res_ref[r] = alpha * x_ref[r] + g1_ref[0] * y


def _merge(att, hy, gate, x, g1, sh2, sc2, wba, wbh, wo, l1g, l1b, rw_t, rb, alpha, tm=512):
    t, d = x.shape
    ne = rw_t.shape[0]
    nt = t // tm
    per_b = nt // g1.shape[0]
    t1 = lambda i: jnp.minimum(i, nt - 1)
    t2 = lambda i: jnp.maximum(i - 1, 0)
    row1 = lambda i: (t1(i), 0)
    row2 = lambda i: (t2(i), 0)
    const = lambda i: (0, 0)
    return pl.pallas_call(
        functools.partial(_merge_kernel, alpha),
        grid=(nt + 1,),
        in_specs=[pl.BlockSpec((tm, Q_W), row1),
                  pl.BlockSpec((tm, HY_WIDTH), row1),
                  pl.BlockSpec((tm, 2 * d), row1),
                  pl.BlockSpec((tm, d), row1),
                  pl.BlockSpec((1, 1, d), lambda i: (t1(i) // per_b, 0, 0)),
                  pl.BlockSpec((1, 1, d), lambda i: (t2(i) // per_b, 0, 0)),
                  pl.BlockSpec((1, 1, d), lambda i: (t2(i) // per_b, 0, 0)),
                  pl.BlockSpec((Q_W, d), const), pl.BlockSpec((HY_WIDTH, d), const),
                  pl.BlockSpec((d, d), const),
                  pl.BlockSpec((1, d), const), pl.BlockSpec((1, d), const),
                  pl.BlockSpec((ne, d), const), pl.BlockSpec((ne, 1), const)],
        out_specs=[pl.BlockSpec((tm, d), row2), pl.BlockSpec((tm, d), row2),
                   pl.BlockSpec((ne, tm), lambda i: (0, t2(i)))],
        out_shape=[jax.ShapeDtypeStruct((t, d), F32),
                   jax.ShapeDtypeStruct((t, d), BF16),
                   jax.ShapeDtypeStruct((ne, t), F32)],
        scratch_shapes=[pltpu.VMEM((tm, d), F32)],
        compiler_params=_cparams(("arbitrary",)),
        name="merge_ln_router",
    )(att, hy, gate, x, g1, sh2, sc2, wba, wbh, wo, l1g.reshape(1, d), l1b.reshape(1, d),
      rw_t, rb.reshape(ne, 1))


def _route_kernel(lg_ref, gate_ref, pos_ref, tcnt_ref, tbase_ref, cnt_ref, run_ref):
    @pl.when(pl.program_id(0) == 0)
    def _():
        run_ref[...] = jnp.zeros_like(run_ref)

    tiles = tcnt_ref.shape[0]
    tr = lg_ref.shape[1] // tiles
    for s in range(tiles):
        cols = slice(s * tr, (s + 1) * tr)
        gate, pos, cnt_row = _route_tile(lg_ref[:, cols])
        gate_ref[:, cols] = gate
        pos_ref[:, cols] = pos
        tcnt_ref[s] = cnt_row.astype(jnp.int32)
        tbase_ref[s] = run_ref[...].astype(jnp.int32)
        run_ref[...] = run_ref[...] + cnt_row
    cnt_ref[...] = run_ref[...].astype(jnp.int32)


def _route_tile(lg):
    ne, tr = lg.shape
    sub = lax.broadcasted_iota(jnp.int32, (ne, tr), 0)
    work = lg
    vals, hots = [], []
    for _ in range(TOP_K):
        m = jnp.max(work, axis=0, keepdims=True)
        idx = jnp.min(jnp.where(work == m, sub, ne), axis=0, keepdims=True)
        hot = sub == idx
        vals.append(m)
        hots.append(hot)
        work = jnp.where(hot, -jnp.inf, work)
    exps = [jnp.exp(v - vals[0]) for v in vals]
    den = exps[0] + exps[1] + exps[2] + exps[3]
    member = jnp.zeros((ne, tr), F32)
    for hot in hots:
        member = member + jnp.where(hot, 1.0, 0.0)
    r_i = lax.broadcasted_iota(jnp.int32, (tr, tr), 0)
    c_i = lax.broadcasted_iota(jnp.int32, (tr, tr), 1)
    earlier = jnp.where(r_i < c_i, 1.0, 0.0).astype(BF16)
    prefix = _bdot(member.astype(BF16), earlier)
    cnt = jnp.broadcast_to(jnp.sum(member, axis=1, keepdims=True), (ne, 128))
    e_r = lax.broadcasted_iota(jnp.int32, (ne, ne), 0)
    e_c = lax.broadcasted_iota(jnp.int32, (ne, ne), 1)
    lower = jnp.where(e_c < e_r, 1.0, 0.0).astype(BF16)
    off = _bdot(lower, cnt.astype(BF16))
    base = off[:, 0:1] + prefix
    sub_k = lax.broadcasted_iota(jnp.int32, (TOP_K, tr), 0)
    gate = jnp.zeros((TOP_K, tr), F32)
    pos = jnp.zeros((TOP_K, tr), F32)
    for k in range(TOP_K):
        pk = jnp.sum(jnp.where(hots[k], base, 0.0), axis=0, keepdims=True)
        gate = jnp.where(sub_k == k, exps[k] / den, gate)
        pos = jnp.where(sub_k == k, pk, pos)
    cnt_row = lax.dot_general(jnp.ones((8, tr), BF16), member.astype(BF16),
                              (((1,), (1,)), ((), ())), preferred_element_type=F32)
    return gate, pos.astype(jnp.int32), cnt_row


def _route(logits_t, tr, tiles_per_step=4):
    ne, t = logits_t.shape
    nt = t // tr
    tb = tr * tiles_per_step
    tok = lambda i: (0, i)
    tile = lambda i: (i, 0, 0)
    return pl.pallas_call(
        _route_kernel,
        grid=(nt // tiles_per_step,),
        in_specs=[pl.BlockSpec((ne, tb), tok)],
        out_specs=[pl.BlockSpec((TOP_K, tb), tok), pl.BlockSpec((TOP_K, tb), tok),
                   pl.BlockSpec((tiles_per_step, 8, ne), tile),
                   pl.BlockSpec((tiles_per_step, 8, ne), tile),
                   pl.BlockSpec((8, ne), lambda i: (0, 0))],
        out_shape=[jax.ShapeDtypeStruct((TOP_K, t), F32),
                   jax.ShapeDtypeStruct((TOP_K, t), jnp.int32),
                   jax.ShapeDtypeStruct((nt, 8, ne), jnp.int32),
                   jax.ShapeDtypeStruct((nt, 8, ne), jnp.int32),
                   jax.ShapeDtypeStruct((8, ne), jnp.int32)],
        scratch_shapes=[pltpu.VMEM((8, ne), F32)],
        compiler_params=_cparams(("arbitrary",)),
        name="route_topk",
    )(logits_t)


def _row_tile(d):
    return (d // 256, 128)


def _pack_rows(v, bf16_exact=False):
    n, d = v.shape
    if bf16_exact:
        bits = lax.bitcast_convert_type(v, jnp.uint32)
        word = bits[:, d // 2:] | (bits[:, :d // 2] >> 16)
    else:
        bits = lax.bitcast_convert_type(v.astype(BF16).astype(F32), jnp.uint32)
        word = (bits[:, d // 2:] & jnp.uint32(0xFFFF0000)) | (bits[:, :d // 2] >> 16)
    return word.reshape((n,) + _row_tile(d))


def _unpack_rows(w):
    n = w.shape[0]
    word = w.reshape(n, w.shape[1] * w.shape[2])
    lo = lax.bitcast_convert_type(word << 16, F32)
    hi = lax.bitcast_convert_type(word & jnp.uint32(0xFFFF0000), F32)
    return jnp.concatenate([lo, hi], axis=1)


def _strip_copies(n_ref, dst_ref, make_copy, max_rows):
    ne = dst_ref.shape[2]
    all_bits = [1 << s for s in range(max_rows.bit_length() - 1, -1, -1)]
    small = pl.next_power_of_2(max(1, 2 * TOP_K * max_rows // ne))
    longest = n_ref[0, 0, ne]

    def run(bits):
        def body(e, off):
            n = n_ref[0, 0, e]
            dst = dst_ref[0, 0, e]
            done = jnp.int32(0)
            for b, bit in enumerate(bits):
                part = n & bit

                @pl.when(part != 0)
                def _():
                    make_copy(off + done, dst + done, bit).start(priority=b % 2)

                done = done + part
            return off + n

        lax.fori_loop(0, ne, body, jnp.int32(0))

    @pl.when(longest < small)
    def _():
        run([b for b in all_bits if b < small])

    @pl.when(longest >= small)
    def _():
        run(all_bits)


def _dispatch_kernel(n_ref, dst_ref, pos_ref, h_ref, xs_ref, srt_ref, sem):
    i = pl.program_id(0)
    slot = i % 2
    td = h_ref.shape[0]
    n = td * TOP_K

    @pl.when(i == 0)
    def _():
        tail = xs_ref.shape[0] - EXPERT_TILE
        srt_ref[1, 0:EXPERT_TILE] = jnp.zeros((EXPERT_TILE,) + srt_ref.shape[2:], srt_ref.dtype)
        zero_tail = pltpu.make_async_copy(srt_ref.at[1, pl.ds(0, EXPERT_TILE)],
                                          xs_ref.at[pl.ds(tail, EXPERT_TILE)], sem.at[1])
        zero_tail.start()
        zero_tail.wait()

    pos = pos_ref[...]
    rows = lax.broadcasted_iota(jnp.int32, (n, td), 0)
    hit = rows == pos[0:1]
    for k in range(1, TOP_K):
        hit = hit | (rows == pos[k:k + 1])
    perm = jnp.where(hit, 1.0, 0.0).astype(BF16)
    srt_ref[slot] = _pack_rows(_bdot(perm, h_ref[...]), bf16_exact=True)

    def all_rows(s):
        return pltpu.make_async_copy(srt_ref.at[s], xs_ref.at[pl.ds(0, n)], sem.at[s])

    _strip_copies(n_ref, dst_ref,
                  lambda src, dst, size: pltpu.make_async_copy(
                      srt_ref.at[slot, pl.ds(src, size)], xs_ref.at[pl.ds(dst, size)], sem.at[slot]),
                  td)

    @pl.when(i > 0)
    def _():
        all_rows(1 - slot).wait()

    @pl.when(i == pl.num_programs(0) - 1)
    def _():
        all_rows(slot).wait()


def _dispatch(h2, pos_t, tile_n, tile_dst, td):
    t, d = h2.shape
    a = t * TOP_K
    n = td * TOP_K
    nt = t // td
    smem = lambda v: pl.BlockSpec((1, 1, v.shape[-1]), lambda i: (i, 0, 0),
                                  memory_space=pltpu.SMEM)
    return pl.pallas_call(
        _dispatch_kernel,
        grid=(nt,),
        in_specs=[smem(tile_n), smem(tile_dst),
                  pl.BlockSpec((TOP_K, td), lambda i: (0, i)),
                  pl.BlockSpec((td, d), lambda i: (i, 0))],
        out_specs=pl.BlockSpec(memory_space=pl.ANY),
        out_shape=jax.ShapeDtypeStruct((a + EXPERT_TILE,) + _row_tile(d), jnp.uint32),
        scratch_shapes=[pltpu.VMEM((2, n) + _row_tile(d), jnp.uint32),
                        pltpu.SemaphoreType.DMA((2,))],
        compiler_params=_cparams(("arbitrary",)),
        name="moe_dispatch",
    )(tile_n[:, None, :], tile_dst[:, None, :], pos_t, h2)


def _expert_kernel(exp_ref, row_ref, cls_ref, nitem_ref,
                   xs_ref, w1_ref, b1_ref, w2_ref, b2_ref, ys_ref,
                   w1b_ref, w2b_ref, xbuf_ref, ybuf_ref, sem_in, sem_out):
    w = pl.program_id(0)
    n_items = nitem_ref[0]
    last = exp_ref.shape[0] - 1
    slot = w % 2
    dff = w2_ref.shape[1]

    def fetch(item, s):
        return pltpu.make_async_copy(xs_ref.at[pl.ds(row_ref[item], EXPERT_TILE)],
                                     xbuf_ref.at[s], sem_in.at[s])

    def write(item, s, rows):
        return pltpu.make_async_copy(ybuf_ref.at[s, pl.ds(0, rows)],
                                     ys_ref.at[pl.ds(row_ref[item], rows)], sem_out.at[s])

    def for_tile_size(item, fn):
        for c, rows in enumerate(EXPERT_LAST_TILES):
            @pl.when(cls_ref[item] == c)
            def _():
                fn(rows)

    @pl.when(w == 0)
    def _():
        fetch(0, 0).start()
        ybuf_ref[1] = jnp.zeros(ybuf_ref.shape[1:], ybuf_ref.dtype)
        zero_tail = pltpu.make_async_copy(
            ybuf_ref.at[1], ys_ref.at[pl.ds(ys_ref.shape[0] - EXPERT_TILE, EXPERT_TILE)],
            sem_out.at[1])
        zero_tail.start()
        zero_tail.wait()

    @pl.when(w + 1 < n_items)
    def _():
        fetch(jnp.minimum(w + 1, last), 1 - slot).start()

    @pl.when((w == 0) | (exp_ref[w] != exp_ref[jnp.maximum(w - 1, 0)]))
    def _():
        w1b_ref[...] = w1_ref[0].astype(BF16)
        w2b_ref[...] = w2_ref[0].astype(BF16)

    @pl.when(w < n_items)
    def _():
        fetch(w, slot).wait()

        def compute(rows):
            x = _unpack_rows(xbuf_ref[slot, 0:rows]).astype(BF16)
            hb = _bdot(x, w1b_ref[...]) + b1_ref[0]
            glu = jnp.minimum(hb[:, :dff], SWIGLU_LIMIT)
            lin = jnp.clip(hb[:, dff:], -SWIGLU_LIMIT, SWIGLU_LIMIT)
            act = glu * jax.nn.sigmoid(SWIGLU_ALPHA * glu) * (lin + 1.0)
            ybuf_ref[slot, 0:rows] = _pack_rows(_bdot(act.astype(BF16), w2b_ref[...]) + b2_ref[0])

        for_tile_size(w, compute)

        @pl.when(w > 0)
        def _():
            for_tile_size(w - 1, lambda rows: write(w - 1, 1 - slot, rows).wait())

        for_tile_size(w, lambda rows: write(w, slot, rows).start())

        @pl.when(w == n_items - 1)
        def _():
            for_tile_size(w, lambda rows: write(w, slot, rows).wait())


def _experts(xs, items, w1, b1, w2, b2):
    ne, d, dff2 = w1.shape
    dff = dff2 // 2
    exp_w, row_w, cls_w, nitem = items
    per_expert = lambda w, ex, ro, cl, ni: (ex[w], 0, 0)
    tile = (EXPERT_TILE,) + _row_tile(d)
    grid_spec = pltpu.PrefetchScalarGridSpec(
        num_scalar_prefetch=4,
        grid=(exp_w.shape[0],),
        in_specs=[pl.BlockSpec(memory_space=pl.ANY),
                  pl.BlockSpec((1, d, dff2), per_expert),
                  pl.BlockSpec((1, 1, dff2), per_expert),
                  pl.BlockSpec((1, dff, d), per_expert),
                  pl.BlockSpec((1, 1, d), per_expert)],
        out_specs=pl.BlockSpec(memory_space=pl.ANY),
        scratch_shapes=[pltpu.VMEM((d, dff2), BF16), pltpu.VMEM((dff, d), BF16),
                        pltpu.VMEM((2,) + tile, jnp.uint32), pltpu.VMEM((2,) + tile, jnp.uint32),
                        pltpu.SemaphoreType.DMA((2,)), pltpu.SemaphoreType.DMA((2,))],
    )
    return pl.pallas_call(
        _expert_kernel,
        grid_spec=grid_spec,
        out_shape=jax.ShapeDtypeStruct(xs.shape, jnp.uint32),
        compiler_params=_cparams(("arbitrary",)),
        name="moe_experts",
    )(exp_w, row_w, cls_w, nitem, xs, w1, b1.reshape(ne, 1, dff2), w2, b2.reshape(ne, 1, d))


def _work_items(counts, a):
    ne = counts.shape[0]
    n_items = a // EXPERT_TILE + ne
    ends = jnp.cumsum(counts)
    starts = ends - counts
    n_full = counts // EXPERT_TILE
    rem = counts - n_full * EXPERT_TILE
    n_tiles = n_full + (rem > 0).astype(jnp.int32)
    tile_end = jnp.cumsum(n_tiles)
    tile_start = tile_end - n_tiles
    total = tile_end[-1]
    w = jnp.minimum(jnp.arange(n_items, dtype=jnp.int32), total - 1)
    e_w = jnp.sum((tile_end[None, :] <= w[:, None]).astype(jnp.int32), axis=1)
    hot = e_w[:, None] == jnp.arange(ne, dtype=jnp.int32)[None, :]
    pick = lambda v: jnp.sum(jnp.where(hot, v[None, :], 0), axis=1)
    k = w - pick(tile_start)
    row_w = (pick(starts) + k * EXPERT_TILE).astype(jnp.int32)
    rem_w = pick(rem)
    rem_cls = jnp.zeros_like(rem_w)
    for c, rows in enumerate(EXPERT_LAST_TILES[:-1]):
        rem_cls = rem_cls + (rem_w > rows).astype(jnp.int32)
    full_cls = len(EXPERT_LAST_TILES) - 1
    cls_w = jnp.where(k < pick(n_full), full_cls, rem_cls).astype(jnp.int32)
    return starts, (e_w.astype(jnp.int32), row_w, cls_w, total.reshape(1).astype(jnp.int32))


def _combine_kernel(alpha, n_ref, dst_ref, nn_ref, ndst_ref, pos_ref, gate_ref,
                    ys_ref, xm_ref, g2_ref, lg_ref, lb_ref, o_ref, srt_ref, sem):
    i = pl.program_id(0)
    last = pl.num_programs(0) - 1
    slot = i % 2
    td = xm_ref.shape[0]
    n = td * TOP_K

    def fetch(cnt_ref, from_ref, s):
        _strip_copies(cnt_ref, from_ref,
                      lambda row, src, size: pltpu.make_async_copy(
                          ys_ref.at[pl.ds(src, size)], srt_ref.at[s, pl.ds(row, size)], sem.at[s]),
                      td)

    @pl.when(i == 0)
    def _():
        fetch(n_ref, dst_ref, slot)

    @pl.when(i < last)
    def _():
        fetch(nn_ref, ndst_ref, 1 - slot)

    pltpu.make_async_copy(ys_ref.at[pl.ds(0, n)], srt_ref.at[slot], sem.at[slot]).wait()

    pos, gate = jnp.transpose(pos_ref[...]), jnp.transpose(gate_ref[...])
    lanes = lax.broadcasted_iota(jnp.int32, (td, n), 1)
    wsel = jnp.where(lanes == pos[:, 0:1], gate[:, 0:1], 0.0)
    for k in range(1, TOP_K):
        wsel = wsel + jnp.where(lanes == pos[:, k:k + 1], gate[:, k:k + 1], 0.0)
    y = _unpack_rows(srt_ref[slot]).astype(BF16)
    f = _bdot(wsel.astype(BF16), y)
    o_ref[...] = _ln(alpha * xm_ref[...] + g2_ref[0] * f) * lg_ref[...] + lb_ref[...]


def _combine(ys, pos, gate, tile_n, tile_dst, x_mid, g2, lg, lb, alpha, td):
    t, d = x_mid.shape
    n = td * TOP_K
    nt = t // td
    per_b = nt // g2.shape[0]
    first = lambda v: pl.BlockSpec((1, 1, v.shape[-1]), lambda i: (0, 0, 0),
                                   memory_space=pltpu.SMEM)
    nxt = lambda v: pl.BlockSpec((1, 1, v.shape[-1]),
                                 lambda i: (jnp.minimum(i + 1, nt - 1), 0, 0),
                                 memory_space=pltpu.SMEM)
    const = lambda i: (0, 0)
    tile_n = tile_n[:, None, :]
    tile_dst = tile_dst[:, None, :]
    return pl.pallas_call(
        functools.partial(_combine_kernel, alpha),
        grid=(nt,),
        in_specs=[first(tile_n), first(tile_dst), nxt(tile_n), nxt(tile_dst),
                  pl.BlockSpec((TOP_K, td), lambda i: (0, i)),
                  pl.BlockSpec((TOP_K, td), lambda i: (0, i)),
                  pl.BlockSpec(memory_space=pl.ANY),
                  pl.BlockSpec((td, d), lambda i: (i, 0)),
                  pl.BlockSpec((1, 1, d), lambda i: (i // per_b, 0, 0)),
                  pl.BlockSpec((1, d), const), pl.BlockSpec((1, d), const)],
        out_specs=pl.BlockSpec((td, d), lambda i: (i, 0)),
        out_shape=jax.ShapeDtypeStruct((t, d), F32),
        scratch_shapes=[pltpu.VMEM((2, n) + _row_tile(d), jnp.uint32),
                        pltpu.SemaphoreType.DMA((2,))],
        compiler_params=_cparams(("arbitrary",)),
        name="moe_combine_ln",
    )(tile_n, tile_dst, tile_n, tile_dst, pos, gate, ys, x_mid, g2,
      lg.reshape(1, d), lb.reshape(1, d))


def _rope_tables(l):
    f32 = np.float32
    rows = l // GRID_W
    row = np.repeat(np.arange(rows, dtype=f32), GRID_W)
    col = np.tile(np.arange(GRID_W, dtype=f32), rows)
    n_freq = HEAD_DIM // 4
    inv_freq = np.power(f32(ROPE_BASE), -np.arange(n_freq, dtype=f32) / f32(n_freq)).astype(f32)
    ang_r = (row[:, None] * inv_freq).astype(f32)
    ang_c = (col[:, None] * inv_freq).astype(f32)
    zero = np.zeros_like(ang_r)
    cos_r, sin_r, cos_c, sin_c = np.cos(ang_r), np.sin(ang_r), np.cos(ang_c), np.sin(ang_c)
    cos_h = np.concatenate([cos_r, cos_r, cos_c, cos_c], axis=1)
    sa_h = np.concatenate([-sin_r, zero, -sin_c, zero], axis=1)
    sb_h = np.concatenate([zero, sin_r, zero, sin_c], axis=1)
    rep = 128 // HEAD_DIM
    return tuple(np.tile(a, (1, rep)).astype(f32) for a in (cos_h, sa_h, sb_h))


def _filter_features(l):
    f32 = np.float32
    bands = (HY_EMB_DIM - 1) // 2
    t = np.linspace(0.0, 1.0, l, dtype=f32)[:, None]
    omega = (f32(2.0 * math.pi) * np.arange(l, dtype=f32)[:, None] / f32(l)).astype(f32)
    f = np.linspace(1e-4, bands - 1, bands, dtype=f32)[None, :]
    ang = (f * omega).astype(f32)
    z = np.concatenate([t, np.cos(ang), -np.sin(ang)], axis=-1).astype(f32)
    min_decay = math.log(HY_DECAY_TARGET) / HY_FAST_DECAY_PCT
    max_decay = math.log(HY_DECAY_TARGET) / HY_SLOW_DECAY_PCT
    deltas = np.abs(np.linspace(min_decay, max_decay, HY_WIDTH, dtype=f32))
    decay = np.exp(-t * deltas).astype(f32)
    return z, decay


def kernel(x, c, ctx, c_ctx, w_mod, b_mod, w_in, attn_sink, hy_conv_w, hy_conv_b, hy_filt_w1,
           hy_filt_b1, hy_filt_w2, hy_filt_b2, hy_filt_w3, hy_skip, w_branch_attn, w_branch_hyena,
           w_out, ln1_g, ln1_b, router_w, router_b, exp_w1, exp_b1, exp_w2, exp_b2, ln2_g, ln2_b):
    depth = w_mod.shape[0]
    assert depth == 1, "only the single-layer configuration is implemented"
    b, l, d = x.shape
    t = b * l
    alpha = (2 * depth) ** 0.25
    hy_wc = 256

    n_cond = b + 1
    pad = (-n_cond) % 8
    cond = jnp.concatenate([c, c_ctx[None], jnp.zeros((pad, d), F32)], axis=0)
    mod = _modulation(cond, w_mod[0], b_mod[0])
    mod_x = mod[:b].reshape(b, 1, 6, d)
    sh1, sc1, g1, sh2, sc2, g2 = (mod_x[:, :, i] for i in range(6))
    mod_c = mod[b:b + 1].reshape(1, 1, 6, d)
    csh1, csc1 = mod_c[:, :, 0], mod_c[:, :, 1]

    w_in_b = w_in[0].astype(BF16)
    cos_t, sa_t, sb_t = (jnp.asarray(a) for a in _rope_tables(l))
    q, k, v, u_hy, gate_x = _in_projection(x, sh1, sc1, w_in_b, cos_t, sa_t, sb_t)
    k_c, v_c = _ctx_kv(ctx, csh1, csc1, w_in_b[:, K_OFF:HY_OFF])
    att = _attention(q, k, v, k_c, v_c, attn_sink[0])

    tabs_np = _dft_tables(l, hy_wc)
    tabs = (jnp.asarray(tabs_np[0]).astype(BF16), jnp.asarray(tabs_np[1]).astype(BF16)) + tuple(
        jnp.asarray(a) for a in tabs_np[2:])
    zfeat, decay = _filter_features(l)
    emb_pad = (-HY_EMB_DIM) % 128
    zfeat = np.pad(zfeat, ((0, 0), (0, emb_pad)))
    fw1 = jnp.pad(hy_filt_w1[0], ((0, emb_pad), (0, 0)))
    spectra = _hyena_filters(jnp.asarray(zfeat[0::2]), jnp.asarray(zfeat[1::2]), fw1, hy_filt_b1[0],
                             hy_filt_w2[0], hy_filt_b2[0], hy_filt_w3[0], jnp.asarray(decay[0::2]),
                             jnp.asarray(decay[1::2]), tabs, hy_wc)
    hy = _hyena(u_hy, hy_conv_w[0], hy_conv_b[0], hy_skip[0], spectra, tabs, hy_wc)

    x_mid, h2, logits_t = _merge(att.reshape(t, Q_W), hy.reshape(t, HY_WIDTH),
                                 gate_x.reshape(t, 2 * d), x.reshape(t, d), g1, sh2, sc2,
                                 w_branch_attn[0].astype(BF16), w_branch_hyena[0].astype(BF16),
                                 w_out[0].astype(BF16), ln1_g[0], ln1_b[0],
                                 jnp.transpose(router_w[0]), router_b[0], alpha)

    moe_td = 256
    gate_t, pos_t, tile_cnt, tile_base, counts = _route(logits_t, moe_td)
    starts, items = _work_items(counts[0], t * TOP_K)
    tile_n = tile_cnt[:, 0, :]
    tile_n = jnp.concatenate([tile_n, jnp.max(tile_n, axis=1, keepdims=True)], axis=1)
    tile_dst = starts[None, :] + tile_base[:, 0, :]
    xs = _dispatch(h2.reshape(t, d), pos_t, tile_n, tile_dst, moe_td)
    ys = _experts(xs, items, exp_w1[0], exp_b1[0], exp_w2[0], exp_b2[0])
    out = _combine(ys, pos_t, gate_t, tile_n, tile_dst,
                   x_mid.reshape(t, d), g2, ln2_g[0], ln2_b[0], alpha, moe_td)
    return out.reshape(b, l, d)
```

```python
import functools
import math

import numpy as np
import jax
import jax.numpy as jnp
from jax import lax
from jax.experimental import pallas as pl
from jax.experimental.pallas import tpu as pltpu

F32 = jnp.float32
BF16 = jnp.bfloat16
HIGHEST = lax.Precision.HIGHEST

GRID_W = 64
N_HEADS = 8
N_KV_HEADS = 2
GQA_GROUP = N_HEADS // N_KV_HEADS
HEAD_DIM = 64
WINDOW = 128
ATTN_BLOCK = 128
ROPE_BASE = 10000.0

HY_WIDTH = 512
HY_ORDER = 2
HY_SHORT_CONV = 3
HY_EMB_DIM = 33
HY_DECAY_TARGET = 1e-2
HY_FAST_DECAY_PCT = 0.3
HY_SLOW_DECAY_PCT = 1.5

N_EXPERTS = 32
TOP_K = 4
SWIGLU_LIMIT = 7.0
SWIGLU_ALPHA = 1.702
LN_EPS = 1e-5

Q_W = N_HEADS * HEAD_DIM
KV_W = N_KV_HEADS * HEAD_DIM
K_OFF = Q_W
V_OFF = K_OFF + KV_W
HY_OFF = V_OFF + KV_W
GATE_OFF = HY_OFF + (HY_ORDER + 1) * HY_WIDTH

VMEM_LIMIT = 56 * 1024 * 1024
NEG_BIG = -1e30
MERGE_CHUNKS = 2
EXPERT_TILE = 1024
EXPERT_LAST_TILES = (256, 512, 768, 1024)


def _cparams(sem):
    return pltpu.CompilerParams(dimension_semantics=sem, vmem_limit_bytes=VMEM_LIMIT)


def _ln(x):
    mu = jnp.mean(x, axis=-1, keepdims=True)
    xc = x - mu
    var = jnp.mean(xc * xc, axis=-1, keepdims=True)
    return xc * lax.rsqrt(var + LN_EPS)


def _bdot(a, b):
    return jnp.dot(a, b, preferred_element_type=F32)


def _fdot(a, b):
    return jnp.dot(a, b, preferred_element_type=F32, precision=HIGHEST)


def _mod_kernel(c_ref, w_ref, b_ref, o_ref):
    c = c_ref[...]
    s = c * jax.nn.sigmoid(c)
    o_ref[...] = _fdot(s, w_ref[...]) + b_ref[...]


def _modulation(cond, w, b, tn=512):
    r, d = cond.shape
    n = w.shape[1]
    return pl.pallas_call(
        _mod_kernel,
        grid=(n // tn,),
        in_specs=[pl.BlockSpec((r, d), lambda j: (0, 0)),
                  pl.BlockSpec((d, tn), lambda j: (0, j)),
                  pl.BlockSpec((1, tn), lambda j: (0, j))],
        out_specs=pl.BlockSpec((r, tn), lambda j: (0, j)),
        out_shape=jax.ShapeDtypeStruct((r, n), F32),
        compiler_params=_cparams(("arbitrary",)),
        name="modulation",
    )(cond, w, b.reshape(1, n))


def _rope(t, cos, sa, sb):
    n = t.shape[-1]
    return t * cos + pltpu.roll(t, n - 16, 1) * sa + pltpu.roll(t, 16, 1) * sb


def _inproj_kernel(x_ref, sh_ref, sc_ref, w_ref, cos_ref, sa_ref, sb_ref,
                   q_ref, k_ref, v_ref, u_ref, g_ref, hb_ref):
    @pl.when(pl.program_id(0) == 0)
    def _():
        hb_ref[...] = jnp.zeros_like(hb_ref)

    tm = x_ref.shape[1]
    scale = HEAD_DIM ** -0.5
    n_gate = w_ref.shape[1] - GATE_OFF
    rc = tm // MERGE_CHUNKS
    for r in (slice(c * rc, (c + 1) * rc) for c in range(MERGE_CHUNKS)):
        hb = hb_ref[r]
        cos, sa, sb = cos_ref[r], sa_ref[r], sb_ref[r]
        for j in range(Q_W // 128):
            t = _bdot(hb, w_ref[:, j * 128:(j + 1) * 128])
            q_ref[0, r, j * 128:(j + 1) * 128] = (_rope(t, cos, sa, sb) * scale).astype(BF16)
        t = _bdot(hb, w_ref[:, K_OFF:V_OFF])
        k_ref[0, r] = _rope(t, cos, sa, sb).astype(BF16)
        v_ref[0, r] = _bdot(hb, w_ref[:, V_OFF:HY_OFF]).astype(BF16)
        for j in range((GATE_OFF - HY_OFF) // 512):
            u_ref[0, r, j * 512:(j + 1) * 512] = _bdot(
                hb, w_ref[:, HY_OFF + j * 512:HY_OFF + (j + 1) * 512])
        for j in range(n_gate // 512):
            g_ref[0, r, j * 512:(j + 1) * 512] = jax.nn.sigmoid(_bdot(
                hb, w_ref[:, GATE_OFF + j * 512:GATE_OFF + (j + 1) * 512])).astype(BF16)

        h = _ln(x_ref[0, r]) * (1.0 + sc_ref[0]) + sh_ref[0]
        hb_ref[r] = h.astype(BF16)


def _in_projection(x, sh, sc, w_in_b, cos_t, sa_t, sb_t, tm=512):
    b, l, d = x.shape
    in_w = w_in_b.shape[1]
    hy_w = GATE_OFF - HY_OFF
    g_w = in_w - GATE_OFF
    per_b = l // tm
    nt = b * per_b
    t1 = lambda i: jnp.minimum(i, nt - 1)
    t2 = lambda i: jnp.maximum(i - 1, 0)
    row1 = lambda i: (t1(i) // per_b, t1(i) % per_b, 0)
    row2 = lambda i: (t2(i) // per_b, t2(i) % per_b, 0)
    vec1 = lambda i: (t1(i) // per_b, 0, 0)
    tab2 = lambda i: (t2(i) % per_b, 0)
    return pl.pallas_call(
        _inproj_kernel,
        grid=(nt + 1,),
        in_specs=[pl.BlockSpec((1, tm, d), row1),
                  pl.BlockSpec((1, 1, d), vec1),
                  pl.BlockSpec((1, 1, d), vec1),
                  pl.BlockSpec((d, in_w), lambda i: (0, 0)),
                  pl.BlockSpec((tm, 128), tab2),
                  pl.BlockSpec((tm, 128), tab2),
                  pl.BlockSpec((tm, 128), tab2)],
        out_specs=[pl.BlockSpec((1, tm, Q_W), row2),
                   pl.BlockSpec((1, tm, KV_W), row2),
                   pl.BlockSpec((1, tm, KV_W), row2),
                   pl.BlockSpec((1, tm, hy_w), row2),
                   pl.BlockSpec((1, tm, g_w), row2)],
        out_shape=[jax.ShapeDtypeStruct((b, l, Q_W), BF16),
                   jax.ShapeDtypeStruct((b, l, KV_W), BF16),
                   jax.ShapeDtypeStruct((b, l, KV_W), BF16),
                   jax.ShapeDtypeStruct((b, l, hy_w), F32),
                   jax.ShapeDtypeStruct((b, l, g_w), BF16)],
        scratch_shapes=[pltpu.VMEM((tm, d), BF16)],
        compiler_params=_cparams(("arbitrary",)),
        name="in_projection",
    )(x, sh, sc, w_in_b, cos_t, sa_t, sb_t)


def _ctx_kv_kernel(x_ref, sh_ref, sc_ref, w_ref, k_ref, v_ref):
    h = _ln(x_ref[0]) * (1.0 + sc_ref[0]) + sh_ref[0]
    kv = _bdot(h.astype(BF16), w_ref[...])
    k_ref[0] = kv[:, :KV_W].astype(BF16)
    v_ref[0] = kv[:, KV_W:].astype(BF16)


def _ctx_kv(ctx, sh, sc, w_kv_b):
    b, c, d = ctx.shape
    row = lambda bi: (bi, 0, 0)
    return pl.pallas_call(
        _ctx_kv_kernel,
        grid=(b,),
        in_specs=[pl.BlockSpec((1, c, d), row),
                  pl.BlockSpec((1, 1, d), lambda bi: (0, 0, 0)),
                  pl.BlockSpec((1, 1, d), lambda bi: (0, 0, 0)),
                  pl.BlockSpec((d, 2 * KV_W), lambda bi: (0, 0))],
        out_specs=[pl.BlockSpec((1, c, KV_W), row), pl.BlockSpec((1, c, KV_W), row)],
        out_shape=[jax.ShapeDtypeStruct((b, c, KV_W), BF16),
                   jax.ShapeDtypeStruct((b, c, KV_W), BF16)],
        compiler_params=_cparams(("arbitrary",)),
        name="ctx_kv",
    )(ctx, sh, sc, w_kv_b)


def _attn_kernel(sink_ref, q_ref, k_ref, v_ref, kc_ref, vc_ref, bias_ref, o_ref):
    for qb in range(q_ref.shape[1] // ATTN_BLOCK):
        rows = slice(qb * ATTN_BLOCK, (qb + 1) * ATTN_BLOCK)
        j = pl.program_id(1) * (q_ref.shape[1] // ATTN_BLOCK) + qb
        o_ref[0, rows] = _attn_block(j, sink_ref, q_ref.at[0, rows], k_ref, v_ref, kc_ref, vc_ref,
                                     bias_ref)


def _band_bias():
    span = ATTN_BLOCK + 2 * WINDOW
    q = np.arange(GQA_GROUP * ATTN_BLOCK)[None, :, None] % ATTN_BLOCK
    k = np.arange(span)[None, None, :] - ATTN_BLOCK * np.arange(span // ATTN_BLOCK)[:, None, None]
    return np.where(np.abs(k - q) <= WINDOW, 0.0, NEG_BIG).astype(np.float32)


def _attn_block(j, sink_ref, q_ref, k_ref, v_ref, kc_ref, vc_ref, bias_ref):
    l = k_ref.shape[1]
    span = ATTN_BLOCK + 2 * WINDOW
    q0 = j * ATTN_BLOCK
    start = pl.multiple_of(jnp.clip(q0 - WINDOW, 0, l - span), ATTN_BLOCK)
    rows = GQA_GROUP * ATTN_BLOCK
    bias = bias_ref[(q0 - start) // ATTN_BLOCK]
    head_of_row = lax.broadcasted_iota(jnp.int32, (rows, 1), 0) // ATTN_BLOCK
    dn = (((1,), (1,)), ((), ()))
    outs = []
    for kv in range(N_KV_HEADS):
        ks = slice(kv * HEAD_DIM, (kv + 1) * HEAD_DIM)
        kl = k_ref[0, pl.ds(start, span), ks]
        vl = v_ref[0, pl.ds(start, span), ks]
        kc = kc_ref[0, :, ks]
        vc = vc_ref[0, :, ks]
        heads = [kv * GQA_GROUP + g for g in range(GQA_GROUP)]
        qg = jnp.concatenate([q_ref[:, h * HEAD_DIM:(h + 1) * HEAD_DIM] for h in heads], axis=0)
        sink = jnp.zeros((rows, 1), F32)
        for g, h in enumerate(heads):
            sink = jnp.where(head_of_row == g, sink_ref[h], sink)
        s_loc = lax.dot_general(qg, kl, dn, preferred_element_type=F32) + bias
        s_ctx = lax.dot_general(qg, kc, dn, preferred_element_type=F32)
        blocks = [s[:, c:c + 128] for s in (s_loc, s_ctx) for c in range(0, s.shape[1], 128)]
        folded = functools.reduce(jnp.maximum, blocks)
        m = jnp.maximum(jnp.max(folded, axis=1, keepdims=True), sink)
        p_loc = jnp.exp(s_loc - m).astype(BF16)
        p_ctx = jnp.exp(s_ctx - m).astype(BF16)
        ones_l = jnp.ones((span, HEAD_DIM), BF16)
        ones_c = jnp.ones((kc.shape[0], HEAD_DIM), BF16)
        pv = (_bdot(p_loc, jnp.concatenate([vl, ones_l], axis=1))
              + _bdot(p_ctx, jnp.concatenate([vc, ones_c], axis=1)))
        den = pv[:, HEAD_DIM:HEAD_DIM + 1] + jnp.exp(sink - m)
        o = pv[:, :HEAD_DIM] / den
        outs.extend(o[g * ATTN_BLOCK:(g + 1) * ATTN_BLOCK] for g in range(GQA_GROUP))
    return jnp.concatenate(outs, axis=1).astype(BF16)


def _attention(q, k, v, kc, vc, sink, q_blocks=8):
    b, l, _ = q.shape
    c = kc.shape[1]
    tq = q_blocks * ATTN_BLOCK
    bias = jnp.asarray(_band_bias())
    full = lambda bi, j, s: (bi, 0, 0)
    grid_spec = pltpu.PrefetchScalarGridSpec(
        num_scalar_prefetch=1,
        grid=(b, l // tq),
        in_specs=[pl.BlockSpec((1, tq, Q_W), lambda bi, j, s: (bi, j, 0)),
                  pl.BlockSpec((1, l, KV_W), full),
                  pl.BlockSpec((1, l, KV_W), full),
                  pl.BlockSpec((1, c, KV_W), full),
                  pl.BlockSpec((1, c, KV_W), full),
                  pl.BlockSpec(bias.shape, lambda bi, j, s: (0, 0, 0))],
        out_specs=pl.BlockSpec((1, tq, Q_W), lambda bi, j, s: (bi, j, 0)),
    )
    return pl.pallas_call(
        _attn_kernel,
        grid_spec=grid_spec,
        out_shape=jax.ShapeDtypeStruct((b, l, Q_W), BF16),
        compiler_params=_cparams(("arbitrary", "arbitrary")),
        name="window_attention",
    )(sink, q, k, v, kc, vc, bias)


def _dft_tables(l, wc):
    h = l // 2
    idx = np.arange(h, dtype=np.int64)
    ang = 2.0 * np.pi * ((idx[:, None] * idx[None, :]) % l).astype(np.float64) / l
    cm = np.cos(ang).astype(np.float32)
    sm = np.sin(ang).astype(np.float32)
    tw = 2.0 * np.pi * idx.astype(np.float64) / (2 * l)
    ones = np.ones((1, wc), np.float32)
    wr = np.cos(tw).astype(np.float32)[:, None] * ones
    wi = (-np.sin(tw)).astype(np.float32)[:, None] * ones
    sgn = np.where(idx % 2 == 0, 1.0, -1.0).astype(np.float32)[:, None] * ones
    wk = np.where(idx == 0, 1.0 / (2 * l), 2.0 / (2 * l)).astype(np.float32)[:, None] * ones
    return cm, sm, wr, wi, sgn, wk


def _half_spectrum(se, so, cm, sm, wr, wi, sgn):
    seb, sob = se.astype(BF16), so.astype(BF16)
    ce, ss_e = _bdot(cm, seb), _bdot(sm, seb)
    co, ss_o = _bdot(cm, sob), _bdot(sm, sob)
    vr = wr * co + wi * ss_o
    vi = wi * co - wr * ss_o
    zar, zai = ce + vr, vi - ss_e
    zbr, zbi = ce - vr, -ss_e - vi
    e_ny = jnp.sum(sgn * se, axis=0, keepdims=True)
    o_ny = jnp.sum(sgn * so, axis=0, keepdims=True)
    return zar, zai, zbr, zbi, e_ny, -o_ny


def _filter_kernel(ze_ref, zo_ref, w1_ref, b1_ref, w2_ref, b2_ref, w3f_ref, w3b_ref,
                   de_ref, do_ref, cm_ref, sm_ref, wr_ref, wi_ref, sgn_ref, wk_ref,
                   har_ref, hai_ref, hbr_ref, hbi_ref, hny_ref, ae_ref, ao_ref):
    cm, sm = cm_ref[...], sm_ref[...]
    wr, wi, sgn, wk = wr_ref[...], wi_ref[...], sgn_ref[...], wk_ref[...]
    h = cm.shape[0]

    @pl.when((pl.program_id(0) == 0) & (pl.program_id(1) == 0))
    def _():
        for z_ref, a_ref in ((ze_ref, ae_ref), (zo_ref, ao_ref)):
            a = jnp.sin(_fdot(z_ref[...], w1_ref[...]) + b1_ref[...])
            a_ref[...] = jnp.sin(_fdot(a, w2_ref[...]) + b2_ref[...])

    def taps(a_ref, w3_ref, d_ref):
        return _fdot(a_ref[...], w3_ref[0]) * d_ref[...]

    fe, fo = taps(ae_ref, w3f_ref, de_ref), taps(ao_ref, w3f_ref, do_ref)
    be, bo = taps(ae_ref, w3b_ref, de_ref), taps(ao_ref, w3b_ref, do_ref)
    row = lax.broadcasted_iota(jnp.int32, be.shape, 0)
    be = jnp.where(row == 0, 0.0, be)
    pe, po = (fe + be).astype(BF16), (fo + bo).astype(BF16)
    me, mo = (fe - be).astype(BF16), (fo - bo).astype(BF16)
    ce_p = _bdot(cm, pe)
    vr_p = wr * _bdot(cm, po) + wi * _bdot(sm, po)
    se_m = _bdot(sm, me)
    vi_m = wi * _bdot(cm, mo) - wr * _bdot(sm, mo)
    har_ref[0] = wk * (ce_p + vr_p)
    hai_ref[0] = wk * (vi_m - se_m)
    hbr_ref[0] = wk * (ce_p - vr_p)
    hbi_ref[0] = wk * (-se_m - vi_m)
    ny_scale = 2.0 / (4 * h)
    nr = ny_scale * jnp.sum(sgn * (fe + be), axis=0, keepdims=True)
    ni = -ny_scale * jnp.sum(sgn * (fo - bo), axis=0, keepdims=True)
    rows = lax.broadcasted_iota(jnp.int32, (8, nr.shape[1]), 0)
    hny_ref[0] = jnp.where(rows == 0, nr, jnp.where(rows == 1, ni, 0.0))


def _hyena_filters(zfe, zfo, w1, b1, w2, b2, w3, dec_e, dec_o, tabs, wc):
    cm, sm, wr, wi, sgn, wk = tabs
    h = cm.shape[0]
    emb = zfe.shape[1]
    hid = w2.shape[0]
    nblk = HY_WIDTH // wc
    w3r = w3.reshape(hid, HY_ORDER * 2, HY_WIDTH).transpose(1, 0, 2)
    const = lambda o, cb: (0, 0)
    chan = lambda o, cb: (0, cb)
    out_spec = pl.BlockSpec((1, h, wc), lambda o, cb: (o, 0, cb))
    out_shape = jax.ShapeDtypeStruct((HY_ORDER, h, HY_WIDTH), F32)
    return pl.pallas_call(
        _filter_kernel,
        grid=(HY_ORDER, nblk),
        in_specs=[pl.BlockSpec((h, emb), const), pl.BlockSpec((h, emb), const),
                  pl.BlockSpec((emb, hid), const), pl.BlockSpec((1, hid), const),
                  pl.BlockSpec((hid, hid), const), pl.BlockSpec((1, hid), const),
                  pl.BlockSpec((1, hid, wc), lambda o, cb: (2 * o, 0, cb)),
                  pl.BlockSpec((1, hid, wc), lambda o, cb: (2 * o + 1, 0, cb)),
                  pl.BlockSpec((h, wc), chan), pl.BlockSpec((h, wc), chan),
                  pl.BlockSpec((h, h), const), pl.BlockSpec((h, h), const),
                  pl.BlockSpec((h, wc), const), pl.BlockSpec((h, wc), const),
                  pl.BlockSpec((h, wc), const), pl.BlockSpec((h, wc), const)],
        out_specs=[out_spec, out_spec, out_spec, out_spec,
                   pl.BlockSpec((1, 8, wc), lambda o, cb: (o, 0, cb))],
        out_shape=[out_shape, out_shape, out_shape, out_shape,
                   jax.ShapeDtypeStruct((HY_ORDER, 8, HY_WIDTH), F32)],
        scratch_shapes=[pltpu.VMEM((h, hid), F32), pltpu.VMEM((h, hid), F32)],
        compiler_params=_cparams(("arbitrary", "arbitrary")),
        name="hyena_filter_spectrum",
    )(zfe, zfo, w1, b1.reshape(1, hid), w2, b2.reshape(1, hid), w3r, w3r,
      dec_e, dec_o, cm, sm, wr, wi, sgn, wk)


def _long_conv(ze, zo, har, hai, hbr, hbi, hny, cm, sm, wr, wi, sgn):
    zar, zai, zbr, zbi, znr, zni = _half_spectrum(ze, zo, cm, sm, wr, wi, sgn)
    yar, yai = zar * har - zai * hai, zar * hai + zai * har
    ybr, ybi = zbr * hbr - zbi * hbi, zbr * hbi + zbi * hbr
    dr, di = yar - ybr, yai - ybi
    sar, sai = (yar + ybr).astype(BF16), (yai + ybi).astype(BF16)
    sbr, sbi = (dr * wr + di * wi).astype(BF16), (di * wr - dr * wi).astype(BF16)
    hnr, hni = hny[0:1], hny[1:2]
    ynr = znr * hnr - zni * hni
    yni = znr * hni + zni * hnr
    ye = _bdot(cm, sar) - _bdot(sm, sai) + sgn * ynr
    yo = _bdot(cm, sbr) - _bdot(sm, sbi) - sgn * yni
    return ye, yo


def _hyena_kernel(*refs):
    nsub = (len(refs) - 18) // 4
    u_refs = [refs[s * nsub:(s + 1) * nsub] for s in range(HY_ORDER + 1)]
    rest = refs[(HY_ORDER + 1) * nsub:]
    (cw0_ref, cw1_ref, cw2_ref, cb0_ref, cb1_ref, cb2_ref, skip_ref,
     har_ref, hai_ref, hbr_ref, hbi_ref, hny_ref,
     cm_ref, sm_ref, wr_ref, wi_ref, sgn_ref, o_ref) = rest[:18]
    il_refs = rest[18:]
    h = cm_ref.shape[0]
    cm, sm = cm_ref[...], sm_ref[...]
    wr, wi, sgn = wr_ref[...], wi_ref[...], sgn_ref[...]
    row = lax.broadcasted_iota(jnp.int32, (h, wr.shape[1]), 0)

    def short_conv(u_slabs, cw_ref, cb_ref):
        ue = jnp.concatenate([r[0, pl.ds(0, h, stride=2), :] for r in u_slabs], axis=1)
        uo = jnp.concatenate([r[0, pl.ds(1, h, stride=2), :] for r in u_slabs], axis=1)
        w0, w1, w2 = cw_ref[0:1], cw_ref[1:2], cw_ref[2:3]
        uo_prev = jnp.where(row == 0, 0.0, pltpu.roll(uo, 1, 0))
        ue_next = jnp.where(row == h - 1, 0.0, pltpu.roll(ue, h - 1, 0))
        cb = cb_ref[...]
        se = cb + uo_prev * w0 + ue * w1 + uo * w2
        so = cb + ue * w0 + uo * w1 + ue_next * w2
        return se, so

    x_streams = [short_conv(u_refs[0], cw0_ref, cb0_ref), short_conv(u_refs[1], cw1_ref, cb1_ref)]
    ze, zo = short_conv(u_refs[2], cw2_ref, cb2_ref)
    for o in range(HY_ORDER):
        ce, co = _long_conv(ze, zo, har_ref[o], hai_ref[o], hbr_ref[o], hbi_ref[o], hny_ref[o],
                            cm, sm, wr, wi, sgn)
        sk = skip_ref[o:o + 1]
        xe, xo = x_streams[o]
        ze = xe * (ce + sk * ze)
        zo = xo * (co + sk * zo)
    for j, il in enumerate(il_refs):
        il[pl.ds(0, h, stride=2), :] = ze[:, j * 128:(j + 1) * 128]
        il[pl.ds(1, h, stride=2), :] = zo[:, j * 128:(j + 1) * 128]
        o_ref[0, :, j * 128:(j + 1) * 128] = il[...]


def _hyena(u, conv_w, conv_b, skip, spectra, tabs, wc):
    b, l, _ = u.shape
    har, hai, hbr, hbi, hny = spectra
    cm, sm, wr, wi, sgn, _ = tabs
    h = l // 2
    nblk = HY_WIDTH // wc
    nsub = wc // 128
    conv_b = conv_b.reshape(1, -1)

    def ublk(s, j):
        return pl.BlockSpec((1, l, 128), lambda cb, bi: (bi, 0, (cb + s * nblk) * nsub + j))

    def wblk(r, s):
        return pl.BlockSpec((r, wc), lambda cb, bi: (0, cb + s * nblk))

    const = lambda cb, bi: (0, 0)
    spec = pl.BlockSpec((HY_ORDER, h, wc), lambda cb, bi: (0, 0, cb))
    return pl.pallas_call(
        _hyena_kernel,
        grid=(nblk, b),
        in_specs=[ublk(s, j) for s in range(HY_ORDER + 1) for j in range(nsub)] + [
                  wblk(HY_SHORT_CONV, 0), wblk(HY_SHORT_CONV, 1), wblk(HY_SHORT_CONV, 2),
                  wblk(1, 0), wblk(1, 1), wblk(1, 2),
                  pl.BlockSpec((HY_ORDER, wc), lambda cb, bi: (0, cb)),
                  spec, spec, spec, spec,
                  pl.BlockSpec((HY_ORDER, 8, wc), lambda cb, bi: (0, 0, cb)),
                  pl.BlockSpec((h, h), const), pl.BlockSpec((h, h), const),
                  pl.BlockSpec((h, wc), const), pl.BlockSpec((h, wc), const),
                  pl.BlockSpec((h, wc), const)],
        out_specs=pl.BlockSpec((1, l, wc), lambda cb, bi: (bi, 0, cb)),
        out_shape=jax.ShapeDtypeStruct((b, l, HY_WIDTH), F32),
        scratch_shapes=[pltpu.VMEM((l, 128), F32) for _ in range(nsub)],
        compiler_params=_cparams(("arbitrary", "arbitrary")),
        name="hyena_long_conv",
    )(*([u] * ((HY_ORDER + 1) * nsub)), conv_w, conv_w, conv_w, conv_b, conv_b, conv_b, skip,
      har, hai, hbr, hbi, hny, cm, sm, wr, wi, sgn)


def _merge_kernel(alpha, att_ref, hy_ref, gate_ref, x_ref, g1_ref, sh2_ref, sc2_ref,
                  wba_ref, wbh_ref, wo_ref, l1g_ref, l1b_ref, rw_ref, rb_ref,
                  xmid_ref, h2_ref, logit_ref, res_ref):
    d = x_ref.shape[1]

    @pl.when(pl.program_id(0) == 0)
    def _():
        res_ref[...] = jnp.zeros_like(res_ref)

    rw = rw_ref[...]
    r_hi = rw.astype(BF16)
    r_lo = (rw - r_hi.astype(F32)).astype(BF16)
    dn = (((1,), (1,)), ((), ()))
    nt_dot = lambda p, q: lax.dot_general(p, q, dn, preferred_element_type=F32)
    tm = x_ref.shape[0]
    rc = tm // MERGE_CHUNKS
    for r in (slice(c * rc, (c + 1) * rc) for c in range(MERGE_CHUNKS)):
        xm = _ln(res_ref[r]) * l1g_ref[...] + l1b_ref[...]
        xmid_ref[r] = xm
        h2 = _ln(xm) * (1.0 + sc2_ref[0]) + sh2_ref[0]
        h_hi = h2.astype(BF16)
        h2_ref[r] = h_hi
        h_lo = (h2 - h_hi.astype(F32)).astype(BF16)
        logit_ref[:, r] = nt_dot(r_hi, h_hi) + nt_dot(r_hi, h_lo) + nt_dot(r_lo, h_hi) + rb_ref[...]

        a = _bdot(att_ref[r], wba_ref[...])
        hh = _bdot(hy_ref[r].astype(BF16), wbh_ref[...])
        ga = gate_ref[r, :d].astype(F32)
        gh = gate_ref[r, d:].astype(F32)
        y = _bdot((ga * a + gh * hh).astype(BF16), wo_ref[...])
        res_ref[r] = alpha * x_ref[r] + g1_ref[0] * y


def _merge(att, hy, gate, x, g1, sh2, sc2, wba, wbh, wo, l1g, l1b, rw_t, rb, alpha, tm=512):
    t, d = x.shape
    ne = rw_t.shape[0]
    nt = t // tm
    per_b = nt // g1.shape[0]
    t1 = lambda i: jnp.minimum(i, nt - 1)
    t2 = lambda i: jnp.maximum(i - 1, 0)
    row1 = lambda i: (t1(i), 0)
    row2 = lambda i: (t2(i), 0)
    const = lambda i: (0, 0)
    return pl.pallas_call(
        functools.partial(_merge_kernel, alpha),
        grid=(nt + 1,),
        in_specs=[pl.BlockSpec((tm, Q_W), row1),
                  pl.BlockSpec((tm, HY_WIDTH), row1),
                  pl.BlockSpec((tm, 2 * d), row1),
                  pl.BlockSpec((tm, d), row1),
                  pl.BlockSpec((1, 1, d), lambda i: (t1(i) // per_b, 0, 0)),
                  pl.BlockSpec((1, 1, d), lambda i: (t2(i) // per_b, 0, 0)),
                  pl.BlockSpec((1, 1, d), lambda i: (t2(i) // per_b, 0, 0)),
                  pl.BlockSpec((Q_W, d), const), pl.BlockSpec((HY_WIDTH, d), const),
                  pl.BlockSpec((d, d), const),
                  pl.BlockSpec((1, d), const), pl.BlockSpec((1, d), const),
                  pl.BlockSpec((ne, d), const), pl.BlockSpec((ne, 1), const)],
        out_specs=[pl.BlockSpec((tm, d), row2), pl.BlockSpec((tm, d), row2),
                   pl.BlockSpec((ne, tm), lambda i: (0, t2(i)))],
        out_shape=[jax.ShapeDtypeStruct((t, d), F32),
                   jax.ShapeDtypeStruct((t, d), BF16),
                   jax.ShapeDtypeStruct((ne, t), F32)],
        scratch_shapes=[pltpu.VMEM((tm, d), F32)],
        compiler_params=_cparams(("arbitrary",)),
        name="merge_ln_router",
    )(att, hy, gate, x, g1, sh2, sc2, wba, wbh, wo, l1g.reshape(1, d), l1b.reshape(1, d),
      rw_t, rb.reshape(ne, 1))


def _route_kernel(lg_ref, gate_ref, pos_ref, tcnt_ref, tbase_ref, cnt_ref, run_ref):
    @pl.when(pl.program_id(0) == 0)
    def _():
        run_ref[...] = jnp.zeros_like(run_ref)

    tiles = tcnt_ref.shape[0]
    tr = lg_ref.shape[1] // tiles
    for s in range(tiles):
        cols = slice(s * tr, (s + 1) * tr)
        gate, pos, cnt_row = _route_tile(lg_ref[:, cols])
        gate_ref[:, cols] = gate
        pos_ref[:, cols] = pos
        tcnt_ref[s] = cnt_row.astype(jnp.int32)
        tbase_ref[s] = run_ref[...].astype(jnp.int32)
        run_ref[...] = run_ref[...] + cnt_row
    cnt_ref[...] = run_ref[...].astype(jnp.int32)


def _route_tile(lg):
    ne, tr = lg.shape
    sub = lax.broadcasted_iota(jnp.int32, (ne, tr), 0)
    work = lg
    vals, hots = [], []
    for _ in range(TOP_K):
        m = jnp.max(work, axis=0, keepdims=True)
        idx = jnp.min(jnp.where(work == m, sub, ne), axis=0, keepdims=True)
        hot = sub == idx
        vals.append(m)
        hots.append(hot)
        work = jnp.where(hot, -jnp.inf, work)
    exps = [jnp.exp(v - vals[0]) for v in vals]
    den = exps[0] + exps[1] + exps[2] + exps[3]
    member = jnp.zeros((ne, tr), F32)
    for hot in hots:
        member = member + jnp.where(hot, 1.0, 0.0)
    r_i = lax.broadcasted_iota(jnp.int32, (tr, tr), 0)
    c_i = lax.broadcasted_iota(jnp.int32, (tr, tr), 1)
    earlier = jnp.where(r_i < c_i, 1.0, 0.0).astype(BF16)
    prefix = _bdot(member.astype(BF16), earlier)
    cnt = jnp.broadcast_to(jnp.sum(member, axis=1, keepdims=True), (ne, 128))
    e_r = lax.broadcasted_iota(jnp.int32, (ne, ne), 0)
    e_c = lax.broadcasted_iota(jnp.int32, (ne, ne), 1)
    lower = jnp.where(e_c < e_r, 1.0, 0.0).astype(BF16)
    off = _bdot(lower, cnt.astype(BF16))
    base = off[:, 0:1] + prefix
    sub_k = lax.broadcasted_iota(jnp.int32, (TOP_K, tr), 0)
    gate = jnp.zeros((TOP_K, tr), F32)
    pos = jnp.zeros((TOP_K, tr), F32)
    for k in range(TOP_K):
        pk = jnp.sum(jnp.where(hots[k], base, 0.0), axis=0, keepdims=True)
        gate = jnp.where(sub_k == k, exps[k] / den, gate)
        pos = jnp.where(sub_k == k, pk, pos)
    cnt_row = lax.dot_general(jnp.ones((8, tr), BF16), member.astype(BF16),
                              (((1,), (1,)), ((), ())), preferred_element_type=F32)
    return gate, pos.astype(jnp.int32), cnt_row


def _route(logits_t, tr, tiles_per_step=4):
    ne, t = logits_t.shape
    nt = t // tr
    tb = tr * tiles_per_step
    tok = lambda i: (0, i)
    tile = lambda i: (i, 0, 0)
    return pl.pallas_call(
        _route_kernel,
        grid=(nt // tiles_per_step,),
        in_specs=[pl.BlockSpec((ne, tb), tok)],
        out_specs=[pl.BlockSpec((TOP_K, tb), tok), pl.BlockSpec((TOP_K, tb), tok),
                   pl.BlockSpec((tiles_per_step, 8, ne), tile),
                   pl.BlockSpec((tiles_per_step, 8, ne), tile),
                   pl.BlockSpec((8, ne), lambda i: (0, 0))],
        out_shape=[jax.ShapeDtypeStruct((TOP_K, t), F32),
                   jax.ShapeDtypeStruct((TOP_K, t), jnp.int32),
                   jax.ShapeDtypeStruct((nt, 8, ne), jnp.int32),
                   jax.ShapeDtypeStruct((nt, 8, ne), jnp.int32),
                   jax.ShapeDtypeStruct((8, ne), jnp.int32)],
        scratch_shapes=[pltpu.VMEM((8, ne), F32)],
        compiler_params=_cparams(("arbitrary",)),
        name="route_topk",
    )(logits_t)


def _row_tile(d):
    return (d // 256, 128)


def _pack_rows(v, bf16_exact=False):
    n, d = v.shape
    if bf16_exact:
        bits = lax.bitcast_convert_type(v, jnp.uint32)
        word = bits[:, d // 2:] | (bits[:, :d // 2] >> 16)
    else:
        bits = lax.bitcast_convert_type(v.astype(BF16).astype(F32), jnp.uint32)
        word = (bits[:, d // 2:] & jnp.uint32(0xFFFF0000)) | (bits[:, :d // 2] >> 16)
    return word.reshape((n,) + _row_tile(d))


def _unpack_rows(w):
    n = w.shape[0]
    word = w.reshape(n, w.shape[1] * w.shape[2])
    lo = lax.bitcast_convert_type(word << 16, F32)
    hi = lax.bitcast_convert_type(word & jnp.uint32(0xFFFF0000), F32)
    return jnp.concatenate([lo, hi], axis=1)


def _strip_copies(n_ref, dst_ref, make_copy, max_rows):
    ne = dst_ref.shape[2]
    all_bits = [1 << s for s in range(max_rows.bit_length() - 1, -1, -1)]
    small = pl.next_power_of_2(max(1, 2 * TOP_K * max_rows // ne))
    longest = n_ref[0, 0, ne]

    def run(bits):
        def body(e, off):
            n = n_ref[0, 0, e]
            dst = dst_ref[0, 0, e]
            done = jnp.int32(0)
            for bit in bits:
                part = n & bit

                @pl.when(part != 0)
                def _():
                    make_copy(off + done, dst + done, bit).start()

                done = done + part
            return off + n

        lax.fori_loop(0, ne, body, jnp.int32(0))

    @pl.when(longest < small)
    def _():
        run([b for b in all_bits if b < small])

    @pl.when(longest >= small)
    def _():
        run(all_bits)


def _dispatch_kernel(n_ref, dst_ref, pos_ref, h_ref, xs_ref, srt_ref, sem):
    i = pl.program_id(0)
    slot = i % 2
    td = h_ref.shape[0]
    n = td * TOP_K

    @pl.when(i == 0)
    def _():
        tail = xs_ref.shape[0] - EXPERT_TILE
        srt_ref[1, 0:EXPERT_TILE] = jnp.zeros((EXPERT_TILE,) + srt_ref.shape[2:], srt_ref.dtype)
        zero_tail = pltpu.make_async_copy(srt_ref.at[1, pl.ds(0, EXPERT_TILE)],
                                          xs_ref.at[pl.ds(tail, EXPERT_TILE)], sem.at[1])
        zero_tail.start()
        zero_tail.wait()

    pos = pos_ref[...]
    rows = lax.broadcasted_iota(jnp.int32, (n, td), 0)
    hit = rows == pos[0:1]
    for k in range(1, TOP_K):
        hit = hit | (rows == pos[k:k + 1])
    perm = jnp.where(hit, 1.0, 0.0).astype(BF16)
    srt_ref[slot] = _pack_rows(_bdot(perm, h_ref[...]), bf16_exact=True)

    def all_rows(s):
        return pltpu.make_async_copy(srt_ref.at[s], xs_ref.at[pl.ds(0, n)], sem.at[s])

    _strip_copies(n_ref, dst_ref,
                  lambda src, dst, size: pltpu.make_async_copy(
                      srt_ref.at[slot, pl.ds(src, size)], xs_ref.at[pl.ds(dst, size)], sem.at[slot]),
                  td)

    @pl.when(i > 0)
    def _():
        all_rows(1 - slot).wait()

    @pl.when(i == pl.num_programs(0) - 1)
    def _():
        all_rows(slot).wait()


def _dispatch(h2, pos_t, tile_n, tile_dst, td):
    t, d = h2.shape
    a = t * TOP_K
    n = td * TOP_K
    nt = t // td
    smem = lambda v: pl.BlockSpec((1, 1, v.shape[-1]), lambda i: (i, 0, 0),
                                  memory_space=pltpu.SMEM)
    return pl.pallas_call(
        _dispatch_kernel,
        grid=(nt,),
        in_specs=[smem(tile_n), smem(tile_dst),
                  pl.BlockSpec((TOP_K, td), lambda i: (0, i)),
                  pl.BlockSpec((td, d), lambda i: (i, 0))],
        out_specs=pl.BlockSpec(memory_space=pl.ANY),
        out_shape=jax.ShapeDtypeStruct((a + EXPERT_TILE,) + _row_tile(d), jnp.uint32),
        scratch_shapes=[pltpu.VMEM((2, n) + _row_tile(d), jnp.uint32),
                        pltpu.SemaphoreType.DMA((2,))],
        compiler_params=_cparams(("arbitrary",)),
        name="moe_dispatch",
    )(tile_n[:, None, :], tile_dst[:, None, :], pos_t, h2)


def _expert_kernel(exp_ref, row_ref, cls_ref, nitem_ref,
                   xs_ref, w1_ref, b1_ref, w2_ref, b2_ref, ys_ref,
                   w1b_ref, w2b_ref, xbuf_ref, ybuf_ref, sem_in, sem_out):
    w = pl.program_id(0)
    n_items = nitem_ref[0]
    last = exp_ref.shape[0] - 1
    slot = w % 2
    dff = w2_ref.shape[1]

    def fetch(item, s):
        return pltpu.make_async_copy(xs_ref.at[pl.ds(row_ref[item], EXPERT_TILE)],
                                     xbuf_ref.at[s], sem_in.at[s])

    def write(item, s, rows):
        return pltpu.make_async_copy(ybuf_ref.at[s, pl.ds(0, rows)],
                                     ys_ref.at[pl.ds(row_ref[item], rows)], sem_out.at[s])

    def for_tile_size(item, fn):
        for c, rows in enumerate(EXPERT_LAST_TILES):
            @pl.when(cls_ref[item] == c)
            def _():
                fn(rows)

    @pl.when(w == 0)
    def _():
        fetch(0, 0).start()
        ybuf_ref[1] = jnp.zeros(ybuf_ref.shape[1:], ybuf_ref.dtype)
        zero_tail = pltpu.make_async_copy(
            ybuf_ref.at[1], ys_ref.at[pl.ds(ys_ref.shape[0] - EXPERT_TILE, EXPERT_TILE)],
            sem_out.at[1])
        zero_tail.start()
        zero_tail.wait()

    @pl.when(w + 1 < n_items)
    def _():
        fetch(jnp.minimum(w + 1, last), 1 - slot).start()

    @pl.when((w == 0) | (exp_ref[w] != exp_ref[jnp.maximum(w - 1, 0)]))
    def _():
        w1b_ref[...] = w1_ref[0].astype(BF16)
        w2b_ref[...] = w2_ref[0].astype(BF16)

    @pl.when(w < n_items)
    def _():
        fetch(w, slot).wait()

        def compute(rows):
            x = _unpack_rows(xbuf_ref[slot, 0:rows]).astype(BF16)
            hb = _bdot(x, w1b_ref[...]) + b1_ref[0]
            glu = jnp.minimum(hb[:, :dff], SWIGLU_LIMIT)
            lin = jnp.clip(hb[:, dff:], -SWIGLU_LIMIT, SWIGLU_LIMIT)
            act = glu * jax.nn.sigmoid(SWIGLU_ALPHA * glu) * (lin + 1.0)
            ybuf_ref[slot, 0:rows] = _pack_rows(_bdot(act.astype(BF16), w2b_ref[...]) + b2_ref[0])

        for_tile_size(w, compute)

        @pl.when(w > 0)
        def _():
            for_tile_size(w - 1, lambda rows: write(w - 1, 1 - slot, rows).wait())

        for_tile_size(w, lambda rows: write(w, slot, rows).start())

        @pl.when(w == n_items - 1)
        def _():
            for_tile_size(w, lambda rows: write(w, slot, rows).wait())


def _experts(xs, items, w1, b1, w2, b2):
    ne, d, dff2 = w1.shape
    dff = dff2 // 2
    exp_w, row_w, cls_w, nitem = items
    per_expert = lambda w, ex, ro, cl, ni: (ex[w], 0, 0)
    tile = (EXPERT_TILE,) + _row_tile(d)
    grid_spec = pltpu.PrefetchScalarGridSpec(
        num_scalar_prefetch=4,
        grid=(exp_w.shape[0],),
        in_specs=[pl.BlockSpec(memory_space=pl.ANY),
                  pl.BlockSpec((1, d, dff2), per_expert),
                  pl.BlockSpec((1, 1, dff2), per_expert),
                  pl.BlockSpec((1, dff, d), per_expert),
                  pl.BlockSpec((1, 1, d), per_expert)],
        out_specs=pl.BlockSpec(memory_space=pl.ANY),
        scratch_shapes=[pltpu.VMEM((d, dff2), BF16), pltpu.VMEM((dff, d), BF16),
                        pltpu.VMEM((2,) + tile, jnp.uint32), pltpu.VMEM((2,) + tile, jnp.uint32),
                        pltpu.SemaphoreType.DMA((2,)), pltpu.SemaphoreType.DMA((2,))],
    )
    return pl.pallas_call(
        _expert_kernel,
        grid_spec=grid_spec,
        out_shape=jax.ShapeDtypeStruct(xs.shape, jnp.uint32),
        compiler_params=_cparams(("arbitrary",)),
        name="moe_experts",
    )(exp_w, row_w, cls_w, nitem, xs, w1, b1.reshape(ne, 1, dff2), w2, b2.reshape(ne, 1, d))


def _work_items(counts, a):
    ne = counts.shape[0]
    n_items = a // EXPERT_TILE + ne
    ends = jnp.cumsum(counts)
    starts = ends - counts
    n_full = counts // EXPERT_TILE
    rem = counts - n_full * EXPERT_TILE
    n_tiles = n_full + (rem > 0).astype(jnp.int32)
    tile_end = jnp.cumsum(n_tiles)
    tile_start = tile_end - n_tiles
    total = tile_end[-1]
    w = jnp.minimum(jnp.arange(n_items, dtype=jnp.int32), total - 1)
    e_w = jnp.sum((tile_end[None, :] <= w[:, None]).astype(jnp.int32), axis=1)
    hot = e_w[:, None] == jnp.arange(ne, dtype=jnp.int32)[None, :]
    pick = lambda v: jnp.sum(jnp.where(hot, v[None, :], 0), axis=1)
    k = w - pick(tile_start)
    row_w = (pick(starts) + k * EXPERT_TILE).astype(jnp.int32)
    rem_w = pick(rem)
    rem_cls = jnp.zeros_like(rem_w)
    for c, rows in enumerate(EXPERT_LAST_TILES[:-1]):
        rem_cls = rem_cls + (rem_w > rows).astype(jnp.int32)
    full_cls = len(EXPERT_LAST_TILES) - 1
    cls_w = jnp.where(k < pick(n_full), full_cls, rem_cls).astype(jnp.int32)
    return starts, (e_w.astype(jnp.int32), row_w, cls_w, total.reshape(1).astype(jnp.int32))


def _combine_kernel(alpha, n_ref, dst_ref, nn_ref, ndst_ref, pos_ref, gate_ref,
                    ys_ref, xm_ref, g2_ref, lg_ref, lb_ref, o_ref, srt_ref, sem):
    i = pl.program_id(0)
    last = pl.num_programs(0) - 1
    slot = i % 2
    td = xm_ref.shape[0]
    n = td * TOP_K

    def fetch(cnt_ref, from_ref, s):
        _strip_copies(cnt_ref, from_ref,
                      lambda row, src, size: pltpu.make_async_copy(
                          ys_ref.at[pl.ds(src, size)], srt_ref.at[s, pl.ds(row, size)], sem.at[s]),
                      td)

    @pl.when(i == 0)
    def _():
        fetch(n_ref, dst_ref, slot)

    @pl.when(i < last)
    def _():
        fetch(nn_ref, ndst_ref, 1 - slot)

    pltpu.make_async_copy(ys_ref.at[pl.ds(0, n)], srt_ref.at[slot], sem.at[slot]).wait()

    pos, gate = jnp.transpose(pos_ref[...]), jnp.transpose(gate_ref[...])
    lanes = lax.broadcasted_iota(jnp.int32, (td, n), 1)
    wsel = jnp.zeros((td, n), F32)
    for k in range(TOP_K):
        wsel = jnp.where(lanes == pos[:, k:k + 1], gate[:, k:k + 1], wsel)
    y = _unpack_rows(srt_ref[slot]).astype(BF16)
    f = _bdot(wsel.astype(BF16), y)
    o_ref[...] = _ln(alpha * xm_ref[...] + g2_ref[0] * f) * lg_ref[...] + lb_ref[...]


def _combine(ys, pos, gate, tile_n, tile_dst, x_mid, g2, lg, lb, alpha, td):
    t, d = x_mid.shape
    n = td * TOP_K
    nt = t // td
    per_b = nt // g2.shape[0]
    first = lambda v: pl.BlockSpec((1, 1, v.shape[-1]), lambda i: (0, 0, 0),
                                   memory_space=pltpu.SMEM)
    nxt = lambda v: pl.BlockSpec((1, 1, v.shape[-1]),
                                 lambda i: (jnp.minimum(i + 1, nt - 1), 0, 0),
                                 memory_space=pltpu.SMEM)
    const = lambda i: (0, 0)
    tile_n = tile_n[:, None, :]
    tile_dst = tile_dst[:, None, :]
    return pl.pallas_call(
        functools.partial(_combine_kernel, alpha),
        grid=(nt,),
        in_specs=[first(tile_n), first(tile_dst), nxt(tile_n), nxt(tile_dst),
                  pl.BlockSpec((TOP_K, td), lambda i: (0, i)),
                  pl.BlockSpec((TOP_K, td), lambda i: (0, i)),
                  pl.BlockSpec(memory_space=pl.ANY),
                  pl.BlockSpec((td, d), lambda i: (i, 0)),
                  pl.BlockSpec((1, 1, d), lambda i: (i // per_b, 0, 0)),
                  pl.BlockSpec((1, d), const), pl.BlockSpec((1, d), const)],
        out_specs=pl.BlockSpec((td, d), lambda i: (i, 0)),
        out_shape=jax.ShapeDtypeStruct((t, d), F32),
        scratch_shapes=[pltpu.VMEM((2, n) + _row_tile(d), jnp.uint32),
                        pltpu.SemaphoreType.DMA((2,))],
        compiler_params=_cparams(("arbitrary",)),
        name="moe_combine_ln",
    )(tile_n, tile_dst, tile_n, tile_dst, pos, gate, ys, x_mid, g2,
      lg.reshape(1, d), lb.reshape(1, d))


def _rope_tables(l):
    f32 = np.float32
    rows = l // GRID_W
    row = np.repeat(np.arange(rows, dtype=f32), GRID_W)
    col = np.tile(np.arange(GRID_W, dtype=f32), rows)
    n_freq = HEAD_DIM // 4
    inv_freq = np.power(f32(ROPE_BASE), -np.arange(n_freq, dtype=f32) / f32(n_freq)).astype(f32)
    ang_r = (row[:, None] * inv_freq).astype(f32)
    ang_c = (col[:, None] * inv_freq).astype(f32)
    zero = np.zeros_like(ang_r)
    cos_r, sin_r, cos_c, sin_c = np.cos(ang_r), np.sin(ang_r), np.cos(ang_c), np.sin(ang_c)
    cos_h = np.concatenate([cos_r, cos_r, cos_c, cos_c], axis=1)
    sa_h = np.concatenate([-sin_r, zero, -sin_c, zero], axis=1)
    sb_h = np.concatenate([zero, sin_r, zero, sin_c], axis=1)
    rep = 128 // HEAD_DIM
    return tuple(np.tile(a, (1, rep)).astype(f32) for a in (cos_h, sa_h, sb_h))


def _filter_features(l):
    f32 = np.float32
    bands = (HY_EMB_DIM - 1) // 2
    t = np.linspace(0.0, 1.0, l, dtype=f32)[:, None]
    omega = (f32(2.0 * math.pi) * np.arange(l, dtype=f32)[:, None] / f32(l)).astype(f32)
    f = np.linspace(1e-4, bands - 1, bands, dtype=f32)[None, :]
    ang = (f * omega).astype(f32)
    z = np.concatenate([t, np.cos(ang), -np.sin(ang)], axis=-1).astype(f32)
    min_decay = math.log(HY_DECAY_TARGET) / HY_FAST_DECAY_PCT
    max_decay = math.log(HY_DECAY_TARGET) / HY_SLOW_DECAY_PCT
    deltas = np.abs(np.linspace(min_decay, max_decay, HY_WIDTH, dtype=f32))
    decay = np.exp(-t * deltas).astype(f32)
    return z, decay


def kernel(x, c, ctx, c_ctx, w_mod, b_mod, w_in, attn_sink, hy_conv_w, hy_conv_b, hy_filt_w1,
           hy_filt_b1, hy_filt_w2, hy_filt_b2, hy_filt_w3, hy_skip, w_branch_attn, w_branch_hyena,
           w_out, ln1_g, ln1_b, router_w, router_b, exp_w1, exp_b1, exp_w2, exp_b2, ln2_g, ln2_b):
    depth = w_mod.shape[0]
    assert depth == 1, "only the single-layer configuration is implemented"
    b, l, d = x.shape
    t = b * l
    alpha = (2 * depth) ** 0.25
    hy_wc = 256

    n_cond = b + 1
    pad = (-n_cond) % 8
    cond = jnp.concatenate([c, c_ctx[None], jnp.zeros((pad, d), F32)], axis=0)
    mod = _modulation(cond, w_mod[0], b_mod[0])
    mod_x = mod[:b].reshape(b, 1, 6, d)
    sh1, sc1, g1, sh2, sc2, g2 = (mod_x[:, :, i] for i in range(6))
    mod_c = mod[b:b + 1].reshape(1, 1, 6, d)
    csh1, csc1 = mod_c[:, :, 0], mod_c[:, :, 1]

    w_in_b = w_in[0].astype(BF16)
    cos_t, sa_t, sb_t = (jnp.asarray(a) for a in _rope_tables(l))
    q, k, v, u_hy, gate_x = _in_projection(x, sh1, sc1, w_in_b, cos_t, sa_t, sb_t)
    k_c, v_c = _ctx_kv(ctx, csh1, csc1, w_in_b[:, K_OFF:HY_OFF])
    att = _attention(q, k, v, k_c, v_c, attn_sink[0])

    tabs_np = _dft_tables(l, hy_wc)
    tabs = (jnp.asarray(tabs_np[0]).astype(BF16), jnp.asarray(tabs_np[1]).astype(BF16)) + tuple(
        jnp.asarray(a) for a in tabs_np[2:])
    zfeat, decay = _filter_features(l)
    emb_pad = (-HY_EMB_DIM) % 128
    zfeat = np.pad(zfeat, ((0, 0), (0, emb_pad)))
    fw1 = jnp.pad(hy_filt_w1[0], ((0, emb_pad), (0, 0)))
    spectra = _hyena_filters(jnp.asarray(zfeat[0::2]), jnp.asarray(zfeat[1::2]), fw1, hy_filt_b1[0],
                             hy_filt_w2[0], hy_filt_b2[0], hy_filt_w3[0], jnp.asarray(decay[0::2]),
                             jnp.asarray(decay[1::2]), tabs, hy_wc)
    hy = _hyena(u_hy, hy_conv_w[0], hy_conv_b[0], hy_skip[0], spectra, tabs, hy_wc)

    x_mid, h2, logits_t = _merge(att.reshape(t, Q_W), hy.reshape(t, HY_WIDTH),
                                 gate_x.reshape(t, 2 * d), x.reshape(t, d), g1, sh2, sc2,
                                 w_branch_attn[0].astype(BF16), w_branch_hyena[0].astype(BF16),
                                 w_out[0].astype(BF16), ln1_g[0], ln1_b[0],
                                 jnp.transpose(router_w[0]), router_b[0], alpha)

    moe_td = 256
    gate_t, pos_t, tile_cnt, tile_base, counts = _route(logits_t, moe_td)
    starts, items = _work_items(counts[0], t * TOP_K)
    tile_n = tile_cnt[:, 0, :]
    tile_n = jnp.concatenate([tile_n, jnp.max(tile_n, axis=1, keepdims=True)], axis=1)
    tile_dst = starts[None, :] + tile_base[:, 0, :]
    xs = _dispatch(h2.reshape(t, d), pos_t, tile_n, tile_dst, moe_td)
    ys = _experts(xs, items, exp_w1[0], exp_b1[0], exp_w2[0], exp_b2[0])
    out = _combine(ys, pos_t, gate_t, tile_n, tile_dst,
                   x_mid.reshape(t, d), g2, ln2_g[0], ln2_b[0], alpha, moe_td)
    return out.reshape(b, l, d)
```

```python
import functools
import math

import numpy as np
import jax
import jax.numpy as jnp
from jax import lax
from jax.experimental import pallas as pl
from jax.experimental.pallas import tpu as pltpu

F32 = jnp.float32
BF16 = jnp.bfloat16
HIGHEST = lax.Precision.HIGHEST

GRID_W = 64
N_HEADS = 8
N_KV_HEADS = 2
GQA_GROUP = N_HEADS // N_KV_HEADS
HEAD_DIM = 64
WINDOW = 128
ATTN_BLOCK = 128
ROPE_BASE = 10000.0

HY_WIDTH = 512
HY_ORDER = 2
HY_SHORT_CONV = 3
HY_EMB_DIM = 33
HY_DECAY_TARGET = 1e-2
HY_FAST_DECAY_PCT = 0.3
HY_SLOW_DECAY_PCT = 1.5

N_EXPERTS = 32
TOP_K = 4
SWIGLU_LIMIT = 7.0
SWIGLU_ALPHA = 1.702
LN_EPS = 1e-5

Q_W = N_HEADS * HEAD_DIM
KV_W = N_KV_HEADS * HEAD_DIM
K_OFF = Q_W
V_OFF = K_OFF + KV_W
HY_OFF = V_OFF + KV_W
GATE_OFF = HY_OFF + (HY_ORDER + 1) * HY_WIDTH

VMEM_LIMIT = 56 * 1024 * 1024
NEG_BIG = -1e30
MERGE_CHUNKS = 2
EXPERT_TILE = 1024
EXPERT_LAST_TILES = (256, 512, 768, 1024)


def _cparams(sem):
    return pltpu.CompilerParams(dimension_semantics=sem, vmem_limit_bytes=VMEM_LIMIT)


def _ln(x):
    mu = jnp.mean(x, axis=-1, keepdims=True)
    xc = x - mu
    var = jnp.mean(xc * xc, axis=-1, keepdims=True)
    return xc * lax.rsqrt(var + LN_EPS)


def _bdot(a, b):
    return jnp.dot(a, b, preferred_element_type=F32)


def _fdot(a, b):
    return jnp.dot(a, b, preferred_element_type=F32, precision=HIGHEST)


def _mod_kernel(c_ref, w_ref, b_ref, o_ref):
    c = c_ref[...]
    s = c * jax.nn.sigmoid(c)
    o_ref[...] = _fdot(s, w_ref[...]) + b_ref[...]


def _modulation(cond, w, b, tn=512):
    r, d = cond.shape
    n = w.shape[1]
    return pl.pallas_call(
        _mod_kernel,
        grid=(n // tn,),
        in_specs=[pl.BlockSpec((r, d), lambda j: (0, 0)),
                  pl.BlockSpec((d, tn), lambda j: (0, j)),
                  pl.BlockSpec((1, tn), lambda j: (0, j))],
        out_specs=pl.BlockSpec((r, tn), lambda j: (0, j)),
        out_shape=jax.ShapeDtypeStruct((r, n), F32),
        compiler_params=_cparams(("arbitrary",)),
        name="modulation",
    )(cond, w, b.reshape(1, n))


def _rope(t, cos, sa, sb):
    n = t.shape[-1]
    return t * cos + pltpu.roll(t, n - 16, 1) * sa + pltpu.roll(t, 16, 1) * sb


def _inproj_kernel(x_ref, sh_ref, sc_ref, w_ref, cos_ref, sa_ref, sb_ref,
                   q_ref, k_ref, v_ref, u_ref, g_ref, hb_ref):
    @pl.when(pl.program_id(0) == 0)
    def _():
        hb_ref[...] = jnp.zeros_like(hb_ref)

    tm = x_ref.shape[1]
    scale = HEAD_DIM ** -0.5
    n_gate = w_ref.shape[1] - GATE_OFF
    rc = tm // MERGE_CHUNKS
    for r in (slice(c * rc, (c + 1) * rc) for c in range(MERGE_CHUNKS)):
        hb = hb_ref[r]
        cos, sa, sb = cos_ref[r], sa_ref[r], sb_ref[r]
        for j in range(Q_W // 128):
            t = _bdot(hb, w_ref[:, j * 128:(j + 1) * 128])
            q_ref[0, r, j * 128:(j + 1) * 128] = (_rope(t, cos, sa, sb) * scale).astype(BF16)
        t = _bdot(hb, w_ref[:, K_OFF:V_OFF])
        k_ref[0, r] = _rope(t, cos, sa, sb).astype(BF16)
        v_ref[0, r] = _bdot(hb, w_ref[:, V_OFF:HY_OFF]).astype(BF16)
        for j in range((GATE_OFF - HY_OFF) // 512):
            u_ref[0, r, j * 512:(j + 1) * 512] = _bdot(
                hb, w_ref[:, HY_OFF + j * 512:HY_OFF + (j + 1) * 512])
        for j in range(n_gate // 512):
            g_ref[0, r, j * 512:(j + 1) * 512] = jax.nn.sigmoid(_bdot(
                hb, w_ref[:, GATE_OFF + j * 512:GATE_OFF + (j + 1) * 512])).astype(BF16)

        h = _ln(x_ref[0, r]) * (1.0 + sc_ref[0]) + sh_ref[0]
        hb_ref[r] = h.astype(BF16)


def _in_projection(x, sh, sc, w_in_b, cos_t, sa_t, sb_t, tm=512):
    b, l, d = x.shape
    in_w = w_in_b.shape[1]
    hy_w = GATE_OFF - HY_OFF
    g_w = in_w - GATE_OFF
    per_b = l // tm
    nt = b * per_b
    t1 = lambda i: jnp.minimum(i, nt - 1)
    t2 = lambda i: jnp.maximum(i - 1, 0)
    row1 = lambda i: (t1(i) // per_b, t1(i) % per_b, 0)
    row2 = lambda i: (t2(i) // per_b, t2(i) % per_b, 0)
    vec1 = lambda i: (t1(i) // per_b, 0, 0)
    tab2 = lambda i: (t2(i) % per_b, 0)
    return pl.pallas_call(
        _inproj_kernel,
        grid=(nt + 1,),
        in_specs=[pl.BlockSpec((1, tm, d), row1),
                  pl.BlockSpec((1, 1, d), vec1),
                  pl.BlockSpec((1, 1, d), vec1),
                  pl.BlockSpec((d, in_w), lambda i: (0, 0)),
                  pl.BlockSpec((tm, 128), tab2),
                  pl.BlockSpec((tm, 128), tab2),
                  pl.BlockSpec((tm, 128), tab2)],
        out_specs=[pl.BlockSpec((1, tm, Q_W), row2),
                   pl.BlockSpec((1, tm, KV_W), row2),
                   pl.BlockSpec((1, tm, KV_W), row2),
                   pl.BlockSpec((1, tm, hy_w), row2),
                   pl.BlockSpec((1, tm, g_w), row2)],
        out_shape=[jax.ShapeDtypeStruct((b, l, Q_W), BF16),
                   jax.ShapeDtypeStruct((b, l, KV_W), BF16),
                   jax.ShapeDtypeStruct((b, l, KV_W), BF16),
                   jax.ShapeDtypeStruct((b, l, hy_w), F32),
                   jax.ShapeDtypeStruct((b, l, g_w), BF16)],
        scratch_shapes=[pltpu.VMEM((tm, d), BF16)],
        compiler_params=_cparams(("arbitrary",)),
        name="in_projection",
    )(x, sh, sc, w_in_b, cos_t, sa_t, sb_t)


def _ctx_kv_kernel(x_ref, sh_ref, sc_ref, w_ref, k_ref, v_ref):
    h = _ln(x_ref[0]) * (1.0 + sc_ref[0]) + sh_ref[0]
    kv = _bdot(h.astype(BF16), w_ref[...])
    k_ref[0] = kv[:, :KV_W].astype(BF16)
    v_ref[0] = kv[:, KV_W:].astype(BF16)


def _ctx_kv(ctx, sh, sc, w_kv_b):
    b, c, d = ctx.shape
    row = lambda bi: (bi, 0, 0)
    return pl.pallas_call(
        _ctx_kv_kernel,
        grid=(b,),
        in_specs=[pl.BlockSpec((1, c, d), row),
                  pl.BlockSpec((1, 1, d), lambda bi: (0, 0, 0)),
                  pl.BlockSpec((1, 1, d), lambda bi: (0, 0, 0)),
                  pl.BlockSpec((d, 2 * KV_W), lambda bi: (0, 0))],
        out_specs=[pl.BlockSpec((1, c, KV_W), row), pl.BlockSpec((1, c, KV_W), row)],
        out_shape=[jax.ShapeDtypeStruct((b, c, KV_W), BF16),
                   jax.ShapeDtypeStruct((b, c, KV_W), BF16)],
        compiler_params=_cparams(("arbitrary",)),
        name="ctx_kv",
    )(ctx, sh, sc, w_kv_b)


def _attn_kernel(sink_ref, q_ref, k_ref, v_ref, kc_ref, vc_ref, bias_ref, o_ref):
    for qb in range(q_ref.shape[1] // ATTN_BLOCK):
        rows = slice(qb * ATTN_BLOCK, (qb + 1) * ATTN_BLOCK)
        j = pl.program_id(1) * (q_ref.shape[1] // ATTN_BLOCK) + qb
        o_ref[0, rows] = _attn_block(j, sink_ref, q_ref.at[0, rows], k_ref, v_ref, kc_ref, vc_ref,
                                     bias_ref)


def _band_bias():
    span = ATTN_BLOCK + 2 * WINDOW
    q = np.arange(GQA_GROUP * ATTN_BLOCK)[None, :, None] % ATTN_BLOCK
    k = np.arange(span)[None, None, :] - ATTN_BLOCK * np.arange(span // ATTN_BLOCK)[:, None, None]
    return np.where(np.abs(k - q) <= WINDOW, 0.0, NEG_BIG).astype(np.float32)


def _attn_block(j, sink_ref, q_ref, k_ref, v_ref, kc_ref, vc_ref, bias_ref):
    l = k_ref.shape[1]
    span = ATTN_BLOCK + 2 * WINDOW
    q0 = j * ATTN_BLOCK
    start = pl.multiple_of(jnp.clip(q0 - WINDOW, 0, l - span), ATTN_BLOCK)
    rows = GQA_GROUP * ATTN_BLOCK
    bias = bias_ref[(q0 - start) // ATTN_BLOCK]
    head_of_row = lax.broadcasted_iota(jnp.int32, (rows, 1), 0) // ATTN_BLOCK
    dn = (((1,), (1,)), ((), ()))
    outs = []
    for kv in range(N_KV_HEADS):
        ks = slice(kv * HEAD_DIM, (kv + 1) * HEAD_DIM)
        kl = k_ref[0, pl.ds(start, span), ks]
        vl = v_ref[0, pl.ds(start, span), ks]
        kc = kc_ref[0, :, ks]
        vc = vc_ref[0, :, ks]
        heads = [kv * GQA_GROUP + g for g in range(GQA_GROUP)]
        qg = jnp.concatenate([q_ref[:, h * HEAD_DIM:(h + 1) * HEAD_DIM] for h in heads], axis=0)
        sink = jnp.zeros((rows, 1), F32)
        for g, h in enumerate(heads):
            sink = jnp.where(head_of_row == g, sink_ref[h], sink)
        s_loc = lax.dot_general(qg, kl, dn, preferred_element_type=F32) + bias
        s_ctx = lax.dot_general(qg, kc, dn, preferred_element_type=F32)
        blocks = [s[:, c:c + 128] for s in (s_loc, s_ctx) for c in range(0, s.shape[1], 128)]
        folded = functools.reduce(jnp.maximum, blocks)
        m = jnp.maximum(jnp.max(folded, axis=1, keepdims=True), sink)
        p_loc = jnp.exp(s_loc - m).astype(BF16)
        p_ctx = jnp.exp(s_ctx - m).astype(BF16)
        ones_l = jnp.ones((span, HEAD_DIM), BF16)
        ones_c = jnp.ones((kc.shape[0], HEAD_DIM), BF16)
        pv = (_bdot(p_loc, jnp.concatenate([vl, ones_l], axis=1))
              + _bdot(p_ctx, jnp.concatenate([vc, ones_c], axis=1)))
        den = pv[:, HEAD_DIM:HEAD_DIM + 1] + jnp.exp(sink - m)
        o = pv[:, :HEAD_DIM] / den
        outs.extend(o[g * ATTN_BLOCK:(g + 1) * ATTN_BLOCK] for g in range(GQA_GROUP))
    return jnp.concatenate(outs, axis=1).astype(BF16)


def _attention(q, k, v, kc, vc, sink, q_blocks=8):
    b, l, _ = q.shape
    c = kc.shape[1]
    tq = q_blocks * ATTN_BLOCK
    bias = jnp.asarray(_band_bias())
    full = lambda bi, j, s: (bi, 0, 0)
    grid_spec = pltpu.PrefetchScalarGridSpec(
        num_scalar_prefetch=1,
        grid=(b, l // tq),
        in_specs=[pl.BlockSpec((1, tq, Q_W), lambda bi, j, s: (bi, j, 0)),
                  pl.BlockSpec((1, l, KV_W), full),
                  pl.BlockSpec((1, l, KV_W), full),
                  pl.BlockSpec((1, c, KV_W), full),
                  pl.BlockSpec((1, c, KV_W), full),
                  pl.BlockSpec(bias.shape, lambda bi, j, s: (0, 0, 0))],
        out_specs=pl.BlockSpec((1, tq, Q_W), lambda bi, j, s: (bi, j, 0)),
    )
    return pl.pallas_call(
        _attn_kernel,
        grid_spec=grid_spec,
        out_shape=jax.ShapeDtypeStruct((b, l, Q_W), BF16),
        compiler_params=_cparams(("arbitrary", "arbitrary")),
        name="window_attention",
    )(sink, q, k, v, kc, vc, bias)


def _dft_tables(l, wc):
    h = l // 2
    idx = np.arange(h, dtype=np.int64)
    ang = 2.0 * np.pi * ((idx[:, None] * idx[None, :]) % l).astype(np.float64) / l
    cm = np.cos(ang).astype(np.float32)
    sm = np.sin(ang).astype(np.float32)
    tw = 2.0 * np.pi * idx.astype(np.float64) / (2 * l)
    ones = np.ones((1, wc), np.float32)
    wr = np.cos(tw).astype(np.float32)[:, None] * ones
    wi = (-np.sin(tw)).astype(np.float32)[:, None] * ones
    sgn = np.where(idx % 2 == 0, 1.0, -1.0).astype(np.float32)[:, None] * ones
    wk = np.where(idx == 0, 1.0 / (2 * l), 2.0 / (2 * l)).astype(np.float32)[:, None] * ones
    return cm, sm, wr, wi, sgn, wk


def _half_spectrum(se, so, cm, sm, wr, wi, sgn):
    seb, sob = se.astype(BF16), so.astype(BF16)
    ce, ss_e = _bdot(cm, seb), _bdot(sm, seb)
    co, ss_o = _bdot(cm, sob), _bdot(sm, sob)
    vr = wr * co + wi * ss_o
    vi = wi * co - wr * ss_o
    zar, zai = ce + vr, vi - ss_e
    zbr, zbi = ce - vr, -ss_e - vi
    e_ny = jnp.sum(sgn * se, axis=0, keepdims=True)
    o_ny = jnp.sum(sgn * so, axis=0, keepdims=True)
    return zar, zai, zbr, zbi, e_ny, -o_ny


def _filter_kernel(ze_ref, zo_ref, w1_ref, b1_ref, w2_ref, b2_ref, w3f_ref, w3b_ref,
                   de_ref, do_ref, cm_ref, sm_ref, wr_ref, wi_ref, sgn_ref, wk_ref,
                   har_ref, hai_ref, hbr_ref, hbi_ref, hny_ref, ae_ref, ao_ref):
    cm, sm = cm_ref[...], sm_ref[...]
    wr, wi, sgn, wk = wr_ref[...], wi_ref[...], sgn_ref[...], wk_ref[...]
    h = cm.shape[0]

    @pl.when((pl.program_id(0) == 0) & (pl.program_id(1) == 0))
    def _():
        for z_ref, a_ref in ((ze_ref, ae_ref), (zo_ref, ao_ref)):
            a = jnp.sin(_fdot(z_ref[...], w1_ref[...]) + b1_ref[...])
            a_ref[...] = jnp.sin(_fdot(a, w2_ref[...]) + b2_ref[...])

    def taps(a_ref, w3_ref, d_ref):
        return _fdot(a_ref[...], w3_ref[0]) * d_ref[...]

    fe, fo = taps(ae_ref, w3f_ref, de_ref), taps(ao_ref, w3f_ref, do_ref)
    be, bo = taps(ae_ref, w3b_ref, de_ref), taps(ao_ref, w3b_ref, do_ref)
    row = lax.broadcasted_iota(jnp.int32, be.shape, 0)
    be = jnp.where(row == 0, 0.0, be)
    pe, po = (fe + be).astype(BF16), (fo + bo).astype(BF16)
    me, mo = (fe - be).astype(BF16), (fo - bo).astype(BF16)
    ce_p = _bdot(cm, pe)
    vr_p = wr * _bdot(cm, po) + wi * _bdot(sm, po)
    se_m = _bdot(sm, me)
    vi_m = wi * _bdot(cm, mo) - wr * _bdot(sm, mo)
    har_ref[0] = wk * (ce_p + vr_p)
    hai_ref[0] = wk * (vi_m - se_m)
    hbr_ref[0] = wk * (ce_p - vr_p)
    hbi_ref[0] = wk * (-se_m - vi_m)
    ny_scale = 2.0 / (4 * h)
    nr = ny_scale * jnp.sum(sgn * (fe + be), axis=0, keepdims=True)
    ni = -ny_scale * jnp.sum(sgn * (fo - bo), axis=0, keepdims=True)
    rows = lax.broadcasted_iota(jnp.int32, (8, nr.shape[1]), 0)
    hny_ref[0] = jnp.where(rows == 0, nr, jnp.where(rows == 1, ni, 0.0))


def _hyena_filters(zfe, zfo, w1, b1, w2, b2, w3, dec_e, dec_o, tabs, wc):
    cm, sm, wr, wi, sgn, wk = tabs
    h = cm.shape[0]
    emb = zfe.shape[1]
    hid = w2.shape[0]
    nblk = HY_WIDTH // wc
    w3r = w3.reshape(hid, HY_ORDER * 2, HY_WIDTH).transpose(1, 0, 2)
    const = lambda o, cb: (0, 0)
    chan = lambda o, cb: (0, cb)
    out_spec = pl.BlockSpec((1, h, wc), lambda o, cb: (o, 0, cb))
    out_shape = jax.ShapeDtypeStruct((HY_ORDER, h, HY_WIDTH), F32)
    return pl.pallas_call(
        _filter_kernel,
        grid=(HY_ORDER, nblk),
        in_specs=[pl.BlockSpec((h, emb), const), pl.BlockSpec((h, emb), const),
                  pl.BlockSpec((emb, hid), const), pl.BlockSpec((1, hid), const),
                  pl.BlockSpec((hid, hid), const), pl.BlockSpec((1, hid), const),
                  pl.BlockSpec((1, hid, wc), lambda o, cb: (2 * o, 0, cb)),
                  pl.BlockSpec((1, hid, wc), lambda o, cb: (2 * o + 1, 0, cb)),
                  pl.BlockSpec((h, wc), chan), pl.BlockSpec((h, wc), chan),
                  pl.BlockSpec((h, h), const), pl.BlockSpec((h, h), const),
                  pl.BlockSpec((h, wc), const), pl.BlockSpec((h, wc), const),
                  pl.BlockSpec((h, wc), const), pl.BlockSpec((h, wc), const)],
        out_specs=[out_spec, out_spec, out_spec, out_spec,
                   pl.BlockSpec((1, 8, wc), lambda o, cb: (o, 0, cb))],
        out_shape=[out_shape, out_shape, out_shape, out_shape,
                   jax.ShapeDtypeStruct((HY_ORDER, 8, HY_WIDTH), F32)],
        scratch_shapes=[pltpu.VMEM((h, hid), F32), pltpu.VMEM((h, hid), F32)],
        compiler_params=_cparams(("arbitrary", "arbitrary")),
        name="hyena_filter_spectrum",
    )(zfe, zfo, w1, b1.reshape(1, hid), w2, b2.reshape(1, hid), w3r, w3r,
      dec_e, dec_o, cm, sm, wr, wi, sgn, wk)


def _long_conv(ze, zo, har, hai, hbr, hbi, hny, cm, sm, wr, wi, sgn):
    zar, zai, zbr, zbi, znr, zni = _half_spectrum(ze, zo, cm, sm, wr, wi, sgn)
    yar, yai = zar * har - zai * hai, zar * hai + zai * har
    ybr, ybi = zbr * hbr - zbi * hbi, zbr * hbi + zbi * hbr
    dr, di = yar - ybr, yai - ybi
    sar, sai = (yar + ybr).astype(BF16), (yai + ybi).astype(BF16)
    sbr, sbi = (dr * wr + di * wi).astype(BF16), (di * wr - dr * wi).astype(BF16)
    hnr, hni = hny[0:1], hny[1:2]
    ynr = znr * hnr - zni * hni
    yni = znr * hni + zni * hnr
    ye = _bdot(cm, sar) - _bdot(sm, sai) + sgn * ynr
    yo = _bdot(cm, sbr) - _bdot(sm, sbi) - sgn * yni
    return ye, yo


def _hyena_kernel(*refs):
    nsub = (len(refs) - 18) // 4
    u_refs = [refs[s * nsub:(s + 1) * nsub] for s in range(HY_ORDER + 1)]
    rest = refs[(HY_ORDER + 1) * nsub:]
    (cw0_ref, cw1_ref, cw2_ref, cb0_ref, cb1_ref, cb2_ref, skip_ref,
     har_ref, hai_ref, hbr_ref, hbi_ref, hny_ref,
     cm_ref, sm_ref, wr_ref, wi_ref, sgn_ref, o_ref) = rest[:18]
    il_refs = rest[18:]
    h = cm_ref.shape[0]
    cm, sm = cm_ref[...], sm_ref[...]
    wr, wi, sgn = wr_ref[...], wi_ref[...], sgn_ref[...]
    row = lax.broadcasted_iota(jnp.int32, (h, wr.shape[1]), 0)

    def short_conv(u_slabs, cw_ref, cb_ref):
        ue = jnp.concatenate([r[0, pl.ds(0, h, stride=2), :] for r in u_slabs], axis=1)
        uo = jnp.concatenate([r[0, pl.ds(1, h, stride=2), :] for r in u_slabs], axis=1)
        w0, w1, w2 = cw_ref[0:1], cw_ref[1:2], cw_ref[2:3]
        uo_prev = jnp.where(row == 0, 0.0, pltpu.roll(uo, 1, 0))
        ue_next = jnp.where(row == h - 1, 0.0, pltpu.roll(ue, h - 1, 0))
        cb = cb_ref[...]
        se = cb + uo_prev * w0 + ue * w1 + uo * w2
        so = cb + ue * w0 + uo * w1 + ue_next * w2
        return se, so

    x_streams = [short_conv(u_refs[0], cw0_ref, cb0_ref), short_conv(u_refs[1], cw1_ref, cb1_ref)]
    ze, zo = short_conv(u_refs[2], cw2_ref, cb2_ref)
    for o in range(HY_ORDER):
        ce, co = _long_conv(ze, zo, har_ref[o], hai_ref[o], hbr_ref[o], hbi_ref[o], hny_ref[o],
                            cm, sm, wr, wi, sgn)
        sk = skip_ref[o:o + 1]
        xe, xo = x_streams[o]
        ze = xe * (ce + sk * ze)
        zo = xo * (co + sk * zo)
    for j, il in enumerate(il_refs):
        il[pl.ds(0, h, stride=2), :] = ze[:, j * 128:(j + 1) * 128]
        il[pl.ds(1, h, stride=2), :] = zo[:, j * 128:(j + 1) * 128]
        o_ref[0, :, j * 128:(j + 1) * 128] = il[...]


def _hyena(u, conv_w, conv_b, skip, spectra, tabs, wc):
    b, l, _ = u.shape
    har, hai, hbr, hbi, hny = spectra
    cm, sm, wr, wi, sgn, _ = tabs
    h = l // 2
    nblk = HY_WIDTH // wc
    nsub = wc // 128
    conv_b = conv_b.reshape(1, -1)

    def ublk(s, j):
        return pl.BlockSpec((1, l, 128), lambda cb, bi: (bi, 0, (cb + s * nblk) * nsub + j))

    def wblk(r, s):
        return pl.BlockSpec((r, wc), lambda cb, bi: (0, cb + s * nblk))

    const = lambda cb, bi: (0, 0)
    spec = pl.BlockSpec((HY_ORDER, h, wc), lambda cb, bi: (0, 0, cb))
    return pl.pallas_call(
        _hyena_kernel,
        grid=(nblk, b),
        in_specs=[ublk(s, j) for s in range(HY_ORDER + 1) for j in range(nsub)] + [
                  wblk(HY_SHORT_CONV, 0), wblk(HY_SHORT_CONV, 1), wblk(HY_SHORT_CONV, 2),
                  wblk(1, 0), wblk(1, 1), wblk(1, 2),
                  pl.BlockSpec((HY_ORDER, wc), lambda cb, bi: (0, cb)),
                  spec, spec, spec, spec,
                  pl.BlockSpec((HY_ORDER, 8, wc), lambda cb, bi: (0, 0, cb)),
                  pl.BlockSpec((h, h), const), pl.BlockSpec((h, h), const),
                  pl.BlockSpec((h, wc), const), pl.BlockSpec((h, wc), const),
                  pl.BlockSpec((h, wc), const)],
        out_specs=pl.BlockSpec((1, l, wc), lambda cb, bi: (bi, 0, cb)),
        out_shape=jax.ShapeDtypeStruct((b, l, HY_WIDTH), F32),
        scratch_shapes=[pltpu.VMEM((l, 128), F32) for _ in range(nsub)],
        compiler_params=_cparams(("arbitrary", "arbitrary")),
        name="hyena_long_conv",
    )(*([u] * ((HY_ORDER + 1) * nsub)), conv_w, conv_w, conv_w, conv_b, conv_b, conv_b, skip,
      har, hai, hbr, hbi, hny, cm, sm, wr, wi, sgn)


def _merge_kernel(alpha, att_ref, hy_ref, gate_ref, x_ref, g1_ref, sh2_ref, sc2_ref,
                  wba_ref, wbh_ref, wo_ref, l1g_ref, l1b_ref, rw_ref, rb_ref,
                  xmid_ref, h2_ref, logit_ref, res_ref):
    d = x_ref.shape[1]

    @pl.when(pl.program_id(0) == 0)
    def _():
        res_ref[...] = jnp.zeros_like(res_ref)

    rw = rw_ref[...]
    r_hi = rw.astype(BF16)
    r_lo = (rw - r_hi.astype(F32)).astype(BF16)
    dn = (((1,), (1,)), ((), ()))
    nt_dot = lambda p, q: lax.dot_general(p, q, dn, preferred_element_type=F32)
    tm = x_ref.shape[0]
    rc = tm // MERGE_CHUNKS
    for r in (slice(c * rc, (c + 1) * rc) for c in range(MERGE_CHUNKS)):
        xm = _ln(res_ref[r]) * l1g_ref[...] + l1b_ref[...]
        xmid_ref[r] = xm
        h2 = _ln(xm) * (1.0 + sc2_ref[0]) + sh2_ref[0]
        h_hi = h2.astype(BF16)
        h2_ref[r] = h_hi
        h_lo = (h2 - h_hi.astype(F32)).astype(BF16)
        logit_ref[:, r] = nt_dot(r_hi, h_hi) + nt_dot(r_hi, h_lo) + nt_dot(r_lo, h_hi) + rb_ref[...]

        a = _bdot(att_ref[r], wba_ref[...])
        hh = _bdot(hy_ref[r].astype(BF16), wbh_ref[...])
        ga = gate_ref[r, :d].astype(F32)
        gh = gate_ref[r, d:].astype(F32)
        y = _bdot((ga * a + gh * hh).astype(BF16), wo_ref[...])
        res_ref[r] = alpha * x_ref[r] + g1_ref[0] * y


def _merge(att, hy, gate, x, g1, sh2, sc2, wba, wbh, wo, l1g, l1b, rw_t, rb, alpha, tm=512):
    t, d = x.shape
    ne = rw_t.shape[0]
    nt = t // tm
    per_b = nt // g1.shape[0]
    t1 = lambda i: jnp.minimum(i, nt - 1)
    t2 = lambda i: jnp.maximum(i - 1, 0)
    row1 = lambda i: (t1(i), 0)
    row2 = lambda i: (t2(i), 0)
    const = lambda i: (0, 0)
    return pl.pallas_call(
        functools.partial(_merge_kernel, alpha),
        grid=(nt + 1,),
        in_specs=[pl.BlockSpec((tm, Q_W), row1),
                  pl.BlockSpec((tm, HY_WIDTH), row1),
                  pl.BlockSpec((tm, 2 * d), row1),
                  pl.BlockSpec((tm, d), row1),
                  pl.BlockSpec((1, 1, d), lambda i: (t1(i) // per_b, 0, 0)),
                  pl.BlockSpec((1, 1, d), lambda i: (t2(i) // per_b, 0, 0)),
                  pl.BlockSpec((1, 1, d), lambda i: (t2(i) // per_b, 0, 0)),
                  pl.BlockSpec((Q_W, d), const), pl.BlockSpec((HY_WIDTH, d), const),
                  pl.BlockSpec((d, d), const),
                  pl.BlockSpec((1, d), const), pl.BlockSpec((1, d), const),
                  pl.BlockSpec((ne, d), const), pl.BlockSpec((ne, 1), const)],
        out_specs=[pl.BlockSpec((tm, d), row2), pl.BlockSpec((tm, d), row2),
                   pl.BlockSpec((ne, tm), lambda i: (0, t2(i)))],
        out_shape=[jax.ShapeDtypeStruct((t, d), F32),
                   jax.ShapeDtypeStruct((t, d), BF16),
                   jax.ShapeDtypeStruct((ne, t), F32)],
        scratch_shapes=[pltpu.VMEM((tm, d), F32)],
        compiler_params=_cparams(("arbitrary",)),
        name="merge_ln_router",
    )(att, hy, gate, x, g1, sh2, sc2, wba, wbh, wo, l1g.reshape(1, d), l1b.reshape(1, d),
      rw_t, rb.reshape(ne, 1))


def _route_kernel(lg_ref, gate_ref, pos_ref, tcnt_ref, tbase_ref, cnt_ref, run_ref):
    @pl.when(pl.program_id(0) == 0)
    def _():
        run_ref[...] = jnp.zeros_like(run_ref)

    tiles = tcnt_ref.shape[0]
    tr = lg_ref.shape[1] // tiles
    for s in range(tiles):
        cols = slice(s * tr, (s + 1) * tr)
        gate, pos, cnt_row = _route_tile(lg_ref[:, cols])
        gate_ref[:, cols] = gate
        pos_ref[:, cols] = pos
        tcnt_ref[s] = cnt_row.astype(jnp.int32)
        tbase_ref[s] = run_ref[...].astype(jnp.int32)
        run_ref[...] = run_ref[...] + cnt_row
    cnt_ref[...] = run_ref[...].astype(jnp.int32)


def _route_tile(lg):
    ne, tr = lg.shape
    sub = lax.broadcasted_iota(jnp.int32, (ne, tr), 0)
    work = lg
    vals, hots = [], []
    for _ in range(TOP_K):
        m = jnp.max(work, axis=0, keepdims=True)
        idx = jnp.min(jnp.where(work == m, sub, ne), axis=0, keepdims=True)
        hot = sub == idx
        vals.append(m)
        hots.append(hot)
        work = jnp.where(hot, -jnp.inf, work)
    exps = [jnp.exp(v - vals[0]) for v in vals]
    den = exps[0] + exps[1] + exps[2] + exps[3]
    member = jnp.zeros((ne, tr), F32)
    for hot in hots:
        member = member + jnp.where(hot, 1.0, 0.0)
    r_i = lax.broadcasted_iota(jnp.int32, (tr, tr), 0)
    c_i = lax.broadcasted_iota(jnp.int32, (tr, tr), 1)
    earlier = jnp.where(r_i < c_i, 1.0, 0.0).astype(BF16)
    prefix = _bdot(member.astype(BF16), earlier)
    cnt = jnp.broadcast_to(jnp.sum(member, axis=1, keepdims=True), (ne, 128))
    e_r = lax.broadcasted_iota(jnp.int32, (ne, ne), 0)
    e_c = lax.broadcasted_iota(jnp.int32, (ne, ne), 1)
    lower = jnp.where(e_c < e_r, 1.0, 0.0).astype(BF16)
    off = _bdot(lower, cnt.astype(BF16))
    base = off[:, 0:1] + prefix
    sub_k = lax.broadcasted_iota(jnp.int32, (TOP_K, tr), 0)
    gate = jnp.zeros((TOP_K, tr), F32)
    pos = jnp.zeros((TOP_K, tr), F32)
    for k in range(TOP_K):
        pk = jnp.sum(jnp.where(hots[k], base, 0.0), axis=0, keepdims=True)
        gate = jnp.where(sub_k == k, exps[k] / den, gate)
        pos = jnp.where(sub_k == k, pk, pos)
    cnt_row = lax.dot_general(jnp.ones((8, tr), BF16), member.astype(BF16),
                              (((1,), (1,)), ((), ())), preferred_element_type=F32)
    return gate, pos.astype(jnp.int32), cnt_row


def _route(logits_t, tr, tiles_per_step=4):
    ne, t = logits_t.shape
    nt = t // tr
    tb = tr * tiles_per_step
    tok = lambda i: (0, i)
    tile = lambda i: (i, 0, 0)
    return pl.pallas_call(
        _route_kernel,
        grid=(nt // tiles_per_step,),
        in_specs=[pl.BlockSpec((ne, tb), tok)],
        out_specs=[pl.BlockSpec((TOP_K, tb), tok), pl.BlockSpec((TOP_K, tb), tok),
                   pl.BlockSpec((tiles_per_step, 8, ne), tile),
                   pl.BlockSpec((tiles_per_step, 8, ne), tile),
                   pl.BlockSpec((8, ne), lambda i: (0, 0))],
        out_shape=[jax.ShapeDtypeStruct((TOP_K, t), F32),
                   jax.ShapeDtypeStruct((TOP_K, t), jnp.int32),
                   jax.ShapeDtypeStruct((nt, 8, ne), jnp.int32),
                   jax.ShapeDtypeStruct((nt, 8, ne), jnp.int32),
                   jax.ShapeDtypeStruct((8, ne), jnp.int32)],
        scratch_shapes=[pltpu.VMEM((8, ne), F32)],
        compiler_params=_cparams(("arbitrary",)),
        name="route_topk",
    )(logits_t)


def _row_tile(d):
    return (d // 256, 128)


def _pack_rows(v, bf16_exact=False):
    n, d = v.shape
    if bf16_exact:
        bits = lax.bitcast_convert_type(v, jnp.uint32)
        word = bits[:, d // 2:] | (bits[:, :d // 2] >> 16)
    else:
        bits = lax.bitcast_convert_type(v.astype(BF16).astype(F32), jnp.uint32)
        word = (bits[:, d // 2:] & jnp.uint32(0xFFFF0000)) | (bits[:, :d // 2] >> 16)
    return word.reshape((n,) + _row_tile(d))


def _unpack_rows(w):
    n = w.shape[0]
    word = w.reshape(n, w.shape[1] * w.shape[2])
    lo = lax.bitcast_convert_type(word << 16, F32)
    hi = lax.bitcast_convert_type(word & jnp.uint32(0xFFFF0000), F32)
    return jnp.concatenate([lo, hi], axis=1)


def _strip_copies(n_ref, dst_ref, make_copy, max_rows):
    ne = dst_ref.shape[2]
    all_bits = [1 << s for s in range(max_rows.bit_length() - 1, -1, -1)]
    small = pl.next_power_of_2(max(1, 2 * TOP_K * max_rows // ne))
    longest = n_ref[0, 0, ne]

    def run(bits):
        def body(e, off):
            n = n_ref[0, 0, e]
            dst = dst_ref[0, 0, e]
            done = jnp.int32(0)
            for bit in bits:
                part = n & bit

                @pl.when(part != 0)
                def _():
                    make_copy(off + done, dst + done, bit).start()

                done = done + part
            return off + n

        lax.fori_loop(0, ne, body, jnp.int32(0))

    @pl.when(longest < small)
    def _():
        run([b for b in all_bits if b < small])

    @pl.when(longest >= small)
    def _():
        run(all_bits)


def _dispatch_kernel(n_ref, dst_ref, pos_ref, h_ref, xs_ref, srt_ref, sem):
    i = pl.program_id(0)
    slot = i % 2
    td = h_ref.shape[0]
    n = td * TOP_K

    @pl.when(i == 0)
    def _():
        tail = xs_ref.shape[0] - EXPERT_TILE
        srt_ref[1, 0:EXPERT_TILE] = jnp.zeros((EXPERT_TILE,) + srt_ref.shape[2:], srt_ref.dtype)
        zero_tail = pltpu.make_async_copy(srt_ref.at[1, pl.ds(0, EXPERT_TILE)],
                                          xs_ref.at[pl.ds(tail, EXPERT_TILE)], sem.at[1])
        zero_tail.start()
        zero_tail.wait()

    pos = pos_ref[...]
    rows = lax.broadcasted_iota(jnp.int32, (n, td), 0)
    hit = rows == pos[0:1]
    for k in range(1, TOP_K):
        hit = hit | (rows == pos[k:k + 1])
    perm = jnp.where(hit, 1.0, 0.0).astype(BF16)
    srt_ref[slot] = _pack_rows(_bdot(perm, h_ref[...]), bf16_exact=True)

    def all_rows(s):
        return pltpu.make_async_copy(srt_ref.at[s], xs_ref.at[pl.ds(0, n)], sem.at[s])

    _strip_copies(n_ref, dst_ref,
                  lambda src, dst, size: pltpu.make_async_copy(
                      srt_ref.at[slot, pl.ds(src, size)], xs_ref.at[pl.ds(dst, size)], sem.at[slot]),
                  td)

    @pl.when(i > 0)
    def _():
        all_rows(1 - slot).wait()

    @pl.when(i == pl.num_programs(0) - 1)
    def _():
        all_rows(slot).wait()


def _dispatch(h2, pos_t, tile_n, tile_dst, td):
    t, d = h2.shape
    a = t * TOP_K
    n = td * TOP_K
    nt = t // td
    smem = lambda v: pl.BlockSpec((1, 1, v.shape[-1]), lambda i: (i, 0, 0),
                                  memory_space=pltpu.SMEM)
    return pl.pallas_call(
        _dispatch_kernel,
        grid=(nt,),
        in_specs=[smem(tile_n), smem(tile_dst),
                  pl.BlockSpec((TOP_K, td), lambda i: (0, i)),
                  pl.BlockSpec((td, d), lambda i: (i, 0))],
        out_specs=pl.BlockSpec(memory_space=pl.ANY),
        out_shape=jax.ShapeDtypeStruct((a + EXPERT_TILE,) + _row_tile(d), jnp.uint32),
        scratch_shapes=[pltpu.VMEM((2, n) + _row_tile(d), jnp.uint32),
                        pltpu.SemaphoreType.DMA((2,))],
        compiler_params=_cparams(("arbitrary",)),
        name="moe_dispatch",
    )(tile_n[:, None, :], tile_dst[:, None, :], pos_t, h2)


def _expert_kernel(exp_ref, row_ref, cls_ref, nitem_ref,
                   xs_ref, w1_ref, b1_ref, w2_ref, b2_ref, ys_ref,
                   w1b_ref, w2b_ref, xbuf_ref, ybuf_ref, sem_in, sem_out):
    w = pl.program_id(0)
    n_items = nitem_ref[0]
    last = exp_ref.shape[0] - 1
    slot = w % 2
    dff = w2_ref.shape[1]

    def fetch(item, s):
        return pltpu.make_async_copy(xs_ref.at[pl.ds(row_ref[item], EXPERT_TILE)],
                                     xbuf_ref.at[s], sem_in.at[s])

    def write(item, s, rows):
        return pltpu.make_async_copy(ybuf_ref.at[s, pl.ds(0, rows)],
                                     ys_ref.at[pl.ds(row_ref[item], rows)], sem_out.at[s])

    def for_tile_size(item, fn):
        for c, rows in enumerate(EXPERT_LAST_TILES):
            @pl.when(cls_ref[item] == c)
            def _():
                fn(rows)

    @pl.when(w == 0)
    def _():
        fetch(0, 0).start()
        ybuf_ref[1] = jnp.zeros(ybuf_ref.shape[1:], ybuf_ref.dtype)
        zero_tail = pltpu.make_async_copy(
            ybuf_ref.at[1], ys_ref.at[pl.ds(ys_ref.shape[0] - EXPERT_TILE, EXPERT_TILE)],
            sem_out.at[1])
        zero_tail.start()
        zero_tail.wait()

    @pl.when(w + 1 < n_items)
    def _():
        fetch(jnp.minimum(w + 1, last), 1 - slot).start()

    @pl.when((w == 0) | (exp_ref[w] != exp_ref[jnp.maximum(w - 1, 0)]))
    def _():
        w1b_ref[...] = w1_ref[0].astype(BF16)
        w2b_ref[...] = w2_ref[0].astype(BF16)

    @pl.when(w < n_items)
    def _():
        fetch(w, slot).wait()

        def compute(rows):
            x = _unpack_rows(xbuf_ref[slot, 0:rows]).astype(BF16)
            hb = _bdot(x, w1b_ref[...]) + b1_ref[0]
            glu = jnp.minimum(hb[:, :dff], SWIGLU_LIMIT)
            lin = jnp.clip(hb[:, dff:], -SWIGLU_LIMIT, SWIGLU_LIMIT)
            act = glu * jax.nn.sigmoid(SWIGLU_ALPHA * glu) * (lin + 1.0)
            ybuf_ref[slot, 0:rows] = _pack_rows(_bdot(act.astype(BF16), w2b_ref[...]) + b2_ref[0])

        for_tile_size(w, compute)

        @pl.when(w > 0)
        def _():
            for_tile_size(w - 1, lambda rows: write(w - 1, 1 - slot, rows).wait())

        for_tile_size(w, lambda rows: write(w, slot, rows).start())

        @pl.when(w == n_items - 1)
        def _():
            for_tile_size(w, lambda rows: write(w, slot, rows).wait())


def _experts(xs, items, w1, b1, w2, b2):
    ne, d, dff2 = w1.shape
    dff = dff2 // 2
    exp_w, row_w, cls_w, nitem = items
    per_expert = lambda w, ex, ro, cl, ni: (ex[w], 0, 0)
    tile = (EXPERT_TILE,) + _row_tile(d)
    grid_spec = pltpu.PrefetchScalarGridSpec(
        num_scalar_prefetch=4,
        grid=(exp_w.shape[0],),
        in_specs=[pl.BlockSpec(memory_space=pl.ANY),
                  pl.BlockSpec((1, d, dff2), per_expert),
                  pl.BlockSpec((1, 1, dff2), per_expert),
                  pl.BlockSpec((1, dff, d), per_expert),
                  pl.BlockSpec((1, 1, d), per_expert)],
        out_specs=pl.BlockSpec(memory_space=pl.ANY),
        scratch_shapes=[pltpu.VMEM((d, dff2), BF16), pltpu.VMEM((dff, d), BF16),
                        pltpu.VMEM((2,) + tile, jnp.uint32), pltpu.VMEM((2,) + tile, jnp.uint32),
                        pltpu.SemaphoreType.DMA((2,)), pltpu.SemaphoreType.DMA((2,))],
    )
    return pl.pallas_call(
        _expert_kernel,
        grid_spec=grid_spec,
        out_shape=jax.ShapeDtypeStruct(xs.shape, jnp.uint32),
        compiler_params=_cparams(("arbitrary",)),
        name="moe_experts",
    )(exp_w, row_w, cls_w, nitem, xs, w1, b1.reshape(ne, 1, dff2), w2, b2.reshape(ne, 1, d))


def _work_items(counts, a):
    ne = counts.shape[0]
    n_items = a // EXPERT_TILE + ne
    ends = jnp.cumsum(counts)
    starts = ends - counts
    n_full = counts // EXPERT_TILE
    rem = counts - n_full * EXPERT_TILE
    n_tiles = n_full + (rem > 0).astype(jnp.int32)
    tile_end = jnp.cumsum(n_tiles)
    tile_start = tile_end - n_tiles
    total = tile_end[-1]
    w = jnp.minimum(jnp.arange(n_items, dtype=jnp.int32), total - 1)
    e_w = jnp.sum((tile_end[None, :] <= w[:, None]).astype(jnp.int32), axis=1)
    hot = e_w[:, None] == jnp.arange(ne, dtype=jnp.int32)[None, :]
    pick = lambda v: jnp.sum(jnp.where(hot, v[None, :], 0), axis=1)
    k = w - pick(tile_start)
    row_w = (pick(starts) + k * EXPERT_TILE).astype(jnp.int32)
    rem_w = pick(rem)
    rem_cls = jnp.zeros_like(rem_w)
    for c, rows in enumerate(EXPERT_LAST_TILES[:-1]):
        rem_cls = rem_cls + (rem_w > rows).astype(jnp.int32)
    full_cls = len(EXPERT_LAST_TILES) - 1
    cls_w = jnp.where(k < pick(n_full), full_cls, rem_cls).astype(jnp.int32)
    return starts, (e_w.astype(jnp.int32), row_w, cls_w, total.reshape(1).astype(jnp.int32))


def _combine_kernel(alpha, n_ref, dst_ref, nn_ref, ndst_ref, pos_ref, gate_ref,
                    ys_ref, xm_ref, g2_ref, lg_ref, lb_ref, o_ref, srt_ref, sem):
    i = pl.program_id(0)
    last = pl.num_programs(0) - 1
    slot = i % 2
    td = xm_ref.shape[0]
    n = td * TOP_K

    def fetch(cnt_ref, from_ref, s):
        _strip_copies(cnt_ref, from_ref,
                      lambda row, src, size: pltpu.make_async_copy(
                          ys_ref.at[pl.ds(src, size)], srt_ref.at[s, pl.ds(row, size)], sem.at[s]),
                      td)

    @pl.when(i == 0)
    def _():
        fetch(n_ref, dst_ref, slot)

    @pl.when(i < last)
    def _():
        fetch(nn_ref, ndst_ref, 1 - slot)

    pltpu.make_async_copy(ys_ref.at[pl.ds(0, n)], srt_ref.at[slot], sem.at[slot]).wait()

    pos, gate = jnp.transpose(pos_ref[...]), jnp.transpose(gate_ref[...])
    lanes = lax.broadcasted_iota(jnp.int32, (td, n), 1)
    wsel = jnp.zeros((td, n), F32)
    for k in range(TOP_K):
        wsel = jnp.where(lanes == pos[:, k:k + 1], gate[:, k:k + 1], wsel)
    y = _unpack_rows(srt_ref[slot]).astype(BF16)
    wb = wsel.astype(BF16)
    rc = td // MERGE_CHUNKS
    for r in (slice(c * rc, (c + 1) * rc) for c in range(MERGE_CHUNKS)):
        f = _bdot(wb[r], y)
        o_ref[r] = _ln(alpha * xm_ref[r] + g2_ref[0] * f) * lg_ref[...] + lb_ref[...]


def _combine(ys, pos, gate, tile_n, tile_dst, x_mid, g2, lg, lb, alpha, td):
    t, d = x_mid.shape
    n = td * TOP_K
    nt = t // td
    per_b = nt // g2.shape[0]
    first = lambda v: pl.BlockSpec((1, 1, v.shape[-1]), lambda i: (0, 0, 0),
                                   memory_space=pltpu.SMEM)
    nxt = lambda v: pl.BlockSpec((1, 1, v.shape[-1]),
                                 lambda i: (jnp.minimum(i + 1, nt - 1), 0, 0),
                                 memory_space=pltpu.SMEM)
    const = lambda i: (0, 0)
    tile_n = tile_n[:, None, :]
    tile_dst = tile_dst[:, None, :]
    return pl.pallas_call(
        functools.partial(_combine_kernel, alpha),
        grid=(nt,),
        in_specs=[first(tile_n), first(tile_dst), nxt(tile_n), nxt(tile_dst),
                  pl.BlockSpec((TOP_K, td), lambda i: (0, i)),
                  pl.BlockSpec((TOP_K, td), lambda i: (0, i)),
                  pl.BlockSpec(memory_space=pl.ANY),
                  pl.BlockSpec((td, d), lambda i: (i, 0)),
                  pl.BlockSpec((1, 1, d), lambda i: (i // per_b, 0, 0)),
                  pl.BlockSpec((1, d), const), pl.BlockSpec((1, d), const)],
        out_specs=pl.BlockSpec((td, d), lambda i: (i, 0)),
        out_shape=jax.ShapeDtypeStruct((t, d), F32),
        scratch_shapes=[pltpu.VMEM((2, n) + _row_tile(d), jnp.uint32),
                        pltpu.SemaphoreType.DMA((2,))],
        compiler_params=_cparams(("arbitrary",)),
        name="moe_combine_ln",
    )(tile_n, tile_dst, tile_n, tile_dst, pos, gate, ys, x_mid, g2,
      lg.reshape(1, d), lb.reshape(1, d))


def _rope_tables(l):
    f32 = np.float32
    rows = l // GRID_W
    row = np.repeat(np.arange(rows, dtype=f32), GRID_W)
    col = np.tile(np.arange(GRID_W, dtype=f32), rows)
    n_freq = HEAD_DIM // 4
    inv_freq = np.power(f32(ROPE_BASE), -np.arange(n_freq, dtype=f32) / f32(n_freq)).astype(f32)
    ang_r = (row[:, None] * inv_freq).astype(f32)
    ang_c = (col[:, None] * inv_freq).astype(f32)
    zero = np.zeros_like(ang_r)
    cos_r, sin_r, cos_c, sin_c = np.cos(ang_r), np.sin(ang_r), np.cos(ang_c), np.sin(ang_c)
    cos_h = np.concatenate([cos_r, cos_r, cos_c, cos_c], axis=1)
    sa_h = np.concatenate([-sin_r, zero, -sin_c, zero], axis=1)
    sb_h = np.concatenate([zero, sin_r, zero, sin_c], axis=1)
    rep = 128 // HEAD_DIM
    return tuple(np.tile(a, (1, rep)).astype(f32) for a in (cos_h, sa_h, sb_h))


def _filter_features(l):
    f32 = np.float32
    bands = (HY_EMB_DIM - 1) // 2
    t = np.linspace(0.0, 1.0, l, dtype=f32)[:, None]
    omega = (f32(2.0 * math.pi) * np.arange(l, dtype=f32)[:, None] / f32(l)).astype(f32)
    f = np.linspace(1e-4, bands - 1, bands, dtype=f32)[None, :]
    ang = (f * omega).astype(f32)
    z = np.concatenate([t, np.cos(ang), -np.sin(ang)], axis=-1).astype(f32)
    min_decay = math.log(HY_DECAY_TARGET) / HY_FAST_DECAY_PCT
    max_decay = math.log(HY_DECAY_TARGET) / HY_SLOW_DECAY_PCT
    deltas = np.abs(np.linspace(min_decay, max_decay, HY_WIDTH, dtype=f32))
    decay = np.exp(-t * deltas).astype(f32)
    return z, decay


def kernel(x, c, ctx, c_ctx, w_mod, b_mod, w_in, attn_sink, hy_conv_w, hy_conv_b, hy_filt_w1,
           hy_filt_b1, hy_filt_w2, hy_filt_b2, hy_filt_w3, hy_skip, w_branch_attn, w_branch_hyena,
           w_out, ln1_g, ln1_b, router_w, router_b, exp_w1, exp_b1, exp_w2, exp_b2, ln2_g, ln2_b):
    depth = w_mod.shape[0]
    assert depth == 1, "only the single-layer configuration is implemented"
    b, l, d = x.shape
    t = b * l
    alpha = (2 * depth) ** 0.25
    hy_wc = 256

    n_cond = b + 1
    pad = (-n_cond) % 8
    cond = jnp.concatenate([c, c_ctx[None], jnp.zeros((pad, d), F32)], axis=0)
    mod = _modulation(cond, w_mod[0], b_mod[0])
    mod_x = mod[:b].reshape(b, 1, 6, d)
    sh1, sc1, g1, sh2, sc2, g2 = (mod_x[:, :, i] for i in range(6))
    mod_c = mod[b:b + 1].reshape(1, 1, 6, d)
    csh1, csc1 = mod_c[:, :, 0], mod_c[:, :, 1]

    w_in_b = w_in[0].astype(BF16)
    cos_t, sa_t, sb_t = (jnp.asarray(a) for a in _rope_tables(l))
    q, k, v, u_hy, gate_x = _in_projection(x, sh1, sc1, w_in_b, cos_t, sa_t, sb_t)
    k_c, v_c = _ctx_kv(ctx, csh1, csc1, w_in_b[:, K_OFF:HY_OFF])
    att = _attention(q, k, v, k_c, v_c, attn_sink[0])

    tabs_np = _dft_tables(l, hy_wc)
    tabs = (jnp.asarray(tabs_np[0]).astype(BF16), jnp.asarray(tabs_np[1]).astype(BF16)) + tuple(
        jnp.asarray(a) for a in tabs_np[2:])
    zfeat, decay = _filter_features(l)
    emb_pad = (-HY_EMB_DIM) % 128
    zfeat = np.pad(zfeat, ((0, 0), (0, emb_pad)))
    fw1 = jnp.pad(hy_filt_w1[0], ((0, emb_pad), (0, 0)))
    spectra = _hyena_filters(jnp.asarray(zfeat[0::2]), jnp.asarray(zfeat[1::2]), fw1, hy_filt_b1[0],
                             hy_filt_w2[0], hy_filt_b2[0], hy_filt_w3[0], jnp.asarray(decay[0::2]),
                             jnp.asarray(decay[1::2]), tabs, hy_wc)
    hy = _hyena(u_hy, hy_conv_w[0], hy_conv_b[0], hy_skip[0], spectra, tabs, hy_wc)

    x_mid, h2, logits_t = _merge(att.reshape(t, Q_W), hy.reshape(t, HY_WIDTH),
                                 gate_x.reshape(t, 2 * d), x.reshape(t, d), g1, sh2, sc2,
                                 w_branch_attn[0].astype(BF16), w_branch_hyena[0].astype(BF16),
                                 w_out[0].astype(BF16), ln1_g[0], ln1_b[0],
                                 jnp.transpose(router_w[0]), router_b[0], alpha)

    moe_td = 256
    gate_t, pos_t, tile_cnt, tile_base, counts = _route(logits_t, moe_td)
    starts, items = _work_items(counts[0], t * TOP_K)
    tile_n = tile_cnt[:, 0, :]
    tile_n = jnp.concatenate([tile_n, jnp.max(tile_n, axis=1, keepdims=True)], axis=1)
    tile_dst = starts[None, :] + tile_base[:, 0, :]
    xs = _dispatch(h2.reshape(t, d), pos_t, tile_n, tile_dst, moe_td)
    ys = _experts(xs, items, exp_w1[0], exp_b1[0], exp_w2[0], exp_b2[0])
    out = _combine(ys, pos_t, gate_t, tile_n, tile_dst,
                   x_mid.reshape(t, d), g2, ln2_g[0], ln2_b[0], alpha, moe_td)
    return out.reshape(b, l, d)
```
